```python
import math
import jax, jax.numpy as jnp
from jax import lax
import numpy as np

D_MODEL = 4096
BATCH = 2
SEQ = 8192
DEPTH = 1
DEC_BATCH = 32
DEC_SEQ = 32
PAST_LEN = 1024

CHUNK = 64
MIX_WIDTH = D_MODEL
ATTN_WIDTH = D_MODEL // 2
HEAD_DIM = 64
N_Q_HEADS = ATTN_WIDTH // HEAD_DIM
N_KV_HEADS = 8
Q_PER_KV = N_Q_HEADS // N_KV_HEADS
KV_WIDTH = N_KV_HEADS * HEAD_DIM
WINDOW = 128
WINDOW_CHUNKS = WINDOW // CHUNK
ROPE_DIM = HEAD_DIM // 4
ROPE_THETA = 500000.0
ATTN_SCALE = HEAD_DIM ** -0.5
SSM_WIDTH = MIX_WIDTH - ATTN_WIDTH
SSM_HEAD_DIM = 64
N_SSM_HEADS = SSM_WIDTH // SSM_HEAD_DIM
N_SSM_GROUPS = 8
HEADS_PER_GROUP = N_SSM_HEADS // N_SSM_GROUPS
D_STATE = 128
BC_WIDTH = N_SSM_GROUPS * D_STATE
CONV_WIDTH = 4
CONV_CH = SSM_WIDTH + 2 * BC_WIDTH
SSD_CHUNK = CHUNK
FFN_HIDDEN = 4 * D_MODEL
RMS_EPS = 1e-6
NEG_INF = -1e30
SPLIT_POINTS = (ATTN_WIDTH,
                ATTN_WIDTH + KV_WIDTH,
                ATTN_WIDTH + 2 * KV_WIDTH,
                ATTN_WIDTH + 2 * KV_WIDTH + SSM_WIDTH,
                ATTN_WIDTH + 2 * KV_WIDTH + 2 * SSM_WIDTH,
                ATTN_WIDTH + 2 * KV_WIDTH + 2 * SSM_WIDTH + BC_WIDTH,
                ATTN_WIDTH + 2 * KV_WIDTH + 2 * SSM_WIDTH + 2 * BC_WIDTH)
IN_PROJ_WIDTH = ATTN_WIDTH + 2 * KV_WIDTH + 2 * SSM_WIDTH + 2 * BC_WIDTH + N_SSM_HEADS

kernel_name = "hymba_swa_sink_mamba2_streaming_step"


def rms_norm(x, w):
    xf = x.astype(jnp.float32)
    xf = xf * lax.rsqrt(jnp.mean(xf * xf, axis=-1, keepdims=True) + RMS_EPS)
    return (xf * w.astype(jnp.float32)).astype(x.dtype)


def partial_rope(x, pos):
    half = ROPE_DIM // 2
    inv_freq = ROPE_THETA ** (-jnp.arange(half, dtype=jnp.float32) / half)
    ang = pos.astype(jnp.float32)[:, None] * inv_freq[None, :]
    cos = jnp.cos(ang)[:, None, :]
    sin = jnp.sin(ang)[:, None, :]
    xr = x[..., :ROPE_DIM].astype(jnp.float32)
    x1, x2 = xr[..., :half], xr[..., half:]
    rot = jnp.concatenate([x1 * cos - x2 * sin, x2 * cos + x1 * sin], axis=-1)
    return jnp.concatenate([rot.astype(x.dtype), x[..., ROPE_DIM:]], axis=-1)


def sink_softmax(s, sink):
    m = jnp.maximum(jnp.max(s, axis=-1, keepdims=True), sink)
    e = jnp.exp(s - m)
    return e / (jnp.sum(e, axis=-1, keepdims=True) + jnp.exp(sink - m))


def band_attention(q, k, v, sinks):
    b, t = q.shape[:2]
    nc = t // CHUNK
    nband = WINDOW_CHUNKS + 1
    qb = q.reshape(b, nc, CHUNK, N_KV_HEADS, Q_PER_KV, HEAD_DIM)
    pad = ((0, 0), (WINDOW, 0), (0, 0), (0, 0))
    kp = jnp.pad(k, pad).reshape(b, nc + WINDOW_CHUNKS, CHUNK, N_KV_HEADS, HEAD_DIM)
    vp = jnp.pad(v, pad).reshape(b, nc + WINDOW_CHUNKS, CHUNK, N_KV_HEADS, HEAD_DIM)
    kb = jnp.concatenate([kp[:, i:i + nc] for i in range(nband)], axis=2)
    vb = jnp.concatenate([vp[:, i:i + nc] for i in range(nband)], axis=2)
    key_pos = jnp.arange(nc)[:, None] * CHUNK - WINDOW + jnp.arange(nband * CHUNK)[None, :]
    valid = (key_pos >= 0)[None, :, None, None, None, :]
    s = jnp.einsum('bcqhgd,bckhd->bchgqk', qb, kb).astype(jnp.float32) * ATTN_SCALE
    s = jnp.where(valid, s, NEG_INF)
    sink = sinks.astype(jnp.float32).reshape(N_KV_HEADS, Q_PER_KV)[None, None, :, :, None, None]
    p = sink_softmax(s, sink)
    o = jnp.einsum('bchgqk,bckhd->bcqhgd', p.astype(v.dtype), vb)
    return o.reshape(b, t, ATTN_WIDTH)


def cached_attention(q, k_all, v_all, sinks):
    b, t = q.shape[:2]
    qg = q.reshape(b, t, N_KV_HEADS, Q_PER_KV, HEAD_DIM)
    s = jnp.einsum('bqhgd,bkhd->bhgqk', qg, k_all).astype(jnp.float32) * ATTN_SCALE
    sink = sinks.astype(jnp.float32).reshape(N_KV_HEADS, Q_PER_KV)[None, :, :, None, None]
    p = sink_softmax(s, sink)
    o = jnp.einsum('bhgqk,bkhd->bqhgd', p.astype(v_all.dtype), v_all)
    return o.reshape(b, t, ATTN_WIDTH)


def causal_conv(xbc, conv_state, conv_w, conv_b):
    t = xbc.shape[1]
    xp = jnp.concatenate([conv_state.astype(xbc.dtype), xbc], axis=1)
    out = conv_b
    for i in range(CONV_WIDTH):
        out = out + xp[:, i:i + t] * conv_w[i]
    return jax.nn.silu(out), xp[:, xp.shape[1] - (CONV_WIDTH - 1):]


def segsum(x):
    t = x.shape[-1]
    xe = jnp.broadcast_to(x[..., None], x.shape + (t,))
    strict = jnp.tril(jnp.ones((t, t), dtype=bool), -1)
    xs = jnp.cumsum(jnp.where(strict, xe, 0.0), axis=-2)
    incl = jnp.tril(jnp.ones((t, t), dtype=bool), 0)
    return jnp.where(incl, xs, -jnp.inf)


def ssd(x, dt, a, bmat, cmat, init_state, chunk):
    b, t = x.shape[:2]
    nc = t // chunk
    xd = (x * dt[..., None]).reshape(b, nc, chunk, N_SSM_GROUPS, HEADS_PER_GROUP, SSM_HEAD_DIM)
    ad = (dt * a).reshape(b, nc, chunk, N_SSM_GROUPS, HEADS_PER_GROUP).transpose(0, 3, 4, 1, 2)
    bb = bmat.reshape(b, nc, chunk, N_SSM_GROUPS, D_STATE)
    cc = cmat.reshape(b, nc, chunk, N_SSM_GROUPS, D_STATE)
    a_cum = jnp.cumsum(ad, axis=-1)
    lmat = jnp.exp(segsum(ad))
    cb = jnp.einsum('bclgn,bcsgn->bgcls', cc, bb)
    y_diag = jnp.einsum('bgrcls,bcsgrp->bclgrp', cb[:, :, None] * lmat, xd)
    decay_states = jnp.exp(a_cum[..., -1:] - a_cum)
    states = jnp.einsum('bclgn,bgrcl,bclgrp->bcgrpn', bb, decay_states, xd)
    init = init_state.reshape(b, N_SSM_GROUPS, HEADS_PER_GROUP, SSM_HEAD_DIM, D_STATE)
    states = jnp.concatenate([init[:, None], states], axis=1)
    a_last = jnp.pad(a_cum[..., -1], ((0, 0), (0, 0), (0, 0), (1, 0)))
    decay_chunk = jnp.exp(segsum(a_last))
    new_states = jnp.einsum('bgrzc,bcgrpn->bzgrpn', decay_chunk, states)
    prev_states, final = new_states[:, :-1], new_states[:, -1]
    y_off = jnp.einsum('bclgn,bcgrpn,bgrcl->bclgrp', cc, prev_states, jnp.exp(a_cum))
    y = (y_diag + y_off).reshape(b, t, N_SSM_HEADS, SSM_HEAD_DIM)
    return y, final.reshape(b, N_SSM_HEADS, SSM_HEAD_DIM, D_STATE)


def gated_group_norm(y, z, w):
    g = y * jax.nn.silu(z)
    b, t, _ = g.shape
    gg = g.reshape(b, t, N_SSM_GROUPS, SSM_WIDTH // N_SSM_GROUPS)
    gg = gg * lax.rsqrt(jnp.mean(gg * gg, axis=-1, keepdims=True) + RMS_EPS)
    return gg.reshape(b, t, SSM_WIDTH) * w.astype(jnp.float32)


def trunk_layer(x, pos, past_k, past_v, conv_state, ssm_state, n_keep,
                norm_mix_w, w_in, q_norm_w, k_norm_w, attn_sinks, conv_w, conv_b,
                dt_bias, a_log, d_skip, ssm_norm_w, w_out, norm_ffn_w, w_up, w_down):
    b, t, _ = x.shape
    h = rms_norm(x, norm_mix_w)
    proj = jnp.einsum('btd,dp->btp', h, w_in)
    q, k, v, xs, z, bm, cm, dt_raw = jnp.split(proj, list(SPLIT_POINTS), axis=-1)
    q = rms_norm(q.reshape(b, t, N_Q_HEADS, HEAD_DIM), q_norm_w)
    k = rms_norm(k.reshape(b, t, N_KV_HEADS, HEAD_DIM), k_norm_w)
    v = v.reshape(b, t, N_KV_HEADS, HEAD_DIM)
    q = partial_rope(q, pos)
    k = partial_rope(k, pos)
    if past_k is None:
        attn = band_attention(q, k, v, attn_sinks)
        new_k = k[:, t - n_keep:]
        new_v = v[:, t - n_keep:]
    else:
        k_all = jnp.concatenate([past_k.astype(k.dtype), k], axis=1)
        v_all = jnp.concatenate([past_v.astype(v.dtype), v], axis=1)
        attn = cached_attention(q, k_all, v_all, attn_sinks)
        new_k = k_all[:, k_all.shape[1] - n_keep:]
        new_v = v_all[:, v_all.shape[1] - n_keep:]
    xbc = jnp.concatenate([xs, bm, cm], axis=-1)
    xbc, new_conv = causal_conv(xbc, conv_state, conv_w, conv_b)
    xs, bm, cm = jnp.split(xbc, [SSM_WIDTH, SSM_WIDTH + BC_WIDTH], axis=-1)
    dt = jax.nn.softplus(dt_raw.astype(jnp.float32) + dt_bias.astype(jnp.float32))
    a = -jnp.exp(a_log.astype(jnp.float32))
    xs_f = xs.astype(jnp.float32).reshape(b, t, N_SSM_HEADS, SSM_HEAD_DIM)
    y, new_ssm = ssd(xs_f, dt, a,
                     bm.astype(jnp.float32).reshape(b, t, N_SSM_GROUPS, D_STATE),
                     cm.astype(jnp.float32).reshape(b, t, N_SSM_GROUPS, D_STATE),
                     ssm_state.astype(jnp.float32), min(SSD_CHUNK, t))
    y = y + xs_f * d_skip.astype(jnp.float32)[:, None]
    y = gated_group_norm(y.reshape(b, t, SSM_WIDTH), z.astype(jnp.float32), ssm_norm_w)
    mix = jnp.concatenate([attn, y.astype(x.dtype)], axis=-1)
    x = x + jnp.einsum('btm,md->btd', mix, w_out)
    h2 = rms_norm(x, norm_ffn_w)
    u = jax.nn.relu(jnp.einsum('btd,df->btf', h2, w_up))
    x = x + jnp.einsum('btf,fd->btd', u * u, w_down)
    return x, new_k, new_v, new_conv, new_ssm.astype(x.dtype)


def setup_inputs(seed: int = 0) -> dict:
    key = jax.random.key(seed)
    ks = jax.random.split(key, 24)
    f32 = jnp.float32

    def nrm(k, shape, scale):
        return scale * jax.random.normal(k, shape, f32)

    n_keep = min(WINDOW, PAST_LEN)
    dt0 = jnp.exp(jax.random.uniform(ks[13], (DEPTH, N_SSM_HEADS), f32,
                                     math.log(1e-3), math.log(1e-1)))
    return {
        "x_prompt": nrm(ks[0], (BATCH, SEQ, D_MODEL), 1.0),
        "x_sample": nrm(ks[1], (DEC_BATCH, DEC_SEQ, D_MODEL), 1.0),
        "cache_k": nrm(ks[2], (DEPTH, DEC_BATCH, n_keep, N_KV_HEADS, HEAD_DIM), 1.0),
        "cache_v": nrm(ks[3], (DEPTH, DEC_BATCH, n_keep, N_KV_HEADS, HEAD_DIM), 1.0),
        "state_conv": nrm(ks[4], (DEPTH, DEC_BATCH, CONV_WIDTH - 1, CONV_CH), 1.0),
        "state_ssm": nrm(ks[5], (DEPTH, DEC_BATCH, N_SSM_HEADS, SSM_HEAD_DIM, D_STATE), 0.1),
        "norm_mix_w": 1.0 + nrm(ks[6], (DEPTH, D_MODEL), 0.02),
        "w_in": nrm(ks[7], (DEPTH, D_MODEL, IN_PROJ_WIDTH), D_MODEL ** -0.5),
        "q_norm_w": 1.0 + nrm(ks[8], (DEPTH, HEAD_DIM), 0.02),
        "k_norm_w": 1.0 + nrm(ks[9], (DEPTH, HEAD_DIM), 0.02),
        "attn_sinks": nrm(ks[10], (DEPTH, N_Q_HEADS), 0.5),
        "conv_w": nrm(ks[11], (DEPTH, CONV_WIDTH, CONV_CH), 0.5),
        "conv_b": nrm(ks[12], (DEPTH, CONV_CH), 0.01),
        "dt_bias": dt0 + jnp.log(-jnp.expm1(-dt0)),
        "a_log": jnp.log(jax.random.uniform(ks[14], (DEPTH, N_SSM_HEADS), f32, 1.0, 16.0)),
        "d_skip": 1.0 + nrm(ks[15], (DEPTH, N_SSM_HEADS), 0.1),
        "ssm_norm_w": 1.0 + nrm(ks[16], (DEPTH, SSM_WIDTH), 0.02),
        "w_out": nrm(ks[17], (DEPTH, MIX_WIDTH, D_MODEL), MIX_WIDTH ** -0.5),
        "norm_ffn_w": 1.0 + nrm(ks[18], (DEPTH, D_MODEL), 0.02),
        "w_up": nrm(ks[19], (DEPTH, D_MODEL, FFN_HIDDEN), D_MODEL ** -0.5),
        "w_down": nrm(ks[20], (DEPTH, FFN_HIDDEN, D_MODEL), FFN_HIDDEN ** -0.5),
    }


def reference(x_prompt, x_sample, cache_k, cache_v, state_conv, state_ssm,
              norm_mix_w, w_in, q_norm_w, k_norm_w, attn_sinks, conv_w, conv_b,
              dt_bias, a_log, d_skip, ssm_norm_w, w_out, norm_ffn_w, w_up, w_down):
    n_keep = cache_k.shape[2]
    t_p = x_prompt.shape[1]
    t_s = x_sample.shape[1]
    b_p = x_prompt.shape[0]
    pos_p = jnp.arange(t_p, dtype=jnp.int32)
    pos_s = PAST_LEN + jnp.arange(t_s, dtype=jnp.int32)
    hp, hs = x_prompt, x_sample
    k_p, v_p, c_p, s_p = [], [], [], []
    k_s, v_s, c_s, s_s = [], [], [], []
    for layer in range(DEPTH):
        lw = (norm_mix_w[layer], w_in[layer], q_norm_w[layer], k_norm_w[layer],
              attn_sinks[layer], conv_w[layer], conv_b[layer], dt_bias[layer],
              a_log[layer], d_skip[layer], ssm_norm_w[layer], w_out[layer],
              norm_ffn_w[layer], w_up[layer], w_down[layer])
        conv0 = jnp.zeros((b_p, CONV_WIDTH - 1, CONV_CH), x_prompt.dtype)
        ssm0 = jnp.zeros((b_p, N_SSM_HEADS, SSM_HEAD_DIM, D_STATE), jnp.float32)
        hp, nk, nv, nconv, nssm = trunk_layer(hp, pos_p, None, None, conv0, ssm0, n_keep, *lw)
        k_p.append(nk); v_p.append(nv); c_p.append(nconv); s_p.append(nssm)
        hs, nk, nv, nconv, nssm = trunk_layer(hs, pos_s, cache_k[layer], cache_v[layer],
                                              state_conv[layer], state_ssm[layer], n_keep, *lw)
        k_s.append(nk); v_s.append(nv); c_s.append(nconv); s_s.append(nssm)
    return (hp, hs, jnp.stack(k_p), jnp.stack(v_p), jnp.stack(c_p), jnp.stack(s_p),
            jnp.stack(k_s), jnp.stack(v_s), jnp.stack(c_s), jnp.stack(s_s))
```

```python
import functools

import jax
import jax.numpy as jnp
from jax import lax
from jax.experimental import pallas as pl
from jax.experimental.pallas import tpu as pltpu

F32 = jnp.float32
BF16 = jnp.bfloat16

D_MODEL = 4096
HEAD_DIM = 64
N_Q_HEADS = 32
N_KV_HEADS = 8
ATTN_W = N_Q_HEADS * HEAD_DIM
KV_W = N_KV_HEADS * HEAD_DIM
ROPE_DIM = 16
ROPE_HALF = ROPE_DIM // 2
ROPE_THETA = 500000.0
ATTN_SCALE = HEAD_DIM ** -0.5
WINDOW = 128
CHUNK = 64
SSM_W = 2048
N_SSM_HEADS = 32
SSM_HEAD_DIM = 64
N_GROUPS = 8
D_STATE = 128
BC_W = N_GROUPS * D_STATE
GROUP_W = SSM_W // N_GROUPS
CONV_W = 4
CONV_CH = SSM_W + 2 * BC_W
FFN_HIDDEN = 4 * D_MODEL
RMS_EPS = 1e-6
NEG_INF = -1e30
LANES = 128
Q_TILES = ATTN_W // LANES
KV_TILES = KV_W // LANES
PROJ_W = ATTN_W + 2 * SSM_W + 2 * BC_W + 2 * KV_W
COL_Q, COL_XS, COL_Z, COL_BC, COL_K, COL_V = 0, 2048, 4096, 6144, 8192, 8704
VMEM_LIMIT = 56 * 1024 * 1024


def _params(*sem):
    return pltpu.CompilerParams(dimension_semantics=sem, vmem_limit_bytes=VMEM_LIMIT)


def _dot(a, b):
    return jnp.dot(a, b, preferred_element_type=F32)


def _dot_nt(a, b):
    return lax.dot_general(a, b, (((1,), (1,)), ((), ())), preferred_element_type=F32)


def _split3(x):
    a = x.astype(BF16)
    r = x - a.astype(F32)
    b = r.astype(BF16)
    c = (r - b.astype(F32)).astype(BF16)
    return a, b, c


def _silu(x):
    return x / (1.0 + jnp.exp(-x))


NORM_ROWS = 64


def _rmsnorm_rows(x_ref, nw_ref, h_scr, copy_ref=None):
    def body(i, carry):
        r = pl.ds(pl.multiple_of(i * NORM_ROWS, NORM_ROWS), NORM_ROWS)
        x = x_ref[r, :]
        ms = jnp.mean(x * x, axis=-1, keepdims=True)
        h_scr[r, :] = (x * lax.rsqrt(ms + RMS_EPS) * nw_ref[...]).astype(BF16)
        if copy_ref is not None:
            copy_ref[r, :] = x
        return carry

    lax.fori_loop(0, x_ref.shape[0] // NORM_ROWS, body, 0)


def _inproj_kernel(x_ref, nw_ref, w_ref, wdt_ref, proj_ref, dt_ref, h_scr):
    @pl.when(pl.program_id(1) == 0)
    def _():
        _rmsnorm_rows(x_ref, nw_ref, h_scr)
        dt_ref[...] = _dot(h_scr[...], wdt_ref[...])

    proj_ref[...] = _dot(h_scr[...], w_ref[...]).astype(BF16)


def _inproj(x2d, norm_w, w_main, w_dt, tm, tn):
    t = x2d.shape[0]
    return pl.pallas_call(
        _inproj_kernel,
        grid=(t // tm, PROJ_W // tn),
        in_specs=[
            pl.BlockSpec((tm, D_MODEL), lambda i, j: (i, 0)),
            pl.BlockSpec((1, D_MODEL), lambda i, j: (0, 0)),
            pl.BlockSpec((D_MODEL, tn), lambda i, j: (0, j)),
            pl.BlockSpec((D_MODEL, LANES), lambda i, j: (0, 0)),
        ],
        out_specs=[
            pl.BlockSpec((tm, tn), lambda i, j: (i, j)),
            pl.BlockSpec((tm, LANES), lambda i, j: (i, 0)),
        ],
        out_shape=[
            jax.ShapeDtypeStruct((t, PROJ_W), BF16),
            jax.ShapeDtypeStruct((t, LANES), F32),
        ],
        scratch_shapes=[pltpu.VMEM((tm, D_MODEL), BF16)],
        compiler_params=_params("arbitrary", "arbitrary"),
        name="inproj",
    )(x2d, norm_w, w_main, w_dt)


def _half_ones():
    r = lax.broadcasted_iota(jnp.int32, (LANES, LANES), 0) // HEAD_DIM
    c = lax.broadcasted_iota(jnp.int32, (LANES, LANES), 1) // HEAD_DIM
    return (r == c).astype(BF16)


def _head_norm(x, w_tile):
    ss = _dot((x * x).astype(BF16), _half_ones())
    return x * lax.rsqrt(ss * (1.0 / HEAD_DIM) + RMS_EPS) * w_tile


def _rope(x, cos_t, sin_lo, sin_hi):
    return (x * cos_t + pltpu.roll(x, LANES - ROPE_HALF, 1) * sin_lo
            + pltpu.roll(x, ROPE_HALF, 1) * sin_hi)


def _lane_is_a(shape):
    return lax.broadcasted_iota(jnp.int32, shape, 1) < HEAD_DIM


def _kv_tiles(k_win, v_win):
    is_a = _lane_is_a((CHUNK, LANES))
    ones_a = jnp.where(is_a, 1.0, 0.0).astype(F32)
    ones_b = 1.0 - ones_a
    kks, vvs = [], []
    for h in range(N_KV_HEADS):
        j, odd = h // 2, h % 2
        kt = k_win[:, LANES * j:LANES * (j + 1)]
        vt = v_win[:, LANES * j:LANES * (j + 1)]
        keep = jnp.logical_not(is_a) if odd else is_a
        k_own = jnp.where(keep, kt, 0.0)
        v_own = jnp.where(keep, vt, 0.0)
        k_sw = pltpu.roll(k_own, HEAD_DIM, 1)
        v_sw = pltpu.roll(v_own, HEAD_DIM, 1)
        k_a, k_b = (k_sw, k_own) if odd else (k_own, k_sw)
        v_a, v_b = (v_sw, v_own) if odd else (v_own, v_sw)
        kks.append(jnp.concatenate([k_a, k_b], axis=0).astype(BF16))
        vv = jnp.concatenate([jnp.concatenate([v_a, ones_a], axis=1),
                              jnp.concatenate([v_b, ones_b], axis=1)], axis=0)
        vvs.append(vv.astype(BF16))
    return kks, vvs


def _prep_q(q_ref, qw_ref, cos_ref, slo_ref, shi_ref, tq):
    q2 = jnp.concatenate([q_ref[:, LANES * i:LANES * (i + 1)].astype(F32) for i in range(Q_TILES)], axis=0)
    q2 = _head_norm(q2, qw_ref[...])
    cos_t, slo, shi = cos_ref[...], slo_ref[...], shi_ref[...]
    return [_rope(q2[tq * i:tq * (i + 1)], cos_t, slo, shi).astype(BF16) for i in range(Q_TILES)]


def _attend(q_tiles, kk_of, vv_of, valid_of, sink_ref, o_ref, tq):
    is_a = _lane_is_a((2 * tq, LANES))
    for h in range(N_KV_HEADS):
        qh = jnp.concatenate([q_tiles[2 * h], q_tiles[2 * h + 1]], axis=0)
        s = _dot_nt(qh, kk_of(h)) * ATTN_SCALE
        sw = [jnp.where(valid_of(w), s[:, LANES * w:LANES * (w + 1)], NEG_INF) for w in range(3)]
        mt = jnp.maximum(jnp.maximum(sw[0], sw[1]), sw[2])
        m_a = jnp.max(jnp.where(is_a, mt, NEG_INF), axis=-1, keepdims=True)
        m_b = jnp.max(jnp.where(is_a, NEG_INF, mt), axis=-1, keepdims=True)
        sink = jnp.concatenate([jnp.broadcast_to(sink_ref[2 * h:2 * h + 1, :], (tq, LANES)),
                                jnp.broadcast_to(sink_ref[2 * h + 1:2 * h + 2, :], (tq, LANES))], axis=0)
        m = jnp.maximum(jnp.where(is_a, m_a, m_b), sink)
        e = jnp.concatenate([jnp.exp(x - m) for x in sw], axis=1).astype(BF16)
        oa = _dot(e, vv_of(h))
        o = oa[:, :LANES] / (oa[:, LANES:] + jnp.exp(sink - m))
        o_ref[:, LANES * 2 * h:LANES * (2 * h + 1)] = o[:tq].astype(o_ref.dtype)
        o_ref[:, LANES * (2 * h + 1):LANES * (2 * h + 2)] = o[tq:].astype(o_ref.dtype)


def _prep_k(k_raw, kw_ref, cos_ref, slo_ref, shi_ref, rows):
    k2 = jnp.concatenate([k_raw[:, LANES * j:LANES * (j + 1)].astype(F32) for j in range(KV_TILES)], axis=0)
    k2 = _head_norm(k2, kw_ref[...])
    cos_t, slo, shi = cos_ref[...], slo_ref[...], shi_ref[...]
    return jnp.concatenate([_rope(k2[rows * j:rows * (j + 1)], cos_t, slo, shi) for j in range(KV_TILES)], axis=1)


def _attn_prompt_kernel(q_ref, k_ref, v_ref, cos_ref, slo_ref, shi_ref, qw_ref, kw_ref, sink_ref,
                        o_ref, nk_ref, nv_ref, kk_scr, vv_scr):
    c = pl.program_id(1)

    @pl.when(c == 0)
    def _():
        kk_scr[...] = jnp.zeros_like(kk_scr)
        vv_scr[...] = jnp.zeros_like(vv_scr)

    k_fin = _prep_k(k_ref[...], kw_ref, cos_ref, slo_ref, shi_ref, CHUNK)
    v_cur = v_ref[...].astype(F32)
    nk_ref[0] = k_fin
    nv_ref[0] = v_cur
    kks, vvs = _kv_tiles(k_fin, v_cur)
    slot = c % 3
    for h in range(N_KV_HEADS):
        kk_scr[slot, h] = kks[h]
        vv_scr[slot, h] = vvs[h]

    q_tiles = _prep_q(q_ref, qw_ref, cos_ref, slo_ref, shi_ref, CHUNK)
    _attend(q_tiles,
            lambda h: jnp.concatenate([kk_scr[w, h] for w in range(3)], axis=0),
            lambda h: jnp.concatenate([vv_scr[w, h] for w in range(3)], axis=0),
            lambda w: (c - w + 3) % 3 <= c,
            sink_ref, o_ref, CHUNK)


def _attn_prompt(proj, rope_tabs, qw_tile, kw_tile, sink_tab, batch, t):
    nc = t // CHUNK
    keep = WINDOW // CHUNK
    cos_t, slo, shi = rope_tabs
    row = lambda b, c: b * nc + c
    tab_spec = pl.BlockSpec((CHUNK, LANES), lambda b, c: (c, 0))
    const = lambda shape: pl.BlockSpec(shape, lambda b, c: (0,) * len(shape))
    keep_idx = lambda b, c: (b, jnp.maximum(c - (nc - keep), 0), 0)
    return pl.pallas_call(
        _attn_prompt_kernel,
        grid=(batch, nc),
        in_specs=[
            pl.BlockSpec((CHUNK, ATTN_W), lambda b, c: (row(b, c), COL_Q // ATTN_W)),
            pl.BlockSpec((CHUNK, KV_W), lambda b, c: (row(b, c), COL_K // KV_W)),
            pl.BlockSpec((CHUNK, KV_W), lambda b, c: (row(b, c), COL_V // KV_W)),
            tab_spec, tab_spec, tab_spec,
            const((1, LANES)), const((1, LANES)), const((Q_TILES, LANES)),
        ],
        out_specs=[
            pl.BlockSpec((CHUNK, ATTN_W), lambda b, c: (row(b, c), 0)),
            pl.BlockSpec((1, CHUNK, KV_W), keep_idx),
            pl.BlockSpec((1, CHUNK, KV_W), keep_idx),
        ],
        out_shape=[
            jax.ShapeDtypeStruct((batch * t, ATTN_W), BF16),
            jax.ShapeDtypeStruct((batch, WINDOW, KV_W), F32),
            jax.ShapeDtypeStruct((batch, WINDOW, KV_W), F32),
        ],
        scratch_shapes=[
            pltpu.VMEM((3, N_KV_HEADS, 2 * CHUNK, LANES), BF16),
            pltpu.VMEM((3, N_KV_HEADS, 2 * CHUNK, 2 * LANES), BF16),
        ],
        compiler_params=_params("arbitrary", "arbitrary"),
        name="attn_prompt",
    )(proj, proj, proj, cos_t, slo, shi, qw_tile, kw_tile, sink_tab)


def _attn_sample_kernel(q_ref, k_ref, v_ref, ck_ref, cv_ref, cos_ref, slo_ref, shi_ref, qw_ref, kw_ref,
                        sink_ref, o_ref, nk_ref, nv_ref, *, tq):
    k_fin = _prep_k(k_ref[...], kw_ref, cos_ref, slo_ref, shi_ref, tq)
    v_cur = v_ref[...].astype(F32)
    nk_ref[0] = k_fin
    nv_ref[0] = v_cur
    pad = jnp.zeros((CHUNK - tq, KV_W), F32)
    wins = [(ck_ref[0, :CHUNK, :], cv_ref[0, :CHUNK, :]),
            (ck_ref[0, CHUNK:, :], cv_ref[0, CHUNK:, :]),
            (jnp.concatenate([k_fin, pad], axis=0), jnp.concatenate([v_cur, pad], axis=0))]
    tiles = [_kv_tiles(kw, vw) for kw, vw in wins]
    q_tiles = _prep_q(q_ref, qw_ref, cos_ref, slo_ref, shi_ref, tq)
    new_valid = lax.broadcasted_iota(jnp.int32, (2 * tq, LANES), 1) % HEAD_DIM < tq
    _attend(q_tiles,
            lambda h: jnp.concatenate([tiles[w][0][h] for w in range(3)], axis=0),
            lambda h: jnp.concatenate([tiles[w][1][h] for w in range(3)], axis=0),
            lambda w: new_valid if w == 2 else True,
            sink_ref, o_ref, tq)


def _attn_sample(proj, cache_k, cache_v, rope_tabs, qw_tile, kw_tile, sink_tab, batch, tq):
    cos_t, slo, shi = rope_tabs
    tab_spec = pl.BlockSpec((tq, LANES), lambda b: (0, 0))
    const = lambda shape: pl.BlockSpec(shape, lambda b: (0,) * len(shape))
    cache_spec = pl.BlockSpec((1, WINDOW, KV_W), lambda b: (b, 0, 0))
    new_spec = pl.BlockSpec((1, tq, KV_W), lambda b: (b, 0, 0))
    return pl.pallas_call(
        functools.partial(_attn_sample_kernel, tq=tq),
        grid=(batch,),
        in_specs=[
            pl.BlockSpec((tq, ATTN_W), lambda b: (b, COL_Q // ATTN_W)),
            pl.BlockSpec((tq, KV_W), lambda b: (b, COL_K // KV_W)),
            pl.BlockSpec((tq, KV_W), lambda b: (b, COL_V // KV_W)),
            cache_spec, cache_spec,
            tab_spec, tab_spec, tab_spec,
            const((1, LANES)), const((1, LANES)), const((Q_TILES, LANES)),
        ],
        out_specs=[pl.BlockSpec((tq, ATTN_W), lambda b: (b, 0)), new_spec, new_spec],
        out_shape=[
            jax.ShapeDtypeStruct((batch * tq, ATTN_W), BF16),
            jax.ShapeDtypeStruct((batch, tq, KV_W), F32),
            jax.ShapeDtypeStruct((batch, tq, KV_W), F32),
        ],
        compiler_params=_params("arbitrary"),
        name="attn_sample",
    )(proj, proj, proj, cache_k, cache_v, cos_t, slo, shi, qw_tile, kw_tile, sink_tab)


def _pad_rows(a, rows):
    if a.shape[0] == rows:
        return a
    return jnp.concatenate([a, jnp.zeros((rows - a.shape[0], a.shape[1]), a.dtype)], axis=0)


def _ssd_kernel(xs_ref, z_ref, bc_ref, dt_ref, cst_ref, st0_ref, cw_ref, cb_ref, dtb_ref, alog_ref,
                dskip_ref, nw_ref, sel_ref, y_ref, ncv_ref, nst_ref, ext_scr, st_scr, *, L):
    c = pl.program_id(1)
    nc = pl.num_programs(1)
    TAIL = 8

    @pl.when(c == 0)
    def _():
        ext_scr[0:TAIL, :] = jnp.zeros((TAIL, CONV_CH), F32)
        ext_scr[TAIL - (CONV_W - 1):TAIL, :] = cst_ref[0]
        st_scr[...] = st0_ref[0].T

    ext_scr[TAIL:TAIL + L, 0:SSM_W] = xs_ref[...].astype(F32)
    ext_scr[TAIL:TAIL + L, SSM_W:CONV_CH] = bc_ref[...].astype(F32)
    conv = cb_ref[...] + ext_scr[pl.ds(TAIL - 3, L), :] * cw_ref[0:1, :]
    for i in range(1, CONV_W):
        conv = conv + ext_scr[pl.ds(TAIL - 3 + i, L), :] * cw_ref[i:i + 1, :]
    conv = _silu(conv)
    xs = conv[:, :SSM_W]

    @pl.when(c == nc - 1)
    def _():
        ncv_ref[0] = ext_scr[pl.ds(TAIL + L - (CONV_W - 1), CONV_W - 1), :]

    ext_scr[0:TAIL, :] = ext_scr[L:L + TAIL, :]

    x_dt = dt_ref[...] + dtb_ref[...]
    dt = jnp.maximum(x_dt, 0.0) + jnp.log1p(jnp.exp(-jnp.abs(x_dt)))
    ad = dt * (-jnp.exp(alog_ref[...]))
    li = lax.broadcasted_iota(jnp.int32, (L, L), 0)
    si = lax.broadcasted_iota(jnp.int32, (L, L), 1)
    tril = (si <= li).astype(BF16)
    a_cum = _dot(jnp.concatenate([tril] * 3, axis=1), jnp.concatenate(_split3(ad), axis=0))
    sel = sel_ref[...]
    ex = _dot(jnp.concatenate(_split3(a_cum) + _split3(dt), axis=0), sel)
    col = ex[0:L] + ex[L:2 * L] + ex[2 * L:3 * L]
    dt_all = ex[3 * L:4 * L] + ex[4 * L:5 * L] + ex[5 * L:6 * L]
    lane_s = lax.broadcasted_iota(jnp.int32, (L, SSM_W), 1) % SSM_HEAD_DIM
    row_l = lax.broadcasted_iota(jnp.int32, (L, SSM_W), 0)
    diag = jnp.where(lane_s == row_l, col, 0.0)
    row = _dot(jnp.ones((L, 3 * L), BF16), jnp.concatenate(_split3(diag), axis=0))
    lmat = jnp.exp(jnp.where(lane_s <= row_l, col - row, NEG_INF))
    a_last = col[L - 1:L, :]
    exp_a = jnp.exp(col)
    decay = jnp.exp(a_last - col)
    exp_last = jnp.exp(a_last)

    xd = xs * dt_all
    xdd = xd * decay
    bi = lax.broadcasted_iota(jnp.int32, (GROUP_W, GROUP_W), 0) // SSM_HEAD_DIM
    bj = lax.broadcasted_iota(jnp.int32, (GROUP_W, GROUP_W), 1) // SSM_HEAD_DIM
    blockdiag = bi == bj

    for g in range(N_GROUPS):
        gs = slice(GROUP_W * g, GROUP_W * (g + 1))
        b_g = conv[:, SSM_W + D_STATE * g:SSM_W + D_STATE * (g + 1)]
        c_g = conv[:, SSM_W + BC_W + D_STATE * g:SSM_W + BC_W + D_STATE * (g + 1)].astype(BF16)
        b_pad = _pad_rows(b_g, CHUNK)
        cb = _dot_nt(c_g, jnp.concatenate([b_pad] * 4, axis=0).astype(BF16))
        m_g = (cb * lmat[:, gs]).astype(BF16)
        xd_bd = jnp.where(blockdiag, jnp.concatenate([_pad_rows(xd[:, gs], CHUNK)] * 4, axis=0), 0.0)
        y_diag = _dot(m_g, xd_bd.astype(BF16))
        st_g = st_scr[:, gs]
        y_off = _dot(c_g, st_g.astype(BF16)) * exp_a[:, gs]
        new_st = _dot(b_pad.T.astype(BF16), _pad_rows(xdd[:, gs], CHUNK).astype(BF16))
        st_scr[:, gs] = st_g * exp_last[:, gs] + new_st

        y = y_diag + y_off + xs[:, gs] * dskip_ref[:, gs]
        gated = y * _silu(z_ref[:, gs].astype(F32))
        ms = jnp.mean(gated * gated, axis=-1, keepdims=True)
        y_ref[:, gs] = (gated * lax.rsqrt(ms + RMS_EPS) * nw_ref[:, gs]).astype(y_ref.dtype)

    @pl.when(c == nc - 1)
    def _():
        nst_ref[0] = st_scr[...].T


def _ssd(proj, dt_raw, conv_state, ssm_state, conv_w, conv_b, dtb, alog, dskip_all, ssm_nw, sel, batch, t, L):
    nc = t // L
    row = lambda b, c: b * nc + c
    const = lambda shape: pl.BlockSpec(shape, lambda b, c: (0,) * len(shape))
    per_b = lambda shape: pl.BlockSpec(shape, lambda b, c: (b,) + (0,) * (len(shape) - 1))
    hp = N_SSM_HEADS * SSM_HEAD_DIM
    return pl.pallas_call(
        functools.partial(_ssd_kernel, L=L),
        grid=(batch, nc),
        in_specs=[
            pl.BlockSpec((L, SSM_W), lambda b, c: (row(b, c), COL_XS // SSM_W)),
            pl.BlockSpec((L, SSM_W), lambda b, c: (row(b, c), COL_Z // SSM_W)),
            pl.BlockSpec((L, 2 * BC_W), lambda b, c: (row(b, c), COL_BC // (2 * BC_W))),
            pl.BlockSpec((L, LANES), lambda b, c: (row(b, c), 0)),
            per_b((1, CONV_W - 1, CONV_CH)),
            per_b((1, hp, D_STATE)),
            const((CONV_W, CONV_CH)), const((1, CONV_CH)), const((1, LANES)), const((1, LANES)),
            const((1, SSM_W)), const((1, SSM_W)), const((LANES, SSM_W)),
        ],
        out_specs=[
            pl.BlockSpec((L, SSM_W), lambda b, c: (row(b, c), 0)),
            per_b((1, CONV_W - 1, CONV_CH)),
            per_b((1, hp, D_STATE)),
        ],
        out_shape=[
            jax.ShapeDtypeStruct((batch * t, SSM_W), BF16),
            jax.ShapeDtypeStruct((batch, CONV_W - 1, CONV_CH), F32),
            jax.ShapeDtypeStruct((batch, hp, D_STATE), F32),
        ],
        scratch_shapes=[
            pltpu.VMEM((8 + L, CONV_CH), F32),
            pltpu.VMEM((D_STATE, hp), F32),
        ],
        compiler_params=_params("arbitrary", "arbitrary"),
        name="ssd",
    )(proj, proj, proj, dt_raw, conv_state, ssm_state, conv_w, conv_b, dtb, alog, dskip_all, ssm_nw, sel)


def _outproj_kernel(a_ref, y_ref, wa_ref, wy_ref, x_ref, o_ref):
    o_ref[...] = x_ref[...] + _dot(a_ref[...], wa_ref[...]) + _dot(y_ref[...], wy_ref[...])


def _outproj(attn, y, w_out, x2d, tm, tn):
    t = x2d.shape[0]
    half = D_MODEL // 2
    return pl.pallas_call(
        _outproj_kernel,
        grid=(t // tm, D_MODEL // tn),
        in_specs=[
            pl.BlockSpec((tm, half), lambda i, j: (i, 0)),
            pl.BlockSpec((tm, half), lambda i, j: (i, 0)),
            pl.BlockSpec((half, tn), lambda i, j: (0, j)),
            pl.BlockSpec((half, tn), lambda i, j: (1, j)),
            pl.BlockSpec((tm, tn), lambda i, j: (i, j)),
        ],
        out_specs=pl.BlockSpec((tm, tn), lambda i, j: (i, j)),
        out_shape=jax.ShapeDtypeStruct((t, D_MODEL), F32),
        compiler_params=_params("arbitrary", "arbitrary"),
        name="outproj",
    )(attn, y, w_out, w_out, x2d)


def _ffn_kernel(x_ref, nw_ref, wu_ref, wd_ref, o_ref, h_scr, *, n_split):
    @pl.when(pl.program_id(1) == 0)
    def _():
        _rmsnorm_rows(x_ref, nw_ref, h_scr, copy_ref=o_ref)

    u = jnp.maximum(_dot(h_scr[...], wu_ref[...]), 0.0)
    u = (u * u).astype(BF16)
    wn = D_MODEL // n_split
    for n in range(n_split):
        o_ref[:, wn * n:wn * (n + 1)] += _dot(u, wd_ref[:, wn * n:wn * (n + 1)])


def _ffn(x2d, norm_w, w_up, w_down, tm, tf):
    t = x2d.shape[0]
    return pl.pallas_call(
        functools.partial(_ffn_kernel, n_split=4),
        grid=(t // tm, FFN_HIDDEN // tf),
        in_specs=[
            pl.BlockSpec((tm, D_MODEL), lambda i, f: (i, 0), pipeline_mode=pl.Buffered(1)),
            pl.BlockSpec((1, D_MODEL), lambda i, f: (0, 0)),
            pl.BlockSpec((D_MODEL, tf), lambda i, f: (0, f)),
            pl.BlockSpec((tf, D_MODEL), lambda i, f: (f, 0)),
        ],
        out_specs=pl.BlockSpec((tm, D_MODEL), lambda i, f: (i, 0)),
        out_shape=jax.ShapeDtypeStruct((t, D_MODEL), F32),
        scratch_shapes=[pltpu.VMEM((tm, D_MODEL), BF16)],
        compiler_params=_params("arbitrary", "arbitrary"),
        name="ffn",
    )(x2d, norm_w, w_up, w_down)


def _rope_tables(pos):
    inv_freq = ROPE_THETA ** (-jnp.arange(ROPE_HALF, dtype=F32) / ROPE_HALF)
    ang = pos.astype(F32)[:, None] * inv_freq[None, :]
    cos, sin = jnp.cos(ang), jnp.sin(ang)
    n = pos.shape[0]
    rest = HEAD_DIM - ROPE_DIM
    head = lambda lo, hi, fill: jnp.concatenate([lo, hi, jnp.full((n, rest), fill, F32)], axis=1)
    zeros = jnp.zeros_like(sin)
    two = lambda a: jnp.concatenate([a, a], axis=1)
    return two(head(cos, cos, 1.0)), two(head(-sin, zeros, 0.0)), two(head(zeros, sin, 0.0))


def _layer_params(norm_mix_w, w_in, q_norm_w, k_norm_w, attn_sinks, conv_w, conv_b, dt_bias, a_log, d_skip,
                  ssm_norm_w, w_out, norm_ffn_w, w_up, w_down):
    q_end, k_end, v_end = ATTN_W, ATTN_W + KV_W, ATTN_W + 2 * KV_W
    xs_end, z_end = v_end + SSM_W, v_end + 2 * SSM_W
    bc_end = z_end + 2 * BC_W
    w_main = jnp.concatenate([w_in[:, :q_end], w_in[:, v_end:bc_end], w_in[:, q_end:v_end]], axis=1).astype(BF16)
    w_dt = jnp.pad(w_in[:, bc_end:], ((0, 0), (0, LANES - N_SSM_HEADS))).astype(BF16)
    pad_h = lambda a: jnp.pad(a.astype(F32), (0, LANES - N_SSM_HEADS))[None, :]
    sel = (jnp.arange(LANES)[:, None] == (jnp.arange(SSM_W) // SSM_HEAD_DIM)[None, :]).astype(BF16)
    return dict(
        norm_mix=norm_mix_w.astype(F32)[None, :], w_main=w_main, w_dt=w_dt,
        qw=jnp.tile(q_norm_w.astype(F32), 2)[None, :], kw=jnp.tile(k_norm_w.astype(F32), 2)[None, :],
        sinks=jnp.repeat(attn_sinks.astype(F32), HEAD_DIM).reshape(Q_TILES, LANES),
        conv_w=conv_w.astype(F32), conv_b=conv_b.astype(F32)[None, :],
        dtb=pad_h(dt_bias), alog=pad_h(a_log),
        dskip=jnp.repeat(d_skip.astype(F32), SSM_HEAD_DIM)[None, :],
        ssm_nw=ssm_norm_w.astype(F32)[None, :], sel=sel,
        w_out=w_out.astype(BF16), norm_ffn=norm_ffn_w.astype(F32)[None, :],
        w_up=w_up.astype(BF16), w_down=w_down.astype(BF16),
    )


def _stream(x, pos, cache, conv_state, ssm_state, p, tm):
    batch, t, _ = x.shape
    x2d = x.reshape(batch * t, D_MODEL)
    proj, dt_raw = _inproj(x2d, p["norm_mix"], p["w_main"], p["w_dt"], tm, 1024)
    tabs = _rope_tables(pos)
    if cache is None:
        attn, new_k, new_v = _attn_prompt(proj, tabs, p["qw"], p["kw"], p["sinks"], batch, t)
        L = CHUNK
    else:
        past_k, past_v = cache
        attn, k_new, v_new = _attn_sample(proj, past_k.reshape(batch, WINDOW, KV_W),
                                          past_v.reshape(batch, WINDOW, KV_W), tabs, p["qw"], p["kw"],
                                          p["sinks"], batch, t)
        new_k = jnp.concatenate([past_k.reshape(batch, WINDOW, KV_W)[:, t:], k_new], axis=1)
        new_v = jnp.concatenate([past_v.reshape(batch, WINDOW, KV_W)[:, t:], v_new], axis=1)
        L = min(CHUNK, t)
    hp = N_SSM_HEADS * SSM_HEAD_DIM
    y, new_conv, new_ssm = _ssd(proj, dt_raw, conv_state, ssm_state.reshape(batch, hp, D_STATE),
                                p["conv_w"], p["conv_b"], p["dtb"], p["alog"], p["dskip"], p["ssm_nw"],
                                p["sel"], batch, t, L)
    x1 = _outproj(attn, y, p["w_out"], x2d, tm, 1024)
    out = _ffn(x1, p["norm_ffn"], p["w_up"], p["w_down"], tm, 512)
    return (out.reshape(batch, t, D_MODEL),
            new_k.reshape(batch, WINDOW, N_KV_HEADS, HEAD_DIM), new_v.reshape(batch, WINDOW, N_KV_HEADS, HEAD_DIM),
            new_conv, new_ssm.reshape(batch, N_SSM_HEADS, SSM_HEAD_DIM, D_STATE))


def kernel(x_prompt, x_sample, cache_k, cache_v, state_conv, state_ssm, norm_mix_w, w_in, q_norm_w, k_norm_w,
           attn_sinks, conv_w, conv_b, dt_bias, a_log, d_skip, ssm_norm_w, w_out, norm_ffn_w, w_up, w_down):
    depth = w_in.shape[0]
    b_p, t_p, _ = x_prompt.shape
    b_s, t_s, _ = x_sample.shape
    past_len = 1024
    assert cache_k.shape[2] == WINDOW and t_p % CHUNK == 0 and t_p >= WINDOW and t_s <= CHUNK
    pos_p = jnp.arange(t_p, dtype=jnp.int32)
    pos_s = past_len + jnp.arange(t_s, dtype=jnp.int32)
    hp, hs = x_prompt, x_sample
    outs_p, outs_s = [], []
    for layer in range(depth):
        p = _layer_params(norm_mix_w[layer], w_in[layer], q_norm_w[layer], k_norm_w[layer], attn_sinks[layer],
                          conv_w[layer], conv_b[layer], dt_bias[layer], a_log[layer], d_skip[layer],
                          ssm_norm_w[layer], w_out[layer], norm_ffn_w[layer], w_up[layer], w_down[layer])
        conv0 = jnp.zeros((b_p, CONV_W - 1, CONV_CH), F32)
        ssm0 = jnp.zeros((b_p, N_SSM_HEADS, SSM_HEAD_DIM, D_STATE), F32)
        hp, *rest_p = _stream(hp, pos_p, None, conv0, ssm0, p, 512)
        hs, *rest_s = _stream(hs, pos_s, (cache_k[layer], cache_v[layer]), state_conv[layer],
                              state_ssm[layer], p, 512)
        outs_p.append(rest_p)
        outs_s.append(rest_s)
    stack = lambda outs, i: jnp.stack([o[i] for o in outs])
    return (hp, hs, stack(outs_p, 0), stack(outs_p, 1), stack(outs_p, 2), stack(outs_p, 3),
            stack(outs_s, 0), stack(outs_s, 1), stack(outs_s, 2), stack(outs_s, 3))
```

```python
import functools

import jax
import jax.numpy as jnp
from jax import lax
from jax.experimental import pallas as pl
from jax.experimental.pallas import tpu as pltpu

F32 = jnp.float32
BF16 = jnp.bfloat16

D_MODEL = 4096
HEAD_DIM = 64
N_Q_HEADS = 32
N_KV_HEADS = 8
ATTN_W = N_Q_HEADS * HEAD_DIM
KV_W = N_KV_HEADS * HEAD_DIM
ROPE_DIM = 16
ROPE_HALF = ROPE_DIM // 2
ROPE_THETA = 500000.0
ATTN_SCALE = HEAD_DIM ** -0.5
WINDOW = 128
CHUNK = 64
PAST_LEN = 1024
SSM_W = 2048
N_SSM_HEADS = 32
SSM_HEAD_DIM = 64
N_GROUPS = 8
D_STATE = 128
BC_W = N_GROUPS * D_STATE
GROUP_W = SSM_W // N_GROUPS
CONV_W = 4
CONV_CH = SSM_W + 2 * BC_W
FFN_HIDDEN = 4 * D_MODEL
RMS_EPS = 1e-6
NEG_INF = -1e30
LANES = 128
BF16_ROWS = 16
Q_TILES = ATTN_W // LANES
KV_TILES = KV_W // LANES
PROJ_W = ATTN_W + 2 * KV_W + 2 * SSM_W + 2 * BC_W
COL_Q, COL_K, COL_V, COL_XS, COL_Z, COL_B, COL_C = 0, 2048, 2560, 3072, 5120, 7168, 8192
VMEM_LIMIT = 56 * 1024 * 1024
STREAMS_PER_STEP = 2


def _params(*sem):
    return pltpu.CompilerParams(dimension_semantics=sem, vmem_limit_bytes=VMEM_LIMIT)


def _dot(a, b):
    return jnp.dot(a, b, preferred_element_type=F32)


def _dot_nt(a, b):
    return lax.dot_general(a, b, (((1,), (1,)), ((), ())), preferred_element_type=F32)


def _split3(x):
    a = x.astype(BF16)
    r = x - a.astype(F32)
    b = r.astype(BF16)
    c = (r - b.astype(F32)).astype(BF16)
    return a, b, c


def _silu(x):
    return x / (1.0 + jnp.exp(-x))


NORM_ROWS = 64


def _rmsnorm_rows(x_ref, nw_ref, h_scr, copy_ref=None):
    def body(i, carry):
        r = pl.ds(pl.multiple_of(i * NORM_ROWS, NORM_ROWS), NORM_ROWS)
        x = x_ref[r, :]
        ms = jnp.mean(x * x, axis=-1, keepdims=True)
        h_scr[r, :] = (x * lax.rsqrt(ms + RMS_EPS) * nw_ref[...]).astype(BF16)
        if copy_ref is not None:
            copy_ref[r, :] = x
        return carry

    lax.fori_loop(0, x_ref.shape[0] // NORM_ROWS, body, 0)


def _inproj_kernel(x_ref, nw_ref, w_ref, wdt_ref, proj_ref, dt_ref, h_scr):
    @pl.when(pl.program_id(1) == 0)
    def _():
        _rmsnorm_rows(x_ref, nw_ref, h_scr)
        real = lax.broadcasted_iota(jnp.int32, wdt_ref.shape, 1) < N_SSM_HEADS
        dt_ref[...] = _dot(h_scr[...], jnp.where(real, wdt_ref[...], jnp.zeros((), BF16)))

    proj_ref[...] = _dot(h_scr[...], w_ref[...]).astype(BF16)


def _inproj(x2d, norm_w, w_in, tm, tn):
    t = x2d.shape[0]
    return pl.pallas_call(
        _inproj_kernel,
        grid=(t // tm, PROJ_W // tn),
        in_specs=[
            pl.BlockSpec((tm, D_MODEL), lambda i, j: (i, 0)),
            pl.BlockSpec((1, D_MODEL), lambda i, j: (0, 0)),
            pl.BlockSpec((D_MODEL, tn), lambda i, j: (0, j)),
            pl.BlockSpec((D_MODEL, LANES), lambda i, j: (0, PROJ_W // LANES)),
        ],
        out_specs=[
            pl.BlockSpec((tm, tn), lambda i, j: (i, j)),
            pl.BlockSpec((tm, LANES), lambda i, j: (i, 0)),
        ],
        out_shape=[
            jax.ShapeDtypeStruct((t, PROJ_W), BF16),
            jax.ShapeDtypeStruct((t, LANES), F32),
        ],
        scratch_shapes=[pltpu.VMEM((tm, D_MODEL), BF16)],
        compiler_params=_params("arbitrary", "arbitrary"),
        name="inproj",
    )(x2d, norm_w, w_in, w_in)


def _half_ones():
    r = lax.broadcasted_iota(jnp.int32, (LANES, LANES), 0) // HEAD_DIM
    c = lax.broadcasted_iota(jnp.int32, (LANES, LANES), 1) // HEAD_DIM
    return (r == c).astype(BF16)


def _head_norm(x, w_tile):
    ss = _dot((x * x).astype(BF16), _half_ones())
    return x * lax.rsqrt(ss * (1.0 / HEAD_DIM) + RMS_EPS) * w_tile


def _rope(x, cos_t, sin_lo, sin_hi):
    return (x * cos_t + pltpu.roll(x, LANES - ROPE_HALF, 1) * sin_lo
            + pltpu.roll(x, ROPE_HALF, 1) * sin_hi)


def _lane_is_a(shape):
    return lax.broadcasted_iota(jnp.int32, shape, 1) < HEAD_DIM


def _kv_tiles(k_win, v_win):
    is_a = _lane_is_a((CHUNK, LANES))
    ones_a = jnp.where(is_a, 1.0, 0.0).astype(F32)
    ones_b = 1.0 - ones_a
    kks, vvs = [], []
    for h in range(N_KV_HEADS):
        j, odd = h // 2, h % 2
        kt = k_win[:, LANES * j:LANES * (j + 1)]
        vt = v_win[:, LANES * j:LANES * (j + 1)]
        keep = jnp.logical_not(is_a) if odd else is_a
        k_own = jnp.where(keep, kt, 0.0)
        v_own = jnp.where(keep, vt, 0.0)
        k_sw = pltpu.roll(k_own, HEAD_DIM, 1)
        v_sw = pltpu.roll(v_own, HEAD_DIM, 1)
        k_a, k_b = (k_sw, k_own) if odd else (k_own, k_sw)
        v_a, v_b = (v_sw, v_own) if odd else (v_own, v_sw)
        kks.append(jnp.concatenate([k_a, k_b], axis=0).astype(BF16))
        vv = jnp.concatenate([jnp.concatenate([v_a, ones_a], axis=1),
                              jnp.concatenate([v_b, ones_b], axis=1)], axis=0)
        vvs.append(vv.astype(BF16))
    return kks, vvs


def _prep_q(q_ref, qw_ref, cos_ref, slo_ref, shi_ref, tq):
    q2 = jnp.concatenate([q_ref[:, LANES * i:LANES * (i + 1)].astype(F32) for i in range(Q_TILES)], axis=0)
    q2 = _head_norm(q2, qw_ref[...])
    cos_t, slo, shi = cos_ref[...], slo_ref[...], shi_ref[...]
    return [(_rope(q2[tq * i:tq * (i + 1)], cos_t, slo, shi) * ATTN_SCALE).astype(BF16) for i in range(Q_TILES)]


def _attend(q_tiles, kk_of, vv_of, valid_of, sink_ref, o_ref, tq):
    is_a = _lane_is_a((2 * tq, LANES))
    for h in range(N_KV_HEADS):
        qh = jnp.concatenate([q_tiles[2 * h], q_tiles[2 * h + 1]], axis=0)
        s = _dot_nt(qh, kk_of(h))
        sw = [jnp.where(valid_of(w), s[:, LANES * w:LANES * (w + 1)], NEG_INF) for w in range(3)]
        mt = jnp.maximum(jnp.maximum(sw[0], sw[1]), sw[2])
        m_a = jnp.max(jnp.where(is_a, mt, NEG_INF), axis=-1, keepdims=True)
        m_b = jnp.max(jnp.where(is_a, NEG_INF, mt), axis=-1, keepdims=True)
        sink = jnp.concatenate([jnp.broadcast_to(sink_ref[2 * h:2 * h + 1, :], (tq, LANES)),
                                jnp.broadcast_to(sink_ref[2 * h + 1:2 * h + 2, :], (tq, LANES))], axis=0)
        m = jnp.maximum(jnp.where(is_a, m_a, m_b), sink)
        e = jnp.concatenate([jnp.exp(x - m) for x in sw], axis=1).astype(BF16)
        oa = _dot(e, vv_of(h))
        o = oa[:, :LANES] / (oa[:, LANES:] + jnp.exp(sink - m))
        o_ref[:, LANES * 2 * h:LANES * (2 * h + 1)] = o[:tq].astype(o_ref.dtype)
        o_ref[:, LANES * (2 * h + 1):LANES * (2 * h + 2)] = o[tq:].astype(o_ref.dtype)


def _prep_k(k_raw, kw_ref, cos_ref, slo_ref, shi_ref, rows):
    k2 = jnp.concatenate([k_raw[:, LANES * j:LANES * (j + 1)].astype(F32) for j in range(KV_TILES)], axis=0)
    k2 = _head_norm(k2, kw_ref[...])
    cos_t, slo, shi = cos_ref[...], slo_ref[...], shi_ref[...]
    return jnp.concatenate([_rope(k2[rows * j:rows * (j + 1)], cos_t, slo, shi) for j in range(KV_TILES)], axis=1)


def _attn_prompt_kernel(q_ref, k_ref, v_ref, cos_ref, slo_ref, shi_ref, qw_ref, kw_ref, sink_ref,
                        o_ref, nk_ref, nv_ref, kk_scr, vv_scr):
    c = pl.program_id(1)

    @pl.when(c == 0)
    def _():
        kk_scr[...] = jnp.zeros_like(kk_scr)
        vv_scr[...] = jnp.zeros_like(vv_scr)

    slot = c % 3
    for s in range(q_ref.shape[0]):
        k_fin = _prep_k(k_ref[s], kw_ref, cos_ref, slo_ref, shi_ref, CHUNK)
        v_cur = v_ref[s].astype(F32)
        nk_ref[s] = k_fin
        nv_ref[s] = v_cur
        kks, vvs = _kv_tiles(k_fin, v_cur)
        for h in range(N_KV_HEADS):
            kk_scr[s, slot, h] = kks[h]
            vv_scr[s, slot, h] = vvs[h]

        q_tiles = _prep_q(q_ref.at[s], qw_ref, cos_ref, slo_ref, shi_ref, CHUNK)
        _attend(q_tiles,
                lambda h, s=s: jnp.concatenate([kk_scr[s, w, h] for w in range(3)], axis=0),
                lambda h, s=s: jnp.concatenate([vv_scr[s, w, h] for w in range(3)], axis=0),
                lambda w: (c - w + 3) % 3 <= c,
                sink_ref, o_ref.at[s], CHUNK)


def _attn_prompt(proj3, rope_tabs, qw_tile, kw_tile, sink_tab, ns):
    batch, t, _ = proj3.shape
    nc = t // CHUNK
    keep = WINDOW // CHUNK
    cos_t, slo, shi = rope_tabs
    tab_spec = pl.BlockSpec((CHUNK, LANES), lambda g, c: (c, 0))
    const = lambda shape: pl.BlockSpec(shape, lambda g, c: (0,) * len(shape))
    keep_spec = pl.BlockSpec((ns, CHUNK, KV_W), lambda g, c: (g, jnp.maximum(c - (nc - keep), 0), 0))
    return pl.pallas_call(
        _attn_prompt_kernel,
        grid=(batch // ns, nc),
        in_specs=[
            pl.BlockSpec((ns, CHUNK, ATTN_W), lambda g, c: (g, c, COL_Q // ATTN_W)),
            pl.BlockSpec((ns, CHUNK, KV_W), lambda g, c: (g, c, COL_K // KV_W)),
            pl.BlockSpec((ns, CHUNK, KV_W), lambda g, c: (g, c, COL_V // KV_W)),
            tab_spec, tab_spec, tab_spec,
            const((1, LANES)), const((1, LANES)), const((Q_TILES, LANES)),
        ],
        out_specs=[pl.BlockSpec((ns, CHUNK, ATTN_W), lambda g, c: (g, c, 0)), keep_spec, keep_spec],
        out_shape=[
            jax.ShapeDtypeStruct((batch, t, ATTN_W), BF16),
            jax.ShapeDtypeStruct((batch, WINDOW, KV_W), F32),
            jax.ShapeDtypeStruct((batch, WINDOW, KV_W), F32),
        ],
        scratch_shapes=[
            pltpu.VMEM((ns, 3, N_KV_HEADS, 2 * CHUNK, LANES), BF16),
            pltpu.VMEM((ns, 3, N_KV_HEADS, 2 * CHUNK, 2 * LANES), BF16),
        ],
        compiler_params=_params("arbitrary", "arbitrary"),
        name="attn_prompt",
    )(proj3, proj3, proj3, cos_t, slo, shi, qw_tile, kw_tile, sink_tab)


def _attn_sample_kernel(q_ref, k_ref, v_ref, ck_ref, cv_ref, cos_ref, slo_ref, shi_ref, qw_ref, kw_ref,
                        sink_ref, o_ref, nk_ref, nv_ref):
    tq = q_ref.shape[1]
    new_valid = lax.broadcasted_iota(jnp.int32, (2 * tq, LANES), 1) % HEAD_DIM < tq
    pad = jnp.zeros((CHUNK - tq, KV_W), F32)
    for s in range(q_ref.shape[0]):
        k_fin = _prep_k(k_ref[s], kw_ref, cos_ref, slo_ref, shi_ref, tq)
        v_cur = v_ref[s].astype(F32)
        nk_ref[s, 0:WINDOW - tq, :] = ck_ref[s, tq:WINDOW, :]
        nv_ref[s, 0:WINDOW - tq, :] = cv_ref[s, tq:WINDOW, :]
        nk_ref[s, WINDOW - tq:WINDOW, :] = k_fin
        nv_ref[s, WINDOW - tq:WINDOW, :] = v_cur
        wins = [(ck_ref[s, :CHUNK, :], cv_ref[s, :CHUNK, :]),
                (ck_ref[s, CHUNK:, :], cv_ref[s, CHUNK:, :]),
                (jnp.concatenate([k_fin, pad], axis=0), jnp.concatenate([v_cur, pad], axis=0))]
        tiles = [_kv_tiles(kw, vw) for kw, vw in wins]
        q_tiles = _prep_q(q_ref.at[s], qw_ref, cos_ref, slo_ref, shi_ref, tq)
        _attend(q_tiles,
                lambda h, tiles=tiles: jnp.concatenate([tiles[w][0][h] for w in range(3)], axis=0),
                lambda h, tiles=tiles: jnp.concatenate([tiles[w][1][h] for w in range(3)], axis=0),
                lambda w: new_valid if w == 2 else True,
                sink_ref, o_ref.at[s], tq)


def _attn_sample(proj3, cache_k, cache_v, rope_tabs, qw_tile, kw_tile, sink_tab, ns):
    batch, tq, _ = proj3.shape
    cos_t, slo, shi = rope_tabs
    tab_spec = pl.BlockSpec((tq, LANES), lambda g: (0, 0))
    const = lambda shape: pl.BlockSpec(shape, lambda g: (0,) * len(shape))
    cache_spec = pl.BlockSpec((ns, WINDOW, KV_W), lambda g: (g, 0, 0))
    return pl.pallas_call(
        _attn_sample_kernel,
        grid=(batch // ns,),
        in_specs=[
            pl.BlockSpec((ns, tq, ATTN_W), lambda g: (g, 0, COL_Q // ATTN_W)),
            pl.BlockSpec((ns, tq, KV_W), lambda g: (g, 0, COL_K // KV_W)),
            pl.BlockSpec((ns, tq, KV_W), lambda g: (g, 0, COL_V // KV_W)),
            cache_spec, cache_spec,
            tab_spec, tab_spec, tab_spec,
            const((1, LANES)), const((1, LANES)), const((Q_TILES, LANES)),
        ],
        out_specs=[pl.BlockSpec((ns, tq, ATTN_W), lambda g: (g, 0, 0)), cache_spec, cache_spec],
        out_shape=[
            jax.ShapeDtypeStruct((batch, tq, ATTN_W), BF16),
            jax.ShapeDtypeStruct((batch, WINDOW, KV_W), F32),
            jax.ShapeDtypeStruct((batch, WINDOW, KV_W), F32),
        ],
        compiler_params=_params("arbitrary"),
        name="attn_sample",
    )(proj3, proj3, proj3, cache_k, cache_v, cos_t, slo, shi, qw_tile, kw_tile, sink_tab)


def _pad_rows(a, rows):
    if a.shape[0] == rows:
        return a
    return jnp.concatenate([a, jnp.zeros((rows - a.shape[0], a.shape[1]), a.dtype)], axis=0)


def _ssd_chunk(c, xs0_ref, xs1_ref, z0_ref, z1_ref, b_ref, c_ref, dt_ref, cst_ref, cw_ref, cb_ref, dtb_ref,
               alog_ref, dskip_ref, nw_ref, sel_ref, shift_ref, y_ref, ext_scr, st_scr, L):
    half = SSM_W // 2
    ext_scr[L:2 * L, 0:half] = xs0_ref[...]
    ext_scr[L:2 * L, half:SSM_W] = xs1_ref[...]
    ext_scr[L:2 * L, SSM_W:SSM_W + BC_W] = b_ref[...]
    ext_scr[L:2 * L, SSM_W + BC_W:CONV_CH] = c_ref[...]
    delayed = _dot(shift_ref[...], ext_scr[...])
    cur = ext_scr[L:2 * L, :].astype(F32)
    conv = cb_ref[...] + cur * cw_ref[CONV_W - 1:CONV_W, :]
    for i in range(CONV_W - 1):
        conv = conv + delayed[L * i:L * (i + 1)] * cw_ref[i:i + 1, :]
    s = cst_ref[...]
    w0, w1, w2 = cw_ref[0:1, :], cw_ref[1:2, :], cw_ref[2:3, :]
    head = jnp.concatenate([w0 * s[0:1] + w1 * s[1:2] + w2 * s[2:3], w0 * s[1:2] + w1 * s[2:3], w0 * s[2:3],
                            jnp.zeros((8 - (CONV_W - 1), CONV_CH), F32)], axis=0)
    conv = jnp.concatenate([conv[0:8] + jnp.where(c == 0, head, 0.0), conv[8:]], axis=0)
    conv = _silu(conv)
    xs = conv[:, :SSM_W]
    ext_scr[0:L, :] = ext_scr[L:2 * L, :]

    x_dt = dt_ref[...] + dtb_ref[...]
    dt = jnp.maximum(x_dt, 0.0) + jnp.log1p(jnp.exp(-jnp.abs(x_dt)))
    ad = dt * (-jnp.exp(alog_ref[...]))
    li = lax.broadcasted_iota(jnp.int32, (L, L), 0)
    si = lax.broadcasted_iota(jnp.int32, (L, L), 1)
    tril = (si <= li).astype(BF16)
    a_cum = _dot(jnp.concatenate([tril] * 3, axis=1), jnp.concatenate(_split3(ad), axis=0))
    sel = sel_ref[...]
    ex = _dot(jnp.concatenate(_split3(a_cum) + _split3(dt), axis=0), sel)
    col = ex[0:L] + ex[L:2 * L] + ex[2 * L:3 * L]
    dt_all = ex[3 * L:4 * L] + ex[4 * L:5 * L] + ex[5 * L:6 * L]
    lane_s = lax.broadcasted_iota(jnp.int32, (L, SSM_W), 1) % SSM_HEAD_DIM
    row_l = lax.broadcasted_iota(jnp.int32, (L, SSM_W), 0)
    diag = jnp.where(lane_s == row_l, col, 0.0)
    row = _dot(jnp.ones((L, 3 * L), BF16), jnp.concatenate(_split3(diag), axis=0))
    lmat = jnp.exp(jnp.where(lane_s <= row_l, col - row, NEG_INF))
    a_last = col[L - 1:L, :]
    exp_a = jnp.exp(col)
    decay = jnp.exp(a_last - col)
    exp_last = jnp.exp(a_last)

    xd = xs * dt_all
    xdd = xd * decay
    bi = lax.broadcasted_iota(jnp.int32, (GROUP_W, GROUP_W), 0) // SSM_HEAD_DIM
    bj = lax.broadcasted_iota(jnp.int32, (GROUP_W, GROUP_W), 1) // SSM_HEAD_DIM
    blockdiag = bi == bj

    for g in range(N_GROUPS):
        gs = slice(GROUP_W * g, GROUP_W * (g + 1))
        b_g = conv[:, SSM_W + D_STATE * g:SSM_W + D_STATE * (g + 1)]
        c_g = conv[:, SSM_W + BC_W + D_STATE * g:SSM_W + BC_W + D_STATE * (g + 1)].astype(BF16)
        b_pad = _pad_rows(b_g, CHUNK)
        cb = _dot_nt(c_g, jnp.concatenate([b_pad.astype(BF16)] * 4, axis=0))
        m_g = (cb * lmat[:, gs]).astype(BF16)
        xd_g = _pad_rows(xd[:, gs], CHUNK).astype(BF16)
        xd_bd = jnp.where(blockdiag, jnp.concatenate([xd_g] * 4, axis=0), jnp.zeros((), BF16))
        y_diag = _dot(m_g, xd_bd)
        st_g = st_scr[:, gs]
        y_off = _dot(c_g, st_g.astype(BF16)) * exp_a[:, gs]
        new_st = _dot(b_pad.T.astype(BF16), _pad_rows(xdd[:, gs], CHUNK).astype(BF16))
        st_scr[:, gs] = st_g * exp_last[:, gs] + new_st

        y = y_diag + y_off + xs[:, gs] * dskip_ref[:, gs]
        z_ref = z0_ref if g < N_GROUPS // 2 else z1_ref
        zs = slice(GROUP_W * (g % (N_GROUPS // 2)), GROUP_W * (g % (N_GROUPS // 2) + 1))
        gated = y * _silu(z_ref[:, zs].astype(F32))
        ms = jnp.mean(gated * gated, axis=-1, keepdims=True)
        y_ref[:, gs] = (gated * lax.rsqrt(ms + RMS_EPS) * nw_ref[:, gs]).astype(y_ref.dtype)


def _ssd_kernel(xs0_ref, xs1_ref, z0_ref, z1_ref, b_ref, c_ref, dt_ref, cst_ref, st0_ref, cw_ref, cb_ref,
                dtb_ref, alog_ref, dskip_ref, nw_ref, sel_ref, shift_ref, y_ref, ncv_ref, nst_ref,
                ext_scr, st_scr):
    c = pl.program_id(1)
    nc = pl.num_programs(1)
    ns, L = dt_ref.shape[0], dt_ref.shape[1]

    @pl.when(c == 0)
    def _():
        for s in range(ns):
            ext_scr[s, 0:L, :] = jnp.zeros((L, CONV_CH), BF16)
            st_scr[s] = st0_ref[s].T

    for s in range(ns):
        _ssd_chunk(c, xs0_ref.at[s], xs1_ref.at[s], z0_ref.at[s], z1_ref.at[s], b_ref.at[s], c_ref.at[s],
                   dt_ref.at[s], cst_ref.at[s], cw_ref, cb_ref, dtb_ref, alog_ref, dskip_ref, nw_ref,
                   sel_ref, shift_ref, y_ref.at[s], ext_scr.at[s], st_scr.at[s], L)

    @pl.when(c == nc - 1)
    def _():
        for s in range(ns):
            tail = ext_scr[s, L - BF16_ROWS:L, :].astype(F32)
            ncv_ref[s] = tail[BF16_ROWS - (CONV_W - 1):BF16_ROWS, :]
            nst_ref[s] = st_scr[s].T


def _shift_matrix(L):
    r = jnp.arange(3 * L)
    src = L + r % L - (CONV_W - 1) + r // L
    return (jnp.arange(2 * L)[None, :] == src[:, None]).astype(BF16)


def _ssd(proj3, dt3, conv_state, ssm_state, conv_w, conv_b, dtb, alog, dskip_all, ssm_nw, sel, L, ns):
    batch, t, _ = proj3.shape
    nc = t // L
    const = lambda shape: pl.BlockSpec(shape, lambda g, c: (0,) * len(shape))
    per_g = lambda shape: pl.BlockSpec((ns,) + shape, lambda g, c: (g,) + (0,) * len(shape))
    col = lambda start: pl.BlockSpec((ns, L, BC_W), lambda g, c: (g, c, start // BC_W))
    hp = N_SSM_HEADS * SSM_HEAD_DIM
    return pl.pallas_call(
        _ssd_kernel,
        grid=(batch // ns, nc),
        in_specs=[
            col(COL_XS), col(COL_XS + BC_W), col(COL_Z), col(COL_Z + BC_W), col(COL_B), col(COL_C),
            pl.BlockSpec((ns, L, LANES), lambda g, c: (g, c, 0)),
            per_g((CONV_W - 1, CONV_CH)),
            per_g((hp, D_STATE)),
            const((CONV_W, CONV_CH)), const((1, CONV_CH)), const((1, LANES)), const((1, LANES)),
            const((1, SSM_W)), const((1, SSM_W)), const((LANES, SSM_W)), const((3 * L, 2 * L)),
        ],
        out_specs=[
            pl.BlockSpec((ns, L, SSM_W), lambda g, c: (g, c, 0)),
            per_g((CONV_W - 1, CONV_CH)),
            per_g((hp, D_STATE)),
        ],
        out_shape=[
            jax.ShapeDtypeStruct((batch, t, SSM_W), BF16),
            jax.ShapeDtypeStruct((batch, CONV_W - 1, CONV_CH), F32),
            jax.ShapeDtypeStruct((batch, hp, D_STATE), F32),
        ],
        scratch_shapes=[
            pltpu.VMEM((ns, 2 * L, CONV_CH), BF16),
            pltpu.VMEM((ns, D_STATE, hp), F32),
        ],
        compiler_params=_params("arbitrary", "arbitrary"),
        name="ssd",
    )(proj3, proj3, proj3, proj3, proj3, proj3, dt3, conv_state, ssm_state, conv_w, conv_b, dtb, alog,
      dskip_all, ssm_nw, sel, _shift_matrix(L))


def _outproj_kernel(a_ref, y_ref, wa_ref, wy_ref, x_ref, o_ref):
    o_ref[...] = x_ref[...] + _dot(a_ref[...], wa_ref[...]) + _dot(y_ref[...], wy_ref[...])


def _outproj(attn, y, w_out, x2d, tm, tn):
    t = x2d.shape[0]
    half = D_MODEL // 2
    return pl.pallas_call(
        _outproj_kernel,
        grid=(D_MODEL // tn, t // tm),
        in_specs=[
            pl.BlockSpec((tm, half), lambda j, i: (i, 0)),
            pl.BlockSpec((tm, half), lambda j, i: (i, 0)),
            pl.BlockSpec((half, tn), lambda j, i: (0, j)),
            pl.BlockSpec((half, tn), lambda j, i: (1, j)),
            pl.BlockSpec((tm, tn), lambda j, i: (i, j)),
        ],
        out_specs=pl.BlockSpec((tm, tn), lambda j, i: (i, j)),
        out_shape=jax.ShapeDtypeStruct((t, D_MODEL), F32),
        compiler_params=_params("arbitrary", "arbitrary"),
        name="outproj",
    )(attn, y, w_out, w_out, x2d)


def _ffn_kernel(x_ref, nw_ref, wu_ref, wd_ref, o_ref, h_scr, *, n_split):
    @pl.when(pl.program_id(1) == 0)
    def _():
        _rmsnorm_rows(x_ref, nw_ref, h_scr, copy_ref=o_ref)

    u = jnp.maximum(_dot(h_scr[...], wu_ref[...]), 0.0)
    u = (u * u).astype(BF16)
    wn = D_MODEL // n_split
    for n in range(n_split):
        o_ref[:, wn * n:wn * (n + 1)] += _dot(u, wd_ref[:, wn * n:wn * (n + 1)])


def _ffn(x2d, norm_w, w_up, w_down, tm, tf):
    t = x2d.shape[0]
    return pl.pallas_call(
        functools.partial(_ffn_kernel, n_split=4),
        grid=(t // tm, FFN_HIDDEN // tf),
        in_specs=[
            pl.BlockSpec((tm, D_MODEL), lambda i, f: (i, 0), pipeline_mode=pl.Buffered(1)),
            pl.BlockSpec((1, D_MODEL), lambda i, f: (0, 0)),
            pl.BlockSpec((D_MODEL, tf), lambda i, f: (0, f)),
            pl.BlockSpec((tf, D_MODEL), lambda i, f: (f, 0)),
        ],
        out_specs=pl.BlockSpec((tm, D_MODEL), lambda i, f: (i, 0)),
        out_shape=jax.ShapeDtypeStruct((t, D_MODEL), F32),
        scratch_shapes=[pltpu.VMEM((tm, D_MODEL), BF16)],
        compiler_params=_params("arbitrary", "arbitrary"),
        name="ffn",
    )(x2d, norm_w, w_up, w_down)


def _rope_tables(pos):
    inv_freq = ROPE_THETA ** (-jnp.arange(ROPE_HALF, dtype=F32) / ROPE_HALF)
    ang = pos.astype(F32)[:, None] * inv_freq[None, :]
    cos, sin = jnp.cos(ang), jnp.sin(ang)
    n = pos.shape[0]
    rest = HEAD_DIM - ROPE_DIM
    head = lambda lo, hi, fill: jnp.concatenate([lo, hi, jnp.full((n, rest), fill, F32)], axis=1)
    zeros = jnp.zeros_like(sin)
    two = lambda a: jnp.concatenate([a, a], axis=1)
    return two(head(cos, cos, 1.0)), two(head(-sin, zeros, 0.0)), two(head(zeros, sin, 0.0))


def _layer_params(norm_mix_w, w_in, q_norm_w, k_norm_w, attn_sinks, conv_w, conv_b, dt_bias, a_log, d_skip,
                  ssm_norm_w, w_out, norm_ffn_w, w_up, w_down):
    pad_h = lambda a: jnp.pad(a.astype(F32), (0, LANES - N_SSM_HEADS))[None, :]
    sel = (jnp.arange(LANES)[:, None] == (jnp.arange(SSM_W) // SSM_HEAD_DIM)[None, :]).astype(BF16)
    return dict(
        norm_mix=norm_mix_w.astype(F32)[None, :], w_in=w_in.astype(BF16),
        qw=jnp.tile(q_norm_w.astype(F32), 2)[None, :], kw=jnp.tile(k_norm_w.astype(F32), 2)[None, :],
        sinks=jnp.repeat(attn_sinks.astype(F32), HEAD_DIM).reshape(Q_TILES, LANES),
        conv_w=conv_w.astype(F32), conv_b=conv_b.astype(F32)[None, :],
        dtb=pad_h(dt_bias), alog=pad_h(a_log),
        dskip=jnp.repeat(d_skip.astype(F32), SSM_HEAD_DIM)[None, :],
        ssm_nw=ssm_norm_w.astype(F32)[None, :], sel=sel,
        w_out=w_out.astype(BF16), norm_ffn=norm_ffn_w.astype(F32)[None, :],
        w_up=w_up.astype(BF16), w_down=w_down.astype(BF16),
    )


def _stream(x, pos, cache, conv_state, ssm_state, p, tm):
    batch, t, _ = x.shape
    ns = STREAMS_PER_STEP
    assert batch % ns == 0
    x2d = x.reshape(batch * t, D_MODEL)
    proj, dt_raw = _inproj(x2d, p["norm_mix"], p["w_in"], tm, 1024)
    proj3 = proj.reshape(batch, t, PROJ_W)
    tabs = _rope_tables(pos)
    if cache is None:
        attn, new_k, new_v = _attn_prompt(proj3, tabs, p["qw"], p["kw"], p["sinks"], ns)
        L = CHUNK
    else:
        past_k, past_v = cache
        attn, new_k, new_v = _attn_sample(proj3, past_k.reshape(batch, WINDOW, KV_W),
                                          past_v.reshape(batch, WINDOW, KV_W), tabs, p["qw"], p["kw"],
                                          p["sinks"], ns)
        L = min(CHUNK, t)
    hp = N_SSM_HEADS * SSM_HEAD_DIM
    y, new_conv, new_ssm = _ssd(proj3, dt_raw.reshape(batch, t, LANES), conv_state,
                                ssm_state.reshape(batch, hp, D_STATE), p["conv_w"], p["conv_b"], p["dtb"],
                                p["alog"], p["dskip"], p["ssm_nw"], p["sel"], L, ns)
    x1 = _outproj(attn.reshape(batch * t, ATTN_W), y.reshape(batch * t, SSM_W), p["w_out"], x2d, tm, 1024)
    out = _ffn(x1, p["norm_ffn"], p["w_up"], p["w_down"], tm, 512)
    return (out.reshape(batch, t, D_MODEL),
            new_k.reshape(batch, WINDOW, N_KV_HEADS, HEAD_DIM), new_v.reshape(batch, WINDOW, N_KV_HEADS, HEAD_DIM),
            new_conv, new_ssm.reshape(batch, N_SSM_HEADS, SSM_HEAD_DIM, D_STATE))


def kernel(x_prompt, x_sample, cache_k, cache_v, state_conv, state_ssm, norm_mix_w, w_in, q_norm_w, k_norm_w,
           attn_sinks, conv_w, conv_b, dt_bias, a_log, d_skip, ssm_norm_w, w_out, norm_ffn_w, w_up, w_down):
    depth = w_in.shape[0]
    b_p, t_p, _ = x_prompt.shape
    b_s, t_s, _ = x_sample.shape
    assert cache_k.shape[2] == WINDOW and t_p % CHUNK == 0 and t_p >= WINDOW
    assert t_s <= CHUNK and t_s % BF16_ROWS == 0
    pos_p = jnp.arange(t_p, dtype=jnp.int32)
    pos_s = PAST_LEN + jnp.arange(t_s, dtype=jnp.int32)
    hp, hs = x_prompt, x_sample
    outs_p, outs_s = [], []
    for layer in range(depth):
        p = _layer_params(norm_mix_w[layer], w_in[layer], q_norm_w[layer], k_norm_w[layer], attn_sinks[layer],
                          conv_w[layer], conv_b[layer], dt_bias[layer], a_log[layer], d_skip[layer],
                          ssm_norm_w[layer], w_out[layer], norm_ffn_w[layer], w_up[layer], w_down[layer])
        conv0 = jnp.zeros((b_p, CONV_W - 1, CONV_CH), F32)
        ssm0 = jnp.zeros((b_p, N_SSM_HEADS, SSM_HEAD_DIM, D_STATE), F32)
        hp, *rest_p = _stream(hp, pos_p, None, conv0, ssm0, p, 512)
        hs, *rest_s = _stream(hs, pos_s, (cache_k[layer], cache_v[layer]), state_conv[layer],
                              state_ssm[layer], p, 512)
        outs_p.append(rest_p)
        outs_s.append(rest_s)
    stack = lambda outs, i: jnp.stack([o[i] for o in outs])
    return (hp, hs, stack(outs_p, 0), stack(outs_p, 1), stack(outs_p, 2), stack(outs_p, 3),
            stack(outs_s, 0), stack(outs_s, 1), stack(outs_s, 2), stack(outs_s, 3))
```

```python
import functools

import jax
import jax.numpy as jnp
from jax import lax
from jax.experimental import pallas as pl
from jax.experimental.pallas import tpu as pltpu

F32 = jnp.float32
BF16 = jnp.bfloat16

D_MODEL = 4096
HEAD_DIM = 64
N_Q_HEADS = 32
N_KV_HEADS = 8
ATTN_W = N_Q_HEADS * HEAD_DIM
KV_W = N_KV_HEADS * HEAD_DIM
ROPE_DIM = 16
ROPE_HALF = ROPE_DIM // 2
ROPE_THETA = 500000.0
ATTN_SCALE = HEAD_DIM ** -0.5
WINDOW = 128
CHUNK = 64
PAST_LEN = 1024
SSM_W = 2048
N_SSM_HEADS = 32
SSM_HEAD_DIM = 64
N_GROUPS = 8
D_STATE = 128
BC_W = N_GROUPS * D_STATE
GROUP_W = SSM_W // N_GROUPS
CONV_W = 4
CONV_CH = SSM_W + 2 * BC_W
FFN_HIDDEN = 4 * D_MODEL
RMS_EPS = 1e-6
NEG_INF = -1e30
LANES = 128
BF16_ROWS = 16
Q_TILES = ATTN_W // LANES
KV_TILES = KV_W // LANES
PROJ_W = ATTN_W + 2 * KV_W + 2 * SSM_W + 2 * BC_W
COL_Q, COL_K, COL_V, COL_XS, COL_Z, COL_B, COL_C = 0, 2048, 2560, 3072, 5120, 7168, 8192
VMEM_LIMIT = 56 * 1024 * 1024
STREAMS_PER_STEP = (1, 2, 4)


def _params(*sem):
    return pltpu.CompilerParams(dimension_semantics=sem, vmem_limit_bytes=VMEM_LIMIT)


def _dot(a, b):
    return jnp.dot(a, b, preferred_element_type=F32)


def _dot_nt(a, b):
    return lax.dot_general(a, b, (((1,), (1,)), ((), ())), preferred_element_type=F32)


def _split3(x):
    a = x.astype(BF16)
    r = x - a.astype(F32)
    b = r.astype(BF16)
    c = (r - b.astype(F32)).astype(BF16)
    return a, b, c


def _silu(x):
    return x / (1.0 + jnp.exp(-x))


NORM_ROWS = 16
NORM_UNROLL = 8


def _rmsnorm_rows(x_ref, nw_ref, h_scr, copy_ref=None):
    def body(i, carry):
        r = pl.ds(pl.multiple_of(i * NORM_ROWS, NORM_ROWS), NORM_ROWS)
        x = x_ref[r, :]
        ms = jnp.mean(x * x, axis=-1, keepdims=True)
        h_scr[r, :] = (x * lax.rsqrt(ms + RMS_EPS) * nw_ref[...]).astype(BF16)
        if copy_ref is not None:
            copy_ref[r, :] = x
        return carry

    lax.fori_loop(0, x_ref.shape[0] // NORM_ROWS, body, 0, unroll=NORM_UNROLL)


def _inproj_kernel(x_ref, nw_ref, w_ref, wdt_ref, proj_ref, dt_ref, h_scr):
    @pl.when(pl.program_id(1) == 0)
    def _():
        _rmsnorm_rows(x_ref, nw_ref, h_scr)
        real = lax.broadcasted_iota(jnp.int32, wdt_ref.shape, 1) < N_SSM_HEADS
        dt_ref[...] = _dot(h_scr[...], jnp.where(real, wdt_ref[...], jnp.zeros((), BF16)))

    proj_ref[...] = _dot(h_scr[...], w_ref[...]).astype(BF16)


def _inproj(x2d, norm_w, w_in, tm, tn):
    t = x2d.shape[0]
    return pl.pallas_call(
        _inproj_kernel,
        grid=(t // tm, PROJ_W // tn),
        in_specs=[
            pl.BlockSpec((tm, D_MODEL), lambda i, j: (i, 0)),
            pl.BlockSpec((1, D_MODEL), lambda i, j: (0, 0)),
            pl.BlockSpec((D_MODEL, tn), lambda i, j: (0, j)),
            pl.BlockSpec((D_MODEL, LANES), lambda i, j: (0, PROJ_W // LANES)),
        ],
        out_specs=[
            pl.BlockSpec((tm, tn), lambda i, j: (i, j)),
            pl.BlockSpec((tm, LANES), lambda i, j: (i, 0)),
        ],
        out_shape=[
            jax.ShapeDtypeStruct((t, PROJ_W), BF16),
            jax.ShapeDtypeStruct((t, LANES), F32),
        ],
        scratch_shapes=[pltpu.VMEM((tm, D_MODEL), BF16)],
        compiler_params=_params("arbitrary", "arbitrary"),
        name="inproj",
    )(x2d, norm_w, w_in, w_in)


def _half_ones():
    r = lax.broadcasted_iota(jnp.int32, (LANES, LANES), 0) // HEAD_DIM
    c = lax.broadcasted_iota(jnp.int32, (LANES, LANES), 1) // HEAD_DIM
    return (r == c).astype(BF16)


def _head_norm(x, w_tile):
    ss = _dot((x * x).astype(BF16), _half_ones())
    return x * lax.rsqrt(ss * (1.0 / HEAD_DIM) + RMS_EPS) * w_tile


def _rope(x, cos_t, sin_lo, sin_hi):
    return (x * cos_t + pltpu.roll(x, LANES - ROPE_HALF, 1) * sin_lo
            + pltpu.roll(x, ROPE_HALF, 1) * sin_hi)


def _lane_is_a(shape):
    return lax.broadcasted_iota(jnp.int32, shape, 1) < HEAD_DIM


def _kv_tiles(k_win, v_win):
    is_a = _lane_is_a((CHUNK, LANES))
    ones_a = jnp.where(is_a, 1.0, 0.0).astype(F32)
    ones_b = 1.0 - ones_a
    kks, vvs = [], []
    for h in range(N_KV_HEADS):
        j, odd = h // 2, h % 2
        kt = k_win[:, LANES * j:LANES * (j + 1)]
        vt = v_win[:, LANES * j:LANES * (j + 1)]
        keep = jnp.logical_not(is_a) if odd else is_a
        k_own = jnp.where(keep, kt, 0.0)
        v_own = jnp.where(keep, vt, 0.0)
        k_sw = pltpu.roll(k_own, HEAD_DIM, 1)
        v_sw = pltpu.roll(v_own, HEAD_DIM, 1)
        k_a, k_b = (k_sw, k_own) if odd else (k_own, k_sw)
        v_a, v_b = (v_sw, v_own) if odd else (v_own, v_sw)
        kks.append(jnp.concatenate([k_a, k_b], axis=0).astype(BF16))
        vv = jnp.concatenate([jnp.concatenate([v_a, ones_a], axis=1),
                              jnp.concatenate([v_b, ones_b], axis=1)], axis=0)
        vvs.append(vv.astype(BF16))
    return kks, vvs


def _prep_q(q_ref, qw_ref, cos_ref, slo_ref, shi_ref, tq):
    q2 = jnp.concatenate([q_ref[:, LANES * i:LANES * (i + 1)].astype(F32) for i in range(Q_TILES)], axis=0)
    q2 = _head_norm(q2, qw_ref[...])
    cos_t, slo, shi = cos_ref[...], slo_ref[...], shi_ref[...]
    return [(_rope(q2[tq * i:tq * (i + 1)], cos_t, slo, shi) * ATTN_SCALE).astype(BF16) for i in range(Q_TILES)]


def _attend(q_tiles, kk_of, vv_of, valid_of, sink_ref, o_ref, tq):
    is_a = _lane_is_a((2 * tq, LANES))
    for h in range(N_KV_HEADS):
        qh = jnp.concatenate([q_tiles[2 * h], q_tiles[2 * h + 1]], axis=0)
        s = _dot_nt(qh, kk_of(h))
        sw = [jnp.where(valid_of(w), s[:, LANES * w:LANES * (w + 1)], NEG_INF) for w in range(3)]
        mt = jnp.maximum(jnp.maximum(sw[0], sw[1]), sw[2])
        m_a = jnp.max(jnp.where(is_a, mt, NEG_INF), axis=-1, keepdims=True)
        m_b = jnp.max(jnp.where(is_a, NEG_INF, mt), axis=-1, keepdims=True)
        sink = jnp.concatenate([jnp.broadcast_to(sink_ref[2 * h:2 * h + 1, :], (tq, LANES)),
                                jnp.broadcast_to(sink_ref[2 * h + 1:2 * h + 2, :], (tq, LANES))], axis=0)
        m = jnp.maximum(jnp.where(is_a, m_a, m_b), sink)
        e = jnp.concatenate([jnp.exp(x - m) for x in sw], axis=1).astype(BF16)
        oa = _dot(e, vv_of(h))
        o = oa[:, :LANES] / (oa[:, LANES:] + jnp.exp(sink - m))
        o_ref[:, LANES * 2 * h:LANES * (2 * h + 1)] = o[:tq].astype(o_ref.dtype)
        o_ref[:, LANES * (2 * h + 1):LANES * (2 * h + 2)] = o[tq:].astype(o_ref.dtype)


def _prep_k(k_raw, kw_ref, cos_ref, slo_ref, shi_ref, rows):
    k2 = jnp.concatenate([k_raw[:, LANES * j:LANES * (j + 1)].astype(F32) for j in range(KV_TILES)], axis=0)
    k2 = _head_norm(k2, kw_ref[...])
    cos_t, slo, shi = cos_ref[...], slo_ref[...], shi_ref[...]
    return jnp.concatenate([_rope(k2[rows * j:rows * (j + 1)], cos_t, slo, shi) for j in range(KV_TILES)], axis=1)


def _attn_prompt_kernel(q_ref, k_ref, v_ref, cos_ref, slo_ref, shi_ref, qw_ref, kw_ref, sink_ref,
                        o_ref, nk_ref, nv_ref, kk_scr, vv_scr):
    c = pl.program_id(1)

    @pl.when(c == 0)
    def _():
        kk_scr[...] = jnp.zeros_like(kk_scr)
        vv_scr[...] = jnp.zeros_like(vv_scr)

    slot = c % 3
    for s in range(q_ref.shape[0]):
        k_fin = _prep_k(k_ref[s], kw_ref, cos_ref, slo_ref, shi_ref, CHUNK)
        v_cur = v_ref[s].astype(F32)
        nk_ref[s] = k_fin
        nv_ref[s] = v_cur
        kks, vvs = _kv_tiles(k_fin, v_cur)
        for h in range(N_KV_HEADS):
            kk_scr[s, slot, h] = kks[h]
            vv_scr[s, slot, h] = vvs[h]

        q_tiles = _prep_q(q_ref.at[s], qw_ref, cos_ref, slo_ref, shi_ref, CHUNK)
        _attend(q_tiles,
                lambda h, s=s: jnp.concatenate([kk_scr[s, w, h] for w in range(3)], axis=0),
                lambda h, s=s: jnp.concatenate([vv_scr[s, w, h] for w in range(3)], axis=0),
                lambda w: (c - w + 3) % 3 <= c,
                sink_ref, o_ref.at[s], CHUNK)


def _attn_prompt(proj3, rope_tabs, qw_tile, kw_tile, sink_tab, ns):
    batch, t, _ = proj3.shape
    nc = t // CHUNK
    keep = WINDOW // CHUNK
    cos_t, slo, shi = rope_tabs
    tab_spec = pl.BlockSpec((CHUNK, LANES), lambda g, c: (c, 0))
    const = lambda shape: pl.BlockSpec(shape, lambda g, c: (0,) * len(shape))
    keep_spec = pl.BlockSpec((ns, CHUNK, KV_W), lambda g, c: (g, jnp.maximum(c - (nc - keep), 0), 0))
    return pl.pallas_call(
        _attn_prompt_kernel,
        grid=(batch // ns, nc),
        in_specs=[
            pl.BlockSpec((ns, CHUNK, ATTN_W), lambda g, c: (g, c, COL_Q // ATTN_W)),
            pl.BlockSpec((ns, CHUNK, KV_W), lambda g, c: (g, c, COL_K // KV_W)),
            pl.BlockSpec((ns, CHUNK, KV_W), lambda g, c: (g, c, COL_V // KV_W)),
            tab_spec, tab_spec, tab_spec,
            const((1, LANES)), const((1, LANES)), const((Q_TILES, LANES)),
        ],
        out_specs=[pl.BlockSpec((ns, CHUNK, ATTN_W), lambda g, c: (g, c, 0)), keep_spec, keep_spec],
        out_shape=[
            jax.ShapeDtypeStruct((batch, t, ATTN_W), BF16),
            jax.ShapeDtypeStruct((batch, WINDOW, KV_W), F32),
            jax.ShapeDtypeStruct((batch, WINDOW, KV_W), F32),
        ],
        scratch_shapes=[
            pltpu.VMEM((ns, 3, N_KV_HEADS, 2 * CHUNK, LANES), BF16),
            pltpu.VMEM((ns, 3, N_KV_HEADS, 2 * CHUNK, 2 * LANES), BF16),
        ],
        compiler_params=_params("arbitrary", "arbitrary"),
        name="attn_prompt",
    )(proj3, proj3, proj3, cos_t, slo, shi, qw_tile, kw_tile, sink_tab)


def _attn_sample_kernel(q_ref, k_ref, v_ref, ck_ref, cv_ref, cos_ref, slo_ref, shi_ref, qw_ref, kw_ref,
                        sink_ref, o_ref, nk_ref, nv_ref):
    tq = q_ref.shape[1]
    new_valid = lax.broadcasted_iota(jnp.int32, (2 * tq, LANES), 1) % HEAD_DIM < tq
    pad = jnp.zeros((CHUNK - tq, KV_W), F32)
    for s in range(q_ref.shape[0]):
        k_fin = _prep_k(k_ref[s], kw_ref, cos_ref, slo_ref, shi_ref, tq)
        v_cur = v_ref[s].astype(F32)
        nk_ref[s, 0:WINDOW - tq, :] = ck_ref[s, tq:WINDOW, :]
        nv_ref[s, 0:WINDOW - tq, :] = cv_ref[s, tq:WINDOW, :]
        nk_ref[s, WINDOW - tq:WINDOW, :] = k_fin
        nv_ref[s, WINDOW - tq:WINDOW, :] = v_cur
        wins = [(ck_ref[s, :CHUNK, :], cv_ref[s, :CHUNK, :]),
                (ck_ref[s, CHUNK:, :], cv_ref[s, CHUNK:, :]),
                (jnp.concatenate([k_fin, pad], axis=0), jnp.concatenate([v_cur, pad], axis=0))]
        tiles = [_kv_tiles(kw, vw) for kw, vw in wins]
        q_tiles = _prep_q(q_ref.at[s], qw_ref, cos_ref, slo_ref, shi_ref, tq)
        _attend(q_tiles,
                lambda h, tiles=tiles: jnp.concatenate([tiles[w][0][h] for w in range(3)], axis=0),
                lambda h, tiles=tiles: jnp.concatenate([tiles[w][1][h] for w in range(3)], axis=0),
                lambda w: new_valid if w == 2 else True,
                sink_ref, o_ref.at[s], tq)


def _attn_sample(proj3, cache_k, cache_v, rope_tabs, qw_tile, kw_tile, sink_tab, ns):
    batch, tq, _ = proj3.shape
    cos_t, slo, shi = rope_tabs
    tab_spec = pl.BlockSpec((tq, LANES), lambda g: (0, 0))
    const = lambda shape: pl.BlockSpec(shape, lambda g: (0,) * len(shape))
    cache_spec = pl.BlockSpec((ns, WINDOW, KV_W), lambda g: (g, 0, 0))
    return pl.pallas_call(
        _attn_sample_kernel,
        grid=(batch // ns,),
        in_specs=[
            pl.BlockSpec((ns, tq, ATTN_W), lambda g: (g, 0, COL_Q // ATTN_W)),
            pl.BlockSpec((ns, tq, KV_W), lambda g: (g, 0, COL_K // KV_W)),
            pl.BlockSpec((ns, tq, KV_W), lambda g: (g, 0, COL_V // KV_W)),
            cache_spec, cache_spec,
            tab_spec, tab_spec, tab_spec,
            const((1, LANES)), const((1, LANES)), const((Q_TILES, LANES)),
        ],
        out_specs=[pl.BlockSpec((ns, tq, ATTN_W), lambda g: (g, 0, 0)), cache_spec, cache_spec],
        out_shape=[
            jax.ShapeDtypeStruct((batch, tq, ATTN_W), BF16),
            jax.ShapeDtypeStruct((batch, WINDOW, KV_W), F32),
            jax.ShapeDtypeStruct((batch, WINDOW, KV_W), F32),
        ],
        compiler_params=_params("arbitrary"),
        name="attn_sample",
    )(proj3, proj3, proj3, cache_k, cache_v, cos_t, slo, shi, qw_tile, kw_tile, sink_tab)


def _pad_rows(a, rows):
    if a.shape[0] == rows:
        return a
    return jnp.concatenate([a, jnp.zeros((rows - a.shape[0], a.shape[1]), a.dtype)], axis=0)


def _ssd_chunk(c, xs0_ref, xs1_ref, z0_ref, z1_ref, b_ref, c_ref, dt_ref, cst_ref, cw_ref, cb_ref, dtb_ref,
               alog_ref, dskip_ref, nw_ref, sel_ref, shift_ref, y_ref, ext_scr, st_scr, L):
    half = SSM_W // 2
    ext_scr[L:2 * L, 0:half] = xs0_ref[...]
    ext_scr[L:2 * L, half:SSM_W] = xs1_ref[...]
    ext_scr[L:2 * L, SSM_W:SSM_W + BC_W] = b_ref[...]
    ext_scr[L:2 * L, SSM_W + BC_W:CONV_CH] = c_ref[...]
    delayed = _dot(shift_ref[...], ext_scr[...])
    cur = ext_scr[L:2 * L, :].astype(F32)
    conv = cb_ref[...] + cur * cw_ref[CONV_W - 1:CONV_W, :]
    for i in range(CONV_W - 1):
        conv = conv + delayed[L * i:L * (i + 1)] * cw_ref[i:i + 1, :]
    s = cst_ref[...]
    w0, w1, w2 = cw_ref[0:1, :], cw_ref[1:2, :], cw_ref[2:3, :]
    head = jnp.concatenate([w0 * s[0:1] + w1 * s[1:2] + w2 * s[2:3], w0 * s[1:2] + w1 * s[2:3], w0 * s[2:3],
                            jnp.zeros((8 - (CONV_W - 1), CONV_CH), F32)], axis=0)
    conv = jnp.concatenate([conv[0:8] + jnp.where(c == 0, head, 0.0), conv[8:]], axis=0)
    conv = _silu(conv)
    xs = conv[:, :SSM_W]
    ext_scr[0:L, :] = ext_scr[L:2 * L, :]

    x_dt = dt_ref[...] + dtb_ref[...]
    dt = jnp.maximum(x_dt, 0.0) + jnp.log1p(jnp.exp(-jnp.abs(x_dt)))
    ad = dt * (-jnp.exp(alog_ref[...]))
    li = lax.broadcasted_iota(jnp.int32, (L, L), 0)
    si = lax.broadcasted_iota(jnp.int32, (L, L), 1)
    tril = (si <= li).astype(BF16)
    a_cum = _dot(jnp.concatenate([tril] * 3, axis=1), jnp.concatenate(_split3(ad), axis=0))
    sel = sel_ref[...]
    ex = _dot(jnp.concatenate(_split3(a_cum) + _split3(dt), axis=0), sel)
    col = ex[0:L] + ex[L:2 * L] + ex[2 * L:3 * L]
    dt_all = ex[3 * L:4 * L] + ex[4 * L:5 * L] + ex[5 * L:6 * L]
    lane_s = lax.broadcasted_iota(jnp.int32, (L, SSM_W), 1) % SSM_HEAD_DIM
    row_l = lax.broadcasted_iota(jnp.int32, (L, SSM_W), 0)
    diag = jnp.where(lane_s == row_l, col, 0.0)
    row = _dot(jnp.ones((L, 3 * L), BF16), jnp.concatenate(_split3(diag), axis=0))
    lmat = jnp.exp(jnp.where(lane_s <= row_l, col - row, NEG_INF))
    a_last = col[L - 1:L, :]
    exp_a = jnp.exp(col)
    decay = jnp.exp(a_last - col)
    exp_last = jnp.exp(a_last)

    xd = xs * dt_all
    xdd = xd * decay
    bi = lax.broadcasted_iota(jnp.int32, (GROUP_W, GROUP_W), 0) // SSM_HEAD_DIM
    bj = lax.broadcasted_iota(jnp.int32, (GROUP_W, GROUP_W), 1) // SSM_HEAD_DIM
    blockdiag = bi == bj

    for g in range(N_GROUPS):
        gs = slice(GROUP_W * g, GROUP_W * (g + 1))
        b_g = conv[:, SSM_W + D_STATE * g:SSM_W + D_STATE * (g + 1)]
        c_g = conv[:, SSM_W + BC_W + D_STATE * g:SSM_W + BC_W + D_STATE * (g + 1)].astype(BF16)
        b_pad = _pad_rows(b_g, CHUNK)
        cb = _dot_nt(c_g, jnp.concatenate([b_pad.astype(BF16)] * 4, axis=0))
        m_g = (cb * lmat[:, gs]).astype(BF16)
        xd_g = _pad_rows(xd[:, gs], CHUNK).astype(BF16)
        xd_bd = jnp.where(blockdiag, jnp.concatenate([xd_g] * 4, axis=0), jnp.zeros((), BF16))
        y_diag = _dot(m_g, xd_bd)
        st_g = st_scr[:, gs]
        y_off = _dot(c_g, st_g.astype(BF16)) * exp_a[:, gs]
        new_st = _dot(b_pad.T.astype(BF16), _pad_rows(xdd[:, gs], CHUNK).astype(BF16))
        st_scr[:, gs] = st_g * exp_last[:, gs] + new_st

        y = y_diag + y_off + xs[:, gs] * dskip_ref[:, gs]
        z_ref = z0_ref if g < N_GROUPS // 2 else z1_ref
        zs = slice(GROUP_W * (g % (N_GROUPS // 2)), GROUP_W * (g % (N_GROUPS // 2) + 1))
        gated = y * _silu(z_ref[:, zs].astype(F32))
        ms = jnp.mean(gated * gated, axis=-1, keepdims=True)
        y_ref[:, gs] = (gated * lax.rsqrt(ms + RMS_EPS) * nw_ref[:, gs]).astype(y_ref.dtype)


def _ssd_kernel(xs0_ref, xs1_ref, z0_ref, z1_ref, b_ref, c_ref, dt_ref, cst_ref, st0_ref, cw_ref, cb_ref,
                dtb_ref, alog_ref, dskip_ref, nw_ref, sel_ref, shift_ref, y_ref, ncv_ref, nst_ref,
                ext_scr, st_scr):
    c = pl.program_id(1)
    nc = pl.num_programs(1)
    ns, L = dt_ref.shape[0], dt_ref.shape[1]

    @pl.when(c == 0)
    def _():
        for s in range(ns):
            ext_scr[s, 0:L, :] = jnp.zeros((L, CONV_CH), BF16)
            st_scr[s] = st0_ref[s].T

    for s in range(ns):
        _ssd_chunk(c, xs0_ref.at[s], xs1_ref.at[s], z0_ref.at[s], z1_ref.at[s], b_ref.at[s], c_ref.at[s],
                   dt_ref.at[s], cst_ref.at[s], cw_ref, cb_ref, dtb_ref, alog_ref, dskip_ref, nw_ref,
                   sel_ref, shift_ref, y_ref.at[s], ext_scr.at[s], st_scr.at[s], L)

    @pl.when(c == nc - 1)
    def _():
        for s in range(ns):
            tail = ext_scr[s, L - BF16_ROWS:L, :].astype(F32)
            ncv_ref[s] = tail[BF16_ROWS - (CONV_W - 1):BF16_ROWS, :]
            nst_ref[s] = st_scr[s].T


def _shift_matrix(L):
    r = jnp.arange(3 * L)
    src = L + r % L - (CONV_W - 1) + r // L
    return (jnp.arange(2 * L)[None, :] == src[:, None]).astype(BF16)


def _ssd(proj3, dt3, conv_state, ssm_state, conv_w, conv_b, dtb, alog, dskip_all, ssm_nw, sel, L, ns):
    batch, t, _ = proj3.shape
    nc = t // L
    const = lambda shape: pl.BlockSpec(shape, lambda g, c: (0,) * len(shape))
    per_g = lambda shape: pl.BlockSpec((ns,) + shape, lambda g, c: (g,) + (0,) * len(shape))
    col = lambda start: pl.BlockSpec((ns, L, BC_W), lambda g, c: (g, c, start // BC_W))
    hp = N_SSM_HEADS * SSM_HEAD_DIM
    return pl.pallas_call(
        _ssd_kernel,
        grid=(batch // ns, nc),
        in_specs=[
            col(COL_XS), col(COL_XS + BC_W), col(COL_Z), col(COL_Z + BC_W), col(COL_B), col(COL_C),
            pl.BlockSpec((ns, L, LANES), lambda g, c: (g, c, 0)),
            per_g((CONV_W - 1, CONV_CH)),
            per_g((hp, D_STATE)),
            const((CONV_W, CONV_CH)), const((1, CONV_CH)), const((1, LANES)), const((1, LANES)),
            const((1, SSM_W)), const((1, SSM_W)), const((LANES, SSM_W)), const((3 * L, 2 * L)),
        ],
        out_specs=[
            pl.BlockSpec((ns, L, SSM_W), lambda g, c: (g, c, 0)),
            per_g((CONV_W - 1, CONV_CH)),
            per_g((hp, D_STATE)),
        ],
        out_shape=[
            jax.ShapeDtypeStruct((batch, t, SSM_W), BF16),
            jax.ShapeDtypeStruct((batch, CONV_W - 1, CONV_CH), F32),
            jax.ShapeDtypeStruct((batch, hp, D_STATE), F32),
        ],
        scratch_shapes=[
            pltpu.VMEM((ns, 2 * L, CONV_CH), BF16),
            pltpu.VMEM((ns, D_STATE, hp), F32),
        ],
        compiler_params=_params("arbitrary", "arbitrary"),
        name="ssd",
    )(proj3, proj3, proj3, proj3, proj3, proj3, dt3, conv_state, ssm_state, conv_w, conv_b, dtb, alog,
      dskip_all, ssm_nw, sel, _shift_matrix(L))


def _outproj_kernel(a_ref, y_ref, wa_ref, wy_ref, x_ref, o_ref):
    o_ref[...] = x_ref[...] + _dot(a_ref[...], wa_ref[...]) + _dot(y_ref[...], wy_ref[...])


def _outproj(attn, y, w_out, x2d, tm, tn):
    t = x2d.shape[0]
    half = D_MODEL // 2
    return pl.pallas_call(
        _outproj_kernel,
        grid=(D_MODEL // tn, t // tm),
        in_specs=[
            pl.BlockSpec((tm, half), lambda j, i: (i, 0)),
            pl.BlockSpec((tm, half), lambda j, i: (i, 0)),
            pl.BlockSpec((half, tn), lambda j, i: (0, j)),
            pl.BlockSpec((half, tn), lambda j, i: (1, j)),
            pl.BlockSpec((tm, tn), lambda j, i: (i, j)),
        ],
        out_specs=pl.BlockSpec((tm, tn), lambda j, i: (i, j)),
        out_shape=jax.ShapeDtypeStruct((t, D_MODEL), F32),
        compiler_params=_params("arbitrary", "arbitrary"),
        name="outproj",
    )(attn, y, w_out, w_out, x2d)


def _ffn_kernel(x_ref, nw_ref, wu_ref, wd_ref, o_ref, h_scr, *, n_split):
    @pl.when(pl.program_id(1) == 0)
    def _():
        _rmsnorm_rows(x_ref, nw_ref, h_scr, copy_ref=o_ref)

    u = jnp.maximum(_dot(h_scr[...], wu_ref[...]), 0.0)
    u = (u * u).astype(BF16)
    wn = D_MODEL // n_split
    for n in range(n_split):
        o_ref[:, wn * n:wn * (n + 1)] += _dot(u, wd_ref[:, wn * n:wn * (n + 1)])


def _ffn(x2d, norm_w, w_up, w_down, tm, tf):
    t = x2d.shape[0]
    return pl.pallas_call(
        functools.partial(_ffn_kernel, n_split=4),
        grid=(t // tm, FFN_HIDDEN // tf),
        in_specs=[
            pl.BlockSpec((tm, D_MODEL), lambda i, f: (i, 0)),
            pl.BlockSpec((1, D_MODEL), lambda i, f: (0, 0)),
            pl.BlockSpec((D_MODEL, tf), lambda i, f: (0, f)),
            pl.BlockSpec((tf, D_MODEL), lambda i, f: (f, 0)),
        ],
        out_specs=pl.BlockSpec((tm, D_MODEL), lambda i, f: (i, 0)),
        out_shape=jax.ShapeDtypeStruct((t, D_MODEL), F32),
        scratch_shapes=[pltpu.VMEM((tm, D_MODEL), BF16)],
        compiler_params=_params("arbitrary", "arbitrary"),
        name="ffn",
    )(x2d, norm_w, w_up, w_down)


def _rope_tables(pos):
    inv_freq = ROPE_THETA ** (-jnp.arange(ROPE_HALF, dtype=F32) / ROPE_HALF)
    lane = jnp.arange(LANES) % HEAD_DIM
    freq = jnp.where(lane < ROPE_DIM, inv_freq[lane % ROPE_HALF], 0.0)
    ang = pos.astype(F32)[:, None] * freq[None, :]
    cos, sin = jnp.cos(ang), jnp.sin(ang)
    return cos, jnp.where(lane < ROPE_HALF, -sin, 0.0), jnp.where(lane >= ROPE_HALF, sin, 0.0)


def _layer_params(norm_mix_w, w_in, q_norm_w, k_norm_w, attn_sinks, conv_w, conv_b, dt_bias, a_log, d_skip,
                  ssm_norm_w, w_out, norm_ffn_w, w_up, w_down):
    pad_h = lambda a: jnp.pad(a.astype(F32), (0, LANES - N_SSM_HEADS))[None, :]
    sel = (jnp.arange(LANES)[:, None] == (jnp.arange(SSM_W) // SSM_HEAD_DIM)[None, :]).astype(BF16)
    return dict(
        norm_mix=norm_mix_w.astype(F32)[None, :], w_in=w_in.astype(BF16),
        qw=jnp.tile(q_norm_w.astype(F32), 2)[None, :], kw=jnp.tile(k_norm_w.astype(F32), 2)[None, :],
        sinks=jnp.repeat(attn_sinks.astype(F32), HEAD_DIM).reshape(Q_TILES, LANES),
        conv_w=conv_w.astype(F32), conv_b=conv_b.astype(F32)[None, :],
        dtb=pad_h(dt_bias), alog=pad_h(a_log),
        dskip=jnp.repeat(d_skip.astype(F32), SSM_HEAD_DIM)[None, :],
        ssm_nw=ssm_norm_w.astype(F32)[None, :], sel=sel,
        w_out=w_out.astype(BF16), norm_ffn=norm_ffn_w.astype(F32)[None, :],
        w_up=w_up.astype(BF16), w_down=w_down.astype(BF16),
    )


def _stream(x, pos, cache, conv_state, ssm_state, p, tm):
    batch, t, _ = x.shape
    ns = max(n for n in STREAMS_PER_STEP if batch % n == 0)
    x2d = x.reshape(batch * t, D_MODEL)
    proj, dt_raw = _inproj(x2d, p["norm_mix"], p["w_in"], tm, 1024)
    proj3 = proj.reshape(batch, t, PROJ_W)
    tabs = _rope_tables(pos)
    if cache is None:
        attn, new_k, new_v = _attn_prompt(proj3, tabs, p["qw"], p["kw"], p["sinks"], ns)
        L = CHUNK
    else:
        past_k, past_v = cache
        attn, new_k, new_v = _attn_sample(proj3, past_k.reshape(batch, WINDOW, KV_W),
                                          past_v.reshape(batch, WINDOW, KV_W), tabs, p["qw"], p["kw"],
                                          p["sinks"], ns)
        L = min(CHUNK, t)
    hp = N_SSM_HEADS * SSM_HEAD_DIM
    y, new_conv, new_ssm = _ssd(proj3, dt_raw.reshape(batch, t, LANES), conv_state,
                                ssm_state.reshape(batch, hp, D_STATE), p["conv_w"], p["conv_b"], p["dtb"],
                                p["alog"], p["dskip"], p["ssm_nw"], p["sel"], L, ns)
    x1 = _outproj(attn.reshape(batch * t, ATTN_W), y.reshape(batch * t, SSM_W), p["w_out"], x2d, tm, 1024)
    out = _ffn(x1, p["norm_ffn"], p["w_up"], p["w_down"], tm, 512)
    return (out.reshape(batch, t, D_MODEL),
            new_k.reshape(batch, WINDOW, N_KV_HEADS, HEAD_DIM), new_v.reshape(batch, WINDOW, N_KV_HEADS, HEAD_DIM),
            new_conv, new_ssm.reshape(batch, N_SSM_HEADS, SSM_HEAD_DIM, D_STATE))


def kernel(x_prompt, x_sample, cache_k, cache_v, state_conv, state_ssm, norm_mix_w, w_in, q_norm_w, k_norm_w,
           attn_sinks, conv_w, conv_b, dt_bias, a_log, d_skip, ssm_norm_w, w_out, norm_ffn_w, w_up, w_down):
    depth = w_in.shape[0]
    b_p, t_p, _ = x_prompt.shape
    b_s, t_s, _ = x_sample.shape
    assert cache_k.shape[2] == WINDOW and t_p % CHUNK == 0 and t_p >= WINDOW
    assert t_s <= CHUNK and t_s % BF16_ROWS == 0
    pos_p = jnp.arange(t_p, dtype=jnp.int32)
    pos_s = PAST_LEN + jnp.arange(t_s, dtype=jnp.int32)
    hp, hs = x_prompt, x_sample
    outs_p, outs_s = [], []
    for layer in range(depth):
        p = _layer_params(norm_mix_w[layer], w_in[layer], q_norm_w[layer], k_norm_w[layer], attn_sinks[layer],
                          conv_w[layer], conv_b[layer], dt_bias[layer], a_log[layer], d_skip[layer],
                          ssm_norm_w[layer], w_out[layer], norm_ffn_w[layer], w_up[layer], w_down[layer])
        conv0 = jnp.zeros((b_p, CONV_W - 1, CONV_CH), F32)
        ssm0 = jnp.zeros((b_p, N_SSM_HEADS, SSM_HEAD_DIM, D_STATE), F32)
        hp, *rest_p = _stream(hp, pos_p, None, conv0, ssm0, p, 512)
        hs, *rest_s = _stream(hs, pos_s, (cache_k[layer], cache_v[layer]), state_conv[layer],
                              state_ssm[layer], p, 512)
        outs_p.append(rest_p)
        outs_s.append(rest_s)
    stack = lambda outs, i: jnp.stack([o[i] for o in outs])
    return (hp, hs, stack(outs_p, 0), stack(outs_p, 1), stack(outs_p, 2), stack(outs_p, 3),
            stack(outs_s, 0), stack(outs_s, 1), stack(outs_s, 2), stack(outs_s, 3))
```

```python
import functools

import jax
import jax.numpy as jnp
from jax import lax
from jax.experimental import pallas as pl
from jax.experimental.pallas import tpu as pltpu

F32 = jnp.float32
BF16 = jnp.bfloat16

D_MODEL = 4096
HEAD_DIM = 64
N_Q_HEADS = 32
N_KV_HEADS = 8
ATTN_W = N_Q_HEADS * HEAD_DIM
KV_W = N_KV_HEADS * HEAD_DIM
ROPE_DIM = 16
ROPE_HALF = ROPE_DIM // 2
ROPE_THETA = 500000.0
ATTN_SCALE = HEAD_DIM ** -0.5
WINDOW = 128
CHUNK = 64
PAST_LEN = 1024
SSM_W = 2048
N_SSM_HEADS = 32
SSM_HEAD_DIM = 64
N_GROUPS = 8
D_STATE = 128
BC_W = N_GROUPS * D_STATE
GROUP_W = SSM_W // N_GROUPS
CONV_W = 4
CONV_CH = SSM_W + 2 * BC_W
FFN_HIDDEN = 4 * D_MODEL
RMS_EPS = 1e-6
NEG_INF = -1e30
LANES = 128
BF16_ROWS = 16
Q_TILES = ATTN_W // LANES
KV_TILES = KV_W // LANES
PROJ_W = ATTN_W + 2 * KV_W + 2 * SSM_W + 2 * BC_W
COL_Q, COL_K, COL_V, COL_XS, COL_Z, COL_B, COL_C = 0, 2048, 2560, 3072, 5120, 7168, 8192
VMEM_LIMIT = 56 * 1024 * 1024
STREAMS_PER_STEP = (1, 2, 4)


def _params(*sem):
    return pltpu.CompilerParams(dimension_semantics=sem, vmem_limit_bytes=VMEM_LIMIT)


def _dot(a, b):
    return jnp.dot(a, b, preferred_element_type=F32)


def _dot_nt(a, b):
    return lax.dot_general(a, b, (((1,), (1,)), ((), ())), preferred_element_type=F32)


def _split3(x):
    a = x.astype(BF16)
    r = x - a.astype(F32)
    b = r.astype(BF16)
    c = (r - b.astype(F32)).astype(BF16)
    return a, b, c


def _silu(x):
    return x / (1.0 + jnp.exp(-x))


NORM_ROWS = 16
NORM_UNROLL = 8


def _rmsnorm_rows(x_ref, nw_ref, h_scr, copy_ref=None):
    def body(i, carry):
        r = pl.ds(pl.multiple_of(i * NORM_ROWS, NORM_ROWS), NORM_ROWS)
        x = x_ref[r, :]
        ms = jnp.mean(x * x, axis=-1, keepdims=True)
        h_scr[r, :] = (x * lax.rsqrt(ms + RMS_EPS) * nw_ref[...]).astype(BF16)
        if copy_ref is not None:
            copy_ref[r, :] = x
        return carry

    lax.fori_loop(0, x_ref.shape[0] // NORM_ROWS, body, 0, unroll=NORM_UNROLL)


def _cast_riders(arrays, steps, linear_step):
    in_specs, out_specs, out_shapes = [], [], []
    for a in arrays:
        rows, cols = a.shape
        nb = max(n for n in range(1, steps + 1) if rows % n == 0 and (rows // n) % BF16_ROWS == 0)
        spec = pl.BlockSpec((rows // nb, cols), lambda *g, nb=nb: (jnp.minimum(linear_step(*g), nb - 1), 0))
        in_specs.append(spec)
        out_specs.append(spec)
        out_shapes.append(jax.ShapeDtypeStruct(a.shape, BF16))
    return in_specs, out_specs, out_shapes


def _cast_blocks(src_refs, dst_refs):
    for src, dst in zip(src_refs, dst_refs):
        dst[...] = src[...].astype(BF16)


def _inproj_kernel(*refs, n_cast):
    x_ref, nw_ref, w_ref, wdt_ref = refs[:4]
    cast_in = refs[4:4 + n_cast]
    proj_ref, dt_ref = refs[4 + n_cast:6 + n_cast]
    cast_out, h_scr = refs[6 + n_cast:6 + 2 * n_cast], refs[6 + 2 * n_cast]

    @pl.when(pl.program_id(1) == 0)
    def _():
        _rmsnorm_rows(x_ref, nw_ref, h_scr)
        real = lax.broadcasted_iota(jnp.int32, wdt_ref.shape, 1) < N_SSM_HEADS
        dt_ref[...] = _dot(h_scr[...], jnp.where(real, wdt_ref[...], jnp.zeros((), BF16)))

    proj_ref[...] = _dot(h_scr[...], w_ref[...]).astype(BF16)
    _cast_blocks(cast_in, cast_out)


def _inproj(x2d, norm_w, w_in, tm, tn, cast=()):
    t = x2d.shape[0]
    nj = PROJ_W // tn
    c_in, c_out, c_shape = _cast_riders(cast, (t // tm) * nj, lambda i, j: i * nj + j)
    return pl.pallas_call(
        functools.partial(_inproj_kernel, n_cast=len(cast)),
        grid=(t // tm, nj),
        in_specs=[
            pl.BlockSpec((tm, D_MODEL), lambda i, j: (i, 0)),
            pl.BlockSpec((1, D_MODEL), lambda i, j: (0, 0)),
            pl.BlockSpec((D_MODEL, tn), lambda i, j: (0, j)),
            pl.BlockSpec((D_MODEL, LANES), lambda i, j: (0, PROJ_W // LANES)),
        ] + c_in,
        out_specs=[
            pl.BlockSpec((tm, tn), lambda i, j: (i, j)),
            pl.BlockSpec((tm, LANES), lambda i, j: (i, 0)),
        ] + c_out,
        out_shape=[
            jax.ShapeDtypeStruct((t, PROJ_W), BF16),
            jax.ShapeDtypeStruct((t, LANES), F32),
        ] + c_shape,
        scratch_shapes=[pltpu.VMEM((tm, D_MODEL), BF16)],
        compiler_params=_params("arbitrary", "arbitrary"),
        name="inproj",
    )(x2d, norm_w, w_in, w_in, *cast)


def _half_ones():
    r = lax.broadcasted_iota(jnp.int32, (LANES, LANES), 0) // HEAD_DIM
    c = lax.broadcasted_iota(jnp.int32, (LANES, LANES), 1) // HEAD_DIM
    return (r == c).astype(BF16)


def _head_norm(x, w_tile):
    ss = _dot((x * x).astype(BF16), _half_ones())
    return x * lax.rsqrt(ss * (1.0 / HEAD_DIM) + RMS_EPS) * w_tile


def _rope(x, cos_t, sin_lo, sin_hi):
    return (x * cos_t + pltpu.roll(x, LANES - ROPE_HALF, 1) * sin_lo
            + pltpu.roll(x, ROPE_HALF, 1) * sin_hi)


def _lane_is_a(shape):
    return lax.broadcasted_iota(jnp.int32, shape, 1) < HEAD_DIM


def _kv_tiles(k_win, v_win):
    is_a = _lane_is_a((CHUNK, LANES))
    ones_a = jnp.where(is_a, 1.0, 0.0).astype(F32)
    ones_b = 1.0 - ones_a
    kks, vvs = [], []
    for h in range(N_KV_HEADS):
        j, odd = h // 2, h % 2
        kt = k_win[:, LANES * j:LANES * (j + 1)]
        vt = v_win[:, LANES * j:LANES * (j + 1)]
        keep = jnp.logical_not(is_a) if odd else is_a
        k_own = jnp.where(keep, kt, 0.0)
        v_own = jnp.where(keep, vt, 0.0)
        k_sw = pltpu.roll(k_own, HEAD_DIM, 1)
        v_sw = pltpu.roll(v_own, HEAD_DIM, 1)
        k_a, k_b = (k_sw, k_own) if odd else (k_own, k_sw)
        v_a, v_b = (v_sw, v_own) if odd else (v_own, v_sw)
        kks.append(jnp.concatenate([k_a, k_b], axis=0).astype(BF16))
        vv = jnp.concatenate([jnp.concatenate([v_a, ones_a], axis=1),
                              jnp.concatenate([v_b, ones_b], axis=1)], axis=0)
        vvs.append(vv.astype(BF16))
    return kks, vvs


def _prep_q(q_ref, qw_ref, cos_ref, slo_ref, shi_ref, tq):
    q2 = jnp.concatenate([q_ref[:, LANES * i:LANES * (i + 1)].astype(F32) for i in range(Q_TILES)], axis=0)
    q2 = _head_norm(q2, qw_ref[...])
    cos_t, slo, shi = cos_ref[...], slo_ref[...], shi_ref[...]
    return [(_rope(q2[tq * i:tq * (i + 1)], cos_t, slo, shi) * ATTN_SCALE).astype(BF16) for i in range(Q_TILES)]


def _attend(q_tiles, kk_of, vv_of, valid_of, sink_ref, o_ref, tq):
    is_a = _lane_is_a((2 * tq, LANES))
    for h in range(N_KV_HEADS):
        qh = jnp.concatenate([q_tiles[2 * h], q_tiles[2 * h + 1]], axis=0)
        s = _dot_nt(qh, kk_of(h))
        sw = [jnp.where(valid_of(w), s[:, LANES * w:LANES * (w + 1)], NEG_INF) for w in range(3)]
        mt = jnp.maximum(jnp.maximum(sw[0], sw[1]), sw[2])
        m_a = jnp.max(jnp.where(is_a, mt, NEG_INF), axis=-1, keepdims=True)
        m_b = jnp.max(jnp.where(is_a, NEG_INF, mt), axis=-1, keepdims=True)
        sink = jnp.concatenate([jnp.broadcast_to(sink_ref[2 * h:2 * h + 1, :], (tq, LANES)),
                                jnp.broadcast_to(sink_ref[2 * h + 1:2 * h + 2, :], (tq, LANES))], axis=0)
        m = jnp.maximum(jnp.where(is_a, m_a, m_b), sink)
        e = jnp.concatenate([jnp.exp(x - m) for x in sw], axis=1).astype(BF16)
        oa = _dot(e, vv_of(h))
        o = oa[:, :LANES] / (oa[:, LANES:] + jnp.exp(sink - m))
        o_ref[:, LANES * 2 * h:LANES * (2 * h + 1)] = o[:tq].astype(o_ref.dtype)
        o_ref[:, LANES * (2 * h + 1):LANES * (2 * h + 2)] = o[tq:].astype(o_ref.dtype)


def _prep_k(k_raw, kw_ref, cos_ref, slo_ref, shi_ref, rows):
    k2 = jnp.concatenate([k_raw[:, LANES * j:LANES * (j + 1)].astype(F32) for j in range(KV_TILES)], axis=0)
    k2 = _head_norm(k2, kw_ref[...])
    cos_t, slo, shi = cos_ref[...], slo_ref[...], shi_ref[...]
    return jnp.concatenate([_rope(k2[rows * j:rows * (j + 1)], cos_t, slo, shi) for j in range(KV_TILES)], axis=1)


def _attn_prompt_kernel(q_ref, k_ref, v_ref, cos_ref, slo_ref, shi_ref, qw_ref, kw_ref, sink_ref,
                        o_ref, nk_ref, nv_ref, kk_scr, vv_scr):
    c = pl.program_id(1)

    @pl.when(c == 0)
    def _():
        kk_scr[...] = jnp.zeros_like(kk_scr)
        vv_scr[...] = jnp.zeros_like(vv_scr)

    slot = c % 3
    for s in range(q_ref.shape[0]):
        k_fin = _prep_k(k_ref[s], kw_ref, cos_ref, slo_ref, shi_ref, CHUNK)
        v_cur = v_ref[s].astype(F32)
        nk_ref[s] = k_fin
        nv_ref[s] = v_cur
        kks, vvs = _kv_tiles(k_fin, v_cur)
        for h in range(N_KV_HEADS):
            kk_scr[s, slot, h] = kks[h]
            vv_scr[s, slot, h] = vvs[h]

        q_tiles = _prep_q(q_ref.at[s], qw_ref, cos_ref, slo_ref, shi_ref, CHUNK)
        _attend(q_tiles,
                lambda h, s=s: jnp.concatenate([kk_scr[s, w, h] for w in range(3)], axis=0),
                lambda h, s=s: jnp.concatenate([vv_scr[s, w, h] for w in range(3)], axis=0),
                lambda w: (c - w + 3) % 3 <= c,
                sink_ref, o_ref.at[s], CHUNK)


def _attn_prompt(proj3, rope_tabs, qw_tile, kw_tile, sink_tab, ns):
    batch, t, _ = proj3.shape
    nc = t // CHUNK
    keep = WINDOW // CHUNK
    cos_t, slo, shi = rope_tabs
    tab_spec = pl.BlockSpec((CHUNK, LANES), lambda g, c: (c, 0))
    const = lambda shape: pl.BlockSpec(shape, lambda g, c: (0,) * len(shape))
    keep_spec = pl.BlockSpec((ns, CHUNK, KV_W), lambda g, c: (g, jnp.maximum(c - (nc - keep), 0), 0))
    return pl.pallas_call(
        _attn_prompt_kernel,
        grid=(batch // ns, nc),
        in_specs=[
            pl.BlockSpec((ns, CHUNK, ATTN_W), lambda g, c: (g, c, COL_Q // ATTN_W)),
            pl.BlockSpec((ns, CHUNK, KV_W), lambda g, c: (g, c, COL_K // KV_W)),
            pl.BlockSpec((ns, CHUNK, KV_W), lambda g, c: (g, c, COL_V // KV_W)),
            tab_spec, tab_spec, tab_spec,
            const((1, LANES)), const((1, LANES)), const((Q_TILES, LANES)),
        ],
        out_specs=[pl.BlockSpec((ns, CHUNK, ATTN_W), lambda g, c: (g, c, 0)), keep_spec, keep_spec],
        out_shape=[
            jax.ShapeDtypeStruct((batch, t, ATTN_W), BF16),
            jax.ShapeDtypeStruct((batch, WINDOW, KV_W), F32),
            jax.ShapeDtypeStruct((batch, WINDOW, KV_W), F32),
        ],
        scratch_shapes=[
            pltpu.VMEM((ns, 3, N_KV_HEADS, 2 * CHUNK, LANES), BF16),
            pltpu.VMEM((ns, 3, N_KV_HEADS, 2 * CHUNK, 2 * LANES), BF16),
        ],
        compiler_params=_params("arbitrary", "arbitrary"),
        name="attn_prompt",
    )(proj3, proj3, proj3, cos_t, slo, shi, qw_tile, kw_tile, sink_tab)


def _attn_sample_kernel(q_ref, k_ref, v_ref, ck_ref, cv_ref, cos_ref, slo_ref, shi_ref, qw_ref, kw_ref,
                        sink_ref, o_ref, nk_ref, nv_ref):
    tq = q_ref.shape[1]
    new_valid = lax.broadcasted_iota(jnp.int32, (2 * tq, LANES), 1) % HEAD_DIM < tq
    pad = jnp.zeros((CHUNK - tq, KV_W), F32)
    for s in range(q_ref.shape[0]):
        k_fin = _prep_k(k_ref[s], kw_ref, cos_ref, slo_ref, shi_ref, tq)
        v_cur = v_ref[s].astype(F32)
        nk_ref[s, 0:WINDOW - tq, :] = ck_ref[s, tq:WINDOW, :]
        nv_ref[s, 0:WINDOW - tq, :] = cv_ref[s, tq:WINDOW, :]
        nk_ref[s, WINDOW - tq:WINDOW, :] = k_fin
        nv_ref[s, WINDOW - tq:WINDOW, :] = v_cur
        wins = [(ck_ref[s, :CHUNK, :], cv_ref[s, :CHUNK, :]),
                (ck_ref[s, CHUNK:, :], cv_ref[s, CHUNK:, :]),
                (jnp.concatenate([k_fin, pad], axis=0), jnp.concatenate([v_cur, pad], axis=0))]
        tiles = [_kv_tiles(kw, vw) for kw, vw in wins]
        q_tiles = _prep_q(q_ref.at[s], qw_ref, cos_ref, slo_ref, shi_ref, tq)
        _attend(q_tiles,
                lambda h, tiles=tiles: jnp.concatenate([tiles[w][0][h] for w in range(3)], axis=0),
                lambda h, tiles=tiles: jnp.concatenate([tiles[w][1][h] for w in range(3)], axis=0),
                lambda w: new_valid if w == 2 else True,
                sink_ref, o_ref.at[s], tq)


def _attn_sample(proj3, cache_k, cache_v, rope_tabs, qw_tile, kw_tile, sink_tab, ns):
    batch, tq, _ = proj3.shape
    cos_t, slo, shi = rope_tabs
    tab_spec = pl.BlockSpec((tq, LANES), lambda g: (0, 0))
    const = lambda shape: pl.BlockSpec(shape, lambda g: (0,) * len(shape))
    cache_spec = pl.BlockSpec((ns, WINDOW, KV_W), lambda g: (g, 0, 0))
    return pl.pallas_call(
        _attn_sample_kernel,
        grid=(batch // ns,),
        in_specs=[
            pl.BlockSpec((ns, tq, ATTN_W), lambda g: (g, 0, COL_Q // ATTN_W)),
            pl.BlockSpec((ns, tq, KV_W), lambda g: (g, 0, COL_K // KV_W)),
            pl.BlockSpec((ns, tq, KV_W), lambda g: (g, 0, COL_V // KV_W)),
            cache_spec, cache_spec,
            tab_spec, tab_spec, tab_spec,
            const((1, LANES)), const((1, LANES)), const((Q_TILES, LANES)),
        ],
        out_specs=[pl.BlockSpec((ns, tq, ATTN_W), lambda g: (g, 0, 0)), cache_spec, cache_spec],
        out_shape=[
            jax.ShapeDtypeStruct((batch, tq, ATTN_W), BF16),
            jax.ShapeDtypeStruct((batch, WINDOW, KV_W), F32),
            jax.ShapeDtypeStruct((batch, WINDOW, KV_W), F32),
        ],
        compiler_params=_params("arbitrary"),
        name="attn_sample",
    )(proj3, proj3, proj3, cache_k, cache_v, cos_t, slo, shi, qw_tile, kw_tile, sink_tab)


def _pad_rows(a, rows):
    if a.shape[0] == rows:
        return a
    return jnp.concatenate([a, jnp.zeros((rows - a.shape[0], a.shape[1]), a.dtype)], axis=0)


def _ssd_chunk(c, xs0_ref, xs1_ref, z0_ref, z1_ref, b_ref, c_ref, dt_ref, cst_ref, cw_ref, cb_ref, dtb_ref,
               alog_ref, dskip_ref, nw_ref, sel_ref, shift_ref, y_ref, ext_scr, st_scr, L):
    half = SSM_W // 2
    ext_scr[L:2 * L, 0:half] = xs0_ref[...]
    ext_scr[L:2 * L, half:SSM_W] = xs1_ref[...]
    ext_scr[L:2 * L, SSM_W:SSM_W + BC_W] = b_ref[...]
    ext_scr[L:2 * L, SSM_W + BC_W:CONV_CH] = c_ref[...]
    delayed = _dot(shift_ref[...], ext_scr[...])
    cur = ext_scr[L:2 * L, :].astype(F32)
    conv = cb_ref[...] + cur * cw_ref[CONV_W - 1:CONV_W, :]
    for i in range(CONV_W - 1):
        conv = conv + delayed[L * i:L * (i + 1)] * cw_ref[i:i + 1, :]
    s = cst_ref[...]
    w0, w1, w2 = cw_ref[0:1, :], cw_ref[1:2, :], cw_ref[2:3, :]
    head = jnp.concatenate([w0 * s[0:1] + w1 * s[1:2] + w2 * s[2:3], w0 * s[1:2] + w1 * s[2:3], w0 * s[2:3],
                            jnp.zeros((8 - (CONV_W - 1), CONV_CH), F32)], axis=0)
    conv = jnp.concatenate([conv[0:8] + jnp.where(c == 0, head, 0.0), conv[8:]], axis=0)
    conv = _silu(conv)
    xs = conv[:, :SSM_W]
    ext_scr[0:L, :] = ext_scr[L:2 * L, :]

    x_dt = dt_ref[...] + dtb_ref[...]
    dt = jnp.maximum(x_dt, 0.0) + jnp.log1p(jnp.exp(-jnp.abs(x_dt)))
    ad = dt * (-jnp.exp(alog_ref[...]))
    li = lax.broadcasted_iota(jnp.int32, (L, L), 0)
    si = lax.broadcasted_iota(jnp.int32, (L, L), 1)
    tril = (si <= li).astype(BF16)
    a_cum = _dot(jnp.concatenate([tril] * 3, axis=1), jnp.concatenate(_split3(ad), axis=0))
    sel = sel_ref[...]
    ex = _dot(jnp.concatenate(_split3(a_cum) + _split3(dt), axis=0), sel)
    col = ex[0:L] + ex[L:2 * L] + ex[2 * L:3 * L]
    dt_all = ex[3 * L:4 * L] + ex[4 * L:5 * L] + ex[5 * L:6 * L]
    lane_s = lax.broadcasted_iota(jnp.int32, (L, SSM_W), 1) % SSM_HEAD_DIM
    row_l = lax.broadcasted_iota(jnp.int32, (L, SSM_W), 0)
    diag = jnp.where(lane_s == row_l, col, 0.0)
    row = _dot(jnp.ones((L, 3 * L), BF16), jnp.concatenate(_split3(diag), axis=0))
    lmat = jnp.exp(jnp.where(lane_s <= row_l, col - row, NEG_INF))
    a_last = col[L - 1:L, :]
    exp_a = jnp.exp(col)
    decay = jnp.exp(a_last - col)
    exp_last = jnp.exp(a_last)

    xd = xs * dt_all
    xdd = xd * decay
    bi = lax.broadcasted_iota(jnp.int32, (GROUP_W, GROUP_W), 0) // SSM_HEAD_DIM
    bj = lax.broadcasted_iota(jnp.int32, (GROUP_W, GROUP_W), 1) // SSM_HEAD_DIM
    blockdiag = bi == bj

    for g in range(N_GROUPS):
        gs = slice(GROUP_W * g, GROUP_W * (g + 1))
        b_g = conv[:, SSM_W + D_STATE * g:SSM_W + D_STATE * (g + 1)]
        c_g = conv[:, SSM_W + BC_W + D_STATE * g:SSM_W + BC_W + D_STATE * (g + 1)].astype(BF16)
        b_pad = _pad_rows(b_g, CHUNK)
        cb = _dot_nt(c_g, jnp.concatenate([b_pad.astype(BF16)] * 4, axis=0))
        m_g = (cb * lmat[:, gs]).astype(BF16)
        xd_g = _pad_rows(xd[:, gs], CHUNK).astype(BF16)
        xd_bd = jnp.where(blockdiag, jnp.concatenate([xd_g] * 4, axis=0), jnp.zeros((), BF16))
        y_diag = _dot(m_g, xd_bd)
        st_g = st_scr[:, gs]
        y_off = _dot(c_g, st_g.astype(BF16)) * exp_a[:, gs]
        new_st = _dot(b_pad.T.astype(BF16), _pad_rows(xdd[:, gs], CHUNK).astype(BF16))
        st_scr[:, gs] = st_g * exp_last[:, gs] + new_st

        y = y_diag + y_off + xs[:, gs] * dskip_ref[:, gs]
        z_ref = z0_ref if g < N_GROUPS // 2 else z1_ref
        zs = slice(GROUP_W * (g % (N_GROUPS // 2)), GROUP_W * (g % (N_GROUPS // 2) + 1))
        gated = y * _silu(z_ref[:, zs].astype(F32))
        ms = jnp.mean(gated * gated, axis=-1, keepdims=True)
        y_ref[:, gs] = (gated * lax.rsqrt(ms + RMS_EPS) * nw_ref[:, gs]).astype(y_ref.dtype)


def _ssd_kernel(xs0_ref, xs1_ref, z0_ref, z1_ref, b_ref, c_ref, dt_ref, cst_ref, st0_ref, cw_ref, cb_ref,
                dtb_ref, alog_ref, dskip_ref, nw_ref, sel_ref, shift_ref, y_ref, ncv_ref, nst_ref,
                ext_scr, st_scr):
    c = pl.program_id(1)
    nc = pl.num_programs(1)
    ns, L = dt_ref.shape[0], dt_ref.shape[1]

    @pl.when(c == 0)
    def _():
        for s in range(ns):
            ext_scr[s, 0:L, :] = jnp.zeros((L, CONV_CH), BF16)
            st_scr[s] = st0_ref[s].T

    for s in range(ns):
        _ssd_chunk(c, xs0_ref.at[s], xs1_ref.at[s], z0_ref.at[s], z1_ref.at[s], b_ref.at[s], c_ref.at[s],
                   dt_ref.at[s], cst_ref.at[s], cw_ref, cb_ref, dtb_ref, alog_ref, dskip_ref, nw_ref,
                   sel_ref, shift_ref, y_ref.at[s], ext_scr.at[s], st_scr.at[s], L)

    @pl.when(c == nc - 1)
    def _():
        for s in range(ns):
            tail = ext_scr[s, L - BF16_ROWS:L, :].astype(F32)
            ncv_ref[s] = tail[BF16_ROWS - (CONV_W - 1):BF16_ROWS, :]
            nst_ref[s] = st_scr[s].T


def _shift_matrix(L):
    r = jnp.arange(3 * L)
    src = L + r % L - (CONV_W - 1) + r // L
    return (jnp.arange(2 * L)[None, :] == src[:, None]).astype(BF16)


def _ssd(proj3, dt3, conv_state, ssm_state, conv_w, conv_b, dtb, alog, dskip_all, ssm_nw, sel, L, ns):
    batch, t, _ = proj3.shape
    nc = t // L
    const = lambda shape: pl.BlockSpec(shape, lambda g, c: (0,) * len(shape))
    per_g = lambda shape: pl.BlockSpec((ns,) + shape, lambda g, c: (g,) + (0,) * len(shape))
    col = lambda start: pl.BlockSpec((ns, L, BC_W), lambda g, c: (g, c, start // BC_W))
    hp = N_SSM_HEADS * SSM_HEAD_DIM
    return pl.pallas_call(
        _ssd_kernel,
        grid=(batch // ns, nc),
        in_specs=[
            col(COL_XS), col(COL_XS + BC_W), col(COL_Z), col(COL_Z + BC_W), col(COL_B), col(COL_C),
            pl.BlockSpec((ns, L, LANES), lambda g, c: (g, c, 0)),
            per_g((CONV_W - 1, CONV_CH)),
            per_g((hp, D_STATE)),
            const((CONV_W, CONV_CH)), const((1, CONV_CH)), const((1, LANES)), const((1, LANES)),
            const((1, SSM_W)), const((1, SSM_W)), const((LANES, SSM_W)), const((3 * L, 2 * L)),
        ],
        out_specs=[
            pl.BlockSpec((ns, L, SSM_W), lambda g, c: (g, c, 0)),
            per_g((CONV_W - 1, CONV_CH)),
            per_g((hp, D_STATE)),
        ],
        out_shape=[
            jax.ShapeDtypeStruct((batch, t, SSM_W), BF16),
            jax.ShapeDtypeStruct((batch, CONV_W - 1, CONV_CH), F32),
            jax.ShapeDtypeStruct((batch, hp, D_STATE), F32),
        ],
        scratch_shapes=[
            pltpu.VMEM((ns, 2 * L, CONV_CH), BF16),
            pltpu.VMEM((ns, D_STATE, hp), F32),
        ],
        compiler_params=_params("arbitrary", "arbitrary"),
        name="ssd",
    )(proj3, proj3, proj3, proj3, proj3, proj3, dt3, conv_state, ssm_state, conv_w, conv_b, dtb, alog,
      dskip_all, ssm_nw, sel, _shift_matrix(L))


def _outproj_kernel(*refs, n_cast):
    a_ref, y_ref, wa_ref, wy_ref, x_ref = refs[:5]
    cast_in, o_ref, cast_out = refs[5:5 + n_cast], refs[5 + n_cast], refs[6 + n_cast:]
    o_ref[...] = x_ref[...] + _dot(a_ref[...], wa_ref[...]) + _dot(y_ref[...], wy_ref[...])
    _cast_blocks(cast_in, cast_out)


def _outproj(attn, y, w_out, x2d, tm, tn, cast=()):
    t = x2d.shape[0]
    half = D_MODEL // 2
    ni = t // tm
    c_in, c_out, c_shape = _cast_riders(cast, (D_MODEL // tn) * ni, lambda j, i: j * ni + i)
    return pl.pallas_call(
        functools.partial(_outproj_kernel, n_cast=len(cast)),
        grid=(D_MODEL // tn, ni),
        in_specs=[
            pl.BlockSpec((tm, half), lambda j, i: (i, 0)),
            pl.BlockSpec((tm, half), lambda j, i: (i, 0)),
            pl.BlockSpec((half, tn), lambda j, i: (0, j)),
            pl.BlockSpec((half, tn), lambda j, i: (1, j)),
            pl.BlockSpec((tm, tn), lambda j, i: (i, j)),
        ] + c_in,
        out_specs=[pl.BlockSpec((tm, tn), lambda j, i: (i, j))] + c_out,
        out_shape=[jax.ShapeDtypeStruct((t, D_MODEL), F32)] + c_shape,
        compiler_params=_params("arbitrary", "arbitrary"),
        name="outproj",
    )(attn, y, w_out, w_out, x2d, *cast)


def _ffn_kernel(x_ref, nw_ref, wu_ref, wd_ref, o_ref, h_scr, *, n_split):
    @pl.when(pl.program_id(1) == 0)
    def _():
        _rmsnorm_rows(x_ref, nw_ref, h_scr, copy_ref=o_ref)

    u = jnp.maximum(_dot(h_scr[...], wu_ref[...]), 0.0)
    u = (u * u).astype(BF16)
    wn = D_MODEL // n_split
    for n in range(n_split):
        o_ref[:, wn * n:wn * (n + 1)] += _dot(u, wd_ref[:, wn * n:wn * (n + 1)])


def _ffn(x2d, norm_w, w_up, w_down, tm, tf):
    t = x2d.shape[0]
    return pl.pallas_call(
        functools.partial(_ffn_kernel, n_split=4),
        grid=(t // tm, FFN_HIDDEN // tf),
        in_specs=[
            pl.BlockSpec((tm, D_MODEL), lambda i, f: (i, 0)),
            pl.BlockSpec((1, D_MODEL), lambda i, f: (0, 0)),
            pl.BlockSpec((D_MODEL, tf), lambda i, f: (0, f)),
            pl.BlockSpec((tf, D_MODEL), lambda i, f: (f, 0)),
        ],
        out_specs=pl.BlockSpec((tm, D_MODEL), lambda i, f: (i, 0)),
        out_shape=jax.ShapeDtypeStruct((t, D_MODEL), F32),
        scratch_shapes=[pltpu.VMEM((tm, D_MODEL), BF16)],
        compiler_params=_params("arbitrary", "arbitrary"),
        name="ffn",
    )(x2d, norm_w, w_up, w_down)


def _rope_tables(pos):
    inv_freq = ROPE_THETA ** (-jnp.arange(ROPE_HALF, dtype=F32) / ROPE_HALF)
    lane = jnp.arange(LANES) % HEAD_DIM
    freq = jnp.where(lane < ROPE_DIM, inv_freq[lane % ROPE_HALF], 0.0)
    ang = pos.astype(F32)[:, None] * freq[None, :]
    cos, sin = jnp.cos(ang), jnp.sin(ang)
    return cos, jnp.where(lane < ROPE_HALF, -sin, 0.0), jnp.where(lane >= ROPE_HALF, sin, 0.0)


def _layer_params(norm_mix_w, w_in, q_norm_w, k_norm_w, attn_sinks, conv_w, conv_b, dt_bias, a_log, d_skip,
                  ssm_norm_w, w_out, norm_ffn_w, w_up, w_down):
    pad_h = lambda a: jnp.pad(a.astype(F32), (0, LANES - N_SSM_HEADS))[None, :]
    sel = (jnp.arange(LANES)[:, None] == (jnp.arange(SSM_W) // SSM_HEAD_DIM)[None, :]).astype(BF16)
    return dict(
        norm_mix=norm_mix_w.astype(F32)[None, :], w_in=w_in.astype(BF16),
        qw=jnp.tile(q_norm_w.astype(F32), 2)[None, :], kw=jnp.tile(k_norm_w.astype(F32), 2)[None, :],
        sinks=jnp.repeat(attn_sinks.astype(F32), HEAD_DIM).reshape(Q_TILES, LANES),
        conv_w=conv_w.astype(F32), conv_b=conv_b.astype(F32)[None, :],
        dtb=pad_h(dt_bias), alog=pad_h(a_log),
        dskip=jnp.repeat(d_skip.astype(F32), SSM_HEAD_DIM)[None, :],
        ssm_nw=ssm_norm_w.astype(F32)[None, :], sel=sel,
        norm_ffn=norm_ffn_w.astype(F32)[None, :],
    )


def _stream(x, pos, cache, conv_state, ssm_state, p, tm, w_out, w_up, w_down):
    batch, t, _ = x.shape
    ns = max(n for n in STREAMS_PER_STEP if batch % n == 0)
    x2d = x.reshape(batch * t, D_MODEL)
    early = tuple(w for w in (w_out, w_up) if w.dtype != BF16)
    proj, dt_raw, *cast = _inproj(x2d, p["norm_mix"], p["w_in"], tm, 1024, cast=early)
    if w_up.dtype != BF16:
        w_up = cast.pop()
    if w_out.dtype != BF16:
        w_out = cast.pop()
    proj3 = proj.reshape(batch, t, PROJ_W)
    tabs = _rope_tables(pos)
    if cache is None:
        attn, new_k, new_v = _attn_prompt(proj3, tabs, p["qw"], p["kw"], p["sinks"], ns)
        L = CHUNK
    else:
        past_k, past_v = cache
        attn, new_k, new_v = _attn_sample(proj3, past_k.reshape(batch, WINDOW, KV_W),
                                          past_v.reshape(batch, WINDOW, KV_W), tabs, p["qw"], p["kw"],
                                          p["sinks"], ns)
        L = min(CHUNK, t)
    hp = N_SSM_HEADS * SSM_HEAD_DIM
    y, new_conv, new_ssm = _ssd(proj3, dt_raw.reshape(batch, t, LANES), conv_state,
                                ssm_state.reshape(batch, hp, D_STATE), p["conv_w"], p["conv_b"], p["dtb"],
                                p["alog"], p["dskip"], p["ssm_nw"], p["sel"], L, ns)
    late = () if w_down.dtype == BF16 else (w_down,)
    x1, *cast = _outproj(attn.reshape(batch * t, ATTN_W), y.reshape(batch * t, SSM_W), w_out, x2d, tm, 1024,
                         cast=late)
    if late:
        w_down = cast.pop()
    out = _ffn(x1, p["norm_ffn"], w_up, w_down, tm, 512)
    return (out.reshape(batch, t, D_MODEL),
            new_k.reshape(batch, WINDOW, N_KV_HEADS, HEAD_DIM), new_v.reshape(batch, WINDOW, N_KV_HEADS, HEAD_DIM),
            new_conv, new_ssm.reshape(batch, N_SSM_HEADS, SSM_HEAD_DIM, D_STATE), (w_out, w_up, w_down))


def kernel(x_prompt, x_sample, cache_k, cache_v, state_conv, state_ssm, norm_mix_w, w_in, q_norm_w, k_norm_w,
           attn_sinks, conv_w, conv_b, dt_bias, a_log, d_skip, ssm_norm_w, w_out, norm_ffn_w, w_up, w_down):
    depth = w_in.shape[0]
    b_p, t_p, _ = x_prompt.shape
    b_s, t_s, _ = x_sample.shape
    assert cache_k.shape[2] == WINDOW and t_p % CHUNK == 0 and t_p >= WINDOW
    assert t_s <= CHUNK and t_s % BF16_ROWS == 0
    pos_p = jnp.arange(t_p, dtype=jnp.int32)
    pos_s = PAST_LEN + jnp.arange(t_s, dtype=jnp.int32)
    hp, hs = x_prompt, x_sample
    outs_p, outs_s = [], []
    for layer in range(depth):
        p = _layer_params(norm_mix_w[layer], w_in[layer], q_norm_w[layer], k_norm_w[layer], attn_sinks[layer],
                          conv_w[layer], conv_b[layer], dt_bias[layer], a_log[layer], d_skip[layer],
                          ssm_norm_w[layer], w_out[layer], norm_ffn_w[layer], w_up[layer], w_down[layer])
        conv0 = jnp.zeros((b_p, CONV_W - 1, CONV_CH), F32)
        ssm0 = jnp.zeros((b_p, N_SSM_HEADS, SSM_HEAD_DIM, D_STATE), F32)
        hp, *rest_p, w16 = _stream(hp, pos_p, None, conv0, ssm0, p, 512, w_out[layer], w_up[layer], w_down[layer])
        hs, *rest_s, _ = _stream(hs, pos_s, (cache_k[layer], cache_v[layer]), state_conv[layer],
                                 state_ssm[layer], p, 512, *w16)
        outs_p.append(rest_p)
        outs_s.append(rest_s)
    stack = lambda outs, i: jnp.stack([o[i] for o in outs])
    return (hp, hs, stack(outs_p, 0), stack(outs_p, 1), stack(outs_p, 2), stack(outs_p, 3),
            stack(outs_s, 0), stack(outs_s, 1), stack(outs_s, 2), stack(outs_s, 3))
```

```python
import functools

import jax
import jax.numpy as jnp
from jax import lax
from jax.experimental import pallas as pl
from jax.experimental.pallas import tpu as pltpu

F32 = jnp.float32
BF16 = jnp.bfloat16

D_MODEL = 4096
HEAD_DIM = 64
N_Q_HEADS = 32
N_KV_HEADS = 8
ATTN_W = N_Q_HEADS * HEAD_DIM
KV_W = N_KV_HEADS * HEAD_DIM
ROPE_DIM = 16
ROPE_HALF = ROPE_DIM // 2
ROPE_THETA = 500000.0
ATTN_SCALE = HEAD_DIM ** -0.5
WINDOW = 128
CHUNK = 64
PAST_LEN = 1024
SSM_W = 2048
N_SSM_HEADS = 32
SSM_HEAD_DIM = 64
N_GROUPS = 8
D_STATE = 128
BC_W = N_GROUPS * D_STATE
GROUP_W = SSM_W // N_GROUPS
CONV_W = 4
CONV_CH = SSM_W + 2 * BC_W
FFN_HIDDEN = 4 * D_MODEL
RMS_EPS = 1e-6
NEG_INF = -1e30
LANES = 128
BF16_ROWS = 16
Q_TILES = ATTN_W // LANES
KV_TILES = KV_W // LANES
PROJ_W = ATTN_W + 2 * KV_W + 2 * SSM_W + 2 * BC_W
COL_Q, COL_K, COL_V, COL_XS, COL_Z, COL_B, COL_C = 0, 2048, 2560, 3072, 5120, 7168, 8192
VMEM_LIMIT = 56 * 1024 * 1024
STREAMS_PER_STEP = (1, 2, 4)


def _params(*sem):
    return pltpu.CompilerParams(dimension_semantics=sem, vmem_limit_bytes=VMEM_LIMIT)


def _dot(a, b):
    return jnp.dot(a, b, preferred_element_type=F32)


def _dot_nt(a, b):
    return lax.dot_general(a, b, (((1,), (1,)), ((), ())), preferred_element_type=F32)


def _split3(x):
    a = x.astype(BF16)
    r = x - a.astype(F32)
    b = r.astype(BF16)
    c = (r - b.astype(F32)).astype(BF16)
    return a, b, c


def _silu(x):
    return x / (1.0 + jnp.exp(-x))


NORM_ROWS = 16
NORM_UNROLL = 8


def _rmsnorm_rows(x_ref, nw_ref, h_scr, copy_ref=None):
    def body(i, carry):
        r = pl.ds(pl.multiple_of(i * NORM_ROWS, NORM_ROWS), NORM_ROWS)
        x = x_ref[r, :]
        ms = jnp.mean(x * x, axis=-1, keepdims=True)
        h_scr[r, :] = (x * lax.rsqrt(ms + RMS_EPS) * nw_ref[...]).astype(BF16)
        if copy_ref is not None:
            copy_ref[r, :] = x
        return carry

    lax.fori_loop(0, x_ref.shape[0] // NORM_ROWS, body, 0, unroll=NORM_UNROLL)


def _cast_riders(arrays, steps, linear_step):
    in_specs, out_specs, out_shapes = [], [], []
    for a in arrays:
        rows, cols = a.shape
        nb = max(n for n in range(1, steps + 1) if rows % n == 0 and (rows // n) % BF16_ROWS == 0)
        spec = pl.BlockSpec((rows // nb, cols), lambda *g, nb=nb: (jnp.minimum(linear_step(*g), nb - 1), 0))
        in_specs.append(spec)
        out_specs.append(spec)
        out_shapes.append(jax.ShapeDtypeStruct(a.shape, BF16))
    return in_specs, out_specs, out_shapes


def _cast_blocks(src_refs, dst_refs):
    for src, dst in zip(src_refs, dst_refs):
        dst[...] = src[...].astype(BF16)


def _inproj_kernel(*refs, n_cast, emit_w16):
    n_out = 2 + (2 if emit_w16 else 0)
    x_ref, nw_ref, w_ref, wdt_ref = refs[:4]
    cast_in = refs[4:4 + n_cast]
    outs = refs[4 + n_cast:4 + n_cast + n_out]
    cast_out, h_scr = refs[4 + n_cast + n_out:4 + 2 * n_cast + n_out], refs[4 + 2 * n_cast + n_out]
    proj_ref, dt_ref = outs[:2]

    @pl.when(pl.program_id(1) == 0)
    def _():
        _rmsnorm_rows(x_ref, nw_ref, h_scr)
        real = lax.broadcasted_iota(jnp.int32, wdt_ref.shape, 1) < N_SSM_HEADS
        wdt = jnp.where(real, wdt_ref[...], jnp.zeros((), wdt_ref.dtype)).astype(BF16)
        dt_ref[...] = _dot(h_scr[...], wdt)
        if emit_w16:
            outs[3][...] = wdt

    w = w_ref[...].astype(BF16)
    if emit_w16:
        outs[2][...] = w
    proj_ref[...] = _dot(h_scr[...], w).astype(BF16)
    _cast_blocks(cast_in, cast_out)


def _inproj(x2d, norm_w, w_main, w_tail, tm, tn, cast=()):
    t = x2d.shape[0]
    nj = PROJ_W // tn
    emit_w16 = w_main.dtype != BF16
    tail_block = PROJ_W // LANES if w_tail.shape[1] > LANES else 0
    c_in, c_out, c_shape = _cast_riders(cast, (t // tm) * nj, lambda i, j: i * nj + j)
    w16_specs = [pl.BlockSpec((D_MODEL, tn), lambda i, j: (0, j)), pl.BlockSpec((D_MODEL, LANES), lambda i, j: (0, 0))]
    w16_shapes = [jax.ShapeDtypeStruct((D_MODEL, PROJ_W), BF16), jax.ShapeDtypeStruct((D_MODEL, LANES), BF16)]
    return pl.pallas_call(
        functools.partial(_inproj_kernel, n_cast=len(cast), emit_w16=emit_w16),
        grid=(t // tm, nj),
        in_specs=[
            pl.BlockSpec((tm, D_MODEL), lambda i, j: (i, 0)),
            pl.BlockSpec((1, D_MODEL), lambda i, j: (0, 0)),
            pl.BlockSpec((D_MODEL, tn), lambda i, j: (0, j)),
            pl.BlockSpec((D_MODEL, LANES), lambda i, j: (0, tail_block)),
        ] + c_in,
        out_specs=[
            pl.BlockSpec((tm, tn), lambda i, j: (i, j)),
            pl.BlockSpec((tm, LANES), lambda i, j: (i, 0)),
        ] + (w16_specs if emit_w16 else []) + c_out,
        out_shape=[
            jax.ShapeDtypeStruct((t, PROJ_W), BF16),
            jax.ShapeDtypeStruct((t, LANES), F32),
        ] + (w16_shapes if emit_w16 else []) + c_shape,
        scratch_shapes=[pltpu.VMEM((tm, D_MODEL), BF16)],
        compiler_params=_params("arbitrary", "arbitrary"),
        name="inproj",
    )(x2d, norm_w, w_main, w_tail, *cast)


def _half_ones():
    r = lax.broadcasted_iota(jnp.int32, (LANES, LANES), 0) // HEAD_DIM
    c = lax.broadcasted_iota(jnp.int32, (LANES, LANES), 1) // HEAD_DIM
    return (r == c).astype(BF16)


def _head_norm(x, w_tile):
    ss = _dot((x * x).astype(BF16), _half_ones())
    return x * lax.rsqrt(ss * (1.0 / HEAD_DIM) + RMS_EPS) * w_tile


def _rope(x, cos_t, sin_lo, sin_hi):
    return (x * cos_t + pltpu.roll(x, LANES - ROPE_HALF, 1) * sin_lo
            + pltpu.roll(x, ROPE_HALF, 1) * sin_hi)


def _lane_is_a(shape):
    return lax.broadcasted_iota(jnp.int32, shape, 1) < HEAD_DIM


def _kv_tiles(k_win, v_win):
    is_a = _lane_is_a((CHUNK, LANES))
    ones_a = jnp.where(is_a, 1.0, 0.0).astype(F32)
    ones_b = 1.0 - ones_a
    kks, vvs = [], []
    for h in range(N_KV_HEADS):
        j, odd = h // 2, h % 2
        kt = k_win[:, LANES * j:LANES * (j + 1)]
        vt = v_win[:, LANES * j:LANES * (j + 1)]
        keep = jnp.logical_not(is_a) if odd else is_a
        k_own = jnp.where(keep, kt, 0.0)
        v_own = jnp.where(keep, vt, 0.0)
        k_sw = pltpu.roll(k_own, HEAD_DIM, 1)
        v_sw = pltpu.roll(v_own, HEAD_DIM, 1)
        k_a, k_b = (k_sw, k_own) if odd else (k_own, k_sw)
        v_a, v_b = (v_sw, v_own) if odd else (v_own, v_sw)
        kks.append(jnp.concatenate([k_a, k_b], axis=0).astype(BF16))
        vv = jnp.concatenate([jnp.concatenate([v_a, ones_a], axis=1),
                              jnp.concatenate([v_b, ones_b], axis=1)], axis=0)
        vvs.append(vv.astype(BF16))
    return kks, vvs


def _prep_q(q_ref, qw_ref, cos_ref, slo_ref, shi_ref, tq):
    q2 = jnp.concatenate([q_ref[:, LANES * i:LANES * (i + 1)].astype(F32) for i in range(Q_TILES)], axis=0)
    q2 = _head_norm(q2, qw_ref[...])
    cos_t, slo, shi = cos_ref[...], slo_ref[...], shi_ref[...]
    return [(_rope(q2[tq * i:tq * (i + 1)], cos_t, slo, shi) * ATTN_SCALE).astype(BF16) for i in range(Q_TILES)]


def _attend(q_tiles, kk_of, vv_of, valid_of, sink_ref, o_ref, tq):
    is_a = _lane_is_a((2 * tq, LANES))
    for h in range(N_KV_HEADS):
        qh = jnp.concatenate([q_tiles[2 * h], q_tiles[2 * h + 1]], axis=0)
        s = _dot_nt(qh, kk_of(h))
        sw = [jnp.where(valid_of(w), s[:, LANES * w:LANES * (w + 1)], NEG_INF) for w in range(3)]
        mt = jnp.maximum(jnp.maximum(sw[0], sw[1]), sw[2])
        m_a = jnp.max(jnp.where(is_a, mt, NEG_INF), axis=-1, keepdims=True)
        m_b = jnp.max(jnp.where(is_a, NEG_INF, mt), axis=-1, keepdims=True)
        sink = jnp.concatenate([jnp.broadcast_to(sink_ref[2 * h:2 * h + 1, :], (tq, LANES)),
                                jnp.broadcast_to(sink_ref[2 * h + 1:2 * h + 2, :], (tq, LANES))], axis=0)
        m = jnp.maximum(jnp.where(is_a, m_a, m_b), sink)
        e = jnp.concatenate([jnp.exp(x - m) for x in sw], axis=1).astype(BF16)
        oa = _dot(e, vv_of(h))
        o = oa[:, :LANES] / (oa[:, LANES:] + jnp.exp(sink - m))
        o_ref[:, LANES * 2 * h:LANES * (2 * h + 1)] = o[:tq].astype(o_ref.dtype)
        o_ref[:, LANES * (2 * h + 1):LANES * (2 * h + 2)] = o[tq:].astype(o_ref.dtype)


def _prep_k(k_raw, kw_ref, cos_ref, slo_ref, shi_ref, rows):
    k2 = jnp.concatenate([k_raw[:, LANES * j:LANES * (j + 1)].astype(F32) for j in range(KV_TILES)], axis=0)
    k2 = _head_norm(k2, kw_ref[...])
    cos_t, slo, shi = cos_ref[...], slo_ref[...], shi_ref[...]
    return jnp.concatenate([_rope(k2[rows * j:rows * (j + 1)], cos_t, slo, shi) for j in range(KV_TILES)], axis=1)


def _attn_prompt_kernel(q_ref, k_ref, v_ref, cos_ref, slo_ref, shi_ref, qw_ref, kw_ref, sink_ref,
                        o_ref, nk_ref, nv_ref, kk_scr, vv_scr):
    c = pl.program_id(1)

    @pl.when(c == 0)
    def _():
        kk_scr[...] = jnp.zeros_like(kk_scr)
        vv_scr[...] = jnp.zeros_like(vv_scr)

    slot = c % 3
    for s in range(q_ref.shape[0]):
        k_fin = _prep_k(k_ref[s], kw_ref, cos_ref, slo_ref, shi_ref, CHUNK)
        v_cur = v_ref[s].astype(F32)
        nk_ref[s] = k_fin
        nv_ref[s] = v_cur
        kks, vvs = _kv_tiles(k_fin, v_cur)
        for h in range(N_KV_HEADS):
            kk_scr[s, slot, h] = kks[h]
            vv_scr[s, slot, h] = vvs[h]

        q_tiles = _prep_q(q_ref.at[s], qw_ref, cos_ref, slo_ref, shi_ref, CHUNK)
        _attend(q_tiles,
                lambda h, s=s: jnp.concatenate([kk_scr[s, w, h] for w in range(3)], axis=0),
                lambda h, s=s: jnp.concatenate([vv_scr[s, w, h] for w in range(3)], axis=0),
                lambda w: (c - w + 3) % 3 <= c,
                sink_ref, o_ref.at[s], CHUNK)


def _attn_prompt(proj3, rope_tabs, qw_tile, kw_tile, sink_tab, ns):
    batch, t, _ = proj3.shape
    nc = t // CHUNK
    keep = WINDOW // CHUNK
    cos_t, slo, shi = rope_tabs
    tab_spec = pl.BlockSpec((CHUNK, LANES), lambda g, c: (c, 0))
    const = lambda shape: pl.BlockSpec(shape, lambda g, c: (0,) * len(shape))
    keep_spec = pl.BlockSpec((ns, CHUNK, KV_W), lambda g, c: (g, jnp.maximum(c - (nc - keep), 0), 0))
    return pl.pallas_call(
        _attn_prompt_kernel,
        grid=(batch // ns, nc),
        in_specs=[
            pl.BlockSpec((ns, CHUNK, ATTN_W), lambda g, c: (g, c, COL_Q // ATTN_W)),
            pl.BlockSpec((ns, CHUNK, KV_W), lambda g, c: (g, c, COL_K // KV_W)),
            pl.BlockSpec((ns, CHUNK, KV_W), lambda g, c: (g, c, COL_V // KV_W)),
            tab_spec, tab_spec, tab_spec,
            const((1, LANES)), const((1, LANES)), const((Q_TILES, LANES)),
        ],
        out_specs=[pl.BlockSpec((ns, CHUNK, ATTN_W), lambda g, c: (g, c, 0)), keep_spec, keep_spec],
        out_shape=[
            jax.ShapeDtypeStruct((batch, t, ATTN_W), BF16),
            jax.ShapeDtypeStruct((batch, WINDOW, KV_W), F32),
            jax.ShapeDtypeStruct((batch, WINDOW, KV_W), F32),
        ],
        scratch_shapes=[
            pltpu.VMEM((ns, 3, N_KV_HEADS, 2 * CHUNK, LANES), BF16),
            pltpu.VMEM((ns, 3, N_KV_HEADS, 2 * CHUNK, 2 * LANES), BF16),
        ],
        compiler_params=_params("arbitrary", "arbitrary"),
        name="attn_prompt",
    )(proj3, proj3, proj3, cos_t, slo, shi, qw_tile, kw_tile, sink_tab)


def _attn_sample_kernel(q_ref, k_ref, v_ref, ck_ref, cv_ref, cos_ref, slo_ref, shi_ref, qw_ref, kw_ref,
                        sink_ref, o_ref, nk_ref, nv_ref):
    tq = q_ref.shape[1]
    new_valid = lax.broadcasted_iota(jnp.int32, (2 * tq, LANES), 1) % HEAD_DIM < tq
    pad = jnp.zeros((CHUNK - tq, KV_W), F32)
    for s in range(q_ref.shape[0]):
        k_fin = _prep_k(k_ref[s], kw_ref, cos_ref, slo_ref, shi_ref, tq)
        v_cur = v_ref[s].astype(F32)
        nk_ref[s, 0:WINDOW - tq, :] = ck_ref[s, tq:WINDOW, :]
        nv_ref[s, 0:WINDOW - tq, :] = cv_ref[s, tq:WINDOW, :]
        nk_ref[s, WINDOW - tq:WINDOW, :] = k_fin
        nv_ref[s, WINDOW - tq:WINDOW, :] = v_cur
        wins = [(ck_ref[s, :CHUNK, :], cv_ref[s, :CHUNK, :]),
                (ck_ref[s, CHUNK:, :], cv_ref[s, CHUNK:, :]),
                (jnp.concatenate([k_fin, pad], axis=0), jnp.concatenate([v_cur, pad], axis=0))]
        tiles = [_kv_tiles(kw, vw) for kw, vw in wins]
        q_tiles = _prep_q(q_ref.at[s], qw_ref, cos_ref, slo_ref, shi_ref, tq)
        _attend(q_tiles,
                lambda h, tiles=tiles: jnp.concatenate([tiles[w][0][h] for w in range(3)], axis=0),
                lambda h, tiles=tiles: jnp.concatenate([tiles[w][1][h] for w in range(3)], axis=0),
                lambda w: new_valid if w == 2 else True,
                sink_ref, o_ref.at[s], tq)


def _attn_sample(proj3, cache_k, cache_v, rope_tabs, qw_tile, kw_tile, sink_tab, ns):
    batch, tq, _ = proj3.shape
    cos_t, slo, shi = rope_tabs
    tab_spec = pl.BlockSpec((tq, LANES), lambda g: (0, 0))
    const = lambda shape: pl.BlockSpec(shape, lambda g: (0,) * len(shape))
    cache_spec = pl.BlockSpec((ns, WINDOW, KV_W), lambda g: (g, 0, 0))
    return pl.pallas_call(
        _attn_sample_kernel,
        grid=(batch // ns,),
        in_specs=[
            pl.BlockSpec((ns, tq, ATTN_W), lambda g: (g, 0, COL_Q // ATTN_W)),
            pl.BlockSpec((ns, tq, KV_W), lambda g: (g, 0, COL_K // KV_W)),
            pl.BlockSpec((ns, tq, KV_W), lambda g: (g, 0, COL_V // KV_W)),
            cache_spec, cache_spec,
            tab_spec, tab_spec, tab_spec,
            const((1, LANES)), const((1, LANES)), const((Q_TILES, LANES)),
        ],
        out_specs=[pl.BlockSpec((ns, tq, ATTN_W), lambda g: (g, 0, 0)), cache_spec, cache_spec],
        out_shape=[
            jax.ShapeDtypeStruct((batch, tq, ATTN_W), BF16),
            jax.ShapeDtypeStruct((batch, WINDOW, KV_W), F32),
            jax.ShapeDtypeStruct((batch, WINDOW, KV_W), F32),
        ],
        compiler_params=_params("arbitrary"),
        name="attn_sample",
    )(proj3, proj3, proj3, cache_k, cache_v, cos_t, slo, shi, qw_tile, kw_tile, sink_tab)


def _pad_rows(a, rows):
    if a.shape[0] == rows:
        return a
    return jnp.concatenate([a, jnp.zeros((rows - a.shape[0], a.shape[1]), a.dtype)], axis=0)


def _ssd_chunk(c, xs0_ref, xs1_ref, z0_ref, z1_ref, b_ref, c_ref, dt_ref, cst_ref, cw_ref, cb_ref, dtb_ref,
               alog_ref, dskip_ref, nw_ref, sel_ref, shift_ref, y_ref, ext_scr, st_scr, L):
    half = SSM_W // 2
    ext_scr[L:2 * L, 0:half] = xs0_ref[...]
    ext_scr[L:2 * L, half:SSM_W] = xs1_ref[...]
    ext_scr[L:2 * L, SSM_W:SSM_W + BC_W] = b_ref[...]
    ext_scr[L:2 * L, SSM_W + BC_W:CONV_CH] = c_ref[...]
    delayed = _dot(shift_ref[...], ext_scr[...])
    cur = ext_scr[L:2 * L, :].astype(F32)
    conv = cb_ref[...] + cur * cw_ref[CONV_W - 1:CONV_W, :]
    for i in range(CONV_W - 1):
        conv = conv + delayed[L * i:L * (i + 1)] * cw_ref[i:i + 1, :]
    s = cst_ref[...]
    w0, w1, w2 = cw_ref[0:1, :], cw_ref[1:2, :], cw_ref[2:3, :]
    head = jnp.concatenate([w0 * s[0:1] + w1 * s[1:2] + w2 * s[2:3], w0 * s[1:2] + w1 * s[2:3], w0 * s[2:3],
                            jnp.zeros((8 - (CONV_W - 1), CONV_CH), F32)], axis=0)
    conv = jnp.concatenate([conv[0:8] + jnp.where(c == 0, head, 0.0), conv[8:]], axis=0)
    conv = _silu(conv)
    xs = conv[:, :SSM_W]
    ext_scr[0:L, :] = ext_scr[L:2 * L, :]

    x_dt = dt_ref[...] + dtb_ref[...]
    dt = jnp.maximum(x_dt, 0.0) + jnp.log1p(jnp.exp(-jnp.abs(x_dt)))
    ad = dt * (-jnp.exp(alog_ref[...]))
    li = lax.broadcasted_iota(jnp.int32, (L, L), 0)
    si = lax.broadcasted_iota(jnp.int32, (L, L), 1)
    tril = (si <= li).astype(BF16)
    a_cum = _dot(jnp.concatenate([tril] * 3, axis=1), jnp.concatenate(_split3(ad), axis=0))
    sel = sel_ref[...]
    ex = _dot(jnp.concatenate(_split3(a_cum) + _split3(dt), axis=0), sel)
    col = ex[0:L] + ex[L:2 * L] + ex[2 * L:3 * L]
    dt_all = ex[3 * L:4 * L] + ex[4 * L:5 * L] + ex[5 * L:6 * L]
    lane_s = lax.broadcasted_iota(jnp.int32, (L, SSM_W), 1) % SSM_HEAD_DIM
    row_l = lax.broadcasted_iota(jnp.int32, (L, SSM_W), 0)
    diag = jnp.where(lane_s == row_l, col, 0.0)
    row = _dot(jnp.ones((L, 3 * L), BF16), jnp.concatenate(_split3(diag), axis=0))
    lmat = jnp.exp(jnp.where(lane_s <= row_l, col - row, NEG_INF))
    a_last = col[L - 1:L, :]
    exp_a = jnp.exp(col)
    decay = jnp.exp(a_last - col)
    exp_last = jnp.exp(a_last)

    xd = xs * dt_all
    xdd = xd * decay
    bi = lax.broadcasted_iota(jnp.int32, (GROUP_W, GROUP_W), 0) // SSM_HEAD_DIM
    bj = lax.broadcasted_iota(jnp.int32, (GROUP_W, GROUP_W), 1) // SSM_HEAD_DIM
    blockdiag = bi == bj

    for g in range(N_GROUPS):
        gs = slice(GROUP_W * g, GROUP_W * (g + 1))
        b_g = conv[:, SSM_W + D_STATE * g:SSM_W + D_STATE * (g + 1)]
        c_g = conv[:, SSM_W + BC_W + D_STATE * g:SSM_W + BC_W + D_STATE * (g + 1)].astype(BF16)
        b_pad = _pad_rows(b_g, CHUNK)
        cb = _dot_nt(c_g, jnp.concatenate([b_pad.astype(BF16)] * 4, axis=0))
        m_g = (cb * lmat[:, gs]).astype(BF16)
        xd_g = _pad_rows(xd[:, gs], CHUNK).astype(BF16)
        xd_bd = jnp.where(blockdiag, jnp.concatenate([xd_g] * 4, axis=0), jnp.zeros((), BF16))
        y_diag = _dot(m_g, xd_bd)
        st_g = st_scr[:, gs]
        y_off = _dot(c_g, st_g.astype(BF16)) * exp_a[:, gs]
        new_st = _dot(b_pad.T.astype(BF16), _pad_rows(xdd[:, gs], CHUNK).astype(BF16))
        st_scr[:, gs] = st_g * exp_last[:, gs] + new_st

        y = y_diag + y_off + xs[:, gs] * dskip_ref[:, gs]
        z_ref = z0_ref if g < N_GROUPS // 2 else z1_ref
        zs = slice(GROUP_W * (g % (N_GROUPS // 2)), GROUP_W * (g % (N_GROUPS // 2) + 1))
        gated = y * _silu(z_ref[:, zs].astype(F32))
        ms = jnp.mean(gated * gated, axis=-1, keepdims=True)
        y_ref[:, gs] = (gated * lax.rsqrt(ms + RMS_EPS) * nw_ref[:, gs]).astype(y_ref.dtype)


def _ssd_kernel(xs0_ref, xs1_ref, z0_ref, z1_ref, b_ref, c_ref, dt_ref, cst_ref, st0_ref, cw_ref, cb_ref,
                dtb_ref, alog_ref, dskip_ref, nw_ref, sel_ref, shift_ref, y_ref, ncv_ref, nst_ref,
                ext_scr, st_scr):
    c = pl.program_id(1)
    nc = pl.num_programs(1)
    ns, L = dt_ref.shape[0], dt_ref.shape[1]

    @pl.when(c == 0)
    def _():
        for s in range(ns):
            ext_scr[s, 0:L, :] = jnp.zeros((L, CONV_CH), BF16)
            st_scr[s] = st0_ref[s].T

    for s in range(ns):
        _ssd_chunk(c, xs0_ref.at[s], xs1_ref.at[s], z0_ref.at[s], z1_ref.at[s], b_ref.at[s], c_ref.at[s],
                   dt_ref.at[s], cst_ref.at[s], cw_ref, cb_ref, dtb_ref, alog_ref, dskip_ref, nw_ref,
                   sel_ref, shift_ref, y_ref.at[s], ext_scr.at[s], st_scr.at[s], L)

    @pl.when(c == nc - 1)
    def _():
        for s in range(ns):
            tail = ext_scr[s, L - BF16_ROWS:L, :].astype(F32)
            ncv_ref[s] = tail[BF16_ROWS - (CONV_W - 1):BF16_ROWS, :]
            nst_ref[s] = st_scr[s].T


def _shift_matrix(L):
    r = jnp.arange(3 * L)
    src = L + r % L - (CONV_W - 1) + r // L
    return (jnp.arange(2 * L)[None, :] == src[:, None]).astype(BF16)


def _ssd(proj3, dt3, conv_state, ssm_state, conv_w, conv_b, dtb, alog, dskip_all, ssm_nw, sel, L, ns):
    batch, t, _ = proj3.shape
    nc = t // L
    const = lambda shape: pl.BlockSpec(shape, lambda g, c: (0,) * len(shape))
    per_g = lambda shape: pl.BlockSpec((ns,) + shape, lambda g, c: (g,) + (0,) * len(shape))
    col = lambda start: pl.BlockSpec((ns, L, BC_W), lambda g, c: (g, c, start // BC_W))
    hp = N_SSM_HEADS * SSM_HEAD_DIM
    return pl.pallas_call(
        _ssd_kernel,
        grid=(batch // ns, nc),
        in_specs=[
            col(COL_XS), col(COL_XS + BC_W), col(COL_Z), col(COL_Z + BC_W), col(COL_B), col(COL_C),
            pl.BlockSpec((ns, L, LANES), lambda g, c: (g, c, 0)),
            per_g((CONV_W - 1, CONV_CH)),
            per_g((hp, D_STATE)),
            const((CONV_W, CONV_CH)), const((1, CONV_CH)), const((1, LANES)), const((1, LANES)),
            const((1, SSM_W)), const((1, SSM_W)), const((LANES, SSM_W)), const((3 * L, 2 * L)),
        ],
        out_specs=[
            pl.BlockSpec((ns, L, SSM_W), lambda g, c: (g, c, 0)),
            per_g((CONV_W - 1, CONV_CH)),
            per_g((hp, D_STATE)),
        ],
        out_shape=[
            jax.ShapeDtypeStruct((batch, t, SSM_W), BF16),
            jax.ShapeDtypeStruct((batch, CONV_W - 1, CONV_CH), F32),
            jax.ShapeDtypeStruct((batch, hp, D_STATE), F32),
        ],
        scratch_shapes=[
            pltpu.VMEM((ns, 2 * L, CONV_CH), BF16),
            pltpu.VMEM((ns, D_STATE, hp), F32),
        ],
        compiler_params=_params("arbitrary", "arbitrary"),
        name="ssd",
    )(proj3, proj3, proj3, proj3, proj3, proj3, dt3, conv_state, ssm_state, conv_w, conv_b, dtb, alog,
      dskip_all, ssm_nw, sel, _shift_matrix(L))


def _outproj_kernel(*refs, n_cast):
    a_ref, y_ref, wa_ref, wy_ref, x_ref = refs[:5]
    cast_in, o_ref, cast_out = refs[5:5 + n_cast], refs[5 + n_cast], refs[6 + n_cast:]
    o_ref[...] = x_ref[...] + _dot(a_ref[...], wa_ref[...]) + _dot(y_ref[...], wy_ref[...])
    _cast_blocks(cast_in, cast_out)


def _outproj(attn, y, w_out, x2d, tm, tn, cast=()):
    t = x2d.shape[0]
    half = D_MODEL // 2
    ni = t // tm
    c_in, c_out, c_shape = _cast_riders(cast, (D_MODEL // tn) * ni, lambda j, i: j * ni + i)
    return pl.pallas_call(
        functools.partial(_outproj_kernel, n_cast=len(cast)),
        grid=(D_MODEL // tn, ni),
        in_specs=[
            pl.BlockSpec((tm, half), lambda j, i: (i, 0)),
            pl.BlockSpec((tm, half), lambda j, i: (i, 0)),
            pl.BlockSpec((half, tn), lambda j, i: (0, j)),
            pl.BlockSpec((half, tn), lambda j, i: (1, j)),
            pl.BlockSpec((tm, tn), lambda j, i: (i, j)),
        ] + c_in,
        out_specs=[pl.BlockSpec((tm, tn), lambda j, i: (i, j))] + c_out,
        out_shape=[jax.ShapeDtypeStruct((t, D_MODEL), F32)] + c_shape,
        compiler_params=_params("arbitrary", "arbitrary"),
        name="outproj",
    )(attn, y, w_out, w_out, x2d, *cast)


def _ffn_kernel(x_ref, nw_ref, wu_ref, wd_ref, o_ref, h_scr, *, n_split):
    @pl.when(pl.program_id(1) == 0)
    def _():
        _rmsnorm_rows(x_ref, nw_ref, h_scr, copy_ref=o_ref)

    u = jnp.maximum(_dot(h_scr[...], wu_ref[...]), 0.0)
    u = (u * u).astype(BF16)
    wn = D_MODEL // n_split
    for n in range(n_split):
        o_ref[:, wn * n:wn * (n + 1)] += _dot(u, wd_ref[:, wn * n:wn * (n + 1)])


def _ffn(x2d, norm_w, w_up, w_down, tm, tf):
    t = x2d.shape[0]
    return pl.pallas_call(
        functools.partial(_ffn_kernel, n_split=4),
        grid=(t // tm, FFN_HIDDEN // tf),
        in_specs=[
            pl.BlockSpec((tm, D_MODEL), lambda i, f: (i, 0)),
            pl.BlockSpec((1, D_MODEL), lambda i, f: (0, 0)),
            pl.BlockSpec((D_MODEL, tf), lambda i, f: (0, f)),
            pl.BlockSpec((tf, D_MODEL), lambda i, f: (f, 0)),
        ],
        out_specs=pl.BlockSpec((tm, D_MODEL), lambda i, f: (i, 0)),
        out_shape=jax.ShapeDtypeStruct((t, D_MODEL), F32),
        scratch_shapes=[pltpu.VMEM((tm, D_MODEL), BF16)],
        compiler_params=_params("arbitrary", "arbitrary"),
        name="ffn",
    )(x2d, norm_w, w_up, w_down)


def _rope_tables(pos):
    inv_freq = ROPE_THETA ** (-jnp.arange(ROPE_HALF, dtype=F32) / ROPE_HALF)
    lane = jnp.arange(LANES) % HEAD_DIM
    freq = jnp.where(lane < ROPE_DIM, inv_freq[lane % ROPE_HALF], 0.0)
    ang = pos.astype(F32)[:, None] * freq[None, :]
    cos, sin = jnp.cos(ang), jnp.sin(ang)
    return cos, jnp.where(lane < ROPE_HALF, -sin, 0.0), jnp.where(lane >= ROPE_HALF, sin, 0.0)


def _layer_params(norm_mix_w, w_in, q_norm_w, k_norm_w, attn_sinks, conv_w, conv_b, dt_bias, a_log, d_skip,
                  ssm_norm_w, w_out, norm_ffn_w, w_up, w_down):
    pad_h = lambda a: jnp.pad(a.astype(F32), (0, LANES - N_SSM_HEADS))[None, :]
    sel = (jnp.arange(LANES)[:, None] == (jnp.arange(SSM_W) // SSM_HEAD_DIM)[None, :]).astype(BF16)
    return dict(
        norm_mix=norm_mix_w.astype(F32)[None, :],
        qw=jnp.tile(q_norm_w.astype(F32), 2)[None, :], kw=jnp.tile(k_norm_w.astype(F32), 2)[None, :],
        sinks=jnp.repeat(attn_sinks.astype(F32), HEAD_DIM).reshape(Q_TILES, LANES),
        conv_w=conv_w.astype(F32), conv_b=conv_b.astype(F32)[None, :],
        dtb=pad_h(dt_bias), alog=pad_h(a_log),
        dskip=jnp.repeat(d_skip.astype(F32), SSM_HEAD_DIM)[None, :],
        ssm_nw=ssm_norm_w.astype(F32)[None, :], sel=sel,
        norm_ffn=norm_ffn_w.astype(F32)[None, :],
    )


def _stream_back(x, proj, dt_raw, pos, cache, conv_state, ssm_state, p, tm, w_out, w_up, w_down):
    batch, t, _ = x.shape
    ns = max(n for n in STREAMS_PER_STEP if batch % n == 0)
    x2d = x.reshape(batch * t, D_MODEL)
    proj3 = proj.reshape(batch, t, PROJ_W)
    tabs = _rope_tables(pos)
    if cache is None:
        attn, new_k, new_v = _attn_prompt(proj3, tabs, p["qw"], p["kw"], p["sinks"], ns)
        L = CHUNK
    else:
        past_k, past_v = cache
        attn, new_k, new_v = _attn_sample(proj3, past_k.reshape(batch, WINDOW, KV_W),
                                          past_v.reshape(batch, WINDOW, KV_W), tabs, p["qw"], p["kw"],
                                          p["sinks"], ns)
        L = min(CHUNK, t)
    hp = N_SSM_HEADS * SSM_HEAD_DIM
    y, new_conv, new_ssm = _ssd(proj3, dt_raw.reshape(batch, t, LANES), conv_state,
                                ssm_state.reshape(batch, hp, D_STATE), p["conv_w"], p["conv_b"], p["dtb"],
                                p["alog"], p["dskip"], p["ssm_nw"], p["sel"], L, ns)
    late = () if w_down.dtype == BF16 else (w_down,)
    x1, *cast = _outproj(attn.reshape(batch * t, ATTN_W), y.reshape(batch * t, SSM_W), w_out, x2d, tm, 1024,
                         cast=late)
    if late:
        w_down = cast.pop()
    out = _ffn(x1, p["norm_ffn"], w_up, w_down, tm, 512)
    return (out.reshape(batch, t, D_MODEL),
            new_k.reshape(batch, WINDOW, N_KV_HEADS, HEAD_DIM), new_v.reshape(batch, WINDOW, N_KV_HEADS, HEAD_DIM),
            new_conv, new_ssm.reshape(batch, N_SSM_HEADS, SSM_HEAD_DIM, D_STATE), w_down)


def kernel(x_prompt, x_sample, cache_k, cache_v, state_conv, state_ssm, norm_mix_w, w_in, q_norm_w, k_norm_w,
           attn_sinks, conv_w, conv_b, dt_bias, a_log, d_skip, ssm_norm_w, w_out, norm_ffn_w, w_up, w_down):
    depth = w_in.shape[0]
    b_p, t_p, _ = x_prompt.shape
    b_s, t_s, _ = x_sample.shape
    assert cache_k.shape[2] == WINDOW and t_p % CHUNK == 0 and t_p >= WINDOW
    assert t_s <= CHUNK and t_s % BF16_ROWS == 0
    pos_p = jnp.arange(t_p, dtype=jnp.int32)
    pos_s = PAST_LEN + jnp.arange(t_s, dtype=jnp.int32)
    hp, hs = x_prompt, x_sample
    outs_p, outs_s = [], []
    for layer in range(depth):
        p = _layer_params(norm_mix_w[layer], w_in[layer], q_norm_w[layer], k_norm_w[layer], attn_sinks[layer],
                          conv_w[layer], conv_b[layer], dt_bias[layer], a_log[layer], d_skip[layer],
                          ssm_norm_w[layer], w_out[layer], norm_ffn_w[layer], w_up[layer], w_down[layer])
        conv0 = jnp.zeros((b_p, CONV_W - 1, CONV_CH), F32)
        ssm0 = jnp.zeros((b_p, N_SSM_HEADS, SSM_HEAD_DIM, D_STATE), F32)
        tm = 512
        xs2d, xp2d = hs.reshape(b_s * t_s, D_MODEL), hp.reshape(b_p * t_p, D_MODEL)
        proj_s, dt_s, w_main, w_tail = _inproj(xs2d, p["norm_mix"], w_in[layer], w_in[layer], tm, 512)
        proj_p, dt_p, w_out16, w_up16 = _inproj(xp2d, p["norm_mix"], w_main, w_tail, tm, 1024,
                                                cast=(w_out[layer], w_up[layer]))
        hp, *rest_p, w_down16 = _stream_back(hp, proj_p, dt_p, pos_p, None, conv0, ssm0, p, tm,
                                             w_out16, w_up16, w_down[layer])
        hs, *rest_s, _ = _stream_back(hs, proj_s, dt_s, pos_s, (cache_k[layer], cache_v[layer]),
                                      state_conv[layer], state_ssm[layer], p, tm, w_out16, w_up16, w_down16)
        outs_p.append(rest_p)
        outs_s.append(rest_s)
    stack = lambda outs, i: jnp.stack([o[i] for o in outs])
    return (hp, hs, stack(outs_p, 0), stack(outs_p, 1), stack(outs_p, 2), stack(outs_p, 3),
            stack(outs_s, 0), stack(outs_s, 1), stack(outs_s, 2), stack(outs_s, 3))
```

```python
import functools

import jax
import jax.numpy as jnp
from jax import lax
from jax.experimental import pallas as pl
from jax.experimental.pallas import tpu as pltpu

F32 = jnp.float32
BF16 = jnp.bfloat16

D_MODEL = 4096
HEAD_DIM = 64
N_Q_HEADS = 32
N_KV_HEADS = 8
ATTN_W = N_Q_HEADS * HEAD_DIM
KV_W = N_KV_HEADS * HEAD_DIM
ROPE_DIM = 16
ROPE_HALF = ROPE_DIM // 2
ROPE_THETA = 500000.0
ATTN_SCALE = HEAD_DIM ** -0.5
WINDOW = 128
CHUNK = 64
PAST_LEN = 1024
SSM_W = 2048
N_SSM_HEADS = 32
SSM_HEAD_DIM = 64
N_GROUPS = 8
D_STATE = 128
BC_W = N_GROUPS * D_STATE
GROUP_W = SSM_W // N_GROUPS
CONV_W = 4
CONV_CH = SSM_W + 2 * BC_W
FFN_HIDDEN = 4 * D_MODEL
RMS_EPS = 1e-6
NEG_INF = -1e30
LANES = 128
BF16_ROWS = 16
Q_TILES = ATTN_W // LANES
KV_TILES = KV_W // LANES
PROJ_W = ATTN_W + 2 * KV_W + 2 * SSM_W + 2 * BC_W
COL_Q, COL_K, COL_V, COL_XS, COL_Z, COL_B, COL_C = 0, 2048, 2560, 3072, 5120, 7168, 8192
VMEM_LIMIT = 56 * 1024 * 1024
STREAMS_PER_STEP = (1, 2, 4)


def _params(*sem):
    return pltpu.CompilerParams(dimension_semantics=sem, vmem_limit_bytes=VMEM_LIMIT)


def _dot(a, b):
    return jnp.dot(a, b, preferred_element_type=F32)


def _dot_nt(a, b):
    return lax.dot_general(a, b, (((1,), (1,)), ((), ())), preferred_element_type=F32)


def _split3(x):
    a = x.astype(BF16)
    r = x - a.astype(F32)
    b = r.astype(BF16)
    c = (r - b.astype(F32)).astype(BF16)
    return a, b, c


def _silu(x):
    return x / (1.0 + jnp.exp(-x))


NORM_ROWS = 16
NORM_UNROLL = 8


def _rmsnorm_rows(x_ref, nw_ref, h_scr, copy_ref=None):
    def body(i, carry):
        r = pl.ds(pl.multiple_of(i * NORM_ROWS, NORM_ROWS), NORM_ROWS)
        x = x_ref[r, :]
        ms = jnp.mean(x * x, axis=-1, keepdims=True)
        h_scr[r, :] = (x * lax.rsqrt(ms + RMS_EPS) * nw_ref[...]).astype(BF16)
        if copy_ref is not None:
            copy_ref[r, :] = x
        return carry

    lax.fori_loop(0, x_ref.shape[0] // NORM_ROWS, body, 0, unroll=NORM_UNROLL)


def _cast_riders(arrays, steps, linear_step):
    in_specs, out_specs, out_shapes = [], [], []
    for a in arrays:
        rows, cols = a.shape
        nb = max(n for n in range(1, steps + 1) if rows % n == 0 and (rows // n) % BF16_ROWS == 0)
        spec = pl.BlockSpec((rows // nb, cols), lambda *g, nb=nb: (jnp.minimum(linear_step(*g), nb - 1), 0))
        in_specs.append(spec)
        out_specs.append(spec)
        out_shapes.append(jax.ShapeDtypeStruct(a.shape, BF16))
    return in_specs, out_specs, out_shapes


def _cast_blocks(src_refs, dst_refs):
    for src, dst in zip(src_refs, dst_refs):
        dst[...] = src[...].astype(BF16)


def _inproj_kernel(*refs, n_cast):
    x_ref, nw_ref, w_ref, wdt_ref = refs[:4]
    cast_in = refs[4:4 + n_cast]
    proj_ref, dt_ref = refs[4 + n_cast:6 + n_cast]
    cast_out, h_scr = refs[6 + n_cast:6 + 2 * n_cast], refs[6 + 2 * n_cast]

    @pl.when(pl.program_id(1) == 0)
    def _():
        _rmsnorm_rows(x_ref, nw_ref, h_scr)
        real = lax.broadcasted_iota(jnp.int32, wdt_ref.shape, 0) < N_SSM_HEADS
        dt_ref[...] = _dot_nt(h_scr[...], jnp.where(real, wdt_ref[...], jnp.zeros((), BF16)))

    proj_ref[...] = _dot_nt(h_scr[...], w_ref[...]).astype(BF16)
    _cast_blocks(cast_in, cast_out)


def _inproj(x2d, norm_w, w_in_t, tm, tn, cast=()):
    t = x2d.shape[0]
    nj = PROJ_W // tn
    c_in, c_out, c_shape = _cast_riders(cast, (t // tm) * nj, lambda i, j: i * nj + j)
    return pl.pallas_call(
        functools.partial(_inproj_kernel, n_cast=len(cast)),
        grid=(t // tm, nj),
        in_specs=[
            pl.BlockSpec((tm, D_MODEL), lambda i, j: (i, 0)),
            pl.BlockSpec((1, D_MODEL), lambda i, j: (0, 0)),
            pl.BlockSpec((tn, D_MODEL), lambda i, j: (j, 0)),
            pl.BlockSpec((LANES, D_MODEL), lambda i, j: (PROJ_W // LANES, 0)),
        ] + c_in,
        out_specs=[
            pl.BlockSpec((tm, tn), lambda i, j: (i, j)),
            pl.BlockSpec((tm, LANES), lambda i, j: (i, 0)),
        ] + c_out,
        out_shape=[
            jax.ShapeDtypeStruct((t, PROJ_W), BF16),
            jax.ShapeDtypeStruct((t, LANES), F32),
        ] + c_shape,
        scratch_shapes=[pltpu.VMEM((tm, D_MODEL), BF16)],
        compiler_params=_params("arbitrary", "arbitrary"),
        name="inproj",
    )(x2d, norm_w, w_in_t, w_in_t, *cast)


def _half_ones():
    r = lax.broadcasted_iota(jnp.int32, (LANES, LANES), 0) // HEAD_DIM
    c = lax.broadcasted_iota(jnp.int32, (LANES, LANES), 1) // HEAD_DIM
    return (r == c).astype(BF16)


def _head_norm(x, w_tile):
    ss = _dot((x * x).astype(BF16), _half_ones())
    return x * lax.rsqrt(ss * (1.0 / HEAD_DIM) + RMS_EPS) * w_tile


def _rope(x, cos_t, sin_lo, sin_hi):
    return (x * cos_t + pltpu.roll(x, LANES - ROPE_HALF, 1) * sin_lo
            + pltpu.roll(x, ROPE_HALF, 1) * sin_hi)


def _lane_is_a(shape):
    return lax.broadcasted_iota(jnp.int32, shape, 1) < HEAD_DIM


def _kv_tiles(k_win, v_win):
    is_a = _lane_is_a((CHUNK, LANES))
    ones_a = jnp.where(is_a, 1.0, 0.0).astype(F32)
    ones_b = 1.0 - ones_a
    kks, vvs = [], []
    for h in range(N_KV_HEADS):
        j, odd = h // 2, h % 2
        kt = k_win[:, LANES * j:LANES * (j + 1)]
        vt = v_win[:, LANES * j:LANES * (j + 1)]
        keep = jnp.logical_not(is_a) if odd else is_a
        k_own = jnp.where(keep, kt, 0.0)
        v_own = jnp.where(keep, vt, 0.0)
        k_sw = pltpu.roll(k_own, HEAD_DIM, 1)
        v_sw = pltpu.roll(v_own, HEAD_DIM, 1)
        k_a, k_b = (k_sw, k_own) if odd else (k_own, k_sw)
        v_a, v_b = (v_sw, v_own) if odd else (v_own, v_sw)
        kks.append(jnp.concatenate([k_a, k_b], axis=0).astype(BF16))
        vv = jnp.concatenate([jnp.concatenate([v_a, ones_a], axis=1),
                              jnp.concatenate([v_b, ones_b], axis=1)], axis=0)
        vvs.append(vv.astype(BF16))
    return kks, vvs


def _prep_q(q_ref, qw_ref, cos_ref, slo_ref, shi_ref, tq):
    q2 = jnp.concatenate([q_ref[:, LANES * i:LANES * (i + 1)].astype(F32) for i in range(Q_TILES)], axis=0)
    q2 = _head_norm(q2, qw_ref[...])
    cos_t, slo, shi = cos_ref[...], slo_ref[...], shi_ref[...]
    return [(_rope(q2[tq * i:tq * (i + 1)], cos_t, slo, shi) * ATTN_SCALE).astype(BF16) for i in range(Q_TILES)]


def _attend(q_tiles, kk_of, vv_of, valid_of, sink_ref, o_ref, tq):
    is_a = _lane_is_a((2 * tq, LANES))
    for h in range(N_KV_HEADS):
        qh = jnp.concatenate([q_tiles[2 * h], q_tiles[2 * h + 1]], axis=0)
        s = _dot_nt(qh, kk_of(h))
        sw = [jnp.where(valid_of(w), s[:, LANES * w:LANES * (w + 1)], NEG_INF) for w in range(3)]
        mt = jnp.maximum(jnp.maximum(sw[0], sw[1]), sw[2])
        m_a = jnp.max(jnp.where(is_a, mt, NEG_INF), axis=-1, keepdims=True)
        m_b = jnp.max(jnp.where(is_a, NEG_INF, mt), axis=-1, keepdims=True)
        sink = jnp.concatenate([jnp.broadcast_to(sink_ref[2 * h:2 * h + 1, :], (tq, LANES)),
                                jnp.broadcast_to(sink_ref[2 * h + 1:2 * h + 2, :], (tq, LANES))], axis=0)
        m = jnp.maximum(jnp.where(is_a, m_a, m_b), sink)
        e = jnp.concatenate([jnp.exp(x - m) for x in sw], axis=1).astype(BF16)
        oa = _dot(e, vv_of(h))
        o = oa[:, :LANES] / (oa[:, LANES:] + jnp.exp(sink - m))
        o_ref[:, LANES * 2 * h:LANES * (2 * h + 1)] = o[:tq].astype(o_ref.dtype)
        o_ref[:, LANES * (2 * h + 1):LANES * (2 * h + 2)] = o[tq:].astype(o_ref.dtype)


def _prep_k(k_raw, kw_ref, cos_ref, slo_ref, shi_ref, rows):
    k2 = jnp.concatenate([k_raw[:, LANES * j:LANES * (j + 1)].astype(F32) for j in range(KV_TILES)], axis=0)
    k2 = _head_norm(k2, kw_ref[...])
    cos_t, slo, shi = cos_ref[...], slo_ref[...], shi_ref[...]
    return jnp.concatenate([_rope(k2[rows * j:rows * (j + 1)], cos_t, slo, shi) for j in range(KV_TILES)], axis=1)


def _attn_prompt_kernel(q_ref, k_ref, v_ref, cos_ref, slo_ref, shi_ref, qw_ref, kw_ref, sink_ref,
                        o_ref, nk_ref, nv_ref, kk_scr, vv_scr):
    c = pl.program_id(1)

    @pl.when(c == 0)
    def _():
        kk_scr[...] = jnp.zeros_like(kk_scr)
        vv_scr[...] = jnp.zeros_like(vv_scr)

    slot = c % 3
    for s in range(q_ref.shape[0]):
        k_fin = _prep_k(k_ref[s], kw_ref, cos_ref, slo_ref, shi_ref, CHUNK)
        v_cur = v_ref[s].astype(F32)
        nk_ref[s] = k_fin
        nv_ref[s] = v_cur
        kks, vvs = _kv_tiles(k_fin, v_cur)
        for h in range(N_KV_HEADS):
            kk_scr[s, slot, h] = kks[h]
            vv_scr[s, slot, h] = vvs[h]

        q_tiles = _prep_q(q_ref.at[s], qw_ref, cos_ref, slo_ref, shi_ref, CHUNK)
        _attend(q_tiles,
                lambda h, s=s: jnp.concatenate([kk_scr[s, w, h] for w in range(3)], axis=0),
                lambda h, s=s: jnp.concatenate([vv_scr[s, w, h] for w in range(3)], axis=0),
                lambda w: (c - w + 3) % 3 <= c,
                sink_ref, o_ref.at[s], CHUNK)


def _attn_prompt(proj3, rope_tabs, qw_tile, kw_tile, sink_tab, ns):
    batch, t, _ = proj3.shape
    nc = t // CHUNK
    keep = WINDOW // CHUNK
    cos_t, slo, shi = rope_tabs
    tab_spec = pl.BlockSpec((CHUNK, LANES), lambda g, c: (c, 0))
    const = lambda shape: pl.BlockSpec(shape, lambda g, c: (0,) * len(shape))
    keep_spec = pl.BlockSpec((ns, CHUNK, KV_W), lambda g, c: (g, jnp.maximum(c - (nc - keep), 0), 0))
    return pl.pallas_call(
        _attn_prompt_kernel,
        grid=(batch // ns, nc),
        in_specs=[
            pl.BlockSpec((ns, CHUNK, ATTN_W), lambda g, c: (g, c, COL_Q // ATTN_W)),
            pl.BlockSpec((ns, CHUNK, KV_W), lambda g, c: (g, c, COL_K // KV_W)),
            pl.BlockSpec((ns, CHUNK, KV_W), lambda g, c: (g, c, COL_V // KV_W)),
            tab_spec, tab_spec, tab_spec,
            const((1, LANES)), const((1, LANES)), const((Q_TILES, LANES)),
        ],
        out_specs=[pl.BlockSpec((ns, CHUNK, ATTN_W), lambda g, c: (g, c, 0)), keep_spec, keep_spec],
        out_shape=[
            jax.ShapeDtypeStruct((batch, t, ATTN_W), BF16),
            jax.ShapeDtypeStruct((batch, WINDOW, KV_W), F32),
            jax.ShapeDtypeStruct((batch, WINDOW, KV_W), F32),
        ],
        scratch_shapes=[
            pltpu.VMEM((ns, 3, N_KV_HEADS, 2 * CHUNK, LANES), BF16),
            pltpu.VMEM((ns, 3, N_KV_HEADS, 2 * CHUNK, 2 * LANES), BF16),
        ],
        compiler_params=_params("arbitrary", "arbitrary"),
        name="attn_prompt",
    )(proj3, proj3, proj3, cos_t, slo, shi, qw_tile, kw_tile, sink_tab)


def _attn_sample_kernel(q_ref, k_ref, v_ref, ck_ref, cv_ref, cos_ref, slo_ref, shi_ref, qw_ref, kw_ref,
                        sink_ref, o_ref, nk_ref, nv_ref):
    tq = q_ref.shape[1]
    new_valid = lax.broadcasted_iota(jnp.int32, (2 * tq, LANES), 1) % HEAD_DIM < tq
    pad = jnp.zeros((CHUNK - tq, KV_W), F32)
    for s in range(q_ref.shape[0]):
        k_fin = _prep_k(k_ref[s], kw_ref, cos_ref, slo_ref, shi_ref, tq)
        v_cur = v_ref[s].astype(F32)
        nk_ref[s, 0:WINDOW - tq, :] = ck_ref[s, tq:WINDOW, :]
        nv_ref[s, 0:WINDOW - tq, :] = cv_ref[s, tq:WINDOW, :]
        nk_ref[s, WINDOW - tq:WINDOW, :] = k_fin
        nv_ref[s, WINDOW - tq:WINDOW, :] = v_cur
        wins = [(ck_ref[s, :CHUNK, :], cv_ref[s, :CHUNK, :]),
                (ck_ref[s, CHUNK:, :], cv_ref[s, CHUNK:, :]),
                (jnp.concatenate([k_fin, pad], axis=0), jnp.concatenate([v_cur, pad], axis=0))]
        tiles = [_kv_tiles(kw, vw) for kw, vw in wins]
        q_tiles = _prep_q(q_ref.at[s], qw_ref, cos_ref, slo_ref, shi_ref, tq)
        _attend(q_tiles,
                lambda h, tiles=tiles: jnp.concatenate([tiles[w][0][h] for w in range(3)], axis=0),
                lambda h, tiles=tiles: jnp.concatenate([tiles[w][1][h] for w in range(3)], axis=0),
                lambda w: new_valid if w == 2 else True,
                sink_ref, o_ref.at[s], tq)


def _attn_sample(proj3, cache_k, cache_v, rope_tabs, qw_tile, kw_tile, sink_tab, ns):
    batch, tq, _ = proj3.shape
    cos_t, slo, shi = rope_tabs
    tab_spec = pl.BlockSpec((tq, LANES), lambda g: (0, 0))
    const = lambda shape: pl.BlockSpec(shape, lambda g: (0,) * len(shape))
    cache_spec = pl.BlockSpec((ns, WINDOW, KV_W), lambda g: (g, 0, 0))
    return pl.pallas_call(
        _attn_sample_kernel,
        grid=(batch // ns,),
        in_specs=[
            pl.BlockSpec((ns, tq, ATTN_W), lambda g: (g, 0, COL_Q // ATTN_W)),
            pl.BlockSpec((ns, tq, KV_W), lambda g: (g, 0, COL_K // KV_W)),
            pl.BlockSpec((ns, tq, KV_W), lambda g: (g, 0, COL_V // KV_W)),
            cache_spec, cache_spec,
            tab_spec, tab_spec, tab_spec,
            const((1, LANES)), const((1, LANES)), const((Q_TILES, LANES)),
        ],
        out_specs=[pl.BlockSpec((ns, tq, ATTN_W), lambda g: (g, 0, 0)), cache_spec, cache_spec],
        out_shape=[
            jax.ShapeDtypeStruct((batch, tq, ATTN_W), BF16),
            jax.ShapeDtypeStruct((batch, WINDOW, KV_W), F32),
            jax.ShapeDtypeStruct((batch, WINDOW, KV_W), F32),
        ],
        compiler_params=_params("arbitrary"),
        name="attn_sample",
    )(proj3, proj3, proj3, cache_k, cache_v, cos_t, slo, shi, qw_tile, kw_tile, sink_tab)


def _pad_rows(a, rows):
    if a.shape[0] == rows:
        return a
    return jnp.concatenate([a, jnp.zeros((rows - a.shape[0], a.shape[1]), a.dtype)], axis=0)


def _ssd_chunk(c, xs0_ref, xs1_ref, z0_ref, z1_ref, b_ref, c_ref, dt_ref, cst_ref, cw_ref, cb_ref, dtb_ref,
               alog_ref, dskip_ref, nw_ref, sel_ref, shift_ref, y_ref, ext_scr, st_scr, L):
    half = SSM_W // 2
    ext_scr[L:2 * L, 0:half] = xs0_ref[...]
    ext_scr[L:2 * L, half:SSM_W] = xs1_ref[...]
    ext_scr[L:2 * L, SSM_W:SSM_W + BC_W] = b_ref[...]
    ext_scr[L:2 * L, SSM_W + BC_W:CONV_CH] = c_ref[...]
    delayed = _dot(shift_ref[...], ext_scr[...])
    cur = ext_scr[L:2 * L, :].astype(F32)
    conv = cb_ref[...] + cur * cw_ref[CONV_W - 1:CONV_W, :]
    for i in range(CONV_W - 1):
        conv = conv + delayed[L * i:L * (i + 1)] * cw_ref[i:i + 1, :]
    s = cst_ref[...]
    w0, w1, w2 = cw_ref[0:1, :], cw_ref[1:2, :], cw_ref[2:3, :]
    head = jnp.concatenate([w0 * s[0:1] + w1 * s[1:2] + w2 * s[2:3], w0 * s[1:2] + w1 * s[2:3], w0 * s[2:3],
                            jnp.zeros((8 - (CONV_W - 1), CONV_CH), F32)], axis=0)
    conv = jnp.concatenate([conv[0:8] + jnp.where(c == 0, head, 0.0), conv[8:]], axis=0)
    conv = _silu(conv)
    xs = conv[:, :SSM_W]
    ext_scr[0:L, :] = ext_scr[L:2 * L, :]

    x_dt = dt_ref[...] + dtb_ref[...]
    dt = jnp.maximum(x_dt, 0.0) + jnp.log1p(jnp.exp(-jnp.abs(x_dt)))
    ad = dt * (-jnp.exp(alog_ref[...]))
    li = lax.broadcasted_iota(jnp.int32, (L, L), 0)
    si = lax.broadcasted_iota(jnp.int32, (L, L), 1)
    tril = (si <= li).astype(BF16)
    a_cum = _dot(jnp.concatenate([tril] * 3, axis=1), jnp.concatenate(_split3(ad), axis=0))
    sel = sel_ref[...]
    ex = _dot(jnp.concatenate(_split3(a_cum) + _split3(dt), axis=0), sel)
    col = ex[0:L] + ex[L:2 * L] + ex[2 * L:3 * L]
    dt_all = ex[3 * L:4 * L] + ex[4 * L:5 * L] + ex[5 * L:6 * L]
    lane_s = lax.broadcasted_iota(jnp.int32, (L, SSM_W), 1) % SSM_HEAD_DIM
    row_l = lax.broadcasted_iota(jnp.int32, (L, SSM_W), 0)
    diag = jnp.where(lane_s == row_l, col, 0.0)
    row = _dot(jnp.ones((L, 3 * L), BF16), jnp.concatenate(_split3(diag), axis=0))
    lmat = jnp.exp(jnp.where(lane_s <= row_l, col - row, NEG_INF))
    a_last = col[L - 1:L, :]
    exp_a = jnp.exp(col)
    decay = jnp.exp(a_last - col)
    exp_last = jnp.exp(a_last)

    xd = xs * dt_all
    xdd = xd * decay
    bi = lax.broadcasted_iota(jnp.int32, (GROUP_W, GROUP_W), 0) // SSM_HEAD_DIM
    bj = lax.broadcasted_iota(jnp.int32, (GROUP_W, GROUP_W), 1) // SSM_HEAD_DIM
    blockdiag = bi == bj

    for g in range(N_GROUPS):
        gs = slice(GROUP_W * g, GROUP_W * (g + 1))
        b_g = conv[:, SSM_W + D_STATE * g:SSM_W + D_STATE * (g + 1)]
        c_g = conv[:, SSM_W + BC_W + D_STATE * g:SSM_W + BC_W + D_STATE * (g + 1)].astype(BF16)
        b_pad = _pad_rows(b_g, CHUNK)
        cb = _dot_nt(c_g, jnp.concatenate([b_pad.astype(BF16)] * 4, axis=0))
        m_g = (cb * lmat[:, gs]).astype(BF16)
        xd_g = _pad_rows(xd[:, gs], CHUNK).astype(BF16)
        xd_bd = jnp.where(blockdiag, jnp.concatenate([xd_g] * 4, axis=0), jnp.zeros((), BF16))
        y_diag = _dot(m_g, xd_bd)
        st_g = st_scr[:, gs]
        y_off = _dot(c_g, st_g.astype(BF16)) * exp_a[:, gs]
        new_st = _dot(b_pad.T.astype(BF16), _pad_rows(xdd[:, gs], CHUNK).astype(BF16))
        st_scr[:, gs] = st_g * exp_last[:, gs] + new_st

        y = y_diag + y_off + xs[:, gs] * dskip_ref[:, gs]
        z_ref = z0_ref if g < N_GROUPS // 2 else z1_ref
        zs = slice(GROUP_W * (g % (N_GROUPS // 2)), GROUP_W * (g % (N_GROUPS // 2) + 1))
        gated = y * _silu(z_ref[:, zs].astype(F32))
        ms = jnp.mean(gated * gated, axis=-1, keepdims=True)
        y_ref[:, gs] = (gated * lax.rsqrt(ms + RMS_EPS) * nw_ref[:, gs]).astype(y_ref.dtype)


def _ssd_kernel(xs0_ref, xs1_ref, z0_ref, z1_ref, b_ref, c_ref, dt_ref, cst_ref, st0_ref, cw_ref, cb_ref,
                dtb_ref, alog_ref, dskip_ref, nw_ref, sel_ref, shift_ref, y_ref, ncv_ref, nst_ref,
                ext_scr, st_scr):
    c = pl.program_id(1)
    nc = pl.num_programs(1)
    ns, L = dt_ref.shape[0], dt_ref.shape[1]

    @pl.when(c == 0)
    def _():
        for s in range(ns):
            ext_scr[s, 0:L, :] = jnp.zeros((L, CONV_CH), BF16)
            st_scr[s] = st0_ref[s].T

    for s in range(ns):
        _ssd_chunk(c, xs0_ref.at[s], xs1_ref.at[s], z0_ref.at[s], z1_ref.at[s], b_ref.at[s], c_ref.at[s],
                   dt_ref.at[s], cst_ref.at[s], cw_ref, cb_ref, dtb_ref, alog_ref, dskip_ref, nw_ref,
                   sel_ref, shift_ref, y_ref.at[s], ext_scr.at[s], st_scr.at[s], L)

    @pl.when(c == nc - 1)
    def _():
        for s in range(ns):
            tail = ext_scr[s, L - BF16_ROWS:L, :].astype(F32)
            ncv_ref[s] = tail[BF16_ROWS - (CONV_W - 1):BF16_ROWS, :]
            nst_ref[s] = st_scr[s].T


def _shift_matrix(L):
    r = jnp.arange(3 * L)
    src = L + r % L - (CONV_W - 1) + r // L
    return (jnp.arange(2 * L)[None, :] == src[:, None]).astype(BF16)


def _ssd(proj3, dt3, conv_state, ssm_state, conv_w, conv_b, dtb, alog, dskip_all, ssm_nw, sel, L, ns):
    batch, t, _ = proj3.shape
    nc = t // L
    const = lambda shape: pl.BlockSpec(shape, lambda g, c: (0,) * len(shape))
    per_g = lambda shape: pl.BlockSpec((ns,) + shape, lambda g, c: (g,) + (0,) * len(shape))
    col = lambda start: pl.BlockSpec((ns, L, BC_W), lambda g, c: (g, c, start // BC_W))
    hp = N_SSM_HEADS * SSM_HEAD_DIM
    return pl.pallas_call(
        _ssd_kernel,
        grid=(batch // ns, nc),
        in_specs=[
            col(COL_XS), col(COL_XS + BC_W), col(COL_Z), col(COL_Z + BC_W), col(COL_B), col(COL_C),
            pl.BlockSpec((ns, L, LANES), lambda g, c: (g, c, 0)),
            per_g((CONV_W - 1, CONV_CH)),
            per_g((hp, D_STATE)),
            const((CONV_W, CONV_CH)), const((1, CONV_CH)), const((1, LANES)), const((1, LANES)),
            const((1, SSM_W)), const((1, SSM_W)), const((LANES, SSM_W)), const((3 * L, 2 * L)),
        ],
        out_specs=[
            pl.BlockSpec((ns, L, SSM_W), lambda g, c: (g, c, 0)),
            per_g((CONV_W - 1, CONV_CH)),
            per_g((hp, D_STATE)),
        ],
        out_shape=[
            jax.ShapeDtypeStruct((batch, t, SSM_W), BF16),
            jax.ShapeDtypeStruct((batch, CONV_W - 1, CONV_CH), F32),
            jax.ShapeDtypeStruct((batch, hp, D_STATE), F32),
        ],
        scratch_shapes=[
            pltpu.VMEM((ns, 2 * L, CONV_CH), BF16),
            pltpu.VMEM((ns, D_STATE, hp), F32),
        ],
        compiler_params=_params("arbitrary", "arbitrary"),
        name="ssd",
    )(proj3, proj3, proj3, proj3, proj3, proj3, dt3, conv_state, ssm_state, conv_w, conv_b, dtb, alog,
      dskip_all, ssm_nw, sel, _shift_matrix(L))


def _outproj_kernel(*refs, n_cast):
    a_ref, y_ref, wa_ref, wy_ref, x_ref = refs[:5]
    cast_in, o_ref, cast_out = refs[5:5 + n_cast], refs[5 + n_cast], refs[6 + n_cast:]
    o_ref[...] = x_ref[...] + _dot(a_ref[...], wa_ref[...]) + _dot(y_ref[...], wy_ref[...])
    _cast_blocks(cast_in, cast_out)


def _outproj(attn, y, w_out, x2d, tm, tn, cast=()):
    t = x2d.shape[0]
    half = D_MODEL // 2
    ni = t // tm
    c_in, c_out, c_shape = _cast_riders(cast, (D_MODEL // tn) * ni, lambda j, i: j * ni + i)
    return pl.pallas_call(
        functools.partial(_outproj_kernel, n_cast=len(cast)),
        grid=(D_MODEL // tn, ni),
        in_specs=[
            pl.BlockSpec((tm, half), lambda j, i: (i, 0)),
            pl.BlockSpec((tm, half), lambda j, i: (i, 0)),
            pl.BlockSpec((half, tn), lambda j, i: (0, j)),
            pl.BlockSpec((half, tn), lambda j, i: (1, j)),
            pl.BlockSpec((tm, tn), lambda j, i: (i, j)),
        ] + c_in,
        out_specs=[pl.BlockSpec((tm, tn), lambda j, i: (i, j))] + c_out,
        out_shape=[jax.ShapeDtypeStruct((t, D_MODEL), F32)] + c_shape,
        compiler_params=_params("arbitrary", "arbitrary"),
        name="outproj",
    )(attn, y, w_out, w_out, x2d, *cast)


def _ffn_kernel(x_ref, nw_ref, wu_ref, wd_ref, o_ref, h_scr, *, n_split):
    @pl.when(pl.program_id(1) == 0)
    def _():
        _rmsnorm_rows(x_ref, nw_ref, h_scr, copy_ref=o_ref)

    u = jnp.maximum(_dot(h_scr[...], wu_ref[...]), 0.0)
    u = (u * u).astype(BF16)
    wn = D_MODEL // n_split
    for n in range(n_split):
        o_ref[:, wn * n:wn * (n + 1)] += _dot(u, wd_ref[:, wn * n:wn * (n + 1)])


def _ffn(x2d, norm_w, w_up, w_down, tm, tf):
    t = x2d.shape[0]
    return pl.pallas_call(
        functools.partial(_ffn_kernel, n_split=4),
        grid=(t // tm, FFN_HIDDEN // tf),
        in_specs=[
            pl.BlockSpec((tm, D_MODEL), lambda i, f: (i, 0)),
            pl.BlockSpec((1, D_MODEL), lambda i, f: (0, 0)),
            pl.BlockSpec((D_MODEL, tf), lambda i, f: (0, f)),
            pl.BlockSpec((tf, D_MODEL), lambda i, f: (f, 0)),
        ],
        out_specs=pl.BlockSpec((tm, D_MODEL), lambda i, f: (i, 0)),
        out_shape=jax.ShapeDtypeStruct((t, D_MODEL), F32),
        scratch_shapes=[pltpu.VMEM((tm, D_MODEL), BF16)],
        compiler_params=_params("arbitrary", "arbitrary"),
        name="ffn",
    )(x2d, norm_w, w_up, w_down)


def _rope_tables(pos):
    inv_freq = ROPE_THETA ** (-jnp.arange(ROPE_HALF, dtype=F32) / ROPE_HALF)
    lane = jnp.arange(LANES) % HEAD_DIM
    freq = jnp.where(lane < ROPE_DIM, inv_freq[lane % ROPE_HALF], 0.0)
    ang = pos.astype(F32)[:, None] * freq[None, :]
    cos, sin = jnp.cos(ang), jnp.sin(ang)
    return cos, jnp.where(lane < ROPE_HALF, -sin, 0.0), jnp.where(lane >= ROPE_HALF, sin, 0.0)


def _layer_params(norm_mix_w, w_in, q_norm_w, k_norm_w, attn_sinks, conv_w, conv_b, dt_bias, a_log, d_skip,
                  ssm_norm_w, w_out, norm_ffn_w, w_up, w_down):
    pad_h = lambda a: jnp.pad(a.astype(F32), (0, LANES - N_SSM_HEADS))[None, :]
    sel = (jnp.arange(LANES)[:, None] == (jnp.arange(SSM_W) // SSM_HEAD_DIM)[None, :]).astype(BF16)
    return dict(
        norm_mix=norm_mix_w.astype(F32)[None, :],
        qw=jnp.tile(q_norm_w.astype(F32), 2)[None, :], kw=jnp.tile(k_norm_w.astype(F32), 2)[None, :],
        sinks=jnp.repeat(attn_sinks.astype(F32), HEAD_DIM).reshape(Q_TILES, LANES),
        conv_w=conv_w.astype(F32), conv_b=conv_b.astype(F32)[None, :],
        dtb=pad_h(dt_bias), alog=pad_h(a_log),
        dskip=jnp.repeat(d_skip.astype(F32), SSM_HEAD_DIM)[None, :],
        ssm_nw=ssm_norm_w.astype(F32)[None, :], sel=sel,
        norm_ffn=norm_ffn_w.astype(F32)[None, :],
    )


def _stream_back(x, proj, dt_raw, pos, cache, conv_state, ssm_state, p, tm, w_out, w_up, w_down):
    batch, t, _ = x.shape
    ns = max(n for n in STREAMS_PER_STEP if batch % n == 0)
    x2d = x.reshape(batch * t, D_MODEL)
    proj3 = proj.reshape(batch, t, PROJ_W)
    tabs = _rope_tables(pos)
    if cache is None:
        attn, new_k, new_v = _attn_prompt(proj3, tabs, p["qw"], p["kw"], p["sinks"], ns)
        L = CHUNK
    else:
        past_k, past_v = cache
        attn, new_k, new_v = _attn_sample(proj3, past_k.reshape(batch, WINDOW, KV_W),
                                          past_v.reshape(batch, WINDOW, KV_W), tabs, p["qw"], p["kw"],
                                          p["sinks"], ns)
        L = min(CHUNK, t)
    hp = N_SSM_HEADS * SSM_HEAD_DIM
    y, new_conv, new_ssm = _ssd(proj3, dt_raw.reshape(batch, t, LANES), conv_state,
                                ssm_state.reshape(batch, hp, D_STATE), p["conv_w"], p["conv_b"], p["dtb"],
                                p["alog"], p["dskip"], p["ssm_nw"], p["sel"], L, ns)
    late = () if w_down.dtype == BF16 else (w_down,)
    x1, *cast = _outproj(attn.reshape(batch * t, ATTN_W), y.reshape(batch * t, SSM_W), w_out, x2d, tm, 1024,
                         cast=late)
    if late:
        w_down = cast.pop()
    out = _ffn(x1, p["norm_ffn"], w_up, w_down, tm, 512)
    return (out.reshape(batch, t, D_MODEL),
            new_k.reshape(batch, WINDOW, N_KV_HEADS, HEAD_DIM), new_v.reshape(batch, WINDOW, N_KV_HEADS, HEAD_DIM),
            new_conv, new_ssm.reshape(batch, N_SSM_HEADS, SSM_HEAD_DIM, D_STATE), w_down)


def kernel(x_prompt, x_sample, cache_k, cache_v, state_conv, state_ssm, norm_mix_w, w_in, q_norm_w, k_norm_w,
           attn_sinks, conv_w, conv_b, dt_bias, a_log, d_skip, ssm_norm_w, w_out, norm_ffn_w, w_up, w_down):
    depth = w_in.shape[0]
    b_p, t_p, _ = x_prompt.shape
    b_s, t_s, _ = x_sample.shape
    assert cache_k.shape[2] == WINDOW and t_p % CHUNK == 0 and t_p >= WINDOW
    assert t_s <= CHUNK and t_s % BF16_ROWS == 0
    pos_p = jnp.arange(t_p, dtype=jnp.int32)
    pos_s = PAST_LEN + jnp.arange(t_s, dtype=jnp.int32)
    hp, hs = x_prompt, x_sample
    outs_p, outs_s = [], []
    for layer in range(depth):
        p = _layer_params(norm_mix_w[layer], w_in[layer], q_norm_w[layer], k_norm_w[layer], attn_sinks[layer],
                          conv_w[layer], conv_b[layer], dt_bias[layer], a_log[layer], d_skip[layer],
                          ssm_norm_w[layer], w_out[layer], norm_ffn_w[layer], w_up[layer], w_down[layer])
        conv0 = jnp.zeros((b_p, CONV_W - 1, CONV_CH), F32)
        ssm0 = jnp.zeros((b_p, N_SSM_HEADS, SSM_HEAD_DIM, D_STATE), F32)
        tm = 512
        xs2d, xp2d = hs.reshape(b_s * t_s, D_MODEL), hp.reshape(b_p * t_p, D_MODEL)
        w_in_t = w_in[layer].T.astype(BF16)
        proj_p, dt_p, w_out16, w_up16 = _inproj(xp2d, p["norm_mix"], w_in_t, tm, 1024,
                                                cast=(w_out[layer], w_up[layer]))
        proj_s, dt_s = _inproj(xs2d, p["norm_mix"], w_in_t, tm, 1024)
        hp, *rest_p, w_down16 = _stream_back(hp, proj_p, dt_p, pos_p, None, conv0, ssm0, p, tm,
                                             w_out16, w_up16, w_down[layer])
        hs, *rest_s, _ = _stream_back(hs, proj_s, dt_s, pos_s, (cache_k[layer], cache_v[layer]),
                                      state_conv[layer], state_ssm[layer], p, tm, w_out16, w_up16, w_down16)
        outs_p.append(rest_p)
        outs_s.append(rest_s)
    stack = lambda outs, i: jnp.stack([o[i] for o in outs])
    return (hp, hs, stack(outs_p, 0), stack(outs_p, 1), stack(outs_p, 2), stack(outs_p, 3),
            stack(outs_s, 0), stack(outs_s, 1), stack(outs_s, 2), stack(outs_s, 3))
```

```python
import functools

import jax
import jax.numpy as jnp
from jax import lax
from jax.experimental import pallas as pl
from jax.experimental.pallas import tpu as pltpu

F32 = jnp.float32
BF16 = jnp.bfloat16

D_MODEL = 4096
HEAD_DIM = 64
N_Q_HEADS = 32
N_KV_HEADS = 8
ATTN_W = N_Q_HEADS * HEAD_DIM
KV_W = N_KV_HEADS * HEAD_DIM
ROPE_DIM = 16
ROPE_HALF = ROPE_DIM // 2
ROPE_THETA = 500000.0
ATTN_SCALE = HEAD_DIM ** -0.5
WINDOW = 128
CHUNK = 64
PAST_LEN = 1024
SSM_W = 2048
N_SSM_HEADS = 32
SSM_HEAD_DIM = 64
N_GROUPS = 8
D_STATE = 128
BC_W = N_GROUPS * D_STATE
GROUP_W = SSM_W // N_GROUPS
CONV_W = 4
CONV_CH = SSM_W + 2 * BC_W
FFN_HIDDEN = 4 * D_MODEL
RMS_EPS = 1e-6
NEG_INF = -1e30
LANES = 128
BF16_ROWS = 16
Q_TILES = ATTN_W // LANES
KV_TILES = KV_W // LANES
PROJ_W = ATTN_W + 2 * KV_W + 2 * SSM_W + 2 * BC_W
COL_Q, COL_K, COL_V, COL_XS, COL_Z, COL_B, COL_C = 0, 2048, 2560, 3072, 5120, 7168, 8192
VMEM_LIMIT = 56 * 1024 * 1024
STREAMS_PER_STEP = (1, 2, 4)


def _params(*sem):
    return pltpu.CompilerParams(dimension_semantics=sem, vmem_limit_bytes=VMEM_LIMIT)


def _dot(a, b):
    return jnp.dot(a, b, preferred_element_type=F32)


def _dot_nt(a, b):
    return lax.dot_general(a, b, (((1,), (1,)), ((), ())), preferred_element_type=F32)


def _split3(x):
    a = x.astype(BF16)
    r = x - a.astype(F32)
    b = r.astype(BF16)
    c = (r - b.astype(F32)).astype(BF16)
    return a, b, c


def _silu(x):
    return x / (1.0 + jnp.exp(-x))


NORM_ROWS = 16
NORM_UNROLL = 8


def _rmsnorm_rows(x_ref, nw_ref, h_scr, copy_ref=None):
    def body(i, carry):
        r = pl.ds(pl.multiple_of(i * NORM_ROWS, NORM_ROWS), NORM_ROWS)
        x = x_ref[r, :]
        ms = jnp.mean(x * x, axis=-1, keepdims=True)
        h_scr[r, :] = (x * lax.rsqrt(ms + RMS_EPS) * nw_ref[...]).astype(BF16)
        if copy_ref is not None:
            copy_ref[r, :] = x
        return carry

    lax.fori_loop(0, x_ref.shape[0] // NORM_ROWS, body, 0, unroll=NORM_UNROLL)


def _cast_riders(arrays, steps, linear_step):
    in_specs, out_specs, out_shapes = [], [], []
    for a in arrays:
        rows, cols = a.shape
        nb = max(n for n in range(1, steps + 1) if rows % n == 0 and (rows // n) % BF16_ROWS == 0)
        spec = pl.BlockSpec((rows // nb, cols), lambda *g, nb=nb: (jnp.minimum(linear_step(*g), nb - 1), 0))
        in_specs.append(spec)
        out_specs.append(spec)
        out_shapes.append(jax.ShapeDtypeStruct(a.shape, BF16))
    return in_specs, out_specs, out_shapes


def _cast_blocks(src_refs, dst_refs):
    for src, dst in zip(src_refs, dst_refs):
        dst[...] = src[...].astype(BF16)


def _inproj_kernel(*refs, n_cast):
    x_ref, nw_ref, w_ref, wdt_ref = refs[:4]
    cast_in = refs[4:4 + n_cast]
    proj_ref, dt_ref = refs[4 + n_cast:6 + n_cast]
    cast_out, h_scr = refs[6 + n_cast:6 + 2 * n_cast], refs[6 + 2 * n_cast]

    @pl.when(pl.program_id(1) == 0)
    def _():
        _rmsnorm_rows(x_ref, nw_ref, h_scr)
        real = lax.broadcasted_iota(jnp.int32, wdt_ref.shape, 0) < N_SSM_HEADS
        dt_ref[...] = _dot_nt(h_scr[...], jnp.where(real, wdt_ref[...], jnp.zeros((), BF16)))

    proj_ref[...] = _dot_nt(h_scr[...], w_ref[...]).astype(BF16)
    _cast_blocks(cast_in, cast_out)


def _inproj(x2d, norm_w, w_in_t, tm, tn, cast=()):
    t = x2d.shape[0]
    nj = PROJ_W // tn
    c_in, c_out, c_shape = _cast_riders(cast, (t // tm) * nj, lambda i, j: i * nj + j)
    return pl.pallas_call(
        functools.partial(_inproj_kernel, n_cast=len(cast)),
        grid=(t // tm, nj),
        in_specs=[
            pl.BlockSpec((tm, D_MODEL), lambda i, j: (i, 0)),
            pl.BlockSpec((1, D_MODEL), lambda i, j: (0, 0)),
            pl.BlockSpec((tn, D_MODEL), lambda i, j: (j, 0)),
            pl.BlockSpec((LANES, D_MODEL), lambda i, j: (PROJ_W // LANES, 0)),
        ] + c_in,
        out_specs=[
            pl.BlockSpec((tm, tn), lambda i, j: (i, j)),
            pl.BlockSpec((tm, LANES), lambda i, j: (i, 0)),
        ] + c_out,
        out_shape=[
            jax.ShapeDtypeStruct((t, PROJ_W), BF16),
            jax.ShapeDtypeStruct((t, LANES), F32),
        ] + c_shape,
        scratch_shapes=[pltpu.VMEM((tm, D_MODEL), BF16)],
        compiler_params=_params("arbitrary", "arbitrary"),
        name="inproj",
    )(x2d, norm_w, w_in_t, w_in_t, *cast)


def _half_ones():
    r = lax.broadcasted_iota(jnp.int32, (LANES, LANES), 0) // HEAD_DIM
    c = lax.broadcasted_iota(jnp.int32, (LANES, LANES), 1) // HEAD_DIM
    return (r == c).astype(BF16)


def _head_norm(x, w_tile):
    ss = _dot((x * x).astype(BF16), _half_ones())
    return x * lax.rsqrt(ss * (1.0 / HEAD_DIM) + RMS_EPS) * w_tile


def _rope(x, cos_t, sin_lo, sin_hi):
    return (x * cos_t + pltpu.roll(x, LANES - ROPE_HALF, 1) * sin_lo
            + pltpu.roll(x, ROPE_HALF, 1) * sin_hi)


def _lane_is_a(shape):
    return lax.broadcasted_iota(jnp.int32, shape, 1) < HEAD_DIM


def _kv_tiles(k_win, v_win):
    is_a = _lane_is_a((CHUNK, LANES))
    ones_a = jnp.where(is_a, 1.0, 0.0).astype(F32)
    ones_b = 1.0 - ones_a
    kks, vvs = [], []
    for h in range(N_KV_HEADS):
        j, odd = h // 2, h % 2
        kt = k_win[:, LANES * j:LANES * (j + 1)]
        vt = v_win[:, LANES * j:LANES * (j + 1)]
        keep = jnp.logical_not(is_a) if odd else is_a
        k_own = jnp.where(keep, kt, 0.0)
        v_own = jnp.where(keep, vt, 0.0)
        k_sw = pltpu.roll(k_own, HEAD_DIM, 1)
        v_sw = pltpu.roll(v_own, HEAD_DIM, 1)
        k_a, k_b = (k_sw, k_own) if odd else (k_own, k_sw)
        v_a, v_b = (v_sw, v_own) if odd else (v_own, v_sw)
        kks.append(jnp.concatenate([k_a, k_b], axis=0).astype(BF16))
        vv = jnp.concatenate([jnp.concatenate([v_a, ones_a], axis=1),
                              jnp.concatenate([v_b, ones_b], axis=1)], axis=0)
        vvs.append(vv.astype(BF16))
    return kks, vvs


def _prep_q(q_ref, qw_ref, cos_ref, slo_ref, shi_ref, tq):
    q2 = jnp.concatenate([q_ref[:, LANES * i:LANES * (i + 1)].astype(F32) for i in range(Q_TILES)], axis=0)
    q2 = _head_norm(q2, qw_ref[...])
    cos_t, slo, shi = cos_ref[...], slo_ref[...], shi_ref[...]
    return [(_rope(q2[tq * i:tq * (i + 1)], cos_t, slo, shi) * ATTN_SCALE).astype(BF16) for i in range(Q_TILES)]


def _attend(q_tiles, kk_of, vv_of, valid_of, sink_ref, o_ref, tq):
    is_a = _lane_is_a((2 * tq, LANES))
    for h in range(N_KV_HEADS):
        qh = jnp.concatenate([q_tiles[2 * h], q_tiles[2 * h + 1]], axis=0)
        s = _dot_nt(qh, kk_of(h))
        sw = [jnp.where(valid_of(w), s[:, LANES * w:LANES * (w + 1)], NEG_INF) for w in range(3)]
        mt = jnp.maximum(jnp.maximum(sw[0], sw[1]), sw[2])
        m_a = jnp.max(jnp.where(is_a, mt, NEG_INF), axis=-1, keepdims=True)
        m_b = jnp.max(jnp.where(is_a, NEG_INF, mt), axis=-1, keepdims=True)
        sink = jnp.concatenate([jnp.broadcast_to(sink_ref[2 * h:2 * h + 1, :], (tq, LANES)),
                                jnp.broadcast_to(sink_ref[2 * h + 1:2 * h + 2, :], (tq, LANES))], axis=0)
        m = jnp.maximum(jnp.where(is_a, m_a, m_b), sink)
        e = jnp.concatenate([jnp.exp(x - m) for x in sw], axis=1).astype(BF16)
        oa = _dot(e, vv_of(h))
        o = oa[:, :LANES] / (oa[:, LANES:] + jnp.exp(sink - m))
        o_ref[:, LANES * 2 * h:LANES * (2 * h + 1)] = o[:tq].astype(o_ref.dtype)
        o_ref[:, LANES * (2 * h + 1):LANES * (2 * h + 2)] = o[tq:].astype(o_ref.dtype)


def _prep_k(k_raw, kw_ref, cos_ref, slo_ref, shi_ref, rows):
    k2 = jnp.concatenate([k_raw[:, LANES * j:LANES * (j + 1)].astype(F32) for j in range(KV_TILES)], axis=0)
    k2 = _head_norm(k2, kw_ref[...])
    cos_t, slo, shi = cos_ref[...], slo_ref[...], shi_ref[...]
    return jnp.concatenate([_rope(k2[rows * j:rows * (j + 1)], cos_t, slo, shi) for j in range(KV_TILES)], axis=1)


class _Part:
    def __init__(self, in_specs, operands, out_specs, out_shapes, scratch, body, init=None, final=None):
        self.in_specs, self.operands, self.out_specs, self.out_shapes = in_specs, operands, out_specs, out_shapes
        self.scratch, self.body, self.init, self.final = scratch, body, init, final


def _fused_kernel(*refs, parts):
    n_in = [len(p.in_specs) for p in parts]
    n_out = [len(p.out_specs) for p in parts]
    n_scr = [len(p.scratch) for p in parts]
    pos = 0
    groups = []
    for counts in (n_in, n_out, n_scr):
        groups.append([])
        for n in counts:
            groups[-1].append(refs[pos:pos + n])
            pos += n
    args = [(p, groups[0][i], groups[1][i], groups[2][i]) for i, p in enumerate(parts)]
    c = pl.program_id(1)

    @pl.when(c == 0)
    def _():
        for p, ins, outs, scr in args:
            if p.init is not None:
                p.init(c, ins, outs, scr)

    for p, ins, outs, scr in args:
        p.body(c, ins, outs, scr)

    @pl.when(c == pl.num_programs(1) - 1)
    def _():
        for p, ins, outs, scr in args:
            if p.final is not None:
                p.final(c, ins, outs, scr)


def _fused_call(parts, grid, name):
    outs = pl.pallas_call(
        functools.partial(_fused_kernel, parts=parts),
        grid=grid,
        in_specs=[s for p in parts for s in p.in_specs],
        out_specs=[s for p in parts for s in p.out_specs],
        out_shape=[s for p in parts for s in p.out_shapes],
        scratch_shapes=[s for p in parts for s in p.scratch],
        compiler_params=_params("arbitrary", "arbitrary"),
        name=name,
    )(*[o for p in parts for o in p.operands])
    split, pos = [], 0
    for p in parts:
        split.append(outs[pos:pos + len(p.out_specs)])
        pos += len(p.out_specs)
    return split


def _attn_prompt_init(c, ins, outs, scr):
    for ring in scr:
        ring[...] = jnp.zeros_like(ring)


def _attn_prompt_body(c, ins, outs, scr):
    q_ref, k_ref, v_ref, cos_ref, slo_ref, shi_ref, qw_ref, kw_ref, sink_ref = ins
    o_ref, nk_ref, nv_ref = outs
    kk_scr, vv_scr = scr
    slot = c % 3
    for s in range(q_ref.shape[0]):
        k_fin = _prep_k(k_ref[s], kw_ref, cos_ref, slo_ref, shi_ref, CHUNK)
        v_cur = v_ref[s].astype(F32)
        nk_ref[s] = k_fin
        nv_ref[s] = v_cur
        kks, vvs = _kv_tiles(k_fin, v_cur)
        for h in range(N_KV_HEADS):
            kk_scr[s, slot, h] = kks[h]
            vv_scr[s, slot, h] = vvs[h]

        q_tiles = _prep_q(q_ref.at[s], qw_ref, cos_ref, slo_ref, shi_ref, CHUNK)
        _attend(q_tiles,
                lambda h, s=s: jnp.concatenate([kk_scr[s, w, h] for w in range(3)], axis=0),
                lambda h, s=s: jnp.concatenate([vv_scr[s, w, h] for w in range(3)], axis=0),
                lambda w: (c - w + 3) % 3 <= c,
                sink_ref, o_ref.at[s], CHUNK)


def _attn_prompt_part(proj3, rope_tabs, qw_tile, kw_tile, sink_tab, ns):
    batch, t, _ = proj3.shape
    nc = t // CHUNK
    keep = WINDOW // CHUNK
    cos_t, slo, shi = rope_tabs
    tab_spec = pl.BlockSpec((CHUNK, LANES), lambda g, c: (c, 0))
    const = lambda shape: pl.BlockSpec(shape, lambda g, c: (0,) * len(shape))
    keep_spec = pl.BlockSpec((ns, CHUNK, KV_W), lambda g, c: (g, jnp.maximum(c - (nc - keep), 0), 0))
    return _Part(
        in_specs=[
            pl.BlockSpec((ns, CHUNK, ATTN_W), lambda g, c: (g, c, COL_Q // ATTN_W)),
            pl.BlockSpec((ns, CHUNK, KV_W), lambda g, c: (g, c, COL_K // KV_W)),
            pl.BlockSpec((ns, CHUNK, KV_W), lambda g, c: (g, c, COL_V // KV_W)),
            tab_spec, tab_spec, tab_spec,
            const((1, LANES)), const((1, LANES)), const((Q_TILES, LANES)),
        ],
        operands=[proj3, proj3, proj3, cos_t, slo, shi, qw_tile, kw_tile, sink_tab],
        out_specs=[pl.BlockSpec((ns, CHUNK, ATTN_W), lambda g, c: (g, c, 0)), keep_spec, keep_spec],
        out_shapes=[
            jax.ShapeDtypeStruct((batch, t, ATTN_W), BF16),
            jax.ShapeDtypeStruct((batch, WINDOW, KV_W), F32),
            jax.ShapeDtypeStruct((batch, WINDOW, KV_W), F32),
        ],
        scratch=[
            pltpu.VMEM((ns, 3, N_KV_HEADS, 2 * CHUNK, LANES), BF16),
            pltpu.VMEM((ns, 3, N_KV_HEADS, 2 * CHUNK, 2 * LANES), BF16),
        ],
        body=_attn_prompt_body, init=_attn_prompt_init)


def _attn_sample_body(c, ins, outs, scr):
    q_ref, k_ref, v_ref, ck_ref, cv_ref, cos_ref, slo_ref, shi_ref, qw_ref, kw_ref, sink_ref = ins
    o_ref, nk_ref, nv_ref = outs
    tq = q_ref.shape[1]
    new_valid = lax.broadcasted_iota(jnp.int32, (2 * tq, LANES), 1) % HEAD_DIM < tq
    pad = jnp.zeros((CHUNK - tq, KV_W), F32)
    for s in range(q_ref.shape[0]):
        k_fin = _prep_k(k_ref[s], kw_ref, cos_ref, slo_ref, shi_ref, tq)
        v_cur = v_ref[s].astype(F32)
        nk_ref[s, 0:WINDOW - tq, :] = ck_ref[s, tq:WINDOW, :]
        nv_ref[s, 0:WINDOW - tq, :] = cv_ref[s, tq:WINDOW, :]
        nk_ref[s, WINDOW - tq:WINDOW, :] = k_fin
        nv_ref[s, WINDOW - tq:WINDOW, :] = v_cur
        wins = [(ck_ref[s, :CHUNK, :], cv_ref[s, :CHUNK, :]),
                (ck_ref[s, CHUNK:, :], cv_ref[s, CHUNK:, :]),
                (jnp.concatenate([k_fin, pad], axis=0), jnp.concatenate([v_cur, pad], axis=0))]
        tiles = [_kv_tiles(kw, vw) for kw, vw in wins]
        q_tiles = _prep_q(q_ref.at[s], qw_ref, cos_ref, slo_ref, shi_ref, tq)
        _attend(q_tiles,
                lambda h, tiles=tiles: jnp.concatenate([tiles[w][0][h] for w in range(3)], axis=0),
                lambda h, tiles=tiles: jnp.concatenate([tiles[w][1][h] for w in range(3)], axis=0),
                lambda w: new_valid if w == 2 else True,
                sink_ref, o_ref.at[s], tq)


def _attn_sample_part(proj3, cache_k, cache_v, rope_tabs, qw_tile, kw_tile, sink_tab, ns):
    batch, tq, _ = proj3.shape
    cos_t, slo, shi = rope_tabs
    tab_spec = pl.BlockSpec((tq, LANES), lambda g, c: (0, 0))
    const = lambda shape: pl.BlockSpec(shape, lambda g, c: (0,) * len(shape))
    cache_spec = pl.BlockSpec((ns, WINDOW, KV_W), lambda g, c: (g, 0, 0))
    return _Part(
        in_specs=[
            pl.BlockSpec((ns, tq, ATTN_W), lambda g, c: (g, 0, COL_Q // ATTN_W)),
            pl.BlockSpec((ns, tq, KV_W), lambda g, c: (g, 0, COL_K // KV_W)),
            pl.BlockSpec((ns, tq, KV_W), lambda g, c: (g, 0, COL_V // KV_W)),
            cache_spec, cache_spec,
            tab_spec, tab_spec, tab_spec,
            const((1, LANES)), const((1, LANES)), const((Q_TILES, LANES)),
        ],
        operands=[proj3, proj3, proj3, cache_k, cache_v, cos_t, slo, shi, qw_tile, kw_tile, sink_tab],
        out_specs=[pl.BlockSpec((ns, tq, ATTN_W), lambda g, c: (g, 0, 0)), cache_spec, cache_spec],
        out_shapes=[
            jax.ShapeDtypeStruct((batch, tq, ATTN_W), BF16),
            jax.ShapeDtypeStruct((batch, WINDOW, KV_W), F32),
            jax.ShapeDtypeStruct((batch, WINDOW, KV_W), F32),
        ],
        scratch=[], body=_attn_sample_body)


def _pad_rows(a, rows):
    if a.shape[0] == rows:
        return a
    return jnp.concatenate([a, jnp.zeros((rows - a.shape[0], a.shape[1]), a.dtype)], axis=0)


def _ssd_chunk(c, xs0_ref, xs1_ref, z0_ref, z1_ref, b_ref, c_ref, dt_ref, cst_ref, cw_ref, cb_ref, dtb_ref,
               alog_ref, dskip_ref, nw_ref, sel_ref, shift_ref, y_ref, ext_scr, st_scr, L):
    half = SSM_W // 2
    ext_scr[L:2 * L, 0:half] = xs0_ref[...]
    ext_scr[L:2 * L, half:SSM_W] = xs1_ref[...]
    ext_scr[L:2 * L, SSM_W:SSM_W + BC_W] = b_ref[...]
    ext_scr[L:2 * L, SSM_W + BC_W:CONV_CH] = c_ref[...]
    delayed = _dot(shift_ref[...], ext_scr[...])
    cur = ext_scr[L:2 * L, :].astype(F32)
    conv = cb_ref[...] + cur * cw_ref[CONV_W - 1:CONV_W, :]
    for i in range(CONV_W - 1):
        conv = conv + delayed[L * i:L * (i + 1)] * cw_ref[i:i + 1, :]
    s = cst_ref[...]
    w0, w1, w2 = cw_ref[0:1, :], cw_ref[1:2, :], cw_ref[2:3, :]
    head = jnp.concatenate([w0 * s[0:1] + w1 * s[1:2] + w2 * s[2:3], w0 * s[1:2] + w1 * s[2:3], w0 * s[2:3],
                            jnp.zeros((8 - (CONV_W - 1), CONV_CH), F32)], axis=0)
    conv = jnp.concatenate([conv[0:8] + jnp.where(c == 0, head, 0.0), conv[8:]], axis=0)
    conv = _silu(conv)
    xs = conv[:, :SSM_W]
    ext_scr[0:L, :] = ext_scr[L:2 * L, :]

    x_dt = dt_ref[...] + dtb_ref[...]
    dt = jnp.maximum(x_dt, 0.0) + jnp.log1p(jnp.exp(-jnp.abs(x_dt)))
    ad = dt * (-jnp.exp(alog_ref[...]))
    li = lax.broadcasted_iota(jnp.int32, (L, L), 0)
    si = lax.broadcasted_iota(jnp.int32, (L, L), 1)
    tril = (si <= li).astype(BF16)
    a_cum = _dot(jnp.concatenate([tril] * 3, axis=1), jnp.concatenate(_split3(ad), axis=0))
    sel = sel_ref[...]
    ex = _dot(jnp.concatenate(_split3(a_cum) + _split3(dt), axis=0), sel)
    col = ex[0:L] + ex[L:2 * L] + ex[2 * L:3 * L]
    dt_all = ex[3 * L:4 * L] + ex[4 * L:5 * L] + ex[5 * L:6 * L]
    lane_s = lax.broadcasted_iota(jnp.int32, (L, SSM_W), 1) % SSM_HEAD_DIM
    row_l = lax.broadcasted_iota(jnp.int32, (L, SSM_W), 0)
    diag = jnp.where(lane_s == row_l, col, 0.0)
    row = _dot(jnp.ones((L, 3 * L), BF16), jnp.concatenate(_split3(diag), axis=0))
    lmat = jnp.exp(jnp.where(lane_s <= row_l, col - row, NEG_INF))
    a_last = col[L - 1:L, :]
    exp_a = jnp.exp(col)
    decay = jnp.exp(a_last - col)
    exp_last = jnp.exp(a_last)

    xd = xs * dt_all
    xdd = xd * decay
    bi = lax.broadcasted_iota(jnp.int32, (GROUP_W, GROUP_W), 0) // SSM_HEAD_DIM
    bj = lax.broadcasted_iota(jnp.int32, (GROUP_W, GROUP_W), 1) // SSM_HEAD_DIM
    blockdiag = bi == bj

    for g in range(N_GROUPS):
        gs = slice(GROUP_W * g, GROUP_W * (g + 1))
        b_g = conv[:, SSM_W + D_STATE * g:SSM_W + D_STATE * (g + 1)]
        c_g = conv[:, SSM_W + BC_W + D_STATE * g:SSM_W + BC_W + D_STATE * (g + 1)].astype(BF16)
        b_pad = _pad_rows(b_g, CHUNK)
        cb = _dot_nt(c_g, jnp.concatenate([b_pad.astype(BF16)] * 4, axis=0))
        m_g = (cb * lmat[:, gs]).astype(BF16)
        xd_g = _pad_rows(xd[:, gs], CHUNK).astype(BF16)
        xd_bd = jnp.where(blockdiag, jnp.concatenate([xd_g] * 4, axis=0), jnp.zeros((), BF16))
        y_diag = _dot(m_g, xd_bd)
        st_g = st_scr[:, gs]
        y_off = _dot(c_g, st_g.astype(BF16)) * exp_a[:, gs]
        new_st = _dot(b_pad.T.astype(BF16), _pad_rows(xdd[:, gs], CHUNK).astype(BF16))
        st_scr[:, gs] = st_g * exp_last[:, gs] + new_st

        y = y_diag + y_off + xs[:, gs] * dskip_ref[:, gs]
        z_ref = z0_ref if g < N_GROUPS // 2 else z1_ref
        zs = slice(GROUP_W * (g % (N_GROUPS // 2)), GROUP_W * (g % (N_GROUPS // 2) + 1))
        gated = y * _silu(z_ref[:, zs].astype(F32))
        ms = jnp.mean(gated * gated, axis=-1, keepdims=True)
        y_ref[:, gs] = (gated * lax.rsqrt(ms + RMS_EPS) * nw_ref[:, gs]).astype(y_ref.dtype)


def _ssd_init(c, ins, outs, scr):
    st0_ref = ins[8]
    ext_scr, st_scr = scr
    ns, L = ext_scr.shape[0], ext_scr.shape[1] // 2
    for s in range(ns):
        ext_scr[s, 0:L, :] = jnp.zeros((L, CONV_CH), BF16)
        st_scr[s] = st0_ref[s].T


def _ssd_body(c, ins, outs, scr):
    (xs0_ref, xs1_ref, z0_ref, z1_ref, b_ref, c_ref, dt_ref, cst_ref, _, cw_ref, cb_ref, dtb_ref, alog_ref,
     dskip_ref, nw_ref, sel_ref, shift_ref) = ins
    y_ref = outs[0]
    ext_scr, st_scr = scr
    ns, L = dt_ref.shape[0], dt_ref.shape[1]
    for s in range(ns):
        _ssd_chunk(c, xs0_ref.at[s], xs1_ref.at[s], z0_ref.at[s], z1_ref.at[s], b_ref.at[s], c_ref.at[s],
                   dt_ref.at[s], cst_ref.at[s], cw_ref, cb_ref, dtb_ref, alog_ref, dskip_ref, nw_ref,
                   sel_ref, shift_ref, y_ref.at[s], ext_scr.at[s], st_scr.at[s], L)


def _ssd_final(c, ins, outs, scr):
    _, ncv_ref, nst_ref = outs
    ext_scr, st_scr = scr
    ns, L = ext_scr.shape[0], ext_scr.shape[1] // 2
    for s in range(ns):
        tail = ext_scr[s, L - BF16_ROWS:L, :].astype(F32)
        ncv_ref[s] = tail[BF16_ROWS - (CONV_W - 1):BF16_ROWS, :]
        nst_ref[s] = st_scr[s].T


def _shift_matrix(L):
    r = jnp.arange(3 * L)
    src = L + r % L - (CONV_W - 1) + r // L
    return (jnp.arange(2 * L)[None, :] == src[:, None]).astype(BF16)


def _ssd_part(proj3, dt3, conv_state, ssm_state, conv_w, conv_b, dtb, alog, dskip_all, ssm_nw, sel, L, ns):
    batch, t, _ = proj3.shape
    const = lambda shape: pl.BlockSpec(shape, lambda g, c: (0,) * len(shape))
    per_g = lambda shape: pl.BlockSpec((ns,) + shape, lambda g, c: (g,) + (0,) * len(shape))
    col = lambda start: pl.BlockSpec((ns, L, BC_W), lambda g, c: (g, c, start // BC_W))
    hp = N_SSM_HEADS * SSM_HEAD_DIM
    return _Part(
        in_specs=[
            col(COL_XS), col(COL_XS + BC_W), col(COL_Z), col(COL_Z + BC_W), col(COL_B), col(COL_C),
            pl.BlockSpec((ns, L, LANES), lambda g, c: (g, c, 0)),
            per_g((CONV_W - 1, CONV_CH)),
            per_g((hp, D_STATE)),
            const((CONV_W, CONV_CH)), const((1, CONV_CH)), const((1, LANES)), const((1, LANES)),
            const((1, SSM_W)), const((1, SSM_W)), const((LANES, SSM_W)), const((3 * L, 2 * L)),
        ],
        out_specs=[
            pl.BlockSpec((ns, L, SSM_W), lambda g, c: (g, c, 0)),
            per_g((CONV_W - 1, CONV_CH)),
            per_g((hp, D_STATE)),
        ],
        out_shapes=[
            jax.ShapeDtypeStruct((batch, t, SSM_W), BF16),
            jax.ShapeDtypeStruct((batch, CONV_W - 1, CONV_CH), F32),
            jax.ShapeDtypeStruct((batch, hp, D_STATE), F32),
        ],
        scratch=[
            pltpu.VMEM((ns, 2 * L, CONV_CH), BF16),
            pltpu.VMEM((ns, D_STATE, hp), F32),
        ],
        operands=[proj3, proj3, proj3, proj3, proj3, proj3, dt3, conv_state, ssm_state, conv_w, conv_b, dtb,
                  alog, dskip_all, ssm_nw, sel, _shift_matrix(L)],
        body=_ssd_body, init=_ssd_init, final=_ssd_final)


def _outproj_kernel(*refs, n_cast):
    a_ref, y_ref, wa_ref, wy_ref, x_ref = refs[:5]
    cast_in, o_ref, cast_out = refs[5:5 + n_cast], refs[5 + n_cast], refs[6 + n_cast:]
    o_ref[...] = x_ref[...] + _dot(a_ref[...], wa_ref[...]) + _dot(y_ref[...], wy_ref[...])
    _cast_blocks(cast_in, cast_out)


def _outproj(attn, y, w_out, x2d, tm, tn, cast=()):
    t = x2d.shape[0]
    half = D_MODEL // 2
    ni = t // tm
    c_in, c_out, c_shape = _cast_riders(cast, (D_MODEL // tn) * ni, lambda j, i: j * ni + i)
    return pl.pallas_call(
        functools.partial(_outproj_kernel, n_cast=len(cast)),
        grid=(D_MODEL // tn, ni),
        in_specs=[
            pl.BlockSpec((tm, half), lambda j, i: (i, 0)),
            pl.BlockSpec((tm, half), lambda j, i: (i, 0)),
            pl.BlockSpec((half, tn), lambda j, i: (0, j)),
            pl.BlockSpec((half, tn), lambda j, i: (1, j)),
            pl.BlockSpec((tm, tn), lambda j, i: (i, j)),
        ] + c_in,
        out_specs=[pl.BlockSpec((tm, tn), lambda j, i: (i, j))] + c_out,
        out_shape=[jax.ShapeDtypeStruct((t, D_MODEL), F32)] + c_shape,
        compiler_params=_params("arbitrary", "arbitrary"),
        name="outproj",
    )(attn, y, w_out, w_out, x2d, *cast)


def _ffn_kernel(x_ref, nw_ref, wu_ref, wd_ref, o_ref, h_scr, *, n_split):
    @pl.when(pl.program_id(1) == 0)
    def _():
        _rmsnorm_rows(x_ref, nw_ref, h_scr, copy_ref=o_ref)

    u = jnp.maximum(_dot(h_scr[...], wu_ref[...]), 0.0)
    u = (u * u).astype(BF16)
    wn = D_MODEL // n_split
    for n in range(n_split):
        o_ref[:, wn * n:wn * (n + 1)] += _dot(u, wd_ref[:, wn * n:wn * (n + 1)])


def _ffn(x2d, norm_w, w_up, w_down, tm, tf):
    t = x2d.shape[0]
    return pl.pallas_call(
        functools.partial(_ffn_kernel, n_split=4),
        grid=(t // tm, FFN_HIDDEN // tf),
        in_specs=[
            pl.BlockSpec((tm, D_MODEL), lambda i, f: (i, 0)),
            pl.BlockSpec((1, D_MODEL), lambda i, f: (0, 0)),
            pl.BlockSpec((D_MODEL, tf), lambda i, f: (0, f)),
            pl.BlockSpec((tf, D_MODEL), lambda i, f: (f, 0)),
        ],
        out_specs=pl.BlockSpec((tm, D_MODEL), lambda i, f: (i, 0)),
        out_shape=jax.ShapeDtypeStruct((t, D_MODEL), F32),
        scratch_shapes=[pltpu.VMEM((tm, D_MODEL), BF16)],
        compiler_params=_params("arbitrary", "arbitrary"),
        name="ffn",
    )(x2d, norm_w, w_up, w_down)


def _rope_tables(pos):
    inv_freq = ROPE_THETA ** (-jnp.arange(ROPE_HALF, dtype=F32) / ROPE_HALF)
    lane = jnp.arange(LANES) % HEAD_DIM
    freq = jnp.where(lane < ROPE_DIM, inv_freq[lane % ROPE_HALF], 0.0)
    ang = pos.astype(F32)[:, None] * freq[None, :]
    cos, sin = jnp.cos(ang), jnp.sin(ang)
    return cos, jnp.where(lane < ROPE_HALF, -sin, 0.0), jnp.where(lane >= ROPE_HALF, sin, 0.0)


def _layer_params(norm_mix_w, w_in, q_norm_w, k_norm_w, attn_sinks, conv_w, conv_b, dt_bias, a_log, d_skip,
                  ssm_norm_w, w_out, norm_ffn_w, w_up, w_down):
    pad_h = lambda a: jnp.pad(a.astype(F32), (0, LANES - N_SSM_HEADS))[None, :]
    sel = (jnp.arange(LANES)[:, None] == (jnp.arange(SSM_W) // SSM_HEAD_DIM)[None, :]).astype(BF16)
    return dict(
        norm_mix=norm_mix_w.astype(F32)[None, :],
        qw=jnp.tile(q_norm_w.astype(F32), 2)[None, :], kw=jnp.tile(k_norm_w.astype(F32), 2)[None, :],
        sinks=jnp.repeat(attn_sinks.astype(F32), HEAD_DIM).reshape(Q_TILES, LANES),
        conv_w=conv_w.astype(F32), conv_b=conv_b.astype(F32)[None, :],
        dtb=pad_h(dt_bias), alog=pad_h(a_log),
        dskip=jnp.repeat(d_skip.astype(F32), SSM_HEAD_DIM)[None, :],
        ssm_nw=ssm_norm_w.astype(F32)[None, :], sel=sel,
        norm_ffn=norm_ffn_w.astype(F32)[None, :],
    )


def _stream_back(x, proj, dt_raw, pos, cache, conv_state, ssm_state, p, tm, w_out, w_up, w_down):
    batch, t, _ = x.shape
    ns = max(n for n in STREAMS_PER_STEP if batch % n == 0)
    x2d = x.reshape(batch * t, D_MODEL)
    proj3 = proj.reshape(batch, t, PROJ_W)
    tabs = _rope_tables(pos)
    if cache is None:
        attn_part = _attn_prompt_part(proj3, tabs, p["qw"], p["kw"], p["sinks"], ns)
        L = CHUNK
    else:
        past_k, past_v = cache
        attn_part = _attn_sample_part(proj3, past_k.reshape(batch, WINDOW, KV_W),
                                      past_v.reshape(batch, WINDOW, KV_W), tabs, p["qw"], p["kw"], p["sinks"], ns)
        L = min(CHUNK, t)
    hp = N_SSM_HEADS * SSM_HEAD_DIM
    ssd_part = _ssd_part(proj3, dt_raw.reshape(batch, t, LANES), conv_state,
                         ssm_state.reshape(batch, hp, D_STATE), p["conv_w"], p["conv_b"], p["dtb"], p["alog"],
                         p["dskip"], p["ssm_nw"], p["sel"], L, ns)
    (attn, new_k, new_v), (y, new_conv, new_ssm) = _fused_call([attn_part, ssd_part], (batch // ns, t // L),
                                                              "mixers")
    late = () if w_down.dtype == BF16 else (w_down,)
    x1, *cast = _outproj(attn.reshape(batch * t, ATTN_W), y.reshape(batch * t, SSM_W), w_out, x2d, tm, 1024,
                         cast=late)
    if late:
        w_down = cast.pop()
    out = _ffn(x1, p["norm_ffn"], w_up, w_down, tm, 512)
    return (out.reshape(batch, t, D_MODEL),
            new_k.reshape(batch, WINDOW, N_KV_HEADS, HEAD_DIM), new_v.reshape(batch, WINDOW, N_KV_HEADS, HEAD_DIM),
            new_conv, new_ssm.reshape(batch, N_SSM_HEADS, SSM_HEAD_DIM, D_STATE), w_down)


def kernel(x_prompt, x_sample, cache_k, cache_v, state_conv, state_ssm, norm_mix_w, w_in, q_norm_w, k_norm_w,
           attn_sinks, conv_w, conv_b, dt_bias, a_log, d_skip, ssm_norm_w, w_out, norm_ffn_w, w_up, w_down):
    depth = w_in.shape[0]
    b_p, t_p, _ = x_prompt.shape
    b_s, t_s, _ = x_sample.shape
    assert cache_k.shape[2] == WINDOW and t_p % CHUNK == 0 and t_p >= WINDOW
    assert t_s <= CHUNK and t_s % BF16_ROWS == 0
    pos_p = jnp.arange(t_p, dtype=jnp.int32)
    pos_s = PAST_LEN + jnp.arange(t_s, dtype=jnp.int32)
    hp, hs = x_prompt, x_sample
    outs_p, outs_s = [], []
    for layer in range(depth):
        p = _layer_params(norm_mix_w[layer], w_in[layer], q_norm_w[layer], k_norm_w[layer], attn_sinks[layer],
                          conv_w[layer], conv_b[layer], dt_bias[layer], a_log[layer], d_skip[layer],
                          ssm_norm_w[layer], w_out[layer], norm_ffn_w[layer], w_up[layer], w_down[layer])
        conv0 = jnp.zeros((b_p, CONV_W - 1, CONV_CH), F32)
        ssm0 = jnp.zeros((b_p, N_SSM_HEADS, SSM_HEAD_DIM, D_STATE), F32)
        tm = 512
        xs2d, xp2d = hs.reshape(b_s * t_s, D_MODEL), hp.reshape(b_p * t_p, D_MODEL)
        w_in_t = w_in[layer].T.astype(BF16)
        proj_p, dt_p, w_out16, w_up16 = _inproj(xp2d, p["norm_mix"], w_in_t, tm, 1024,
                                                cast=(w_out[layer], w_up[layer]))
        proj_s, dt_s = _inproj(xs2d, p["norm_mix"], w_in_t, tm, 1024)
        hp, *rest_p, w_down16 = _stream_back(hp, proj_p, dt_p, pos_p, None, conv0, ssm0, p, tm,
                                             w_out16, w_up16, w_down[layer])
        hs, *rest_s, _ = _stream_back(hs, proj_s, dt_s, pos_s, (cache_k[layer], cache_v[layer]),
                                      state_conv[layer], state_ssm[layer], p, tm, w_out16, w_up16, w_down16)
        outs_p.append(rest_p)
        outs_s.append(rest_s)
    stack = lambda outs, i: jnp.stack([o[i] for o in outs])
    return (hp, hs, stack(outs_p, 0), stack(outs_p, 1), stack(outs_p, 2), stack(outs_p, 3),
            stack(outs_s, 0), stack(outs_s, 1), stack(outs_s, 2), stack(outs_s, 3))
```

```python
import functools

import jax
import jax.numpy as jnp
from jax import lax
from jax.experimental import pallas as pl
from jax.experimental.pallas import tpu as pltpu

F32 = jnp.float32
BF16 = jnp.bfloat16

D_MODEL = 4096
HEAD_DIM = 64
N_Q_HEADS = 32
N_KV_HEADS = 8
ATTN_W = N_Q_HEADS * HEAD_DIM
KV_W = N_KV_HEADS * HEAD_DIM
ROPE_DIM = 16
ROPE_HALF = ROPE_DIM // 2
ROPE_THETA = 500000.0
ATTN_SCALE = HEAD_DIM ** -0.5
WINDOW = 128
CHUNK = 64
PAST_LEN = 1024
SSM_W = 2048
N_SSM_HEADS = 32
SSM_HEAD_DIM = 64
N_GROUPS = 8
D_STATE = 128
BC_W = N_GROUPS * D_STATE
GROUP_W = SSM_W // N_GROUPS
CONV_W = 4
CONV_CH = SSM_W + 2 * BC_W
FFN_HIDDEN = 4 * D_MODEL
RMS_EPS = 1e-6
NEG_INF = -1e30
LANES = 128
BF16_ROWS = 16
Q_TILES = ATTN_W // LANES
KV_TILES = KV_W // LANES
PROJ_W = ATTN_W + 2 * KV_W + 2 * SSM_W + 2 * BC_W
COL_Q, COL_K, COL_V, COL_XS, COL_Z, COL_B, COL_C = 0, 2048, 2560, 3072, 5120, 7168, 8192
VMEM_LIMIT = 62 * 1024 * 1024
STREAMS_PER_STEP = (1, 2, 4)


def _params(*sem):
    return pltpu.CompilerParams(dimension_semantics=sem, vmem_limit_bytes=VMEM_LIMIT)


def _dot(a, b):
    return jnp.dot(a, b, preferred_element_type=F32)


def _dot_nt(a, b):
    return lax.dot_general(a, b, (((1,), (1,)), ((), ())), preferred_element_type=F32)


def _split3(x):
    a = x.astype(BF16)
    r = x - a.astype(F32)
    b = r.astype(BF16)
    c = (r - b.astype(F32)).astype(BF16)
    return a, b, c


def _silu(x):
    return x / (1.0 + jnp.exp(-x))


NORM_ROWS = 16
NORM_UNROLL = 8


def _rmsnorm_rows(x_ref, nw_ref, h_scr, copy_ref=None):
    def body(i, carry):
        r = pl.ds(pl.multiple_of(i * NORM_ROWS, NORM_ROWS), NORM_ROWS)
        x = x_ref[r, :]
        ms = jnp.mean(x * x, axis=-1, keepdims=True)
        h_scr[r, :] = (x * lax.rsqrt(ms + RMS_EPS) * nw_ref[...]).astype(BF16)
        if copy_ref is not None:
            copy_ref[r, :] = x
        return carry

    lax.fori_loop(0, x_ref.shape[0] // NORM_ROWS, body, 0, unroll=NORM_UNROLL)


def _cast_riders(arrays, steps, linear_step):
    in_specs, out_specs, out_shapes = [], [], []
    for a in arrays:
        rows, cols = a.shape
        nb = max(n for n in range(1, steps + 1) if rows % n == 0 and (rows // n) % BF16_ROWS == 0)
        spec = pl.BlockSpec((rows // nb, cols), lambda *g, nb=nb: (jnp.minimum(linear_step(*g), nb - 1), 0))
        in_specs.append(spec)
        out_specs.append(spec)
        out_shapes.append(jax.ShapeDtypeStruct(a.shape, BF16))
    return in_specs, out_specs, out_shapes


def _cast_blocks(src_refs, dst_refs):
    for src, dst in zip(src_refs, dst_refs):
        dst[...] = src[...].astype(BF16)


def _inproj_kernel(*refs, n_cast):
    x_ref, nw_ref, w_ref, wdt_ref = refs[:4]
    cast_in = refs[4:4 + n_cast]
    proj_ref, dt_ref = refs[4 + n_cast:6 + n_cast]
    cast_out, h_scr = refs[6 + n_cast:6 + 2 * n_cast], refs[6 + 2 * n_cast]

    @pl.when(pl.program_id(1) == 0)
    def _():
        _rmsnorm_rows(x_ref, nw_ref, h_scr)
        real = lax.broadcasted_iota(jnp.int32, wdt_ref.shape, 0) < N_SSM_HEADS
        dt_ref[...] = _dot_nt(h_scr[...], jnp.where(real, wdt_ref[...], jnp.zeros((), BF16)))

    proj_ref[...] = _dot_nt(h_scr[...], w_ref[...]).astype(BF16)
    _cast_blocks(cast_in, cast_out)


def _inproj(x2d, norm_w, w_in_t, tm, tn, cast=()):
    t = x2d.shape[0]
    nj = PROJ_W // tn
    c_in, c_out, c_shape = _cast_riders(cast, (t // tm) * nj, lambda i, j: i * nj + j)
    return pl.pallas_call(
        functools.partial(_inproj_kernel, n_cast=len(cast)),
        grid=(t // tm, nj),
        in_specs=[
            pl.BlockSpec((tm, D_MODEL), lambda i, j: (i, 0)),
            pl.BlockSpec((1, D_MODEL), lambda i, j: (0, 0)),
            pl.BlockSpec((tn, D_MODEL), lambda i, j: (j, 0)),
            pl.BlockSpec((LANES, D_MODEL), lambda i, j: (PROJ_W // LANES, 0)),
        ] + c_in,
        out_specs=[
            pl.BlockSpec((tm, tn), lambda i, j: (i, j)),
            pl.BlockSpec((tm, LANES), lambda i, j: (i, 0)),
        ] + c_out,
        out_shape=[
            jax.ShapeDtypeStruct((t, PROJ_W), BF16),
            jax.ShapeDtypeStruct((t, LANES), F32),
        ] + c_shape,
        scratch_shapes=[pltpu.VMEM((tm, D_MODEL), BF16)],
        compiler_params=_params("arbitrary", "arbitrary"),
        name="inproj",
    )(x2d, norm_w, w_in_t, w_in_t, *cast)


def _half_ones():
    r = lax.broadcasted_iota(jnp.int32, (LANES, LANES), 0) // HEAD_DIM
    c = lax.broadcasted_iota(jnp.int32, (LANES, LANES), 1) // HEAD_DIM
    return (r == c).astype(BF16)


def _head_norm(x, w_tile):
    ss = _dot((x * x).astype(BF16), _half_ones())
    return x * lax.rsqrt(ss * (1.0 / HEAD_DIM) + RMS_EPS) * w_tile


def _rope(x, cos_t, sin_lo, sin_hi):
    return (x * cos_t + pltpu.roll(x, LANES - ROPE_HALF, 1) * sin_lo
            + pltpu.roll(x, ROPE_HALF, 1) * sin_hi)


def _lane_is_a(shape):
    return lax.broadcasted_iota(jnp.int32, shape, 1) < HEAD_DIM


def _kv_tiles(k_win, v_win):
    is_a = _lane_is_a((CHUNK, LANES))
    ones_a = jnp.where(is_a, 1.0, 0.0).astype(F32)
    ones_b = 1.0 - ones_a
    kks, vvs = [], []
    for h in range(N_KV_HEADS):
        j, odd = h // 2, h % 2
        kt = k_win[:, LANES * j:LANES * (j + 1)]
        vt = v_win[:, LANES * j:LANES * (j + 1)]
        keep = jnp.logical_not(is_a) if odd else is_a
        k_own = jnp.where(keep, kt, 0.0)
        v_own = jnp.where(keep, vt, 0.0)
        k_sw = pltpu.roll(k_own, HEAD_DIM, 1)
        v_sw = pltpu.roll(v_own, HEAD_DIM, 1)
        k_a, k_b = (k_sw, k_own) if odd else (k_own, k_sw)
        v_a, v_b = (v_sw, v_own) if odd else (v_own, v_sw)
        kks.append(jnp.concatenate([k_a, k_b], axis=0).astype(BF16))
        vv = jnp.concatenate([jnp.concatenate([v_a, ones_a], axis=1),
                              jnp.concatenate([v_b, ones_b], axis=1)], axis=0)
        vvs.append(vv.astype(BF16))
    return kks, vvs


def _prep_q(q_ref, qw_ref, cos_ref, slo_ref, shi_ref, tq):
    q2 = jnp.concatenate([q_ref[:, LANES * i:LANES * (i + 1)].astype(F32) for i in range(Q_TILES)], axis=0)
    q2 = _head_norm(q2, qw_ref[...])
    cos_t, slo, shi = cos_ref[...], slo_ref[...], shi_ref[...]
    return [(_rope(q2[tq * i:tq * (i + 1)], cos_t, slo, shi) * ATTN_SCALE).astype(BF16) for i in range(Q_TILES)]


def _attend(q_tiles, kk_of, vv_of, valid_of, sink_ref, o_ref, tq):
    is_a = _lane_is_a((2 * tq, LANES))
    for h in range(N_KV_HEADS):
        qh = jnp.concatenate([q_tiles[2 * h], q_tiles[2 * h + 1]], axis=0)
        s = _dot_nt(qh, kk_of(h))
        sw = [jnp.where(valid_of(w), s[:, LANES * w:LANES * (w + 1)], NEG_INF) for w in range(3)]
        mt = jnp.maximum(jnp.maximum(sw[0], sw[1]), sw[2])
        m_a = jnp.max(jnp.where(is_a, mt, NEG_INF), axis=-1, keepdims=True)
        m_b = jnp.max(jnp.where(is_a, NEG_INF, mt), axis=-1, keepdims=True)
        sink = jnp.concatenate([jnp.broadcast_to(sink_ref[2 * h:2 * h + 1, :], (tq, LANES)),
                                jnp.broadcast_to(sink_ref[2 * h + 1:2 * h + 2, :], (tq, LANES))], axis=0)
        m = jnp.maximum(jnp.where(is_a, m_a, m_b), sink)
        e = jnp.concatenate([jnp.exp(x - m) for x in sw], axis=1).astype(BF16)
        oa = _dot(e, vv_of(h))
        o = oa[:, :LANES] / (oa[:, LANES:] + jnp.exp(sink - m))
        o_ref[:, LANES * 2 * h:LANES * (2 * h + 1)] = o[:tq].astype(o_ref.dtype)
        o_ref[:, LANES * (2 * h + 1):LANES * (2 * h + 2)] = o[tq:].astype(o_ref.dtype)


def _prep_k(k_raw, kw_ref, cos_ref, slo_ref, shi_ref, rows):
    k2 = jnp.concatenate([k_raw[:, LANES * j:LANES * (j + 1)].astype(F32) for j in range(KV_TILES)], axis=0)
    k2 = _head_norm(k2, kw_ref[...])
    cos_t, slo, shi = cos_ref[...], slo_ref[...], shi_ref[...]
    return jnp.concatenate([_rope(k2[rows * j:rows * (j + 1)], cos_t, slo, shi) for j in range(KV_TILES)], axis=1)


class _Part:
    def __init__(self, in_specs, operands, out_specs, out_shapes, scratch, body, init=None, final=None):
        self.in_specs, self.operands, self.out_specs, self.out_shapes = in_specs, operands, out_specs, out_shapes
        self.scratch, self.body, self.init, self.final = scratch, body, init, final


def _fused_kernel(*refs, parts):
    n_in = [len(p.in_specs) for p in parts]
    n_out = [len(p.out_specs) for p in parts]
    n_scr = [len(p.scratch) for p in parts]
    pos = 0
    groups = []
    for counts in (n_in, n_out, n_scr):
        groups.append([])
        for n in counts:
            groups[-1].append(refs[pos:pos + n])
            pos += n
    args = [(p, groups[0][i], groups[1][i], groups[2][i]) for i, p in enumerate(parts)]
    c = pl.program_id(1)

    @pl.when(c == 0)
    def _():
        for p, ins, outs, scr in args:
            if p.init is not None:
                p.init(c, ins, outs, scr)

    for p, ins, outs, scr in args:
        p.body(c, ins, outs, scr)

    @pl.when(c == pl.num_programs(1) - 1)
    def _():
        for p, ins, outs, scr in args:
            if p.final is not None:
                p.final(c, ins, outs, scr)


def _fused_call(parts, grid, name):
    outs = pl.pallas_call(
        functools.partial(_fused_kernel, parts=parts),
        grid=grid,
        in_specs=[s for p in parts for s in p.in_specs],
        out_specs=[s for p in parts for s in p.out_specs],
        out_shape=[s for p in parts for s in p.out_shapes],
        scratch_shapes=[s for p in parts for s in p.scratch],
        compiler_params=_params("arbitrary", "arbitrary"),
        name=name,
    )(*[o for p in parts for o in p.operands])
    split, pos = [], 0
    for p in parts:
        split.append(outs[pos:pos + len(p.out_specs)])
        pos += len(p.out_specs)
    return split


def _attn_prompt_init(c, ins, outs, scr):
    for ring in scr:
        ring[...] = jnp.zeros_like(ring)


def _attn_prompt_body(c, ins, outs, scr):
    q_ref, k_ref, v_ref, cos_ref, slo_ref, shi_ref, qw_ref, kw_ref, sink_ref = ins
    o_ref, nk_ref, nv_ref = outs
    kk_scr, vv_scr = scr
    slot = c % 3
    for s in range(q_ref.shape[0]):
        k_fin = _prep_k(k_ref[s], kw_ref, cos_ref, slo_ref, shi_ref, CHUNK)
        v_cur = v_ref[s].astype(F32)
        nk_ref[s] = k_fin
        nv_ref[s] = v_cur
        kks, vvs = _kv_tiles(k_fin, v_cur)
        for h in range(N_KV_HEADS):
            kk_scr[s, slot, h] = kks[h]
            vv_scr[s, slot, h] = vvs[h]

        q_tiles = _prep_q(q_ref.at[s], qw_ref, cos_ref, slo_ref, shi_ref, CHUNK)
        _attend(q_tiles,
                lambda h, s=s: jnp.concatenate([kk_scr[s, w, h] for w in range(3)], axis=0),
                lambda h, s=s: jnp.concatenate([vv_scr[s, w, h] for w in range(3)], axis=0),
                lambda w: (c - w + 3) % 3 <= c,
                sink_ref, o_ref.at[s], CHUNK)


def _attn_prompt_part(proj3, rope_tabs, qw_tile, kw_tile, sink_tab, ns):
    batch, t, _ = proj3.shape
    nc = t // CHUNK
    keep = WINDOW // CHUNK
    cos_t, slo, shi = rope_tabs
    tab_spec = pl.BlockSpec((CHUNK, LANES), lambda g, c: (c, 0))
    const = lambda shape: pl.BlockSpec(shape, lambda g, c: (0,) * len(shape))
    keep_spec = pl.BlockSpec((ns, CHUNK, KV_W), lambda g, c: (g, jnp.maximum(c - (nc - keep), 0), 0))
    return _Part(
        in_specs=[
            pl.BlockSpec((ns, CHUNK, ATTN_W), lambda g, c: (g, c, COL_Q // ATTN_W)),
            pl.BlockSpec((ns, CHUNK, KV_W), lambda g, c: (g, c, COL_K // KV_W)),
            pl.BlockSpec((ns, CHUNK, KV_W), lambda g, c: (g, c, COL_V // KV_W)),
            tab_spec, tab_spec, tab_spec,
            const((1, LANES)), const((1, LANES)), const((Q_TILES, LANES)),
        ],
        operands=[proj3, proj3, proj3, cos_t, slo, shi, qw_tile, kw_tile, sink_tab],
        out_specs=[pl.BlockSpec((ns, CHUNK, ATTN_W), lambda g, c: (g, c, 0)), keep_spec, keep_spec],
        out_shapes=[
            jax.ShapeDtypeStruct((batch, t, ATTN_W), BF16),
            jax.ShapeDtypeStruct((batch, WINDOW, KV_W), F32),
            jax.ShapeDtypeStruct((batch, WINDOW, KV_W), F32),
        ],
        scratch=[
            pltpu.VMEM((ns, 3, N_KV_HEADS, 2 * CHUNK, LANES), BF16),
            pltpu.VMEM((ns, 3, N_KV_HEADS, 2 * CHUNK, 2 * LANES), BF16),
        ],
        body=_attn_prompt_body, init=_attn_prompt_init)


def _attn_sample_body(c, ins, outs, scr):
    q_ref, k_ref, v_ref, ck_ref, cv_ref, cos_ref, slo_ref, shi_ref, qw_ref, kw_ref, sink_ref = ins
    o_ref, nk_ref, nv_ref = outs
    tq = q_ref.shape[1]
    new_valid = lax.broadcasted_iota(jnp.int32, (2 * tq, LANES), 1) % HEAD_DIM < tq
    pad = jnp.zeros((CHUNK - tq, KV_W), F32)
    for s in range(q_ref.shape[0]):
        k_fin = _prep_k(k_ref[s], kw_ref, cos_ref, slo_ref, shi_ref, tq)
        v_cur = v_ref[s].astype(F32)
        nk_ref[s, 0:WINDOW - tq, :] = ck_ref[s, tq:WINDOW, :]
        nv_ref[s, 0:WINDOW - tq, :] = cv_ref[s, tq:WINDOW, :]
        nk_ref[s, WINDOW - tq:WINDOW, :] = k_fin
        nv_ref[s, WINDOW - tq:WINDOW, :] = v_cur
        wins = [(ck_ref[s, :CHUNK, :], cv_ref[s, :CHUNK, :]),
                (ck_ref[s, CHUNK:, :], cv_ref[s, CHUNK:, :]),
                (jnp.concatenate([k_fin, pad], axis=0), jnp.concatenate([v_cur, pad], axis=0))]
        tiles = [_kv_tiles(kw, vw) for kw, vw in wins]
        q_tiles = _prep_q(q_ref.at[s], qw_ref, cos_ref, slo_ref, shi_ref, tq)
        _attend(q_tiles,
                lambda h, tiles=tiles: jnp.concatenate([tiles[w][0][h] for w in range(3)], axis=0),
                lambda h, tiles=tiles: jnp.concatenate([tiles[w][1][h] for w in range(3)], axis=0),
                lambda w: new_valid if w == 2 else True,
                sink_ref, o_ref.at[s], tq)


def _attn_sample_part(proj3, cache_k, cache_v, rope_tabs, qw_tile, kw_tile, sink_tab, ns):
    batch, tq, _ = proj3.shape
    cos_t, slo, shi = rope_tabs
    tab_spec = pl.BlockSpec((tq, LANES), lambda g, c: (0, 0))
    const = lambda shape: pl.BlockSpec(shape, lambda g, c: (0,) * len(shape))
    cache_spec = pl.BlockSpec((ns, WINDOW, KV_W), lambda g, c: (g, 0, 0))
    return _Part(
        in_specs=[
            pl.BlockSpec((ns, tq, ATTN_W), lambda g, c: (g, 0, COL_Q // ATTN_W)),
            pl.BlockSpec((ns, tq, KV_W), lambda g, c: (g, 0, COL_K // KV_W)),
            pl.BlockSpec((ns, tq, KV_W), lambda g, c: (g, 0, COL_V // KV_W)),
            cache_spec, cache_spec,
            tab_spec, tab_spec, tab_spec,
            const((1, LANES)), const((1, LANES)), const((Q_TILES, LANES)),
        ],
        operands=[proj3, proj3, proj3, cache_k, cache_v, cos_t, slo, shi, qw_tile, kw_tile, sink_tab],
        out_specs=[pl.BlockSpec((ns, tq, ATTN_W), lambda g, c: (g, 0, 0)), cache_spec, cache_spec],
        out_shapes=[
            jax.ShapeDtypeStruct((batch, tq, ATTN_W), BF16),
            jax.ShapeDtypeStruct((batch, WINDOW, KV_W), F32),
            jax.ShapeDtypeStruct((batch, WINDOW, KV_W), F32),
        ],
        scratch=[], body=_attn_sample_body)


def _pad_rows(a, rows):
    if a.shape[0] == rows:
        return a
    return jnp.concatenate([a, jnp.zeros((rows - a.shape[0], a.shape[1]), a.dtype)], axis=0)


def _ssd_chunk(c, xs0_ref, xs1_ref, z0_ref, z1_ref, b_ref, c_ref, dt_ref, cst_ref, cw_ref, cb_ref, dtb_ref,
               alog_ref, dskip_ref, nw_ref, sel_ref, shift_ref, y_ref, ext_scr, st_scr, L):
    half = SSM_W // 2
    ext_scr[L:2 * L, 0:half] = xs0_ref[...]
    ext_scr[L:2 * L, half:SSM_W] = xs1_ref[...]
    ext_scr[L:2 * L, SSM_W:SSM_W + BC_W] = b_ref[...]
    ext_scr[L:2 * L, SSM_W + BC_W:CONV_CH] = c_ref[...]
    delayed = _dot(shift_ref[...], ext_scr[...])
    cur = ext_scr[L:2 * L, :].astype(F32)
    conv = cb_ref[...] + cur * cw_ref[CONV_W - 1:CONV_W, :]
    for i in range(CONV_W - 1):
        conv = conv + delayed[L * i:L * (i + 1)] * cw_ref[i:i + 1, :]
    s = cst_ref[...]
    w0, w1, w2 = cw_ref[0:1, :], cw_ref[1:2, :], cw_ref[2:3, :]
    head = jnp.concatenate([w0 * s[0:1] + w1 * s[1:2] + w2 * s[2:3], w0 * s[1:2] + w1 * s[2:3], w0 * s[2:3],
                            jnp.zeros((8 - (CONV_W - 1), CONV_CH), F32)], axis=0)
    conv = jnp.concatenate([conv[0:8] + jnp.where(c == 0, head, 0.0), conv[8:]], axis=0)
    conv = _silu(conv)
    xs = conv[:, :SSM_W]
    ext_scr[0:L, :] = ext_scr[L:2 * L, :]

    x_dt = dt_ref[...] + dtb_ref[...]
    dt = jnp.maximum(x_dt, 0.0) + jnp.log1p(jnp.exp(-jnp.abs(x_dt)))
    ad = dt * (-jnp.exp(alog_ref[...]))
    li = lax.broadcasted_iota(jnp.int32, (L, L), 0)
    si = lax.broadcasted_iota(jnp.int32, (L, L), 1)
    tril = (si <= li).astype(BF16)
    a_cum = _dot(jnp.concatenate([tril] * 3, axis=1), jnp.concatenate(_split3(ad), axis=0))
    sel = sel_ref[...]
    ex = _dot(jnp.concatenate(_split3(a_cum) + _split3(dt), axis=0), sel)
    col = ex[0:L] + ex[L:2 * L] + ex[2 * L:3 * L]
    dt_all = ex[3 * L:4 * L] + ex[4 * L:5 * L] + ex[5 * L:6 * L]
    lane_s = lax.broadcasted_iota(jnp.int32, (L, SSM_W), 1) % SSM_HEAD_DIM
    row_l = lax.broadcasted_iota(jnp.int32, (L, SSM_W), 0)
    diag = jnp.where(lane_s == row_l, col, 0.0)
    row = _dot(jnp.ones((L, 3 * L), BF16), jnp.concatenate(_split3(diag), axis=0))
    lmat = jnp.exp(jnp.where(lane_s <= row_l, col - row, NEG_INF))
    a_last = col[L - 1:L, :]
    exp_a = jnp.exp(col)
    decay = jnp.exp(a_last - col)
    exp_last = jnp.exp(a_last)

    xd = xs * dt_all
    xdd = xd * decay
    bi = lax.broadcasted_iota(jnp.int32, (GROUP_W, GROUP_W), 0) // SSM_HEAD_DIM
    bj = lax.broadcasted_iota(jnp.int32, (GROUP_W, GROUP_W), 1) // SSM_HEAD_DIM
    blockdiag = bi == bj

    for g in range(N_GROUPS):
        gs = slice(GROUP_W * g, GROUP_W * (g + 1))
        b_g = conv[:, SSM_W + D_STATE * g:SSM_W + D_STATE * (g + 1)]
        c_g = conv[:, SSM_W + BC_W + D_STATE * g:SSM_W + BC_W + D_STATE * (g + 1)].astype(BF16)
        b_pad = _pad_rows(b_g, CHUNK)
        cb = _dot_nt(c_g, jnp.concatenate([b_pad.astype(BF16)] * 4, axis=0))
        m_g = (cb * lmat[:, gs]).astype(BF16)
        xd_g = _pad_rows(xd[:, gs], CHUNK).astype(BF16)
        xd_bd = jnp.where(blockdiag, jnp.concatenate([xd_g] * 4, axis=0), jnp.zeros((), BF16))
        y_diag = _dot(m_g, xd_bd)
        st_g = st_scr[:, gs]
        y_off = _dot(c_g, st_g.astype(BF16)) * exp_a[:, gs]
        new_st = _dot(b_pad.T.astype(BF16), _pad_rows(xdd[:, gs], CHUNK).astype(BF16))
        st_scr[:, gs] = st_g * exp_last[:, gs] + new_st

        y = y_diag + y_off + xs[:, gs] * dskip_ref[:, gs]
        z_ref = z0_ref if g < N_GROUPS // 2 else z1_ref
        zs = slice(GROUP_W * (g % (N_GROUPS // 2)), GROUP_W * (g % (N_GROUPS // 2) + 1))
        gated = y * _silu(z_ref[:, zs].astype(F32))
        ms = jnp.mean(gated * gated, axis=-1, keepdims=True)
        y_ref[:, gs] = (gated * lax.rsqrt(ms + RMS_EPS) * nw_ref[:, gs]).astype(y_ref.dtype)


def _ssd_init(c, ins, outs, scr):
    st0_ref = ins[8]
    ext_scr, st_scr = scr
    ns, L = ext_scr.shape[0], ext_scr.shape[1] // 2
    for s in range(ns):
        ext_scr[s, 0:L, :] = jnp.zeros((L, CONV_CH), BF16)
        st_scr[s] = st0_ref[s].T


def _ssd_body(c, ins, outs, scr):
    (xs0_ref, xs1_ref, z0_ref, z1_ref, b_ref, c_ref, dt_ref, cst_ref, _, cw_ref, cb_ref, dtb_ref, alog_ref,
     dskip_ref, nw_ref, sel_ref, shift_ref) = ins
    y_ref = outs[0]
    ext_scr, st_scr = scr
    ns, L = dt_ref.shape[0], dt_ref.shape[1]
    for s in range(ns):
        _ssd_chunk(c, xs0_ref.at[s], xs1_ref.at[s], z0_ref.at[s], z1_ref.at[s], b_ref.at[s], c_ref.at[s],
                   dt_ref.at[s], cst_ref.at[s], cw_ref, cb_ref, dtb_ref, alog_ref, dskip_ref, nw_ref,
                   sel_ref, shift_ref, y_ref.at[s], ext_scr.at[s], st_scr.at[s], L)


def _ssd_final(c, ins, outs, scr):
    _, ncv_ref, nst_ref = outs
    ext_scr, st_scr = scr
    ns, L = ext_scr.shape[0], ext_scr.shape[1] // 2
    for s in range(ns):
        tail = ext_scr[s, L - BF16_ROWS:L, :].astype(F32)
        ncv_ref[s] = tail[BF16_ROWS - (CONV_W - 1):BF16_ROWS, :]
        nst_ref[s] = st_scr[s].T


def _shift_matrix(L):
    r = jnp.arange(3 * L)
    src = L + r % L - (CONV_W - 1) + r // L
    return (jnp.arange(2 * L)[None, :] == src[:, None]).astype(BF16)


def _ssd_part(proj3, dt3, conv_state, ssm_state, conv_w, conv_b, dtb, alog, dskip_all, ssm_nw, sel, L, ns):
    batch, t, _ = proj3.shape
    const = lambda shape: pl.BlockSpec(shape, lambda g, c: (0,) * len(shape))
    per_g = lambda shape: pl.BlockSpec((ns,) + shape, lambda g, c: (g,) + (0,) * len(shape))
    col = lambda start: pl.BlockSpec((ns, L, BC_W), lambda g, c: (g, c, start // BC_W))
    hp = N_SSM_HEADS * SSM_HEAD_DIM
    return _Part(
        in_specs=[
            col(COL_XS), col(COL_XS + BC_W), col(COL_Z), col(COL_Z + BC_W), col(COL_B), col(COL_C),
            pl.BlockSpec((ns, L, LANES), lambda g, c: (g, c, 0)),
            per_g((CONV_W - 1, CONV_CH)),
            per_g((hp, D_STATE)),
            const((CONV_W, CONV_CH)), const((1, CONV_CH)), const((1, LANES)), const((1, LANES)),
            const((1, SSM_W)), const((1, SSM_W)), const((LANES, SSM_W)), const((3 * L, 2 * L)),
        ],
        out_specs=[
            pl.BlockSpec((ns, L, SSM_W), lambda g, c: (g, c, 0)),
            per_g((CONV_W - 1, CONV_CH)),
            per_g((hp, D_STATE)),
        ],
        out_shapes=[
            jax.ShapeDtypeStruct((batch, t, SSM_W), BF16),
            jax.ShapeDtypeStruct((batch, CONV_W - 1, CONV_CH), F32),
            jax.ShapeDtypeStruct((batch, hp, D_STATE), F32),
        ],
        scratch=[
            pltpu.VMEM((ns, 2 * L, CONV_CH), BF16),
            pltpu.VMEM((ns, D_STATE, hp), F32),
        ],
        operands=[proj3, proj3, proj3, proj3, proj3, proj3, dt3, conv_state, ssm_state, conv_w, conv_b, dtb,
                  alog, dskip_all, ssm_nw, sel, _shift_matrix(L)],
        body=_ssd_body, init=_ssd_init, final=_ssd_final)


def _outproj_kernel(*refs, n_cast):
    a_ref, y_ref, wa_ref, wy_ref, x_ref = refs[:5]
    cast_in, o_ref, cast_out = refs[5:5 + n_cast], refs[5 + n_cast], refs[6 + n_cast:]
    o_ref[...] = x_ref[...] + _dot(a_ref[...], wa_ref[...]) + _dot(y_ref[...], wy_ref[...])
    _cast_blocks(cast_in, cast_out)


def _outproj(attn, y, w_out, x2d, tm, tn, cast=()):
    t = x2d.shape[0]
    half = D_MODEL // 2
    ni = t // tm
    c_in, c_out, c_shape = _cast_riders(cast, (D_MODEL // tn) * ni, lambda j, i: j * ni + i)
    return pl.pallas_call(
        functools.partial(_outproj_kernel, n_cast=len(cast)),
        grid=(D_MODEL // tn, ni),
        in_specs=[
            pl.BlockSpec((tm, half), lambda j, i: (i, 0)),
            pl.BlockSpec((tm, half), lambda j, i: (i, 0)),
            pl.BlockSpec((half, tn), lambda j, i: (0, j)),
            pl.BlockSpec((half, tn), lambda j, i: (1, j)),
            pl.BlockSpec((tm, tn), lambda j, i: (i, j)),
        ] + c_in,
        out_specs=[pl.BlockSpec((tm, tn), lambda j, i: (i, j))] + c_out,
        out_shape=[jax.ShapeDtypeStruct((t, D_MODEL), F32)] + c_shape,
        compiler_params=_params("arbitrary", "arbitrary"),
        name="outproj",
    )(attn, y, w_out, w_out, x2d, *cast)


def _ffn_kernel(x_ref, nw_ref, wu_ref, wd_ref, o_ref, h_scr, *, n_split):
    @pl.when(pl.program_id(1) == 0)
    def _():
        _rmsnorm_rows(x_ref, nw_ref, h_scr, copy_ref=o_ref)

    u = jnp.maximum(_dot(h_scr[...], wu_ref[...]), 0.0)
    u = (u * u).astype(BF16)
    wn = D_MODEL // n_split
    for n in range(n_split):
        o_ref[:, wn * n:wn * (n + 1)] += _dot(u, wd_ref[:, wn * n:wn * (n + 1)])


def _ffn(x2d, norm_w, w_up, w_down, tm, tf):
    t = x2d.shape[0]
    return pl.pallas_call(
        functools.partial(_ffn_kernel, n_split=4),
        grid=(t // tm, FFN_HIDDEN // tf),
        in_specs=[
            pl.BlockSpec((tm, D_MODEL), lambda i, f: (i, 0)),
            pl.BlockSpec((1, D_MODEL), lambda i, f: (0, 0)),
            pl.BlockSpec((D_MODEL, tf), lambda i, f: (0, f)),
            pl.BlockSpec((tf, D_MODEL), lambda i, f: (f, 0)),
        ],
        out_specs=pl.BlockSpec((tm, D_MODEL), lambda i, f: (i, 0)),
        out_shape=jax.ShapeDtypeStruct((t, D_MODEL), F32),
        scratch_shapes=[pltpu.VMEM((tm, D_MODEL), BF16)],
        compiler_params=_params("arbitrary", "arbitrary"),
        name="ffn",
    )(x2d, norm_w, w_up, w_down)


def _rope_tables(pos):
    inv_freq = ROPE_THETA ** (-jnp.arange(ROPE_HALF, dtype=F32) / ROPE_HALF)
    lane = jnp.arange(LANES) % HEAD_DIM
    freq = jnp.where(lane < ROPE_DIM, inv_freq[lane % ROPE_HALF], 0.0)
    ang = pos.astype(F32)[:, None] * freq[None, :]
    cos, sin = jnp.cos(ang), jnp.sin(ang)
    return cos, jnp.where(lane < ROPE_HALF, -sin, 0.0), jnp.where(lane >= ROPE_HALF, sin, 0.0)


def _layer_params(norm_mix_w, w_in, q_norm_w, k_norm_w, attn_sinks, conv_w, conv_b, dt_bias, a_log, d_skip,
                  ssm_norm_w, w_out, norm_ffn_w, w_up, w_down):
    pad_h = lambda a: jnp.pad(a.astype(F32), (0, LANES - N_SSM_HEADS))[None, :]
    sel = (jnp.arange(LANES)[:, None] == (jnp.arange(SSM_W) // SSM_HEAD_DIM)[None, :]).astype(BF16)
    return dict(
        norm_mix=norm_mix_w.astype(F32)[None, :],
        qw=jnp.tile(q_norm_w.astype(F32), 2)[None, :], kw=jnp.tile(k_norm_w.astype(F32), 2)[None, :],
        sinks=jnp.repeat(attn_sinks.astype(F32), HEAD_DIM).reshape(Q_TILES, LANES),
        conv_w=conv_w.astype(F32), conv_b=conv_b.astype(F32)[None, :],
        dtb=pad_h(dt_bias), alog=pad_h(a_log),
        dskip=jnp.repeat(d_skip.astype(F32), SSM_HEAD_DIM)[None, :],
        ssm_nw=ssm_norm_w.astype(F32)[None, :], sel=sel,
        norm_ffn=norm_ffn_w.astype(F32)[None, :],
    )


def _stream_back(x, proj, dt_raw, pos, cache, conv_state, ssm_state, p, tm, w_out, w_up, w_down):
    batch, t, _ = x.shape
    ns = max(n for n in STREAMS_PER_STEP if batch % n == 0)
    x2d = x.reshape(batch * t, D_MODEL)
    proj3 = proj.reshape(batch, t, PROJ_W)
    tabs = _rope_tables(pos)
    if cache is None:
        attn_part = _attn_prompt_part(proj3, tabs, p["qw"], p["kw"], p["sinks"], ns)
        L = CHUNK
    else:
        past_k, past_v = cache
        attn_part = _attn_sample_part(proj3, past_k.reshape(batch, WINDOW, KV_W),
                                      past_v.reshape(batch, WINDOW, KV_W), tabs, p["qw"], p["kw"], p["sinks"], ns)
        L = min(CHUNK, t)
    hp = N_SSM_HEADS * SSM_HEAD_DIM
    ssd_part = _ssd_part(proj3, dt_raw.reshape(batch, t, LANES), conv_state,
                         ssm_state.reshape(batch, hp, D_STATE), p["conv_w"], p["conv_b"], p["dtb"], p["alog"],
                         p["dskip"], p["ssm_nw"], p["sel"], L, ns)
    (attn, new_k, new_v), (y, new_conv, new_ssm) = _fused_call([attn_part, ssd_part], (batch // ns, t // L),
                                                              "mixers")
    late = () if w_down.dtype == BF16 else (w_down,)
    x1, *cast = _outproj(attn.reshape(batch * t, ATTN_W), y.reshape(batch * t, SSM_W), w_out, x2d, tm, 1024,
                         cast=late)
    if late:
        w_down = cast.pop()
    out = _ffn(x1, p["norm_ffn"], w_up, w_down, tm, 512)
    return (out.reshape(batch, t, D_MODEL),
            new_k.reshape(batch, WINDOW, N_KV_HEADS, HEAD_DIM), new_v.reshape(batch, WINDOW, N_KV_HEADS, HEAD_DIM),
            new_conv, new_ssm.reshape(batch, N_SSM_HEADS, SSM_HEAD_DIM, D_STATE), w_down)


def kernel(x_prompt, x_sample, cache_k, cache_v, state_conv, state_ssm, norm_mix_w, w_in, q_norm_w, k_norm_w,
           attn_sinks, conv_w, conv_b, dt_bias, a_log, d_skip, ssm_norm_w, w_out, norm_ffn_w, w_up, w_down):
    depth = w_in.shape[0]
    b_p, t_p, _ = x_prompt.shape
    b_s, t_s, _ = x_sample.shape
    assert cache_k.shape[2] == WINDOW and t_p % CHUNK == 0 and t_p >= WINDOW
    assert t_s <= CHUNK and t_s % BF16_ROWS == 0
    pos_p = jnp.arange(t_p, dtype=jnp.int32)
    pos_s = PAST_LEN + jnp.arange(t_s, dtype=jnp.int32)
    hp, hs = x_prompt, x_sample
    outs_p, outs_s = [], []
    for layer in range(depth):
        p = _layer_params(norm_mix_w[layer], w_in[layer], q_norm_w[layer], k_norm_w[layer], attn_sinks[layer],
                          conv_w[layer], conv_b[layer], dt_bias[layer], a_log[layer], d_skip[layer],
                          ssm_norm_w[layer], w_out[layer], norm_ffn_w[layer], w_up[layer], w_down[layer])
        conv0 = jnp.zeros((b_p, CONV_W - 1, CONV_CH), F32)
        ssm0 = jnp.zeros((b_p, N_SSM_HEADS, SSM_HEAD_DIM, D_STATE), F32)
        tm = 512
        xs2d, xp2d = hs.reshape(b_s * t_s, D_MODEL), hp.reshape(b_p * t_p, D_MODEL)
        w_in_t = w_in[layer].T.astype(BF16)
        proj_p, dt_p, w_out16, w_up16 = _inproj(xp2d, p["norm_mix"], w_in_t, tm, 1536,
                                                cast=(w_out[layer], w_up[layer]))
        proj_s, dt_s = _inproj(xs2d, p["norm_mix"], w_in_t, tm, 1536)
        hp, *rest_p, w_down16 = _stream_back(hp, proj_p, dt_p, pos_p, None, conv0, ssm0, p, tm,
                                             w_out16, w_up16, w_down[layer])
        hs, *rest_s, _ = _stream_back(hs, proj_s, dt_s, pos_s, (cache_k[layer], cache_v[layer]),
                                      state_conv[layer], state_ssm[layer], p, tm, w_out16, w_up16, w_down16)
        outs_p.append(rest_p)
        outs_s.append(rest_s)
    stack = lambda outs, i: jnp.stack([o[i] for o in outs])
    return (hp, hs, stack(outs_p, 0), stack(outs_p, 1), stack(outs_p, 2), stack(outs_p, 3),
            stack(outs_s, 0), stack(outs_s, 1), stack(outs_s, 2), stack(outs_s, 3))
```

```python
import functools

import jax
import jax.numpy as jnp
from jax import lax
from jax.experimental import pallas as pl
from jax.experimental.pallas import tpu as pltpu

F32 = jnp.float32
BF16 = jnp.bfloat16

D_MODEL = 4096
HEAD_DIM = 64
N_Q_HEADS = 32
N_KV_HEADS = 8
ATTN_W = N_Q_HEADS * HEAD_DIM
KV_W = N_KV_HEADS * HEAD_DIM
ROPE_DIM = 16
ROPE_HALF = ROPE_DIM // 2
ROPE_THETA = 500000.0
ATTN_SCALE = HEAD_DIM ** -0.5
WINDOW = 128
CHUNK = 64
PAST_LEN = 1024
SSM_W = 2048
N_SSM_HEADS = 32
SSM_HEAD_DIM = 64
N_GROUPS = 8
D_STATE = 128
BC_W = N_GROUPS * D_STATE
GROUP_W = SSM_W // N_GROUPS
CONV_W = 4
CONV_CH = SSM_W + 2 * BC_W
FFN_HIDDEN = 4 * D_MODEL
RMS_EPS = 1e-6
NEG_INF = -1e30
LANES = 128
BF16_ROWS = 16
Q_TILES = ATTN_W // LANES
KV_TILES = KV_W // LANES
PROJ_W = ATTN_W + 2 * KV_W + 2 * SSM_W + 2 * BC_W
COL_Q, COL_K, COL_V, COL_XS, COL_Z, COL_B, COL_C = 0, 2048, 2560, 3072, 5120, 7168, 8192
QKV_W = ATTN_W + 2 * KV_W
VMEM_LIMIT = 62 * 1024 * 1024
STREAMS_PER_STEP = (1, 2, 4)


def _params(*sem):
    return pltpu.CompilerParams(dimension_semantics=sem, vmem_limit_bytes=VMEM_LIMIT)


def _dot(a, b):
    return jnp.dot(a, b, preferred_element_type=F32)


def _dot_nt(a, b):
    return lax.dot_general(a, b, (((1,), (1,)), ((), ())), preferred_element_type=F32)


def _split3(x):
    a = x.astype(BF16)
    r = x - a.astype(F32)
    b = r.astype(BF16)
    c = (r - b.astype(F32)).astype(BF16)
    return a, b, c


def _silu(x):
    return x / (1.0 + jnp.exp(-x))


NORM_ROWS = 16
NORM_UNROLL = 8


def _rmsnorm_rows(x_ref, nw_ref, h_scr, copy_ref=None):
    def body(i, carry):
        r = pl.ds(pl.multiple_of(i * NORM_ROWS, NORM_ROWS), NORM_ROWS)
        x = x_ref[r, :]
        ms = jnp.mean(x * x, axis=-1, keepdims=True)
        h_scr[r, :] = (x * lax.rsqrt(ms + RMS_EPS) * nw_ref[...]).astype(BF16)
        if copy_ref is not None:
            copy_ref[r, :] = x
        return carry

    lax.fori_loop(0, x_ref.shape[0] // NORM_ROWS, body, 0, unroll=NORM_UNROLL)


def _cast_riders(arrays, steps, linear_step):
    in_specs, out_specs, out_shapes = [], [], []
    for a in arrays:
        rows, cols = a.shape
        nb = max(n for n in range(1, steps + 1) if rows % n == 0 and (rows // n) % BF16_ROWS == 0)
        spec = pl.BlockSpec((rows // nb, cols), lambda *g, nb=nb: (jnp.minimum(linear_step(*g), nb - 1), 0))
        in_specs.append(spec)
        out_specs.append(spec)
        out_shapes.append(jax.ShapeDtypeStruct(a.shape, BF16))
    return in_specs, out_specs, out_shapes


def _cast_blocks(src_refs, dst_refs):
    for src, dst in zip(src_refs, dst_refs):
        dst[...] = src[...].astype(BF16)


def _inproj_kernel(*refs, n_cast):
    x_ref, nw_ref, w_ref, wdt_ref = refs[:4]
    cast_in = refs[4:4 + n_cast]
    proj_ref, dt_ref = refs[4 + n_cast:6 + n_cast]
    cast_out, h_scr = refs[6 + n_cast:6 + 2 * n_cast], refs[6 + 2 * n_cast]

    @pl.when(pl.program_id(1) == 0)
    def _():
        _rmsnorm_rows(x_ref, nw_ref, h_scr)
        real = lax.broadcasted_iota(jnp.int32, wdt_ref.shape, 0) < N_SSM_HEADS
        dt_ref[...] = _dot_nt(h_scr[...], jnp.where(real, wdt_ref[...], jnp.zeros((), BF16)))

    proj_ref[...] = _dot_nt(h_scr[...], w_ref[...]).astype(BF16)
    _cast_blocks(cast_in, cast_out)


def _inproj(x2d, norm_w, w_in_t, tm, tn, cast=()):
    t = x2d.shape[0]
    nj = PROJ_W // tn
    c_in, c_out, c_shape = _cast_riders(cast, (t // tm) * nj, lambda i, j: i * nj + j)
    return pl.pallas_call(
        functools.partial(_inproj_kernel, n_cast=len(cast)),
        grid=(t // tm, nj),
        in_specs=[
            pl.BlockSpec((tm, D_MODEL), lambda i, j: (i, 0)),
            pl.BlockSpec((1, D_MODEL), lambda i, j: (0, 0)),
            pl.BlockSpec((tn, D_MODEL), lambda i, j: (j, 0)),
            pl.BlockSpec((LANES, D_MODEL), lambda i, j: (PROJ_W // LANES, 0)),
        ] + c_in,
        out_specs=[
            pl.BlockSpec((tm, tn), lambda i, j: (i, j)),
            pl.BlockSpec((tm, LANES), lambda i, j: (i, 0)),
        ] + c_out,
        out_shape=[
            jax.ShapeDtypeStruct((t, PROJ_W), BF16),
            jax.ShapeDtypeStruct((t, LANES), F32),
        ] + c_shape,
        scratch_shapes=[pltpu.VMEM((tm, D_MODEL), BF16)],
        compiler_params=_params("arbitrary", "arbitrary"),
        name="inproj",
    )(x2d, norm_w, w_in_t, w_in_t, *cast)


def _half_ones():
    r = lax.broadcasted_iota(jnp.int32, (LANES, LANES), 0) // HEAD_DIM
    c = lax.broadcasted_iota(jnp.int32, (LANES, LANES), 1) // HEAD_DIM
    return (r == c).astype(BF16)


def _head_norm(x, w_tile):
    ss = _dot((x * x).astype(BF16), _half_ones())
    return x * lax.rsqrt(ss * (1.0 / HEAD_DIM) + RMS_EPS) * w_tile


def _rope(x, cos_t, sin_lo, sin_hi):
    return (x * cos_t + pltpu.roll(x, LANES - ROPE_HALF, 1) * sin_lo
            + pltpu.roll(x, ROPE_HALF, 1) * sin_hi)


def _lane_is_a(shape):
    return lax.broadcasted_iota(jnp.int32, shape, 1) < HEAD_DIM


def _kv_tiles(k_win, v_win):
    is_a = _lane_is_a((CHUNK, LANES))
    ones_a = jnp.where(is_a, 1.0, 0.0).astype(F32)
    ones_b = 1.0 - ones_a
    kks, vvs = [], []
    for h in range(N_KV_HEADS):
        j, odd = h // 2, h % 2
        kt = k_win[:, LANES * j:LANES * (j + 1)]
        vt = v_win[:, LANES * j:LANES * (j + 1)]
        keep = jnp.logical_not(is_a) if odd else is_a
        k_own = jnp.where(keep, kt, 0.0)
        v_own = jnp.where(keep, vt, 0.0)
        k_sw = pltpu.roll(k_own, HEAD_DIM, 1)
        v_sw = pltpu.roll(v_own, HEAD_DIM, 1)
        k_a, k_b = (k_sw, k_own) if odd else (k_own, k_sw)
        v_a, v_b = (v_sw, v_own) if odd else (v_own, v_sw)
        kks.append(jnp.concatenate([k_a, k_b], axis=0).astype(BF16))
        vv = jnp.concatenate([jnp.concatenate([v_a, ones_a], axis=1),
                              jnp.concatenate([v_b, ones_b], axis=1)], axis=0)
        vvs.append(vv.astype(BF16))
    return kks, vvs


def _prep_q(q_ref, qw_ref, cos_ref, slo_ref, shi_ref, tq):
    q2 = jnp.concatenate([q_ref[:, LANES * i:LANES * (i + 1)].astype(F32) for i in range(Q_TILES)], axis=0)
    q2 = _head_norm(q2, qw_ref[...])
    cos_t, slo, shi = cos_ref[...], slo_ref[...], shi_ref[...]
    return [(_rope(q2[tq * i:tq * (i + 1)], cos_t, slo, shi) * ATTN_SCALE).astype(BF16) for i in range(Q_TILES)]


def _attend(q_tiles, kk_of, vv_of, valid_of, sink_ref, o_ref, tq):
    is_a = _lane_is_a((2 * tq, LANES))
    for h in range(N_KV_HEADS):
        qh = jnp.concatenate([q_tiles[2 * h], q_tiles[2 * h + 1]], axis=0)
        s = _dot_nt(qh, kk_of(h))
        sw = [jnp.where(valid_of(w), s[:, LANES * w:LANES * (w + 1)], NEG_INF) for w in range(3)]
        mt = jnp.maximum(jnp.maximum(sw[0], sw[1]), sw[2])
        m_a = jnp.max(jnp.where(is_a, mt, NEG_INF), axis=-1, keepdims=True)
        m_b = jnp.max(jnp.where(is_a, NEG_INF, mt), axis=-1, keepdims=True)
        sink = jnp.concatenate([jnp.broadcast_to(sink_ref[2 * h:2 * h + 1, :], (tq, LANES)),
                                jnp.broadcast_to(sink_ref[2 * h + 1:2 * h + 2, :], (tq, LANES))], axis=0)
        m = jnp.maximum(jnp.where(is_a, m_a, m_b), sink)
        e = jnp.concatenate([jnp.exp(x - m) for x in sw], axis=1).astype(BF16)
        oa = _dot(e, vv_of(h))
        o = oa[:, :LANES] / (oa[:, LANES:] + jnp.exp(sink - m))
        o_ref[:, LANES * 2 * h:LANES * (2 * h + 1)] = o[:tq].astype(o_ref.dtype)
        o_ref[:, LANES * (2 * h + 1):LANES * (2 * h + 2)] = o[tq:].astype(o_ref.dtype)


def _prep_k(k_raw, kw_ref, cos_ref, slo_ref, shi_ref, rows):
    k2 = jnp.concatenate([k_raw[:, LANES * j:LANES * (j + 1)].astype(F32) for j in range(KV_TILES)], axis=0)
    k2 = _head_norm(k2, kw_ref[...])
    cos_t, slo, shi = cos_ref[...], slo_ref[...], shi_ref[...]
    return jnp.concatenate([_rope(k2[rows * j:rows * (j + 1)], cos_t, slo, shi) for j in range(KV_TILES)], axis=1)


class _Part:
    def __init__(self, in_specs, operands, out_specs, out_shapes, scratch, body, init=None, final=None):
        self.in_specs, self.operands, self.out_specs, self.out_shapes = in_specs, operands, out_specs, out_shapes
        self.scratch, self.body, self.init, self.final = scratch, body, init, final


def _fused_kernel(*refs, parts):
    n_in = [len(p.in_specs) for p in parts]
    n_out = [len(p.out_specs) for p in parts]
    n_scr = [len(p.scratch) for p in parts]
    pos = 0
    groups = []
    for counts in (n_in, n_out, n_scr):
        groups.append([])
        for n in counts:
            groups[-1].append(refs[pos:pos + n])
            pos += n
    args = [(p, groups[0][i], groups[1][i], groups[2][i]) for i, p in enumerate(parts)]
    c = pl.program_id(1)

    @pl.when(c == 0)
    def _():
        for p, ins, outs, scr in args:
            if p.init is not None:
                p.init(c, ins, outs, scr)

    for p, ins, outs, scr in args:
        p.body(c, ins, outs, scr)

    @pl.when(c == pl.num_programs(1) - 1)
    def _():
        for p, ins, outs, scr in args:
            if p.final is not None:
                p.final(c, ins, outs, scr)


def _fused_call(parts, grid, name):
    outs = pl.pallas_call(
        functools.partial(_fused_kernel, parts=parts),
        grid=grid,
        in_specs=[s for p in parts for s in p.in_specs],
        out_specs=[s for p in parts for s in p.out_specs],
        out_shape=[s for p in parts for s in p.out_shapes],
        scratch_shapes=[s for p in parts for s in p.scratch],
        compiler_params=_params("arbitrary", "arbitrary"),
        name=name,
    )(*[o for p in parts for o in p.operands])
    split, pos = [], 0
    for p in parts:
        split.append(outs[pos:pos + len(p.out_specs)])
        pos += len(p.out_specs)
    return split


def _attn_prompt_init(c, ins, outs, scr):
    for ring in scr:
        ring[...] = jnp.zeros_like(ring)


def _attn_prompt_body(c, ins, outs, scr):
    qkv_ref, tab_ref, qw_ref, kw_ref, sink_ref = ins
    cos_ref, slo_ref, shi_ref = _rope_views(tab_ref)
    o_ref, nk_ref, nv_ref = outs
    kk_scr, vv_scr = scr
    slot = c % 3
    for s in range(qkv_ref.shape[0]):
        q_ref = qkv_ref.at[s, :, pl.ds(COL_Q, ATTN_W)]
        k_fin = _prep_k(qkv_ref[s, :, COL_K:COL_K + KV_W], kw_ref, cos_ref, slo_ref, shi_ref, CHUNK)
        v_cur = qkv_ref[s, :, COL_V:COL_V + KV_W].astype(F32)
        nk_ref[s] = k_fin
        nv_ref[s] = v_cur
        kks, vvs = _kv_tiles(k_fin, v_cur)
        for h in range(N_KV_HEADS):
            kk_scr[s, slot, h] = kks[h]
            vv_scr[s, slot, h] = vvs[h]

        q_tiles = _prep_q(q_ref, qw_ref, cos_ref, slo_ref, shi_ref, CHUNK)
        _attend(q_tiles,
                lambda h, s=s: jnp.concatenate([kk_scr[s, w, h] for w in range(3)], axis=0),
                lambda h, s=s: jnp.concatenate([vv_scr[s, w, h] for w in range(3)], axis=0),
                lambda w: (c - w + 3) % 3 <= c,
                sink_ref, o_ref.at[s], CHUNK)


def _attn_prompt_part(proj3, rope_tabs, qw_tile, kw_tile, sink_tab, ns):
    batch, t, _ = proj3.shape
    nc = t // CHUNK
    keep = WINDOW // CHUNK
    const = lambda shape: pl.BlockSpec(shape, lambda g, c: (0,) * len(shape))
    keep_spec = pl.BlockSpec((ns, CHUNK, KV_W), lambda g, c: (g, jnp.maximum(c - (nc - keep), 0), 0))
    return _Part(
        in_specs=[
            pl.BlockSpec((ns, CHUNK, QKV_W), lambda g, c: (g, c, 0)),
            pl.BlockSpec((CHUNK, 3 * LANES), lambda g, c: (c, 0)),
            const((1, LANES)), const((1, LANES)), const((Q_TILES, LANES)),
        ],
        operands=[proj3, rope_tabs, qw_tile, kw_tile, sink_tab],
        out_specs=[pl.BlockSpec((ns, CHUNK, ATTN_W), lambda g, c: (g, c, 0)), keep_spec, keep_spec],
        out_shapes=[
            jax.ShapeDtypeStruct((batch, t, ATTN_W), BF16),
            jax.ShapeDtypeStruct((batch, WINDOW, KV_W), F32),
            jax.ShapeDtypeStruct((batch, WINDOW, KV_W), F32),
        ],
        scratch=[
            pltpu.VMEM((ns, 3, N_KV_HEADS, 2 * CHUNK, LANES), BF16),
            pltpu.VMEM((ns, 3, N_KV_HEADS, 2 * CHUNK, 2 * LANES), BF16),
        ],
        body=_attn_prompt_body, init=_attn_prompt_init)


def _attn_sample_body(c, ins, outs, scr):
    qkv_ref, ck_ref, cv_ref, tab_ref, qw_ref, kw_ref, sink_ref = ins
    cos_ref, slo_ref, shi_ref = _rope_views(tab_ref)
    o_ref, nk_ref, nv_ref = outs
    tq = qkv_ref.shape[1]
    new_valid = lax.broadcasted_iota(jnp.int32, (2 * tq, LANES), 1) % HEAD_DIM < tq
    pad = jnp.zeros((CHUNK - tq, KV_W), F32)
    for s in range(qkv_ref.shape[0]):
        q_ref = qkv_ref.at[s, :, pl.ds(COL_Q, ATTN_W)]
        k_fin = _prep_k(qkv_ref[s, :, COL_K:COL_K + KV_W], kw_ref, cos_ref, slo_ref, shi_ref, tq)
        v_cur = qkv_ref[s, :, COL_V:COL_V + KV_W].astype(F32)
        nk_ref[s, 0:WINDOW - tq, :] = ck_ref[s, tq:WINDOW, :]
        nv_ref[s, 0:WINDOW - tq, :] = cv_ref[s, tq:WINDOW, :]
        nk_ref[s, WINDOW - tq:WINDOW, :] = k_fin
        nv_ref[s, WINDOW - tq:WINDOW, :] = v_cur
        wins = [(ck_ref[s, :CHUNK, :], cv_ref[s, :CHUNK, :]),
                (ck_ref[s, CHUNK:, :], cv_ref[s, CHUNK:, :]),
                (jnp.concatenate([k_fin, pad], axis=0), jnp.concatenate([v_cur, pad], axis=0))]
        tiles = [_kv_tiles(kw, vw) for kw, vw in wins]
        q_tiles = _prep_q(q_ref, qw_ref, cos_ref, slo_ref, shi_ref, tq)
        _attend(q_tiles,
                lambda h, tiles=tiles: jnp.concatenate([tiles[w][0][h] for w in range(3)], axis=0),
                lambda h, tiles=tiles: jnp.concatenate([tiles[w][1][h] for w in range(3)], axis=0),
                lambda w: new_valid if w == 2 else True,
                sink_ref, o_ref.at[s], tq)


def _attn_sample_part(proj3, cache_k, cache_v, rope_tabs, qw_tile, kw_tile, sink_tab, ns):
    batch, tq, _ = proj3.shape
    const = lambda shape: pl.BlockSpec(shape, lambda g, c: (0,) * len(shape))
    cache_spec = pl.BlockSpec((ns, WINDOW, KV_W), lambda g, c: (g, 0, 0))
    return _Part(
        in_specs=[
            pl.BlockSpec((ns, tq, QKV_W), lambda g, c: (g, 0, 0)),
            cache_spec, cache_spec,
            pl.BlockSpec((tq, 3 * LANES), lambda g, c: (0, 0)),
            const((1, LANES)), const((1, LANES)), const((Q_TILES, LANES)),
        ],
        operands=[proj3, cache_k, cache_v, rope_tabs, qw_tile, kw_tile, sink_tab],
        out_specs=[pl.BlockSpec((ns, tq, ATTN_W), lambda g, c: (g, 0, 0)), cache_spec, cache_spec],
        out_shapes=[
            jax.ShapeDtypeStruct((batch, tq, ATTN_W), BF16),
            jax.ShapeDtypeStruct((batch, WINDOW, KV_W), F32),
            jax.ShapeDtypeStruct((batch, WINDOW, KV_W), F32),
        ],
        scratch=[], body=_attn_sample_body)


def _pad_rows(a, rows):
    if a.shape[0] == rows:
        return a
    return jnp.concatenate([a, jnp.zeros((rows - a.shape[0], a.shape[1]), a.dtype)], axis=0)


def _ssd_chunk(c, xs0_ref, xs1_ref, z0_ref, z1_ref, b_ref, c_ref, dt_ref, cst_ref, cw_ref, cb_ref, dtb_ref,
               alog_ref, dskip_ref, nw_ref, sel_ref, shift_ref, y_ref, ext_scr, st_scr, L):
    half = SSM_W // 2
    ext_scr[L:2 * L, 0:half] = xs0_ref[...]
    ext_scr[L:2 * L, half:SSM_W] = xs1_ref[...]
    ext_scr[L:2 * L, SSM_W:SSM_W + BC_W] = b_ref[...]
    ext_scr[L:2 * L, SSM_W + BC_W:CONV_CH] = c_ref[...]
    delayed = _dot(shift_ref[...], ext_scr[...])
    cur = ext_scr[L:2 * L, :].astype(F32)
    conv = cb_ref[...] + cur * cw_ref[CONV_W - 1:CONV_W, :]
    for i in range(CONV_W - 1):
        conv = conv + delayed[L * i:L * (i + 1)] * cw_ref[i:i + 1, :]
    s = cst_ref[...]
    w0, w1, w2 = cw_ref[0:1, :], cw_ref[1:2, :], cw_ref[2:3, :]
    head = jnp.concatenate([w0 * s[0:1] + w1 * s[1:2] + w2 * s[2:3], w0 * s[1:2] + w1 * s[2:3], w0 * s[2:3],
                            jnp.zeros((8 - (CONV_W - 1), CONV_CH), F32)], axis=0)
    conv = jnp.concatenate([conv[0:8] + jnp.where(c == 0, head, 0.0), conv[8:]], axis=0)
    conv = _silu(conv)
    xs = conv[:, :SSM_W]
    ext_scr[0:L, :] = ext_scr[L:2 * L, :]

    x_dt = dt_ref[...] + dtb_ref[...]
    dt = jnp.maximum(x_dt, 0.0) + jnp.log1p(jnp.exp(-jnp.abs(x_dt)))
    ad = dt * (-jnp.exp(alog_ref[...]))
    li = lax.broadcasted_iota(jnp.int32, (L, L), 0)
    si = lax.broadcasted_iota(jnp.int32, (L, L), 1)
    tril = (si <= li).astype(BF16)
    a_cum = _dot(jnp.concatenate([tril] * 3, axis=1), jnp.concatenate(_split3(ad), axis=0))
    sel = sel_ref[...]
    ex = _dot(jnp.concatenate(_split3(a_cum) + _split3(dt), axis=0), sel)
    col = ex[0:L] + ex[L:2 * L] + ex[2 * L:3 * L]
    dt_all = ex[3 * L:4 * L] + ex[4 * L:5 * L] + ex[5 * L:6 * L]
    lane_s = lax.broadcasted_iota(jnp.int32, (L, SSM_W), 1) % SSM_HEAD_DIM
    row_l = lax.broadcasted_iota(jnp.int32, (L, SSM_W), 0)
    row = jnp.sum(jnp.where(lane_s == row_l, col, 0.0), axis=0, keepdims=True)
    lmat = jnp.exp(jnp.where(lane_s <= row_l, col - row, NEG_INF))
    a_last = col[L - 1:L, :]
    exp_a = jnp.exp(col)
    decay = jnp.exp(a_last - col)
    exp_last = jnp.exp(a_last)

    xd = xs * dt_all
    xdd = xd * decay
    bi = lax.broadcasted_iota(jnp.int32, (GROUP_W, GROUP_W), 0) // SSM_HEAD_DIM
    bj = lax.broadcasted_iota(jnp.int32, (GROUP_W, GROUP_W), 1) // SSM_HEAD_DIM
    blockdiag = bi == bj

    for g in range(N_GROUPS):
        gs = slice(GROUP_W * g, GROUP_W * (g + 1))
        b_g = conv[:, SSM_W + D_STATE * g:SSM_W + D_STATE * (g + 1)]
        c_g = conv[:, SSM_W + BC_W + D_STATE * g:SSM_W + BC_W + D_STATE * (g + 1)].astype(BF16)
        b_pad = _pad_rows(b_g, CHUNK)
        cb = _dot_nt(c_g, jnp.concatenate([b_pad.astype(BF16)] * 4, axis=0))
        m_g = (cb * lmat[:, gs]).astype(BF16)
        xd_g = _pad_rows(xd[:, gs], CHUNK).astype(BF16)
        xd_bd = jnp.where(blockdiag, jnp.concatenate([xd_g] * 4, axis=0), jnp.zeros((), BF16))
        y_diag = _dot(m_g, xd_bd)
        st_g = st_scr[:, gs]
        y_off = _dot(c_g, st_g.astype(BF16)) * exp_a[:, gs]
        new_st = _dot(b_pad.T.astype(BF16), _pad_rows(xdd[:, gs], CHUNK).astype(BF16))
        st_scr[:, gs] = st_g * exp_last[:, gs] + new_st

        y = y_diag + y_off + xs[:, gs] * dskip_ref[:, gs]
        z_ref = z0_ref if g < N_GROUPS // 2 else z1_ref
        zs = slice(GROUP_W * (g % (N_GROUPS // 2)), GROUP_W * (g % (N_GROUPS // 2) + 1))
        gated = y * _silu(z_ref[:, zs].astype(F32))
        ms = jnp.mean(gated * gated, axis=-1, keepdims=True)
        y_ref[:, gs] = (gated * lax.rsqrt(ms + RMS_EPS) * nw_ref[:, gs]).astype(y_ref.dtype)


def _ssd_init(c, ins, outs, scr):
    st0_ref = ins[4]
    ext_scr, st_scr = scr
    ns, L = ext_scr.shape[0], ext_scr.shape[1] // 2
    for s in range(ns):
        ext_scr[s, 0:L, :] = jnp.zeros((L, CONV_CH), BF16)
        st_scr[s] = st0_ref[s].T


def _ssd_body(c, ins, outs, scr):
    (lo_ref, hi_ref, dt_ref, cst_ref, _, cw_ref, cb_ref, dtb_ref, alog_ref, dskip_ref, nw_ref, sel_ref,
     shift_ref) = ins
    y_ref = outs[0]
    ext_scr, st_scr = scr
    ns, L = dt_ref.shape[0], dt_ref.shape[1]
    half = SSM_W // 2
    for s in range(ns):
        view = lambda ref, start, s=s: ref.at[s, :, pl.ds(start, half)]
        _ssd_chunk(c, view(lo_ref, 0), view(lo_ref, half), view(lo_ref, 2 * half), view(hi_ref, 0),
                   view(hi_ref, half), view(hi_ref, 2 * half),
                   dt_ref.at[s], cst_ref.at[s], cw_ref, cb_ref, dtb_ref, alog_ref, dskip_ref, nw_ref,
                   sel_ref, shift_ref, y_ref.at[s], ext_scr.at[s], st_scr.at[s], L)


def _ssd_final(c, ins, outs, scr):
    _, ncv_ref, nst_ref = outs
    ext_scr, st_scr = scr
    ns, L = ext_scr.shape[0], ext_scr.shape[1] // 2
    for s in range(ns):
        tail = ext_scr[s, L - BF16_ROWS:L, :].astype(F32)
        ncv_ref[s] = tail[BF16_ROWS - (CONV_W - 1):BF16_ROWS, :]
        nst_ref[s] = st_scr[s].T


def _shift_matrix(L):
    r = jnp.arange(3 * L)
    src = L + r % L - (CONV_W - 1) + r // L
    return (jnp.arange(2 * L)[None, :] == src[:, None]).astype(BF16)


def _ssd_part(proj3, dt3, conv_state, ssm_state, conv_w, conv_b, dtb, alog, dskip_all, ssm_nw, sel, L, ns):
    batch, t, _ = proj3.shape
    const = lambda shape: pl.BlockSpec(shape, lambda g, c: (0,) * len(shape))
    per_g = lambda shape: pl.BlockSpec((ns,) + shape, lambda g, c: (g,) + (0,) * len(shape))
    third = lambda k: pl.BlockSpec((ns, L, QKV_W), lambda g, c: (g, c, k))
    hp = N_SSM_HEADS * SSM_HEAD_DIM
    assert COL_XS == QKV_W and PROJ_W == 3 * QKV_W
    return _Part(
        in_specs=[
            third(1), third(2),
            pl.BlockSpec((ns, L, LANES), lambda g, c: (g, c, 0)),
            per_g((CONV_W - 1, CONV_CH)),
            per_g((hp, D_STATE)),
            const((CONV_W, CONV_CH)), const((1, CONV_CH)), const((1, LANES)), const((1, LANES)),
            const((1, SSM_W)), const((1, SSM_W)), const((LANES, SSM_W)), const((3 * L, 2 * L)),
        ],
        out_specs=[
            pl.BlockSpec((ns, L, SSM_W), lambda g, c: (g, c, 0)),
            per_g((CONV_W - 1, CONV_CH)),
            per_g((hp, D_STATE)),
        ],
        out_shapes=[
            jax.ShapeDtypeStruct((batch, t, SSM_W), BF16),
            jax.ShapeDtypeStruct((batch, CONV_W - 1, CONV_CH), F32),
            jax.ShapeDtypeStruct((batch, hp, D_STATE), F32),
        ],
        scratch=[
            pltpu.VMEM((ns, 2 * L, CONV_CH), BF16),
            pltpu.VMEM((ns, D_STATE, hp), F32),
        ],
        operands=[proj3, proj3, dt3, conv_state, ssm_state, conv_w, conv_b, dtb, alog, dskip_all, ssm_nw, sel,
                  _shift_matrix(L)],
        body=_ssd_body, init=_ssd_init, final=_ssd_final)


def _outproj_kernel(*refs, n_cast):
    a_ref, y_ref, wa_ref, wy_ref, x_ref = refs[:5]
    cast_in, o_ref, cast_out = refs[5:5 + n_cast], refs[5 + n_cast], refs[6 + n_cast:]
    o_ref[...] = x_ref[...] + _dot(a_ref[...], wa_ref[...]) + _dot(y_ref[...], wy_ref[...])
    _cast_blocks(cast_in, cast_out)


def _outproj(attn, y, w_out, x2d, tm, tn, cast=()):
    t = x2d.shape[0]
    half = D_MODEL // 2
    ni = t // tm
    c_in, c_out, c_shape = _cast_riders(cast, (D_MODEL // tn) * ni, lambda j, i: j * ni + i)
    return pl.pallas_call(
        functools.partial(_outproj_kernel, n_cast=len(cast)),
        grid=(D_MODEL // tn, ni),
        in_specs=[
            pl.BlockSpec((tm, half), lambda j, i: (i, 0)),
            pl.BlockSpec((tm, half), lambda j, i: (i, 0)),
            pl.BlockSpec((half, tn), lambda j, i: (0, j)),
            pl.BlockSpec((half, tn), lambda j, i: (1, j)),
            pl.BlockSpec((tm, tn), lambda j, i: (i, j)),
        ] + c_in,
        out_specs=[pl.BlockSpec((tm, tn), lambda j, i: (i, j))] + c_out,
        out_shape=[jax.ShapeDtypeStruct((t, D_MODEL), F32)] + c_shape,
        compiler_params=_params("arbitrary", "arbitrary"),
        name="outproj",
    )(attn, y, w_out, w_out, x2d, *cast)


def _ffn_kernel(x_ref, nw_ref, wu_ref, wd_ref, o_ref, h_scr, *, n_split):
    @pl.when(pl.program_id(1) == 0)
    def _():
        _rmsnorm_rows(x_ref, nw_ref, h_scr, copy_ref=o_ref)

    u = jnp.maximum(_dot(h_scr[...], wu_ref[...]), 0.0)
    u = (u * u).astype(BF16)
    wn = D_MODEL // n_split
    for n in range(n_split):
        o_ref[:, wn * n:wn * (n + 1)] += _dot(u, wd_ref[:, wn * n:wn * (n + 1)])


def _ffn(x2d, norm_w, w_up, w_down, tm, tf):
    t = x2d.shape[0]
    return pl.pallas_call(
        functools.partial(_ffn_kernel, n_split=4),
        grid=(t // tm, FFN_HIDDEN // tf),
        in_specs=[
            pl.BlockSpec((tm, D_MODEL), lambda i, f: (i, 0)),
            pl.BlockSpec((1, D_MODEL), lambda i, f: (0, 0)),
            pl.BlockSpec((D_MODEL, tf), lambda i, f: (0, f)),
            pl.BlockSpec((tf, D_MODEL), lambda i, f: (f, 0)),
        ],
        out_specs=pl.BlockSpec((tm, D_MODEL), lambda i, f: (i, 0)),
        out_shape=jax.ShapeDtypeStruct((t, D_MODEL), F32),
        scratch_shapes=[pltpu.VMEM((tm, D_MODEL), BF16)],
        compiler_params=_params("arbitrary", "arbitrary"),
        name="ffn",
    )(x2d, norm_w, w_up, w_down)


def _rope_tables(pos):
    inv_freq = ROPE_THETA ** (-jnp.arange(ROPE_HALF, dtype=F32) / ROPE_HALF)
    which = jnp.arange(3 * LANES) // LANES
    lane = jnp.arange(3 * LANES) % HEAD_DIM
    freq = jnp.where(lane < ROPE_DIM, inv_freq[lane % ROPE_HALF], 0.0)
    ang = pos.astype(F32)[:, None] * freq[None, :]
    cos, sin = jnp.cos(ang), jnp.sin(ang)
    return jnp.where(which == 0, cos, jnp.where(which == 1, jnp.where(lane < ROPE_HALF, -sin, 0.0),
                                                 jnp.where(lane >= ROPE_HALF, sin, 0.0)))


def _rope_views(tab_ref):
    return tuple(tab_ref.at[:, pl.ds(LANES * i, LANES)] for i in range(3))


def _layer_params(norm_mix_w, w_in, q_norm_w, k_norm_w, attn_sinks, conv_w, conv_b, dt_bias, a_log, d_skip,
                  ssm_norm_w, w_out, norm_ffn_w, w_up, w_down):
    pad_h = lambda a: jnp.pad(a.astype(F32), (0, LANES - N_SSM_HEADS))[None, :]
    sel = (jnp.arange(LANES)[:, None] == (jnp.arange(SSM_W) // SSM_HEAD_DIM)[None, :]).astype(BF16)
    return dict(
        norm_mix=norm_mix_w.astype(F32)[None, :],
        qw=jnp.tile(q_norm_w.astype(F32), 2)[None, :], kw=jnp.tile(k_norm_w.astype(F32), 2)[None, :],
        sinks=jnp.repeat(attn_sinks.astype(F32), HEAD_DIM).reshape(Q_TILES, LANES),
        conv_w=conv_w.astype(F32), conv_b=conv_b.astype(F32)[None, :],
        dtb=pad_h(dt_bias), alog=pad_h(a_log),
        dskip=jnp.repeat(d_skip.astype(F32), SSM_HEAD_DIM)[None, :],
        ssm_nw=ssm_norm_w.astype(F32)[None, :], sel=sel,
        norm_ffn=norm_ffn_w.astype(F32)[None, :],
    )


def _stream_back(x, proj, dt_raw, pos, cache, conv_state, ssm_state, p, tm, w_out, w_up, w_down):
    batch, t, _ = x.shape
    ns = max(n for n in STREAMS_PER_STEP if batch % n == 0)
    x2d = x.reshape(batch * t, D_MODEL)
    proj3 = proj.reshape(batch, t, PROJ_W)
    tabs = _rope_tables(pos)
    if cache is None:
        attn_part = _attn_prompt_part(proj3, tabs, p["qw"], p["kw"], p["sinks"], ns)
        L = CHUNK
    else:
        past_k, past_v = cache
        attn_part = _attn_sample_part(proj3, past_k.reshape(batch, WINDOW, KV_W),
                                      past_v.reshape(batch, WINDOW, KV_W), tabs, p["qw"], p["kw"], p["sinks"], ns)
        L = min(CHUNK, t)
    hp = N_SSM_HEADS * SSM_HEAD_DIM
    ssd_part = _ssd_part(proj3, dt_raw.reshape(batch, t, LANES), conv_state,
                         ssm_state.reshape(batch, hp, D_STATE), p["conv_w"], p["conv_b"], p["dtb"], p["alog"],
                         p["dskip"], p["ssm_nw"], p["sel"], L, ns)
    (attn, new_k, new_v), (y, new_conv, new_ssm) = _fused_call([attn_part, ssd_part], (batch // ns, t // L),
                                                              "mixers")
    late = () if w_down.dtype == BF16 else (w_down,)
    x1, *cast = _outproj(attn.reshape(batch * t, ATTN_W), y.reshape(batch * t, SSM_W), w_out, x2d, tm, 1024,
                         cast=late)
    if late:
        w_down = cast.pop()
    out = _ffn(x1, p["norm_ffn"], w_up, w_down, tm, 512)
    return (out.reshape(batch, t, D_MODEL),
            new_k.reshape(batch, WINDOW, N_KV_HEADS, HEAD_DIM), new_v.reshape(batch, WINDOW, N_KV_HEADS, HEAD_DIM),
            new_conv, new_ssm.reshape(batch, N_SSM_HEADS, SSM_HEAD_DIM, D_STATE), w_down)


def kernel(x_prompt, x_sample, cache_k, cache_v, state_conv, state_ssm, norm_mix_w, w_in, q_norm_w, k_norm_w,
           attn_sinks, conv_w, conv_b, dt_bias, a_log, d_skip, ssm_norm_w, w_out, norm_ffn_w, w_up, w_down):
    depth = w_in.shape[0]
    b_p, t_p, _ = x_prompt.shape
    b_s, t_s, _ = x_sample.shape
    assert cache_k.shape[2] == WINDOW and t_p % CHUNK == 0 and t_p >= WINDOW
    assert t_s <= CHUNK and t_s % BF16_ROWS == 0
    pos_p = jnp.arange(t_p, dtype=jnp.int32)
    pos_s = PAST_LEN + jnp.arange(t_s, dtype=jnp.int32)
    hp, hs = x_prompt, x_sample
    outs_p, outs_s = [], []
    for layer in range(depth):
        p = _layer_params(norm_mix_w[layer], w_in[layer], q_norm_w[layer], k_norm_w[layer], attn_sinks[layer],
                          conv_w[layer], conv_b[layer], dt_bias[layer], a_log[layer], d_skip[layer],
                          ssm_norm_w[layer], w_out[layer], norm_ffn_w[layer], w_up[layer], w_down[layer])
        conv0 = jnp.zeros((b_p, CONV_W - 1, CONV_CH), F32)
        ssm0 = jnp.zeros((b_p, N_SSM_HEADS, SSM_HEAD_DIM, D_STATE), F32)
        tm = 512
        xs2d, xp2d = hs.reshape(b_s * t_s, D_MODEL), hp.reshape(b_p * t_p, D_MODEL)
        w_in_t = w_in[layer].T.astype(BF16)
        proj_p, dt_p, w_out16, w_up16 = _inproj(xp2d, p["norm_mix"], w_in_t, tm, 1536,
                                                cast=(w_out[layer], w_up[layer]))
        proj_s, dt_s = _inproj(xs2d, p["norm_mix"], w_in_t, tm, 1536)
        hp, *rest_p, w_down16 = _stream_back(hp, proj_p, dt_p, pos_p, None, conv0, ssm0, p, tm,
                                             w_out16, w_up16, w_down[layer])
        hs, *rest_s, _ = _stream_back(hs, proj_s, dt_s, pos_s, (cache_k[layer], cache_v[layer]),
                                      state_conv[layer], state_ssm[layer], p, tm, w_out16, w_up16, w_down16)
        outs_p.append(rest_p)
        outs_s.append(rest_s)
    stack = lambda outs, i: jnp.stack([o[i] for o in outs])
    return (hp, hs, stack(outs_p, 0), stack(outs_p, 1), stack(outs_p, 2), stack(outs_p, 3),
            stack(outs_s, 0), stack(outs_s, 1), stack(outs_s, 2), stack(outs_s, 3))
```

```python
import functools

import jax
import jax.numpy as jnp
from jax import lax
from jax.experimental import pallas as pl
from jax.experimental.pallas import tpu as pltpu

F32 = jnp.float32
BF16 = jnp.bfloat16

D_MODEL = 4096
HEAD_DIM = 64
N_Q_HEADS = 32
N_KV_HEADS = 8
ATTN_W = N_Q_HEADS * HEAD_DIM
KV_W = N_KV_HEADS * HEAD_DIM
ROPE_DIM = 16
ROPE_HALF = ROPE_DIM // 2
ROPE_THETA = 500000.0
ATTN_SCALE = HEAD_DIM ** -0.5
WINDOW = 128
CHUNK = 64
PAST_LEN = 1024
SSM_W = 2048
N_SSM_HEADS = 32
SSM_HEAD_DIM = 64
N_GROUPS = 8
D_STATE = 128
BC_W = N_GROUPS * D_STATE
GROUP_W = SSM_W // N_GROUPS
CONV_W = 4
CONV_CH = SSM_W + 2 * BC_W
FFN_HIDDEN = 4 * D_MODEL
RMS_EPS = 1e-6
NEG_INF = -1e30
LANES = 128
BF16_ROWS = 16
Q_TILES = ATTN_W // LANES
KV_TILES = KV_W // LANES
PROJ_W = ATTN_W + 2 * KV_W + 2 * SSM_W + 2 * BC_W
COL_Q, COL_K, COL_V, COL_XS, COL_Z, COL_B, COL_C = 0, 2048, 2560, 3072, 5120, 7168, 8192
QKV_W = ATTN_W + 2 * KV_W
VMEM_LIMIT = 62 * 1024 * 1024
STREAMS_PER_STEP = (1, 2, 4)


def _params(*sem):
    return pltpu.CompilerParams(dimension_semantics=sem, vmem_limit_bytes=VMEM_LIMIT)


def _dot(a, b):
    return jnp.dot(a, b, preferred_element_type=F32)


def _dot_nt(a, b):
    return lax.dot_general(a, b, (((1,), (1,)), ((), ())), preferred_element_type=F32)


def _split3(x):
    a = x.astype(BF16)
    r = x - a.astype(F32)
    b = r.astype(BF16)
    c = (r - b.astype(F32)).astype(BF16)
    return a, b, c


def _silu(x):
    return x / (1.0 + jnp.exp(-x))


NORM_ROWS = 16
NORM_UNROLL = 8


def _rmsnorm_rows(x_ref, nw_ref, h_scr, copy_ref=None):
    def body(i, carry):
        r = pl.ds(pl.multiple_of(i * NORM_ROWS, NORM_ROWS), NORM_ROWS)
        x = x_ref[r, :]
        ms = jnp.mean(x * x, axis=-1, keepdims=True)
        h_scr[r, :] = (x * lax.rsqrt(ms + RMS_EPS) * nw_ref[...]).astype(BF16)
        if copy_ref is not None:
            copy_ref[r, :] = x
        return carry

    lax.fori_loop(0, x_ref.shape[0] // NORM_ROWS, body, 0, unroll=NORM_UNROLL)


def _cast_riders(arrays, steps, linear_step):
    in_specs, out_specs, out_shapes = [], [], []
    for a in arrays:
        rows, cols = a.shape
        nb = max(n for n in range(1, steps + 1) if rows % n == 0 and (rows // n) % BF16_ROWS == 0)
        spec = pl.BlockSpec((rows // nb, cols), lambda *g, nb=nb: (jnp.minimum(linear_step(*g), nb - 1), 0))
        in_specs.append(spec)
        out_specs.append(spec)
        out_shapes.append(jax.ShapeDtypeStruct(a.shape, BF16))
    return in_specs, out_specs, out_shapes


def _cast_blocks(src_refs, dst_refs):
    for src, dst in zip(src_refs, dst_refs):
        dst[...] = src[...].astype(BF16)


def _inproj_kernel(*refs, n_cast):
    x_ref, nw_ref, w_ref, wdt_ref = refs[:4]
    cast_in = refs[4:4 + n_cast]
    proj_ref, dt_ref = refs[4 + n_cast:6 + n_cast]
    cast_out, h_scr = refs[6 + n_cast:6 + 2 * n_cast], refs[6 + 2 * n_cast]

    @pl.when(pl.program_id(1) == 0)
    def _():
        _rmsnorm_rows(x_ref, nw_ref, h_scr)
        real = lax.broadcasted_iota(jnp.int32, wdt_ref.shape, 0) < N_SSM_HEADS
        dt_ref[...] = _dot_nt(h_scr[...], jnp.where(real, wdt_ref[...], jnp.zeros((), BF16)))

    proj_ref[...] = _dot_nt(h_scr[...], w_ref[...]).astype(BF16)
    _cast_blocks(cast_in, cast_out)


def _inproj(x2d, norm_w, w_in_t, tm, tn, cast=()):
    t = x2d.shape[0]
    nj = PROJ_W // tn
    c_in, c_out, c_shape = _cast_riders(cast, (t // tm) * nj, lambda i, j: i * nj + j)
    return pl.pallas_call(
        functools.partial(_inproj_kernel, n_cast=len(cast)),
        grid=(t // tm, nj),
        in_specs=[
            pl.BlockSpec((tm, D_MODEL), lambda i, j: (i, 0)),
            pl.BlockSpec((1, D_MODEL), lambda i, j: (0, 0)),
            pl.BlockSpec((tn, D_MODEL), lambda i, j: (j, 0)),
            pl.BlockSpec((LANES, D_MODEL), lambda i, j: (PROJ_W // LANES, 0)),
        ] + c_in,
        out_specs=[
            pl.BlockSpec((tm, tn), lambda i, j: (i, j)),
            pl.BlockSpec((tm, LANES), lambda i, j: (i, 0)),
        ] + c_out,
        out_shape=[
            jax.ShapeDtypeStruct((t, PROJ_W), BF16),
            jax.ShapeDtypeStruct((t, LANES), F32),
        ] + c_shape,
        scratch_shapes=[pltpu.VMEM((tm, D_MODEL), BF16)],
        compiler_params=_params("arbitrary", "arbitrary"),
        name="inproj",
    )(x2d, norm_w, w_in_t, w_in_t, *cast)


def _half_ones():
    r = lax.broadcasted_iota(jnp.int32, (LANES, LANES), 0) // HEAD_DIM
    c = lax.broadcasted_iota(jnp.int32, (LANES, LANES), 1) // HEAD_DIM
    return (r == c).astype(BF16)


def _head_norm(x, w_tile):
    ss = _dot((x * x).astype(BF16), _half_ones())
    return x * lax.rsqrt(ss * (1.0 / HEAD_DIM) + RMS_EPS) * w_tile


def _rope(x, cos_t, sin_lo, sin_hi):
    return (x * cos_t + pltpu.roll(x, LANES - ROPE_HALF, 1) * sin_lo
            + pltpu.roll(x, ROPE_HALF, 1) * sin_hi)


def _lane_is_a(shape):
    return lax.broadcasted_iota(jnp.int32, shape, 1) < HEAD_DIM


def _kv_tiles(k_win, v_win):
    is_a = _lane_is_a((CHUNK, LANES))
    ones_a = jnp.where(is_a, 1.0, 0.0).astype(F32)
    ones_b = 1.0 - ones_a
    kks, vvs = [], []
    for h in range(N_KV_HEADS):
        j, odd = h // 2, h % 2
        kt = k_win[:, LANES * j:LANES * (j + 1)]
        vt = v_win[:, LANES * j:LANES * (j + 1)]
        keep = jnp.logical_not(is_a) if odd else is_a
        k_own = jnp.where(keep, kt, 0.0)
        v_own = jnp.where(keep, vt, 0.0)
        k_sw = pltpu.roll(k_own, HEAD_DIM, 1)
        v_sw = pltpu.roll(v_own, HEAD_DIM, 1)
        k_a, k_b = (k_sw, k_own) if odd else (k_own, k_sw)
        v_a, v_b = (v_sw, v_own) if odd else (v_own, v_sw)
        kks.append(jnp.concatenate([k_a, k_b], axis=0).astype(BF16))
        vv = jnp.concatenate([jnp.concatenate([v_a, ones_a], axis=1),
                              jnp.concatenate([v_b, ones_b], axis=1)], axis=0)
        vvs.append(vv.astype(BF16))
    return kks, vvs


def _prep_q(q_ref, qw_ref, cos_ref, slo_ref, shi_ref, tq):
    q2 = jnp.concatenate([q_ref[:, LANES * i:LANES * (i + 1)].astype(F32) for i in range(Q_TILES)], axis=0)
    q2 = _head_norm(q2, qw_ref[...])
    cos_t, slo, shi = cos_ref[...], slo_ref[...], shi_ref[...]
    return [(_rope(q2[tq * i:tq * (i + 1)], cos_t, slo, shi) * ATTN_SCALE).astype(BF16) for i in range(Q_TILES)]


def _attend(q_tiles, kk_of, vv_of, valid_of, sink_ref, o_ref, tq):
    is_a = _lane_is_a((2 * tq, LANES))
    for h in range(N_KV_HEADS):
        qh = jnp.concatenate([q_tiles[2 * h], q_tiles[2 * h + 1]], axis=0)
        s = _dot_nt(qh, kk_of(h))
        sw = [jnp.where(valid_of(w), s[:, LANES * w:LANES * (w + 1)], NEG_INF) for w in range(3)]
        mt = jnp.maximum(jnp.maximum(sw[0], sw[1]), sw[2])
        m_a = jnp.max(jnp.where(is_a, mt, NEG_INF), axis=-1, keepdims=True)
        m_b = jnp.max(jnp.where(is_a, NEG_INF, mt), axis=-1, keepdims=True)
        sink = jnp.concatenate([jnp.broadcast_to(sink_ref[2 * h:2 * h + 1, :], (tq, LANES)),
                                jnp.broadcast_to(sink_ref[2 * h + 1:2 * h + 2, :], (tq, LANES))], axis=0)
        m = jnp.maximum(jnp.where(is_a, m_a, m_b), sink)
        e = jnp.concatenate([jnp.exp(x - m) for x in sw], axis=1).astype(BF16)
        oa = _dot(e, vv_of(h))
        o = oa[:, :LANES] / (oa[:, LANES:] + jnp.exp(sink - m))
        o_ref[:, LANES * 2 * h:LANES * (2 * h + 1)] = o[:tq].astype(o_ref.dtype)
        o_ref[:, LANES * (2 * h + 1):LANES * (2 * h + 2)] = o[tq:].astype(o_ref.dtype)


def _prep_k(k_raw, kw_ref, cos_ref, slo_ref, shi_ref, rows):
    k2 = jnp.concatenate([k_raw[:, LANES * j:LANES * (j + 1)].astype(F32) for j in range(KV_TILES)], axis=0)
    k2 = _head_norm(k2, kw_ref[...])
    cos_t, slo, shi = cos_ref[...], slo_ref[...], shi_ref[...]
    return jnp.concatenate([_rope(k2[rows * j:rows * (j + 1)], cos_t, slo, shi) for j in range(KV_TILES)], axis=1)


class _Part:
    def __init__(self, in_specs, operands, out_specs, out_shapes, scratch, body, init=None, final=None):
        self.in_specs, self.operands, self.out_specs, self.out_shapes = in_specs, operands, out_specs, out_shapes
        self.scratch, self.body, self.init, self.final = scratch, body, init, final


def _fused_kernel(*refs, parts):
    n_in = [len(p.in_specs) for p in parts]
    n_out = [len(p.out_specs) for p in parts]
    n_scr = [len(p.scratch) for p in parts]
    pos = 0
    groups = []
    for counts in (n_in, n_out, n_scr):
        groups.append([])
        for n in counts:
            groups[-1].append(refs[pos:pos + n])
            pos += n
    args = [(p, groups[0][i], groups[1][i], groups[2][i]) for i, p in enumerate(parts)]
    c = pl.program_id(1)

    @pl.when(c == 0)
    def _():
        for p, ins, outs, scr in args:
            if p.init is not None:
                p.init(c, ins, outs, scr)

    for p, ins, outs, scr in args:
        p.body(c, ins, outs, scr)

    @pl.when(c == pl.num_programs(1) - 1)
    def _():
        for p, ins, outs, scr in args:
            if p.final is not None:
                p.final(c, ins, outs, scr)


def _fused_call(parts, grid, name):
    outs = pl.pallas_call(
        functools.partial(_fused_kernel, parts=parts),
        grid=grid,
        in_specs=[s for p in parts for s in p.in_specs],
        out_specs=[s for p in parts for s in p.out_specs],
        out_shape=[s for p in parts for s in p.out_shapes],
        scratch_shapes=[s for p in parts for s in p.scratch],
        compiler_params=_params("arbitrary", "arbitrary"),
        name=name,
    )(*[o for p in parts for o in p.operands])
    split, pos = [], 0
    for p in parts:
        split.append(outs[pos:pos + len(p.out_specs)])
        pos += len(p.out_specs)
    return split


def _attn_prompt_init(c, ins, outs, scr):
    for ring in scr:
        ring[...] = jnp.zeros_like(ring)


def _attn_prompt_body(c, ins, outs, scr):
    qkv_ref, cos_ref, slo_ref, shi_ref, qw_ref, kw_ref, sink_ref = ins
    o_ref, nk_ref, nv_ref = outs
    kk_scr, vv_scr = scr
    slot = c % 3
    for s in range(qkv_ref.shape[0]):
        q_ref = qkv_ref.at[s, :, pl.ds(COL_Q, ATTN_W)]
        k_fin = _prep_k(qkv_ref[s, :, COL_K:COL_K + KV_W], kw_ref, cos_ref, slo_ref, shi_ref, CHUNK)
        v_cur = qkv_ref[s, :, COL_V:COL_V + KV_W].astype(F32)
        nk_ref[s] = k_fin
        nv_ref[s] = v_cur
        kks, vvs = _kv_tiles(k_fin, v_cur)
        for h in range(N_KV_HEADS):
            kk_scr[s, slot, h] = kks[h]
            vv_scr[s, slot, h] = vvs[h]

        q_tiles = _prep_q(q_ref, qw_ref, cos_ref, slo_ref, shi_ref, CHUNK)
        _attend(q_tiles,
                lambda h, s=s: jnp.concatenate([kk_scr[s, w, h] for w in range(3)], axis=0),
                lambda h, s=s: jnp.concatenate([vv_scr[s, w, h] for w in range(3)], axis=0),
                lambda w: (c - w + 3) % 3 <= c,
                sink_ref, o_ref.at[s], CHUNK)


def _attn_prompt_part(proj3, rope_tabs, qw_tile, kw_tile, sink_tab, ns):
    batch, t, _ = proj3.shape
    nc = t // CHUNK
    keep = WINDOW // CHUNK
    const = lambda shape: pl.BlockSpec(shape, lambda g, c: (0,) * len(shape))
    keep_spec = pl.BlockSpec((ns, CHUNK, KV_W), lambda g, c: (g, jnp.maximum(c - (nc - keep), 0), 0))
    return _Part(
        in_specs=[
            pl.BlockSpec((ns, CHUNK, QKV_W), lambda g, c: (g, c, 0)),
            *[pl.BlockSpec((CHUNK, LANES), lambda g, c: (c, 0))] * 3,
            const((1, LANES)), const((1, LANES)), const((Q_TILES, LANES)),
        ],
        operands=[proj3, *rope_tabs, qw_tile, kw_tile, sink_tab],
        out_specs=[pl.BlockSpec((ns, CHUNK, ATTN_W), lambda g, c: (g, c, 0)), keep_spec, keep_spec],
        out_shapes=[
            jax.ShapeDtypeStruct((batch, t, ATTN_W), BF16),
            jax.ShapeDtypeStruct((batch, WINDOW, KV_W), F32),
            jax.ShapeDtypeStruct((batch, WINDOW, KV_W), F32),
        ],
        scratch=[
            pltpu.VMEM((ns, 3, N_KV_HEADS, 2 * CHUNK, LANES), BF16),
            pltpu.VMEM((ns, 3, N_KV_HEADS, 2 * CHUNK, 2 * LANES), BF16),
        ],
        body=_attn_prompt_body, init=_attn_prompt_init)


def _attn_sample_body(c, ins, outs, scr):
    qkv_ref, ck_ref, cv_ref, cos_ref, slo_ref, shi_ref, qw_ref, kw_ref, sink_ref = ins
    o_ref, nk_ref, nv_ref = outs
    tq = qkv_ref.shape[1]
    new_valid = lax.broadcasted_iota(jnp.int32, (2 * tq, LANES), 1) % HEAD_DIM < tq
    pad = jnp.zeros((CHUNK - tq, KV_W), F32)
    for s in range(qkv_ref.shape[0]):
        q_ref = qkv_ref.at[s, :, pl.ds(COL_Q, ATTN_W)]
        k_fin = _prep_k(qkv_ref[s, :, COL_K:COL_K + KV_W], kw_ref, cos_ref, slo_ref, shi_ref, tq)
        v_cur = qkv_ref[s, :, COL_V:COL_V + KV_W].astype(F32)
        nk_ref[s, 0:WINDOW - tq, :] = ck_ref[s, tq:WINDOW, :]
        nv_ref[s, 0:WINDOW - tq, :] = cv_ref[s, tq:WINDOW, :]
        nk_ref[s, WINDOW - tq:WINDOW, :] = k_fin
        nv_ref[s, WINDOW - tq:WINDOW, :] = v_cur
        wins = [(ck_ref[s, :CHUNK, :], cv_ref[s, :CHUNK, :]),
                (ck_ref[s, CHUNK:, :], cv_ref[s, CHUNK:, :]),
                (jnp.concatenate([k_fin, pad], axis=0), jnp.concatenate([v_cur, pad], axis=0))]
        tiles = [_kv_tiles(kw, vw) for kw, vw in wins]
        q_tiles = _prep_q(q_ref, qw_ref, cos_ref, slo_ref, shi_ref, tq)
        _attend(q_tiles,
                lambda h, tiles=tiles: jnp.concatenate([tiles[w][0][h] for w in range(3)], axis=0),
                lambda h, tiles=tiles: jnp.concatenate([tiles[w][1][h] for w in range(3)], axis=0),
                lambda w: new_valid if w == 2 else True,
                sink_ref, o_ref.at[s], tq)


def _attn_sample_part(proj3, cache_k, cache_v, rope_tabs, qw_tile, kw_tile, sink_tab, ns):
    batch, tq, _ = proj3.shape
    const = lambda shape: pl.BlockSpec(shape, lambda g, c: (0,) * len(shape))
    cache_spec = pl.BlockSpec((ns, WINDOW, KV_W), lambda g, c: (g, 0, 0))
    return _Part(
        in_specs=[
            pl.BlockSpec((ns, tq, QKV_W), lambda g, c: (g, 0, 0)),
            cache_spec, cache_spec,
            *[pl.BlockSpec((tq, LANES), lambda g, c: (0, 0))] * 3,
            const((1, LANES)), const((1, LANES)), const((Q_TILES, LANES)),
        ],
        operands=[proj3, cache_k, cache_v, *rope_tabs, qw_tile, kw_tile, sink_tab],
        out_specs=[pl.BlockSpec((ns, tq, ATTN_W), lambda g, c: (g, 0, 0)), cache_spec, cache_spec],
        out_shapes=[
            jax.ShapeDtypeStruct((batch, tq, ATTN_W), BF16),
            jax.ShapeDtypeStruct((batch, WINDOW, KV_W), F32),
            jax.ShapeDtypeStruct((batch, WINDOW, KV_W), F32),
        ],
        scratch=[], body=_attn_sample_body)


def _pad_rows(a, rows):
    if a.shape[0] == rows:
        return a
    return jnp.concatenate([a, jnp.zeros((rows - a.shape[0], a.shape[1]), a.dtype)], axis=0)


def _ssd_chunk(c, xs0_ref, xs1_ref, z0_ref, z1_ref, b_ref, c_ref, dt_ref, cst_ref, cw_ref, cb_ref, dtb_ref,
               alog_ref, dskip_ref, nw_ref, sel_ref, shift_ref, y_ref, ext_scr, st_scr, L):
    half = SSM_W // 2
    ext_scr[L:2 * L, 0:half] = xs0_ref[...]
    ext_scr[L:2 * L, half:SSM_W] = xs1_ref[...]
    ext_scr[L:2 * L, SSM_W:SSM_W + BC_W] = b_ref[...]
    ext_scr[L:2 * L, SSM_W + BC_W:CONV_CH] = c_ref[...]
    delayed = _dot(shift_ref[...], ext_scr[...])
    cur = ext_scr[L:2 * L, :].astype(F32)
    conv = cb_ref[...] + cur * cw_ref[CONV_W - 1:CONV_W, :]
    for i in range(CONV_W - 1):
        conv = conv + delayed[L * i:L * (i + 1)] * cw_ref[i:i + 1, :]
    s = cst_ref[...]
    w0, w1, w2 = cw_ref[0:1, :], cw_ref[1:2, :], cw_ref[2:3, :]
    head = jnp.concatenate([w0 * s[0:1] + w1 * s[1:2] + w2 * s[2:3], w0 * s[1:2] + w1 * s[2:3], w0 * s[2:3],
                            jnp.zeros((8 - (CONV_W - 1), CONV_CH), F32)], axis=0)
    conv = jnp.concatenate([conv[0:8] + jnp.where(c == 0, head, 0.0), conv[8:]], axis=0)
    conv = _silu(conv)
    xs = conv[:, :SSM_W]
    ext_scr[0:L, :] = ext_scr[L:2 * L, :]

    x_dt = dt_ref[...] + dtb_ref[...]
    dt = jnp.maximum(x_dt, 0.0) + jnp.log1p(jnp.exp(-jnp.abs(x_dt)))
    ad = dt * (-jnp.exp(alog_ref[...]))
    li = lax.broadcasted_iota(jnp.int32, (L, L), 0)
    si = lax.broadcasted_iota(jnp.int32, (L, L), 1)
    tril = (si <= li).astype(BF16)
    a_cum = _dot(jnp.concatenate([tril] * 3, axis=1), jnp.concatenate(_split3(ad), axis=0))
    sel = sel_ref[...]
    ex = _dot(jnp.concatenate(_split3(a_cum) + _split3(dt), axis=0), sel)
    col = ex[0:L] + ex[L:2 * L] + ex[2 * L:3 * L]
    dt_all = ex[3 * L:4 * L] + ex[4 * L:5 * L] + ex[5 * L:6 * L]
    lane_s = lax.broadcasted_iota(jnp.int32, (L, SSM_W), 1) % SSM_HEAD_DIM
    row_l = lax.broadcasted_iota(jnp.int32, (L, SSM_W), 0)
    row = jnp.sum(jnp.where(lane_s == row_l, col, 0.0), axis=0, keepdims=True)
    lmat = jnp.exp(jnp.where(lane_s <= row_l, col - row, NEG_INF))
    a_last = col[L - 1:L, :]
    exp_a = jnp.exp(col)
    decay = jnp.exp(a_last - col)
    exp_last = jnp.exp(a_last)

    xd = xs * dt_all
    xdd = xd * decay
    bi = lax.broadcasted_iota(jnp.int32, (GROUP_W, GROUP_W), 0) // SSM_HEAD_DIM
    bj = lax.broadcasted_iota(jnp.int32, (GROUP_W, GROUP_W), 1) // SSM_HEAD_DIM
    blockdiag = bi == bj

    for g in range(N_GROUPS):
        gs = slice(GROUP_W * g, GROUP_W * (g + 1))
        b_g = conv[:, SSM_W + D_STATE * g:SSM_W + D_STATE * (g + 1)]
        c_g = conv[:, SSM_W + BC_W + D_STATE * g:SSM_W + BC_W + D_STATE * (g + 1)].astype(BF16)
        b_pad = _pad_rows(b_g, CHUNK)
        cb = _dot_nt(c_g, jnp.concatenate([b_pad.astype(BF16)] * 4, axis=0))
        m_g = (cb * lmat[:, gs]).astype(BF16)
        xd_g = _pad_rows(xd[:, gs], CHUNK).astype(BF16)
        xd_bd = jnp.where(blockdiag, jnp.concatenate([xd_g] * 4, axis=0), jnp.zeros((), BF16))
        y_diag = _dot(m_g, xd_bd)
        st_g = st_scr[:, gs]
        y_off = _dot(c_g, st_g.astype(BF16)) * exp_a[:, gs]
        new_st = _dot(b_pad.T.astype(BF16), _pad_rows(xdd[:, gs], CHUNK).astype(BF16))
        st_scr[:, gs] = st_g * exp_last[:, gs] + new_st

        y = y_diag + y_off + xs[:, gs] * dskip_ref[:, gs]
        z_ref = z0_ref if g < N_GROUPS // 2 else z1_ref
        zs = slice(GROUP_W * (g % (N_GROUPS // 2)), GROUP_W * (g % (N_GROUPS // 2) + 1))
        gated = y * _silu(z_ref[:, zs].astype(F32))
        ms = jnp.mean(gated * gated, axis=-1, keepdims=True)
        y_ref[:, gs] = (gated * lax.rsqrt(ms + RMS_EPS) * nw_ref[:, gs]).astype(y_ref.dtype)


def _ssd_init(c, ins, outs, scr):
    st0_ref = ins[4]
    ext_scr, st_scr = scr
    ns, L = ext_scr.shape[0], ext_scr.shape[1] // 2
    for s in range(ns):
        ext_scr[s, 0:L, :] = jnp.zeros((L, CONV_CH), BF16)
        st_scr[s] = st0_ref[s].T


def _ssd_body(c, ins, outs, scr):
    (lo_ref, hi_ref, dt_ref, cst_ref, _, cw_ref, cb_ref, dtb_ref, alog_ref, dskip_ref, nw_ref, sel_ref,
     shift_ref) = ins
    y_ref = outs[0]
    ext_scr, st_scr = scr
    ns, L = dt_ref.shape[0], dt_ref.shape[1]
    half = SSM_W // 2
    for s in range(ns):
        view = lambda ref, start, s=s: ref.at[s, :, pl.ds(start, half)]
        _ssd_chunk(c, view(lo_ref, 0), view(lo_ref, half), view(lo_ref, 2 * half), view(hi_ref, 0),
                   view(hi_ref, half), view(hi_ref, 2 * half),
                   dt_ref.at[s], cst_ref.at[s], cw_ref, cb_ref, dtb_ref, alog_ref, dskip_ref, nw_ref,
                   sel_ref, shift_ref, y_ref.at[s], ext_scr.at[s], st_scr.at[s], L)


def _ssd_final(c, ins, outs, scr):
    _, ncv_ref, nst_ref = outs
    ext_scr, st_scr = scr
    ns, L = ext_scr.shape[0], ext_scr.shape[1] // 2
    for s in range(ns):
        tail = ext_scr[s, L - BF16_ROWS:L, :].astype(F32)
        ncv_ref[s] = tail[BF16_ROWS - (CONV_W - 1):BF16_ROWS, :]
        nst_ref[s] = st_scr[s].T


def _shift_matrix(L):
    r = jnp.arange(3 * L)
    src = L + r % L - (CONV_W - 1) + r // L
    return (jnp.arange(2 * L)[None, :] == src[:, None]).astype(BF16)


def _ssd_part(proj3, dt3, conv_state, ssm_state, conv_w, conv_b, dtb, alog, dskip_all, ssm_nw, sel, L, ns):
    batch, t, _ = proj3.shape
    const = lambda shape: pl.BlockSpec(shape, lambda g, c: (0,) * len(shape))
    per_g = lambda shape: pl.BlockSpec((ns,) + shape, lambda g, c: (g,) + (0,) * len(shape))
    third = lambda k: pl.BlockSpec((ns, L, QKV_W), lambda g, c: (g, c, k))
    hp = N_SSM_HEADS * SSM_HEAD_DIM
    assert COL_XS == QKV_W and PROJ_W == 3 * QKV_W
    return _Part(
        in_specs=[
            third(1), third(2),
            pl.BlockSpec((ns, L, LANES), lambda g, c: (g, c, 0)),
            per_g((CONV_W - 1, CONV_CH)),
            per_g((hp, D_STATE)),
            const((CONV_W, CONV_CH)), const((1, CONV_CH)), const((1, LANES)), const((1, LANES)),
            const((1, SSM_W)), const((1, SSM_W)), const((LANES, SSM_W)), const((3 * L, 2 * L)),
        ],
        out_specs=[
            pl.BlockSpec((ns, L, SSM_W), lambda g, c: (g, c, 0)),
            per_g((CONV_W - 1, CONV_CH)),
            per_g((hp, D_STATE)),
        ],
        out_shapes=[
            jax.ShapeDtypeStruct((batch, t, SSM_W), BF16),
            jax.ShapeDtypeStruct((batch, CONV_W - 1, CONV_CH), F32),
            jax.ShapeDtypeStruct((batch, hp, D_STATE), F32),
        ],
        scratch=[
            pltpu.VMEM((ns, 2 * L, CONV_CH), BF16),
            pltpu.VMEM((ns, D_STATE, hp), F32),
        ],
        operands=[proj3, proj3, dt3, conv_state, ssm_state, conv_w, conv_b, dtb, alog, dskip_all, ssm_nw, sel,
                  _shift_matrix(L)],
        body=_ssd_body, init=_ssd_init, final=_ssd_final)


def _outproj_kernel(*refs, n_cast):
    a_ref, y_ref, wa_ref, wy_ref, x_ref = refs[:5]
    cast_in, o_ref, cast_out = refs[5:5 + n_cast], refs[5 + n_cast], refs[6 + n_cast:]
    o_ref[...] = x_ref[...] + _dot(a_ref[...], wa_ref[...]) + _dot(y_ref[...], wy_ref[...])
    _cast_blocks(cast_in, cast_out)


def _outproj(attn, y, w_out, x2d, tm, tn, cast=()):
    t = x2d.shape[0]
    half = D_MODEL // 2
    ni = t // tm
    c_in, c_out, c_shape = _cast_riders(cast, (D_MODEL // tn) * ni, lambda j, i: j * ni + i)
    return pl.pallas_call(
        functools.partial(_outproj_kernel, n_cast=len(cast)),
        grid=(D_MODEL // tn, ni),
        in_specs=[
            pl.BlockSpec((tm, half), lambda j, i: (i, 0)),
            pl.BlockSpec((tm, half), lambda j, i: (i, 0)),
            pl.BlockSpec((half, tn), lambda j, i: (0, j)),
            pl.BlockSpec((half, tn), lambda j, i: (1, j)),
            pl.BlockSpec((tm, tn), lambda j, i: (i, j)),
        ] + c_in,
        out_specs=[pl.BlockSpec((tm, tn), lambda j, i: (i, j))] + c_out,
        out_shape=[jax.ShapeDtypeStruct((t, D_MODEL), F32)] + c_shape,
        compiler_params=_params("arbitrary", "arbitrary"),
        name="outproj",
    )(attn, y, w_out, w_out, x2d, *cast)


def _ffn_kernel(x_ref, nw_ref, wu_ref, wd_ref, o_ref, h_scr, *, n_split):
    @pl.when(pl.program_id(1) == 0)
    def _():
        _rmsnorm_rows(x_ref, nw_ref, h_scr, copy_ref=o_ref)

    u = jnp.maximum(_dot(h_scr[...], wu_ref[...]), 0.0)
    u = (u * u).astype(BF16)
    wn = D_MODEL // n_split
    for n in range(n_split):
        o_ref[:, wn * n:wn * (n + 1)] += _dot(u, wd_ref[:, wn * n:wn * (n + 1)])


def _ffn(x2d, norm_w, w_up, w_down, tm, tf):
    t = x2d.shape[0]
    return pl.pallas_call(
        functools.partial(_ffn_kernel, n_split=4),
        grid=(t // tm, FFN_HIDDEN // tf),
        in_specs=[
            pl.BlockSpec((tm, D_MODEL), lambda i, f: (i, 0)),
            pl.BlockSpec((1, D_MODEL), lambda i, f: (0, 0)),
            pl.BlockSpec((D_MODEL, tf), lambda i, f: (0, f)),
            pl.BlockSpec((tf, D_MODEL), lambda i, f: (f, 0)),
        ],
        out_specs=pl.BlockSpec((tm, D_MODEL), lambda i, f: (i, 0)),
        out_shape=jax.ShapeDtypeStruct((t, D_MODEL), F32),
        scratch_shapes=[pltpu.VMEM((tm, D_MODEL), BF16)],
        compiler_params=_params("arbitrary", "arbitrary"),
        name="ffn",
    )(x2d, norm_w, w_up, w_down)


def _rope_tables(pos):
    inv_freq = ROPE_THETA ** (-jnp.arange(ROPE_HALF, dtype=F32) / ROPE_HALF)
    lane = jnp.arange(LANES) % HEAD_DIM
    freq = jnp.where(lane < ROPE_DIM, inv_freq[lane % ROPE_HALF], 0.0)
    ang = pos.astype(F32)[:, None] * freq[None, :]
    cos, sin = jnp.cos(ang), jnp.sin(ang)
    return [cos, jnp.where(lane < ROPE_HALF, -sin, 0.0), jnp.where(lane >= ROPE_HALF, sin, 0.0)]


def _layer_params(norm_mix_w, w_in, q_norm_w, k_norm_w, attn_sinks, conv_w, conv_b, dt_bias, a_log, d_skip,
                  ssm_norm_w, w_out, norm_ffn_w, w_up, w_down):
    pad_h = lambda a: jnp.pad(a.astype(F32), (0, LANES - N_SSM_HEADS))[None, :]
    sel = (jnp.arange(LANES)[:, None] == (jnp.arange(SSM_W) // SSM_HEAD_DIM)[None, :]).astype(BF16)
    return dict(
        norm_mix=norm_mix_w.astype(F32)[None, :],
        qw=jnp.tile(q_norm_w.astype(F32), 2)[None, :], kw=jnp.tile(k_norm_w.astype(F32), 2)[None, :],
        sinks=jnp.repeat(attn_sinks.astype(F32), HEAD_DIM).reshape(Q_TILES, LANES),
        conv_w=conv_w.astype(F32), conv_b=conv_b.astype(F32)[None, :],
        dtb=pad_h(dt_bias), alog=pad_h(a_log),
        dskip=jnp.repeat(d_skip.astype(F32), SSM_HEAD_DIM)[None, :],
        ssm_nw=ssm_norm_w.astype(F32)[None, :], sel=sel,
        norm_ffn=norm_ffn_w.astype(F32)[None, :],
    )


def _stream_back(x, proj, dt_raw, pos, cache, conv_state, ssm_state, p, tm, w_out, w_up, w_down):
    batch, t, _ = x.shape
    ns = max(n for n in STREAMS_PER_STEP if batch % n == 0)
    x2d = x.reshape(batch * t, D_MODEL)
    proj3 = proj.reshape(batch, t, PROJ_W)
    tabs = _rope_tables(pos)
    if cache is None:
        attn_part = _attn_prompt_part(proj3, tabs, p["qw"], p["kw"], p["sinks"], ns)
        L = CHUNK
    else:
        past_k, past_v = cache
        attn_part = _attn_sample_part(proj3, past_k.reshape(batch, WINDOW, KV_W),
                                      past_v.reshape(batch, WINDOW, KV_W), tabs, p["qw"], p["kw"], p["sinks"], ns)
        L = min(CHUNK, t)
    hp = N_SSM_HEADS * SSM_HEAD_DIM
    ssd_part = _ssd_part(proj3, dt_raw.reshape(batch, t, LANES), conv_state,
                         ssm_state.reshape(batch, hp, D_STATE), p["conv_w"], p["conv_b"], p["dtb"], p["alog"],
                         p["dskip"], p["ssm_nw"], p["sel"], L, ns)
    (attn, new_k, new_v), (y, new_conv, new_ssm) = _fused_call([attn_part, ssd_part], (batch // ns, t // L),
                                                              "mixers")
    late = () if w_down.dtype == BF16 else (w_down,)
    x1, *cast = _outproj(attn.reshape(batch * t, ATTN_W), y.reshape(batch * t, SSM_W), w_out, x2d, tm, 1024,
                         cast=late)
    if late:
        w_down = cast.pop()
    out = _ffn(x1, p["norm_ffn"], w_up, w_down, tm, 512)
    return (out.reshape(batch, t, D_MODEL),
            new_k.reshape(batch, WINDOW, N_KV_HEADS, HEAD_DIM), new_v.reshape(batch, WINDOW, N_KV_HEADS, HEAD_DIM),
            new_conv, new_ssm.reshape(batch, N_SSM_HEADS, SSM_HEAD_DIM, D_STATE), w_down)


def kernel(x_prompt, x_sample, cache_k, cache_v, state_conv, state_ssm, norm_mix_w, w_in, q_norm_w, k_norm_w,
           attn_sinks, conv_w, conv_b, dt_bias, a_log, d_skip, ssm_norm_w, w_out, norm_ffn_w, w_up, w_down):
    depth = w_in.shape[0]
    b_p, t_p, _ = x_prompt.shape
    b_s, t_s, _ = x_sample.shape
    assert cache_k.shape[2] == WINDOW and t_p % CHUNK == 0 and t_p >= WINDOW
    assert t_s <= CHUNK and t_s % BF16_ROWS == 0
    pos_p = jnp.arange(t_p, dtype=jnp.int32)
    pos_s = PAST_LEN + jnp.arange(t_s, dtype=jnp.int32)
    hp, hs = x_prompt, x_sample
    outs_p, outs_s = [], []
    for layer in range(depth):
        p = _layer_params(norm_mix_w[layer], w_in[layer], q_norm_w[layer], k_norm_w[layer], attn_sinks[layer],
                          conv_w[layer], conv_b[layer], dt_bias[layer], a_log[layer], d_skip[layer],
                          ssm_norm_w[layer], w_out[layer], norm_ffn_w[layer], w_up[layer], w_down[layer])
        conv0 = jnp.zeros((b_p, CONV_W - 1, CONV_CH), F32)
        ssm0 = jnp.zeros((b_p, N_SSM_HEADS, SSM_HEAD_DIM, D_STATE), F32)
        tm = 512
        xs2d, xp2d = hs.reshape(b_s * t_s, D_MODEL), hp.reshape(b_p * t_p, D_MODEL)
        w_in_t = w_in[layer].T.astype(BF16)
        proj_p, dt_p, w_out16, w_up16 = _inproj(xp2d, p["norm_mix"], w_in_t, tm, 1536,
                                                cast=(w_out[layer], w_up[layer]))
        proj_s, dt_s = _inproj(xs2d, p["norm_mix"], w_in_t, tm, 1536)
        hp, *rest_p, w_down16 = _stream_back(hp, proj_p, dt_p, pos_p, None, conv0, ssm0, p, tm,
                                             w_out16, w_up16, w_down[layer])
        hs, *rest_s, _ = _stream_back(hs, proj_s, dt_s, pos_s, (cache_k[layer], cache_v[layer]),
                                      state_conv[layer], state_ssm[layer], p, tm, w_out16, w_up16, w_down16)
        outs_p.append(rest_p)
        outs_s.append(rest_s)
    stack = lambda outs, i: jnp.stack([o[i] for o in outs])
    return (hp, hs, stack(outs_p, 0), stack(outs_p, 1), stack(outs_p, 2), stack(outs_p, 3),
            stack(outs_s, 0), stack(outs_s, 1), stack(outs_s, 2), stack(outs_s, 3))
```

```python
import functools

import jax
import jax.numpy as jnp
from jax import lax
from jax.experimental import pallas as pl
from jax.experimental.pallas import tpu as pltpu

F32 = jnp.float32
BF16 = jnp.bfloat16

D_MODEL = 4096
HEAD_DIM = 64
N_Q_HEADS = 32
N_KV_HEADS = 8
ATTN_W = N_Q_HEADS * HEAD_DIM
KV_W = N_KV_HEADS * HEAD_DIM
ROPE_DIM = 16
ROPE_HALF = ROPE_DIM // 2
ROPE_THETA = 500000.0
ATTN_SCALE = HEAD_DIM ** -0.5
LOG2E = 1.4426950408889634
WINDOW = 128
CHUNK = 64
PAST_LEN = 1024
SSM_W = 2048
N_SSM_HEADS = 32
SSM_HEAD_DIM = 64
N_GROUPS = 8
D_STATE = 128
BC_W = N_GROUPS * D_STATE
GROUP_W = SSM_W // N_GROUPS
CONV_W = 4
CONV_CH = SSM_W + 2 * BC_W
FFN_HIDDEN = 4 * D_MODEL
RMS_EPS = 1e-6
NEG_INF = -1e30
LANES = 128
BF16_ROWS = 16
Q_TILES = ATTN_W // LANES
KV_TILES = KV_W // LANES
PROJ_W = ATTN_W + 2 * KV_W + 2 * SSM_W + 2 * BC_W
COL_Q, COL_K, COL_V, COL_XS, COL_Z, COL_B, COL_C = 0, 2048, 2560, 3072, 5120, 7168, 8192
QKV_W = ATTN_W + 2 * KV_W
VMEM_LIMIT = 62 * 1024 * 1024
STREAMS_PER_STEP = (1, 2, 4)


def _params(*sem):
    return pltpu.CompilerParams(dimension_semantics=sem, vmem_limit_bytes=VMEM_LIMIT)


def _dot(a, b):
    return jnp.dot(a, b, preferred_element_type=F32)


def _dot_nt(a, b):
    return lax.dot_general(a, b, (((1,), (1,)), ((), ())), preferred_element_type=F32)


def _split3(x):
    a = x.astype(BF16)
    r = x - a.astype(F32)
    b = r.astype(BF16)
    c = (r - b.astype(F32)).astype(BF16)
    return a, b, c


def _silu(x):
    return x / (1.0 + jnp.exp2(x * -LOG2E))


NORM_ROWS = 16
NORM_UNROLL = 8


def _rmsnorm_rows(x_ref, nw_ref, h_scr, copy_ref=None):
    def body(i, carry):
        r = pl.ds(pl.multiple_of(i * NORM_ROWS, NORM_ROWS), NORM_ROWS)
        x = x_ref[r, :]
        ms = jnp.mean(x * x, axis=-1, keepdims=True)
        h_scr[r, :] = (x * lax.rsqrt(ms + RMS_EPS) * nw_ref[...]).astype(BF16)
        if copy_ref is not None:
            copy_ref[r, :] = x
        return carry

    lax.fori_loop(0, x_ref.shape[0] // NORM_ROWS, body, 0, unroll=NORM_UNROLL)


def _cast_riders(arrays, steps, linear_step):
    in_specs, out_specs, out_shapes = [], [], []
    for a in arrays:
        rows, cols = a.shape
        nb = max(n for n in range(1, steps + 1) if rows % n == 0 and (rows // n) % BF16_ROWS == 0)
        spec = pl.BlockSpec((rows // nb, cols), lambda *g, nb=nb: (jnp.minimum(linear_step(*g), nb - 1), 0))
        in_specs.append(spec)
        out_specs.append(spec)
        out_shapes.append(jax.ShapeDtypeStruct(a.shape, BF16))
    return in_specs, out_specs, out_shapes


def _cast_blocks(src_refs, dst_refs):
    for src, dst in zip(src_refs, dst_refs):
        dst[...] = src[...].astype(BF16)


def _inproj_kernel(*refs, n_cast):
    x_ref, nw_ref, w_ref, wdt_ref = refs[:4]
    cast_in = refs[4:4 + n_cast]
    proj_ref, dt_ref = refs[4 + n_cast:6 + n_cast]
    cast_out, h_scr = refs[6 + n_cast:6 + 2 * n_cast], refs[6 + 2 * n_cast]

    @pl.when(pl.program_id(1) == 0)
    def _():
        _rmsnorm_rows(x_ref, nw_ref, h_scr)
        real = lax.broadcasted_iota(jnp.int32, wdt_ref.shape, 0) < N_SSM_HEADS
        dt_ref[...] = _dot_nt(h_scr[...], jnp.where(real, wdt_ref[...], jnp.zeros((), BF16)))

    proj_ref[...] = _dot_nt(h_scr[...], w_ref[...]).astype(BF16)
    _cast_blocks(cast_in, cast_out)


def _inproj(x2d, norm_w, w_in_t, tm, tn, cast=()):
    t = x2d.shape[0]
    nj = PROJ_W // tn
    c_in, c_out, c_shape = _cast_riders(cast, (t // tm) * nj, lambda i, j: i * nj + j)
    return pl.pallas_call(
        functools.partial(_inproj_kernel, n_cast=len(cast)),
        grid=(t // tm, nj),
        in_specs=[
            pl.BlockSpec((tm, D_MODEL), lambda i, j: (i, 0)),
            pl.BlockSpec((1, D_MODEL), lambda i, j: (0, 0)),
            pl.BlockSpec((tn, D_MODEL), lambda i, j: (j, 0)),
            pl.BlockSpec((LANES, D_MODEL), lambda i, j: (PROJ_W // LANES, 0)),
        ] + c_in,
        out_specs=[
            pl.BlockSpec((tm, tn), lambda i, j: (i, j)),
            pl.BlockSpec((tm, LANES), lambda i, j: (i, 0)),
        ] + c_out,
        out_shape=[
            jax.ShapeDtypeStruct((t, PROJ_W), BF16),
            jax.ShapeDtypeStruct((t, LANES), F32),
        ] + c_shape,
        scratch_shapes=[pltpu.VMEM((tm, D_MODEL), BF16)],
        compiler_params=_params("arbitrary", "arbitrary"),
        name="inproj",
    )(x2d, norm_w, w_in_t, w_in_t, *cast)


def _half_ones():
    r = lax.broadcasted_iota(jnp.int32, (LANES, LANES), 0) // HEAD_DIM
    c = lax.broadcasted_iota(jnp.int32, (LANES, LANES), 1) // HEAD_DIM
    return (r == c).astype(BF16)


def _head_norm(x, w_tile):
    ss = _dot((x * x).astype(BF16), _half_ones())
    return x * lax.rsqrt(ss * (1.0 / HEAD_DIM) + RMS_EPS) * w_tile


def _rope(x, cos_t, sin_lo, sin_hi):
    return (x * cos_t + pltpu.roll(x, LANES - ROPE_HALF, 1) * sin_lo
            + pltpu.roll(x, ROPE_HALF, 1) * sin_hi)


def _lane_is_a(shape):
    return lax.broadcasted_iota(jnp.int32, shape, 1) < HEAD_DIM


def _kv_tiles(k_win, v_win):
    is_a = _lane_is_a((CHUNK, LANES))
    ones_a = jnp.where(is_a, 1.0, 0.0).astype(F32)
    ones_b = 1.0 - ones_a
    kks, vvs = [], []
    for h in range(N_KV_HEADS):
        j, odd = h // 2, h % 2
        kt = k_win[:, LANES * j:LANES * (j + 1)]
        vt = v_win[:, LANES * j:LANES * (j + 1)]
        keep = jnp.logical_not(is_a) if odd else is_a
        k_own = jnp.where(keep, kt, 0.0)
        v_own = jnp.where(keep, vt, 0.0)
        k_sw = pltpu.roll(k_own, HEAD_DIM, 1)
        v_sw = pltpu.roll(v_own, HEAD_DIM, 1)
        k_a, k_b = (k_sw, k_own) if odd else (k_own, k_sw)
        v_a, v_b = (v_sw, v_own) if odd else (v_own, v_sw)
        kks.append(jnp.concatenate([k_a, k_b], axis=0).astype(BF16))
        vv = jnp.concatenate([jnp.concatenate([v_a, ones_a], axis=1),
                              jnp.concatenate([v_b, ones_b], axis=1)], axis=0)
        vvs.append(vv.astype(BF16))
    return kks, vvs


def _prep_q(q_ref, qw_ref, cos_ref, slo_ref, shi_ref, tq):
    q2 = jnp.concatenate([q_ref[:, LANES * i:LANES * (i + 1)].astype(F32) for i in range(Q_TILES)], axis=0)
    q2 = _head_norm(q2, qw_ref[...])
    cos_t, slo, shi = cos_ref[...], slo_ref[...], shi_ref[...]
    return [_rope(q2[tq * i:tq * (i + 1)], cos_t, slo, shi).astype(BF16) for i in range(Q_TILES)]


def _attend(q_tiles, kk_of, vv_of, valid_of, sink_ref, o_ref, tq):
    is_a = _lane_is_a((2 * tq, LANES))
    for h in range(N_KV_HEADS):
        qh = jnp.concatenate([q_tiles[2 * h], q_tiles[2 * h + 1]], axis=0)
        s = _dot_nt(qh, kk_of(h))
        sw = [jnp.where(valid_of(w), s[:, LANES * w:LANES * (w + 1)], NEG_INF) for w in range(3)]
        mt = jnp.maximum(jnp.maximum(sw[0], sw[1]), sw[2])
        m_a = jnp.max(jnp.where(is_a, mt, NEG_INF), axis=-1, keepdims=True)
        m_b = jnp.max(jnp.where(is_a, NEG_INF, mt), axis=-1, keepdims=True)
        sink = jnp.concatenate([jnp.broadcast_to(sink_ref[2 * h:2 * h + 1, :], (tq, LANES)),
                                jnp.broadcast_to(sink_ref[2 * h + 1:2 * h + 2, :], (tq, LANES))], axis=0)
        m = jnp.maximum(jnp.where(is_a, m_a, m_b), sink)
        e = jnp.concatenate([jnp.exp2(x - m) for x in sw], axis=1).astype(BF16)
        oa = _dot(e, vv_of(h))
        o = oa[:, :LANES] / (oa[:, LANES:] + jnp.exp2(sink - m))
        o_ref[:, LANES * 2 * h:LANES * (2 * h + 1)] = o[:tq].astype(o_ref.dtype)
        o_ref[:, LANES * (2 * h + 1):LANES * (2 * h + 2)] = o[tq:].astype(o_ref.dtype)


def _prep_k(k_raw, kw_ref, cos_ref, slo_ref, shi_ref, rows):
    k2 = jnp.concatenate([k_raw[:, LANES * j:LANES * (j + 1)].astype(F32) for j in range(KV_TILES)], axis=0)
    k2 = _head_norm(k2, kw_ref[...])
    cos_t, slo, shi = cos_ref[...], slo_ref[...], shi_ref[...]
    return jnp.concatenate([_rope(k2[rows * j:rows * (j + 1)], cos_t, slo, shi) for j in range(KV_TILES)], axis=1)


class _Part:
    def __init__(self, in_specs, operands, out_specs, out_shapes, scratch, body, init=None, final=None):
        self.in_specs, self.operands, self.out_specs, self.out_shapes = in_specs, operands, out_specs, out_shapes
        self.scratch, self.body, self.init, self.final = scratch, body, init, final


def _fused_kernel(*refs, parts):
    n_in = [len(p.in_specs) for p in parts]
    n_out = [len(p.out_specs) for p in parts]
    n_scr = [len(p.scratch) for p in parts]
    pos = 0
    groups = []
    for counts in (n_in, n_out, n_scr):
        groups.append([])
        for n in counts:
            groups[-1].append(refs[pos:pos + n])
            pos += n
    args = [(p, groups[0][i], groups[1][i], groups[2][i]) for i, p in enumerate(parts)]
    c = pl.program_id(1)

    @pl.when(c == 0)
    def _():
        for p, ins, outs, scr in args:
            if p.init is not None:
                p.init(c, ins, outs, scr)

    for p, ins, outs, scr in args:
        p.body(c, ins, outs, scr)

    @pl.when(c == pl.num_programs(1) - 1)
    def _():
        for p, ins, outs, scr in args:
            if p.final is not None:
                p.final(c, ins, outs, scr)


def _fused_call(parts, grid, name):
    outs = pl.pallas_call(
        functools.partial(_fused_kernel, parts=parts),
        grid=grid,
        in_specs=[s for p in parts for s in p.in_specs],
        out_specs=[s for p in parts for s in p.out_specs],
        out_shape=[s for p in parts for s in p.out_shapes],
        scratch_shapes=[s for p in parts for s in p.scratch],
        compiler_params=_params("arbitrary", "arbitrary"),
        name=name,
    )(*[o for p in parts for o in p.operands])
    split, pos = [], 0
    for p in parts:
        split.append(outs[pos:pos + len(p.out_specs)])
        pos += len(p.out_specs)
    return split


def _attn_prompt_init(c, ins, outs, scr):
    for ring in scr:
        ring[...] = jnp.zeros_like(ring)


def _attn_prompt_body(c, ins, outs, scr):
    qkv_ref, cos_ref, slo_ref, shi_ref, qw_ref, kw_ref, sink_ref = ins
    o_ref, nk_ref, nv_ref = outs
    kk_scr, vv_scr = scr
    slot = c % 3
    for s in range(qkv_ref.shape[0]):
        q_ref = qkv_ref.at[s, :, pl.ds(COL_Q, ATTN_W)]
        k_fin = _prep_k(qkv_ref[s, :, COL_K:COL_K + KV_W], kw_ref, cos_ref, slo_ref, shi_ref, CHUNK)
        v_cur = qkv_ref[s, :, COL_V:COL_V + KV_W].astype(F32)
        nk_ref[s] = k_fin
        nv_ref[s] = v_cur
        kks, vvs = _kv_tiles(k_fin, v_cur)
        for h in range(N_KV_HEADS):
            kk_scr[s, slot, h] = kks[h]
            vv_scr[s, slot, h] = vvs[h]

        q_tiles = _prep_q(q_ref, qw_ref, cos_ref, slo_ref, shi_ref, CHUNK)
        _attend(q_tiles,
                lambda h, s=s: jnp.concatenate([kk_scr[s, w, h] for w in range(3)], axis=0),
                lambda h, s=s: jnp.concatenate([vv_scr[s, w, h] for w in range(3)], axis=0),
                lambda w: (c - w + 3) % 3 <= c,
                sink_ref, o_ref.at[s], CHUNK)


def _attn_prompt_part(proj3, rope_tabs, qw_tile, kw_tile, sink_tab, ns):
    batch, t, _ = proj3.shape
    nc = t // CHUNK
    keep = WINDOW // CHUNK
    const = lambda shape: pl.BlockSpec(shape, lambda g, c: (0,) * len(shape))
    keep_spec = pl.BlockSpec((ns, CHUNK, KV_W), lambda g, c: (g, jnp.maximum(c - (nc - keep), 0), 0))
    return _Part(
        in_specs=[
            pl.BlockSpec((ns, CHUNK, QKV_W), lambda g, c: (g, c, 0)),
            *[pl.BlockSpec((CHUNK, LANES), lambda g, c: (c, 0))] * 3,
            const((1, LANES)), const((1, LANES)), const((Q_TILES, LANES)),
        ],
        operands=[proj3, *rope_tabs, qw_tile, kw_tile, sink_tab],
        out_specs=[pl.BlockSpec((ns, CHUNK, ATTN_W), lambda g, c: (g, c, 0)), keep_spec, keep_spec],
        out_shapes=[
            jax.ShapeDtypeStruct((batch, t, ATTN_W), BF16),
            jax.ShapeDtypeStruct((batch, WINDOW, KV_W), F32),
            jax.ShapeDtypeStruct((batch, WINDOW, KV_W), F32),
        ],
        scratch=[
            pltpu.VMEM((ns, 3, N_KV_HEADS, 2 * CHUNK, LANES), BF16),
            pltpu.VMEM((ns, 3, N_KV_HEADS, 2 * CHUNK, 2 * LANES), BF16),
        ],
        body=_attn_prompt_body, init=_attn_prompt_init)


def _attn_sample_body(c, ins, outs, scr):
    qkv_ref, ck_ref, cv_ref, cos_ref, slo_ref, shi_ref, qw_ref, kw_ref, sink_ref = ins
    o_ref, nk_ref, nv_ref = outs
    tq = qkv_ref.shape[1]
    new_valid = lax.broadcasted_iota(jnp.int32, (2 * tq, LANES), 1) % HEAD_DIM < tq
    pad = jnp.zeros((CHUNK - tq, KV_W), F32)
    for s in range(qkv_ref.shape[0]):
        q_ref = qkv_ref.at[s, :, pl.ds(COL_Q, ATTN_W)]
        k_fin = _prep_k(qkv_ref[s, :, COL_K:COL_K + KV_W], kw_ref, cos_ref, slo_ref, shi_ref, tq)
        v_cur = qkv_ref[s, :, COL_V:COL_V + KV_W].astype(F32)
        nk_ref[s, 0:WINDOW - tq, :] = ck_ref[s, tq:WINDOW, :]
        nv_ref[s, 0:WINDOW - tq, :] = cv_ref[s, tq:WINDOW, :]
        nk_ref[s, WINDOW - tq:WINDOW, :] = k_fin
        nv_ref[s, WINDOW - tq:WINDOW, :] = v_cur
        wins = [(ck_ref[s, :CHUNK, :], cv_ref[s, :CHUNK, :]),
                (ck_ref[s, CHUNK:, :], cv_ref[s, CHUNK:, :]),
                (jnp.concatenate([k_fin, pad], axis=0), jnp.concatenate([v_cur, pad], axis=0))]
        tiles = [_kv_tiles(kw, vw) for kw, vw in wins]
        q_tiles = _prep_q(q_ref, qw_ref, cos_ref, slo_ref, shi_ref, tq)
        _attend(q_tiles,
                lambda h, tiles=tiles: jnp.concatenate([tiles[w][0][h] for w in range(3)], axis=0),
                lambda h, tiles=tiles: jnp.concatenate([tiles[w][1][h] for w in range(3)], axis=0),
                lambda w: new_valid if w == 2 else True,
                sink_ref, o_ref.at[s], tq)


def _attn_sample_part(proj3, cache_k, cache_v, rope_tabs, qw_tile, kw_tile, sink_tab, ns):
    batch, tq, _ = proj3.shape
    const = lambda shape: pl.BlockSpec(shape, lambda g, c: (0,) * len(shape))
    cache_spec = pl.BlockSpec((ns, WINDOW, KV_W), lambda g, c: (g, 0, 0))
    return _Part(
        in_specs=[
            pl.BlockSpec((ns, tq, QKV_W), lambda g, c: (g, 0, 0)),
            cache_spec, cache_spec,
            *[pl.BlockSpec((tq, LANES), lambda g, c: (0, 0))] * 3,
            const((1, LANES)), const((1, LANES)), const((Q_TILES, LANES)),
        ],
        operands=[proj3, cache_k, cache_v, *rope_tabs, qw_tile, kw_tile, sink_tab],
        out_specs=[pl.BlockSpec((ns, tq, ATTN_W), lambda g, c: (g, 0, 0)), cache_spec, cache_spec],
        out_shapes=[
            jax.ShapeDtypeStruct((batch, tq, ATTN_W), BF16),
            jax.ShapeDtypeStruct((batch, WINDOW, KV_W), F32),
            jax.ShapeDtypeStruct((batch, WINDOW, KV_W), F32),
        ],
        scratch=[], body=_attn_sample_body)


def _pad_rows(a, rows):
    if a.shape[0] == rows:
        return a
    return jnp.concatenate([a, jnp.zeros((rows - a.shape[0], a.shape[1]), a.dtype)], axis=0)


def _ssd_chunk(c, xs0_ref, xs1_ref, z0_ref, z1_ref, b_ref, c_ref, dt_ref, cst_ref, cw_ref, cb_ref, dtb_ref,
               alog_ref, dskip_ref, nw_ref, sel_ref, shift_ref, y_ref, ext_scr, st_scr, L):
    half = SSM_W // 2
    ext_scr[L:2 * L, 0:half] = xs0_ref[...]
    ext_scr[L:2 * L, half:SSM_W] = xs1_ref[...]
    ext_scr[L:2 * L, SSM_W:SSM_W + BC_W] = b_ref[...]
    ext_scr[L:2 * L, SSM_W + BC_W:CONV_CH] = c_ref[...]
    delayed = _dot(shift_ref[...], ext_scr[...])
    cur = ext_scr[L:2 * L, :].astype(F32)
    conv = cb_ref[...] + cur * cw_ref[CONV_W - 1:CONV_W, :]
    for i in range(CONV_W - 1):
        conv = conv + delayed[L * i:L * (i + 1)] * cw_ref[i:i + 1, :]
    s = cst_ref[...]
    w0, w1, w2 = cw_ref[0:1, :], cw_ref[1:2, :], cw_ref[2:3, :]
    head = jnp.concatenate([w0 * s[0:1] + w1 * s[1:2] + w2 * s[2:3], w0 * s[1:2] + w1 * s[2:3], w0 * s[2:3],
                            jnp.zeros((8 - (CONV_W - 1), CONV_CH), F32)], axis=0)
    conv = jnp.concatenate([conv[0:8] + jnp.where(c == 0, head, 0.0), conv[8:]], axis=0)
    conv = _silu(conv)
    xs = conv[:, :SSM_W]
    ext_scr[0:L, :] = ext_scr[L:2 * L, :]

    x_dt = dt_ref[...] + dtb_ref[...]
    dt = jnp.maximum(x_dt, 0.0) + jnp.log1p(jnp.exp(-jnp.abs(x_dt)))
    ad = dt * (-LOG2E * jnp.exp(alog_ref[...]))
    li = lax.broadcasted_iota(jnp.int32, (L, L), 0)
    si = lax.broadcasted_iota(jnp.int32, (L, L), 1)
    tril = (si <= li).astype(BF16)
    a_cum = _dot(jnp.concatenate([tril] * 3, axis=1), jnp.concatenate(_split3(ad), axis=0))
    sel = sel_ref[...]
    ex = _dot(jnp.concatenate(_split3(a_cum) + _split3(dt), axis=0), sel)
    col = ex[0:L] + ex[L:2 * L] + ex[2 * L:3 * L]
    dt_all = ex[3 * L:4 * L] + ex[4 * L:5 * L] + ex[5 * L:6 * L]
    lane_s = lax.broadcasted_iota(jnp.int32, (L, SSM_W), 1) % SSM_HEAD_DIM
    row_l = lax.broadcasted_iota(jnp.int32, (L, SSM_W), 0)
    row = jnp.sum(jnp.where(lane_s == row_l, col, 0.0), axis=0, keepdims=True)
    lmat = jnp.exp2(jnp.where(lane_s <= row_l, col - row, NEG_INF))
    a_last = col[L - 1:L, :]
    exp_a = jnp.exp2(col)
    decay = jnp.exp2(a_last - col)
    exp_last = jnp.exp2(a_last)

    xd = xs * dt_all
    xdd = xd * decay
    bi = lax.broadcasted_iota(jnp.int32, (GROUP_W, GROUP_W), 0) // SSM_HEAD_DIM
    bj = lax.broadcasted_iota(jnp.int32, (GROUP_W, GROUP_W), 1) // SSM_HEAD_DIM
    blockdiag = bi == bj

    for g in range(N_GROUPS):
        gs = slice(GROUP_W * g, GROUP_W * (g + 1))
        b_g = conv[:, SSM_W + D_STATE * g:SSM_W + D_STATE * (g + 1)]
        c_g = conv[:, SSM_W + BC_W + D_STATE * g:SSM_W + BC_W + D_STATE * (g + 1)].astype(BF16)
        b_pad = _pad_rows(b_g, CHUNK)
        cb = _dot_nt(c_g, jnp.concatenate([b_pad.astype(BF16)] * 4, axis=0))
        m_g = (cb * lmat[:, gs]).astype(BF16)
        xd_g = _pad_rows(xd[:, gs], CHUNK).astype(BF16)
        xd_bd = jnp.where(blockdiag, jnp.concatenate([xd_g] * 4, axis=0), jnp.zeros((), BF16))
        y_diag = _dot(m_g, xd_bd)
        st_g = st_scr[:, gs]
        y_off = _dot(c_g, st_g.astype(BF16)) * exp_a[:, gs]
        new_st = _dot(b_pad.T.astype(BF16), _pad_rows(xdd[:, gs], CHUNK).astype(BF16))
        st_scr[:, gs] = st_g * exp_last[:, gs] + new_st

        y = y_diag + y_off + xs[:, gs] * dskip_ref[:, gs]
        z_ref = z0_ref if g < N_GROUPS // 2 else z1_ref
        zs = slice(GROUP_W * (g % (N_GROUPS // 2)), GROUP_W * (g % (N_GROUPS // 2) + 1))
        gated = y * _silu(z_ref[:, zs].astype(F32))
        ms = jnp.mean(gated * gated, axis=-1, keepdims=True)
        y_ref[:, gs] = (gated * lax.rsqrt(ms + RMS_EPS) * nw_ref[:, gs]).astype(y_ref.dtype)


def _ssd_init(c, ins, outs, scr):
    st0_ref = ins[4]
    ext_scr, st_scr = scr
    ns, L = ext_scr.shape[0], ext_scr.shape[1] // 2
    for s in range(ns):
        ext_scr[s, 0:L, :] = jnp.zeros((L, CONV_CH), BF16)
        st_scr[s] = st0_ref[s].T


def _ssd_body(c, ins, outs, scr):
    (lo_ref, hi_ref, dt_ref, cst_ref, _, cw_ref, cb_ref, dtb_ref, alog_ref, dskip_ref, nw_ref, sel_ref,
     shift_ref) = ins
    y_ref = outs[0]
    ext_scr, st_scr = scr
    ns, L = dt_ref.shape[0], dt_ref.shape[1]
    half = SSM_W // 2
    for s in range(ns):
        view = lambda ref, start, s=s: ref.at[s, :, pl.ds(start, half)]
        _ssd_chunk(c, view(lo_ref, 0), view(lo_ref, half), view(lo_ref, 2 * half), view(hi_ref, 0),
                   view(hi_ref, half), view(hi_ref, 2 * half),
                   dt_ref.at[s], cst_ref.at[s], cw_ref, cb_ref, dtb_ref, alog_ref, dskip_ref, nw_ref,
                   sel_ref, shift_ref, y_ref.at[s], ext_scr.at[s], st_scr.at[s], L)


def _ssd_final(c, ins, outs, scr):
    _, ncv_ref, nst_ref = outs
    ext_scr, st_scr = scr
    ns, L = ext_scr.shape[0], ext_scr.shape[1] // 2
    for s in range(ns):
        tail = ext_scr[s, L - BF16_ROWS:L, :].astype(F32)
        ncv_ref[s] = tail[BF16_ROWS - (CONV_W - 1):BF16_ROWS, :]
        nst_ref[s] = st_scr[s].T


def _shift_matrix(L):
    r = jnp.arange(3 * L)
    src = L + r % L - (CONV_W - 1) + r // L
    return (jnp.arange(2 * L)[None, :] == src[:, None]).astype(BF16)


def _ssd_part(proj3, dt3, conv_state, ssm_state, conv_w, conv_b, dtb, alog, dskip_all, ssm_nw, sel, L, ns):
    batch, t, _ = proj3.shape
    const = lambda shape: pl.BlockSpec(shape, lambda g, c: (0,) * len(shape))
    per_g = lambda shape: pl.BlockSpec((ns,) + shape, lambda g, c: (g,) + (0,) * len(shape))
    third = lambda k: pl.BlockSpec((ns, L, QKV_W), lambda g, c: (g, c, k))
    hp = N_SSM_HEADS * SSM_HEAD_DIM
    assert COL_XS == QKV_W and PROJ_W == 3 * QKV_W
    return _Part(
        in_specs=[
            third(1), third(2),
            pl.BlockSpec((ns, L, LANES), lambda g, c: (g, c, 0)),
            per_g((CONV_W - 1, CONV_CH)),
            per_g((hp, D_STATE)),
            const((CONV_W, CONV_CH)), const((1, CONV_CH)), const((1, LANES)), const((1, LANES)),
            const((1, SSM_W)), const((1, SSM_W)), const((LANES, SSM_W)), const((3 * L, 2 * L)),
        ],
        out_specs=[
            pl.BlockSpec((ns, L, SSM_W), lambda g, c: (g, c, 0)),
            per_g((CONV_W - 1, CONV_CH)),
            per_g((hp, D_STATE)),
        ],
        out_shapes=[
            jax.ShapeDtypeStruct((batch, t, SSM_W), BF16),
            jax.ShapeDtypeStruct((batch, CONV_W - 1, CONV_CH), F32),
            jax.ShapeDtypeStruct((batch, hp, D_STATE), F32),
        ],
        scratch=[
            pltpu.VMEM((ns, 2 * L, CONV_CH), BF16),
            pltpu.VMEM((ns, D_STATE, hp), F32),
        ],
        operands=[proj3, proj3, dt3, conv_state, ssm_state, conv_w, conv_b, dtb, alog, dskip_all, ssm_nw, sel,
                  _shift_matrix(L)],
        body=_ssd_body, init=_ssd_init, final=_ssd_final)


def _outproj_kernel(*refs, n_cast):
    a_ref, y_ref, wa_ref, wy_ref, x_ref = refs[:5]
    cast_in, o_ref, cast_out = refs[5:5 + n_cast], refs[5 + n_cast], refs[6 + n_cast:]
    o_ref[...] = x_ref[...] + _dot(a_ref[...], wa_ref[...]) + _dot(y_ref[...], wy_ref[...])
    _cast_blocks(cast_in, cast_out)


def _outproj(attn, y, w_out, x2d, tm, tn, cast=()):
    t = x2d.shape[0]
    half = D_MODEL // 2
    ni = t // tm
    c_in, c_out, c_shape = _cast_riders(cast, (D_MODEL // tn) * ni, lambda j, i: j * ni + i)
    return pl.pallas_call(
        functools.partial(_outproj_kernel, n_cast=len(cast)),
        grid=(D_MODEL // tn, ni),
        in_specs=[
            pl.BlockSpec((tm, half), lambda j, i: (i, 0)),
            pl.BlockSpec((tm, half), lambda j, i: (i, 0)),
            pl.BlockSpec((half, tn), lambda j, i: (0, j)),
            pl.BlockSpec((half, tn), lambda j, i: (1, j)),
            pl.BlockSpec((tm, tn), lambda j, i: (i, j)),
        ] + c_in,
        out_specs=[pl.BlockSpec((tm, tn), lambda j, i: (i, j))] + c_out,
        out_shape=[jax.ShapeDtypeStruct((t, D_MODEL), F32)] + c_shape,
        compiler_params=_params("arbitrary", "arbitrary"),
        name="outproj",
    )(attn, y, w_out, w_out, x2d, *cast)


def _ffn_kernel(x_ref, nw_ref, wu_ref, wd_ref, o_ref, h_scr, *, n_split):
    @pl.when(pl.program_id(1) == 0)
    def _():
        _rmsnorm_rows(x_ref, nw_ref, h_scr, copy_ref=o_ref)

    u = jnp.maximum(_dot(h_scr[...], wu_ref[...]), 0.0)
    u = (u * u).astype(BF16)
    wn = D_MODEL // n_split
    for n in range(n_split):
        o_ref[:, wn * n:wn * (n + 1)] += _dot(u, wd_ref[:, wn * n:wn * (n + 1)])


def _ffn(x2d, norm_w, w_up, w_down, tm, tf):
    t = x2d.shape[0]
    return pl.pallas_call(
        functools.partial(_ffn_kernel, n_split=4),
        grid=(t // tm, FFN_HIDDEN // tf),
        in_specs=[
            pl.BlockSpec((tm, D_MODEL), lambda i, f: (i, 0)),
            pl.BlockSpec((1, D_MODEL), lambda i, f: (0, 0)),
            pl.BlockSpec((D_MODEL, tf), lambda i, f: (0, f)),
            pl.BlockSpec((tf, D_MODEL), lambda i, f: (f, 0)),
        ],
        out_specs=pl.BlockSpec((tm, D_MODEL), lambda i, f: (i, 0)),
        out_shape=jax.ShapeDtypeStruct((t, D_MODEL), F32),
        scratch_shapes=[pltpu.VMEM((tm, D_MODEL), BF16)],
        compiler_params=_params("arbitrary", "arbitrary"),
        name="ffn",
    )(x2d, norm_w, w_up, w_down)


def _rope_tables(pos):
    inv_freq = ROPE_THETA ** (-jnp.arange(ROPE_HALF, dtype=F32) / ROPE_HALF)
    lane = jnp.arange(LANES) % HEAD_DIM
    freq = jnp.where(lane < ROPE_DIM, inv_freq[lane % ROPE_HALF], 0.0)
    ang = pos.astype(F32)[:, None] * freq[None, :]
    cos, sin = jnp.cos(ang), jnp.sin(ang)
    return [cos, jnp.where(lane < ROPE_HALF, -sin, 0.0), jnp.where(lane >= ROPE_HALF, sin, 0.0)]


def _layer_params(norm_mix_w, w_in, q_norm_w, k_norm_w, attn_sinks, conv_w, conv_b, dt_bias, a_log, d_skip,
                  ssm_norm_w, w_out, norm_ffn_w, w_up, w_down):
    pad_h = lambda a: jnp.pad(a.astype(F32), (0, LANES - N_SSM_HEADS))[None, :]
    sel = (jnp.arange(LANES)[:, None] == (jnp.arange(SSM_W) // SSM_HEAD_DIM)[None, :]).astype(BF16)
    return dict(
        norm_mix=norm_mix_w.astype(F32)[None, :],
        qw=jnp.tile(q_norm_w.astype(F32) * (ATTN_SCALE * LOG2E), 2)[None, :],
        kw=jnp.tile(k_norm_w.astype(F32), 2)[None, :],
        sinks=jnp.repeat(attn_sinks.astype(F32) * LOG2E, HEAD_DIM).reshape(Q_TILES, LANES),
        conv_w=conv_w.astype(F32), conv_b=conv_b.astype(F32)[None, :],
        dtb=pad_h(dt_bias), alog=pad_h(a_log),
        dskip=jnp.repeat(d_skip.astype(F32), SSM_HEAD_DIM)[None, :],
        ssm_nw=ssm_norm_w.astype(F32)[None, :], sel=sel,
        norm_ffn=norm_ffn_w.astype(F32)[None, :],
    )


def _stream_back(x, proj, dt_raw, pos, cache, conv_state, ssm_state, p, tm, w_out, w_up, w_down):
    batch, t, _ = x.shape
    ns = max(n for n in STREAMS_PER_STEP if batch % n == 0)
    x2d = x.reshape(batch * t, D_MODEL)
    proj3 = proj.reshape(batch, t, PROJ_W)
    tabs = _rope_tables(pos)
    if cache is None:
        attn_part = _attn_prompt_part(proj3, tabs, p["qw"], p["kw"], p["sinks"], ns)
        L = CHUNK
    else:
        past_k, past_v = cache
        attn_part = _attn_sample_part(proj3, past_k.reshape(batch, WINDOW, KV_W),
                                      past_v.reshape(batch, WINDOW, KV_W), tabs, p["qw"], p["kw"], p["sinks"], ns)
        L = min(CHUNK, t)
    hp = N_SSM_HEADS * SSM_HEAD_DIM
    ssd_part = _ssd_part(proj3, dt_raw.reshape(batch, t, LANES), conv_state,
                         ssm_state.reshape(batch, hp, D_STATE), p["conv_w"], p["conv_b"], p["dtb"], p["alog"],
                         p["dskip"], p["ssm_nw"], p["sel"], L, ns)
    (attn, new_k, new_v), (y, new_conv, new_ssm) = _fused_call([attn_part, ssd_part], (batch // ns, t // L),
                                                              "mixers")
    late = () if w_down.dtype == BF16 else (w_down,)
    x1, *cast = _outproj(attn.reshape(batch * t, ATTN_W), y.reshape(batch * t, SSM_W), w_out, x2d, tm, 1024,
                         cast=late)
    if late:
        w_down = cast.pop()
    out = _ffn(x1, p["norm_ffn"], w_up, w_down, tm, 512)
    return (out.reshape(batch, t, D_MODEL),
            new_k.reshape(batch, WINDOW, N_KV_HEADS, HEAD_DIM), new_v.reshape(batch, WINDOW, N_KV_HEADS, HEAD_DIM),
            new_conv, new_ssm.reshape(batch, N_SSM_HEADS, SSM_HEAD_DIM, D_STATE), w_down)


def kernel(x_prompt, x_sample, cache_k, cache_v, state_conv, state_ssm, norm_mix_w, w_in, q_norm_w, k_norm_w,
           attn_sinks, conv_w, conv_b, dt_bias, a_log, d_skip, ssm_norm_w, w_out, norm_ffn_w, w_up, w_down):
    depth = w_in.shape[0]
    b_p, t_p, _ = x_prompt.shape
    b_s, t_s, _ = x_sample.shape
    assert cache_k.shape[2] == WINDOW and t_p % CHUNK == 0 and t_p >= WINDOW
    assert t_s <= CHUNK and t_s % BF16_ROWS == 0
    pos_p = jnp.arange(t_p, dtype=jnp.int32)
    pos_s = PAST_LEN + jnp.arange(t_s, dtype=jnp.int32)
    hp, hs = x_prompt, x_sample
    outs_p, outs_s = [], []
    for layer in range(depth):
        p = _layer_params(norm_mix_w[layer], w_in[layer], q_norm_w[layer], k_norm_w[layer], attn_sinks[layer],
                          conv_w[layer], conv_b[layer], dt_bias[layer], a_log[layer], d_skip[layer],
                          ssm_norm_w[layer], w_out[layer], norm_ffn_w[layer], w_up[layer], w_down[layer])
        conv0 = jnp.zeros((b_p, CONV_W - 1, CONV_CH), F32)
        ssm0 = jnp.zeros((b_p, N_SSM_HEADS, SSM_HEAD_DIM, D_STATE), F32)
        tm = 512
        xs2d, xp2d = hs.reshape(b_s * t_s, D_MODEL), hp.reshape(b_p * t_p, D_MODEL)
        w_in_t = w_in[layer].T.astype(BF16)
        proj_p, dt_p, w_out16, w_up16 = _inproj(xp2d, p["norm_mix"], w_in_t, tm, 1536,
                                                cast=(w_out[layer], w_up[layer]))
        proj_s, dt_s = _inproj(xs2d, p["norm_mix"], w_in_t, tm, 1536)
        hp, *rest_p, w_down16 = _stream_back(hp, proj_p, dt_p, pos_p, None, conv0, ssm0, p, tm,
                                             w_out16, w_up16, w_down[layer])
        hs, *rest_s, _ = _stream_back(hs, proj_s, dt_s, pos_s, (cache_k[layer], cache_v[layer]),
                                      state_conv[layer], state_ssm[layer], p, tm, w_out16, w_up16, w_down16)
        outs_p.append(rest_p)
        outs_s.append(rest_s)
    stack = lambda outs, i: jnp.stack([o[i] for o in outs])
    return (hp, hs, stack(outs_p, 0), stack(outs_p, 1), stack(outs_p, 2), stack(outs_p, 3),
            stack(outs_s, 0), stack(outs_s, 1), stack(outs_s, 2), stack(outs_s, 3))
```

```python
import functools

import jax
import jax.numpy as jnp
from jax import lax
from jax.experimental import pallas as pl
from jax.experimental.pallas import tpu as pltpu

F32 = jnp.float32
BF16 = jnp.bfloat16

D_MODEL = 4096
HEAD_DIM = 64
N_Q_HEADS = 32
N_KV_HEADS = 8
ATTN_W = N_Q_HEADS * HEAD_DIM
KV_W = N_KV_HEADS * HEAD_DIM
ROPE_DIM = 16
ROPE_HALF = ROPE_DIM // 2
ROPE_THETA = 500000.0
ATTN_SCALE = HEAD_DIM ** -0.5
WINDOW = 128
CHUNK = 64
PAST_LEN = 1024
SSM_W = 2048
N_SSM_HEADS = 32
SSM_HEAD_DIM = 64
N_GROUPS = 8
D_STATE = 128
BC_W = N_GROUPS * D_STATE
GROUP_W = SSM_W // N_GROUPS
CONV_W = 4
CONV_CH = SSM_W + 2 * BC_W
FFN_HIDDEN = 4 * D_MODEL
RMS_EPS = 1e-6
NEG_INF = -1e30
LANES = 128
BF16_ROWS = 16
Q_TILES = ATTN_W // LANES
KV_TILES = KV_W // LANES
PROJ_W = ATTN_W + 2 * KV_W + 2 * SSM_W + 2 * BC_W
COL_Q, COL_K, COL_V, COL_XS, COL_Z, COL_B, COL_C = 0, 2048, 2560, 3072, 5120, 7168, 8192
QKV_W = ATTN_W + 2 * KV_W
VMEM_LIMIT = 62 * 1024 * 1024
STREAMS_PER_STEP = (1, 2, 4)
TILE_M = 512
INPROJ_TN = 1536
OUTPROJ_TN = 1024
FFN_TF = 512


def _params(*sem):
    return pltpu.CompilerParams(dimension_semantics=sem, vmem_limit_bytes=VMEM_LIMIT)


def _dot(a, b):
    return jnp.dot(a, b, preferred_element_type=F32)


def _dot_nt(a, b):
    return lax.dot_general(a, b, (((1,), (1,)), ((), ())), preferred_element_type=F32)


def _split3(x):
    a = x.astype(BF16)
    r = x - a.astype(F32)
    b = r.astype(BF16)
    c = (r - b.astype(F32)).astype(BF16)
    return a, b, c


def _silu(x):
    return x / (1.0 + jnp.exp(-x))


NORM_ROWS = 16
NORM_UNROLL = 8


def _rmsnorm_rows(x_ref, nw_ref, h_scr, copy_ref=None):
    def body(i, carry):
        r = pl.ds(pl.multiple_of(i * NORM_ROWS, NORM_ROWS), NORM_ROWS)
        x = x_ref[r, :]
        ms = jnp.mean(x * x, axis=-1, keepdims=True)
        h_scr[r, :] = (x * lax.rsqrt(ms + RMS_EPS) * nw_ref[...]).astype(BF16)
        if copy_ref is not None:
            copy_ref[r, :] = x
        return carry

    lax.fori_loop(0, x_ref.shape[0] // NORM_ROWS, body, 0, unroll=NORM_UNROLL)


def _cast_riders(arrays, steps, linear_step):
    in_specs, out_specs, out_shapes = [], [], []
    for a in arrays:
        rows, cols = a.shape
        nb = max(n for n in range(1, steps + 1) if rows % n == 0 and (rows // n) % BF16_ROWS == 0)
        spec = pl.BlockSpec((rows // nb, cols), lambda *g, nb=nb: (jnp.minimum(linear_step(*g), nb - 1), 0))
        in_specs.append(spec)
        out_specs.append(spec)
        out_shapes.append(jax.ShapeDtypeStruct(a.shape, BF16))
    return in_specs, out_specs, out_shapes


def _cast_blocks(src_refs, dst_refs):
    for src, dst in zip(src_refs, dst_refs):
        dst[...] = src[...].astype(BF16)


def _inproj_kernel(*refs, n_cast):
    x_ref, nw_ref, w_ref, wdt_ref = refs[:4]
    cast_in = refs[4:4 + n_cast]
    proj_ref, dt_ref = refs[4 + n_cast:6 + n_cast]
    cast_out, h_scr = refs[6 + n_cast:6 + 2 * n_cast], refs[6 + 2 * n_cast]

    @pl.when(pl.program_id(1) == 0)
    def _():
        _rmsnorm_rows(x_ref, nw_ref, h_scr)
        real = lax.broadcasted_iota(jnp.int32, wdt_ref.shape, 0) < N_SSM_HEADS
        dt_ref[...] = _dot_nt(h_scr[...], jnp.where(real, wdt_ref[...], jnp.zeros((), BF16)))

    proj_ref[...] = _dot_nt(h_scr[...], w_ref[...]).astype(BF16)
    _cast_blocks(cast_in, cast_out)


def _inproj(x2d, norm_w, w_in_t, tm, tn, cast=()):
    t = x2d.shape[0]
    nj = PROJ_W // tn
    c_in, c_out, c_shape = _cast_riders(cast, (t // tm) * nj, lambda i, j: i * nj + j)
    return pl.pallas_call(
        functools.partial(_inproj_kernel, n_cast=len(cast)),
        grid=(t // tm, nj),
        in_specs=[
            pl.BlockSpec((tm, D_MODEL), lambda i, j: (i, 0)),
            pl.BlockSpec((1, D_MODEL), lambda i, j: (0, 0)),
            pl.BlockSpec((tn, D_MODEL), lambda i, j: (j, 0)),
            pl.BlockSpec((LANES, D_MODEL), lambda i, j: (PROJ_W // LANES, 0)),
        ] + c_in,
        out_specs=[
            pl.BlockSpec((tm, tn), lambda i, j: (i, j)),
            pl.BlockSpec((tm, LANES), lambda i, j: (i, 0)),
        ] + c_out,
        out_shape=[
            jax.ShapeDtypeStruct((t, PROJ_W), BF16),
            jax.ShapeDtypeStruct((t, LANES), F32),
        ] + c_shape,
        scratch_shapes=[pltpu.VMEM((tm, D_MODEL), BF16)],
        compiler_params=_params("arbitrary", "arbitrary"),
        name="inproj",
    )(x2d, norm_w, w_in_t, w_in_t, *cast)


def _half_ones():
    r = lax.broadcasted_iota(jnp.int32, (LANES, LANES), 0) // HEAD_DIM
    c = lax.broadcasted_iota(jnp.int32, (LANES, LANES), 1) // HEAD_DIM
    return (r == c).astype(BF16)


def _head_norm(x, w_tile):
    ss = _dot((x * x).astype(BF16), _half_ones())
    return x * lax.rsqrt(ss * (1.0 / HEAD_DIM) + RMS_EPS) * w_tile


def _rope(x, cos_t, sin_lo, sin_hi):
    return (x * cos_t + pltpu.roll(x, LANES - ROPE_HALF, 1) * sin_lo
            + pltpu.roll(x, ROPE_HALF, 1) * sin_hi)


def _lane_is_a(shape):
    return lax.broadcasted_iota(jnp.int32, shape, 1) < HEAD_DIM


def _kv_tiles(k_win, v_win):
    is_a = _lane_is_a((CHUNK, LANES))
    ones_a = jnp.where(is_a, 1.0, 0.0).astype(F32)
    ones_b = 1.0 - ones_a
    kks, vvs = [], []
    for h in range(N_KV_HEADS):
        j, odd = h // 2, h % 2
        kt = k_win[:, LANES * j:LANES * (j + 1)]
        vt = v_win[:, LANES * j:LANES * (j + 1)]
        keep = jnp.logical_not(is_a) if odd else is_a
        k_own = jnp.where(keep, kt, 0.0)
        v_own = jnp.where(keep, vt, 0.0)
        k_sw = pltpu.roll(k_own, HEAD_DIM, 1)
        v_sw = pltpu.roll(v_own, HEAD_DIM, 1)
        k_a, k_b = (k_sw, k_own) if odd else (k_own, k_sw)
        v_a, v_b = (v_sw, v_own) if odd else (v_own, v_sw)
        kks.append(jnp.concatenate([k_a, k_b], axis=0).astype(BF16))
        vv = jnp.concatenate([jnp.concatenate([v_a, ones_a], axis=1),
                              jnp.concatenate([v_b, ones_b], axis=1)], axis=0)
        vvs.append(vv.astype(BF16))
    return kks, vvs


def _prep_q(q_ref, qw_ref, cos_ref, slo_ref, shi_ref, tq):
    q2 = jnp.concatenate([q_ref[:, LANES * i:LANES * (i + 1)].astype(F32) for i in range(Q_TILES)], axis=0)
    q2 = _head_norm(q2, qw_ref[...])
    cos_t, slo, shi = cos_ref[...], slo_ref[...], shi_ref[...]
    return [(_rope(q2[tq * i:tq * (i + 1)], cos_t, slo, shi) * ATTN_SCALE).astype(BF16) for i in range(Q_TILES)]


def _attend(q_tiles, kk_of, vv_of, valid_of, sink_ref, o_ref, tq):
    is_a = _lane_is_a((2 * tq, LANES))
    for h in range(N_KV_HEADS):
        qh = jnp.concatenate([q_tiles[2 * h], q_tiles[2 * h + 1]], axis=0)
        s = _dot_nt(qh, kk_of(h))
        sw = [jnp.where(valid_of(w), s[:, LANES * w:LANES * (w + 1)], NEG_INF) for w in range(3)]
        mt = jnp.maximum(jnp.maximum(sw[0], sw[1]), sw[2])
        m_a = jnp.max(jnp.where(is_a, mt, NEG_INF), axis=-1, keepdims=True)
        m_b = jnp.max(jnp.where(is_a, NEG_INF, mt), axis=-1, keepdims=True)
        sink = jnp.concatenate([jnp.broadcast_to(sink_ref[2 * h:2 * h + 1, :], (tq, LANES)),
                                jnp.broadcast_to(sink_ref[2 * h + 1:2 * h + 2, :], (tq, LANES))], axis=0)
        m = jnp.maximum(jnp.where(is_a, m_a, m_b), sink)
        e = jnp.concatenate([jnp.exp(x - m) for x in sw], axis=1).astype(BF16)
        oa = _dot(e, vv_of(h))
        o = oa[:, :LANES] / (oa[:, LANES:] + jnp.exp(sink - m))
        o_ref[:, LANES * 2 * h:LANES * (2 * h + 1)] = o[:tq].astype(o_ref.dtype)
        o_ref[:, LANES * (2 * h + 1):LANES * (2 * h + 2)] = o[tq:].astype(o_ref.dtype)


def _prep_k(k_raw, kw_ref, cos_ref, slo_ref, shi_ref, rows):
    k2 = jnp.concatenate([k_raw[:, LANES * j:LANES * (j + 1)].astype(F32) for j in range(KV_TILES)], axis=0)
    k2 = _head_norm(k2, kw_ref[...])
    cos_t, slo, shi = cos_ref[...], slo_ref[...], shi_ref[...]
    return jnp.concatenate([_rope(k2[rows * j:rows * (j + 1)], cos_t, slo, shi) for j in range(KV_TILES)], axis=1)


class _Part:
    def __init__(self, in_specs, operands, out_specs, out_shapes, scratch, body, init=None, final=None):
        self.in_specs, self.operands, self.out_specs, self.out_shapes = in_specs, operands, out_specs, out_shapes
        self.scratch, self.body, self.init, self.final = scratch, body, init, final


def _fused_kernel(*refs, parts):
    n_in = [len(p.in_specs) for p in parts]
    n_out = [len(p.out_specs) for p in parts]
    n_scr = [len(p.scratch) for p in parts]
    pos = 0
    groups = []
    for counts in (n_in, n_out, n_scr):
        groups.append([])
        for n in counts:
            groups[-1].append(refs[pos:pos + n])
            pos += n
    args = [(p, groups[0][i], groups[1][i], groups[2][i]) for i, p in enumerate(parts)]
    c = pl.program_id(1)

    @pl.when(c == 0)
    def _():
        for p, ins, outs, scr in args:
            if p.init is not None:
                p.init(c, ins, outs, scr)

    for p, ins, outs, scr in args:
        p.body(c, ins, outs, scr)

    @pl.when(c == pl.num_programs(1) - 1)
    def _():
        for p, ins, outs, scr in args:
            if p.final is not None:
                p.final(c, ins, outs, scr)


def _fused_call(parts, grid, name):
    outs = pl.pallas_call(
        functools.partial(_fused_kernel, parts=parts),
        grid=grid,
        in_specs=[s for p in parts for s in p.in_specs],
        out_specs=[s for p in parts for s in p.out_specs],
        out_shape=[s for p in parts for s in p.out_shapes],
        scratch_shapes=[s for p in parts for s in p.scratch],
        compiler_params=_params("arbitrary", "arbitrary"),
        name=name,
    )(*[o for p in parts for o in p.operands])
    split, pos = [], 0
    for p in parts:
        split.append(outs[pos:pos + len(p.out_specs)])
        pos += len(p.out_specs)
    return split


def _attn_prompt_init(c, ins, outs, scr):
    for ring in scr:
        ring[...] = jnp.zeros_like(ring)


def _attn_prompt_body(c, ins, outs, scr):
    qkv_ref, cos_ref, slo_ref, shi_ref, qw_ref, kw_ref, sink_ref = ins
    o_ref, nk_ref, nv_ref = outs
    kk_scr, vv_scr = scr
    slot = c % 3
    for s in range(qkv_ref.shape[0]):
        q_ref = qkv_ref.at[s, :, pl.ds(COL_Q, ATTN_W)]
        k_fin = _prep_k(qkv_ref[s, :, COL_K:COL_K + KV_W], kw_ref, cos_ref, slo_ref, shi_ref, CHUNK)
        v_cur = qkv_ref[s, :, COL_V:COL_V + KV_W].astype(F32)
        nk_ref[s] = k_fin
        nv_ref[s] = v_cur
        kks, vvs = _kv_tiles(k_fin, v_cur)
        for h in range(N_KV_HEADS):
            kk_scr[s, slot, h] = kks[h]
            vv_scr[s, slot, h] = vvs[h]

        q_tiles = _prep_q(q_ref, qw_ref, cos_ref, slo_ref, shi_ref, CHUNK)
        _attend(q_tiles,
                lambda h, s=s: jnp.concatenate([kk_scr[s, w, h] for w in range(3)], axis=0),
                lambda h, s=s: jnp.concatenate([vv_scr[s, w, h] for w in range(3)], axis=0),
                lambda w: (c - w + 3) % 3 <= c,
                sink_ref, o_ref.at[s], CHUNK)


def _attn_prompt_part(proj3, rope_tabs, qw_tile, kw_tile, sink_tab, ns):
    batch, t, _ = proj3.shape
    nc = t // CHUNK
    keep = WINDOW // CHUNK
    const = lambda shape: pl.BlockSpec(shape, lambda g, c: (0,) * len(shape))
    keep_spec = pl.BlockSpec((ns, CHUNK, KV_W), lambda g, c: (g, jnp.maximum(c - (nc - keep), 0), 0))
    return _Part(
        in_specs=[
            pl.BlockSpec((ns, CHUNK, QKV_W), lambda g, c: (g, c, 0)),
            *[pl.BlockSpec((CHUNK, LANES), lambda g, c: (c, 0))] * 3,
            const((1, LANES)), const((1, LANES)), const((Q_TILES, LANES)),
        ],
        operands=[proj3, *rope_tabs, qw_tile, kw_tile, sink_tab],
        out_specs=[pl.BlockSpec((ns, CHUNK, ATTN_W), lambda g, c: (g, c, 0)), keep_spec, keep_spec],
        out_shapes=[
            jax.ShapeDtypeStruct((batch, t, ATTN_W), BF16),
            jax.ShapeDtypeStruct((batch, WINDOW, KV_W), F32),
            jax.ShapeDtypeStruct((batch, WINDOW, KV_W), F32),
        ],
        scratch=[
            pltpu.VMEM((ns, 3, N_KV_HEADS, 2 * CHUNK, LANES), BF16),
            pltpu.VMEM((ns, 3, N_KV_HEADS, 2 * CHUNK, 2 * LANES), BF16),
        ],
        body=_attn_prompt_body, init=_attn_prompt_init)


def _attn_sample_body(c, ins, outs, scr):
    qkv_ref, ck_ref, cv_ref, cos_ref, slo_ref, shi_ref, qw_ref, kw_ref, sink_ref = ins
    o_ref, nk_ref, nv_ref = outs
    tq = qkv_ref.shape[1]
    new_valid = lax.broadcasted_iota(jnp.int32, (2 * tq, LANES), 1) % HEAD_DIM < tq
    pad = jnp.zeros((CHUNK - tq, KV_W), F32)
    for s in range(qkv_ref.shape[0]):
        q_ref = qkv_ref.at[s, :, pl.ds(COL_Q, ATTN_W)]
        k_fin = _prep_k(qkv_ref[s, :, COL_K:COL_K + KV_W], kw_ref, cos_ref, slo_ref, shi_ref, tq)
        v_cur = qkv_ref[s, :, COL_V:COL_V + KV_W].astype(F32)
        nk_ref[s, 0:WINDOW - tq, :] = ck_ref[s, tq:WINDOW, :]
        nv_ref[s, 0:WINDOW - tq, :] = cv_ref[s, tq:WINDOW, :]
        nk_ref[s, WINDOW - tq:WINDOW, :] = k_fin
        nv_ref[s, WINDOW - tq:WINDOW, :] = v_cur
        wins = [(ck_ref[s, :CHUNK, :], cv_ref[s, :CHUNK, :]),
                (ck_ref[s, CHUNK:, :], cv_ref[s, CHUNK:, :]),
                (jnp.concatenate([k_fin, pad], axis=0), jnp.concatenate([v_cur, pad], axis=0))]
        tiles = [_kv_tiles(kw, vw) for kw, vw in wins]
        q_tiles = _prep_q(q_ref, qw_ref, cos_ref, slo_ref, shi_ref, tq)
        _attend(q_tiles,
                lambda h, tiles=tiles: jnp.concatenate([tiles[w][0][h] for w in range(3)], axis=0),
                lambda h, tiles=tiles: jnp.concatenate([tiles[w][1][h] for w in range(3)], axis=0),
                lambda w: new_valid if w == 2 else True,
                sink_ref, o_ref.at[s], tq)


def _attn_sample_part(proj3, cache_k, cache_v, rope_tabs, qw_tile, kw_tile, sink_tab, ns):
    batch, tq, _ = proj3.shape
    const = lambda shape: pl.BlockSpec(shape, lambda g, c: (0,) * len(shape))
    cache_spec = pl.BlockSpec((ns, WINDOW, KV_W), lambda g, c: (g, 0, 0))
    return _Part(
        in_specs=[
            pl.BlockSpec((ns, tq, QKV_W), lambda g, c: (g, 0, 0)),
            cache_spec, cache_spec,
            *[pl.BlockSpec((tq, LANES), lambda g, c: (0, 0))] * 3,
            const((1, LANES)), const((1, LANES)), const((Q_TILES, LANES)),
        ],
        operands=[proj3, cache_k, cache_v, *rope_tabs, qw_tile, kw_tile, sink_tab],
        out_specs=[pl.BlockSpec((ns, tq, ATTN_W), lambda g, c: (g, 0, 0)), cache_spec, cache_spec],
        out_shapes=[
            jax.ShapeDtypeStruct((batch, tq, ATTN_W), BF16),
            jax.ShapeDtypeStruct((batch, WINDOW, KV_W), F32),
            jax.ShapeDtypeStruct((batch, WINDOW, KV_W), F32),
        ],
        scratch=[], body=_attn_sample_body)


def _pad_rows(a, rows):
    if a.shape[0] == rows:
        return a
    return jnp.concatenate([a, jnp.zeros((rows - a.shape[0], a.shape[1]), a.dtype)], axis=0)


def _ssd_chunk(c, xs0_ref, xs1_ref, z0_ref, z1_ref, b_ref, c_ref, dt_ref, cst_ref, cw_ref, cb_ref, dtb_ref,
               alog_ref, dskip_ref, nw_ref, sel_ref, shift_ref, y_ref, ext_scr, st_scr, L):
    half = SSM_W // 2
    ext_scr[L:2 * L, 0:half] = xs0_ref[...]
    ext_scr[L:2 * L, half:SSM_W] = xs1_ref[...]
    ext_scr[L:2 * L, SSM_W:SSM_W + BC_W] = b_ref[...]
    ext_scr[L:2 * L, SSM_W + BC_W:CONV_CH] = c_ref[...]
    delayed = _dot(shift_ref[...], ext_scr[...])
    cur = ext_scr[L:2 * L, :].astype(F32)
    conv = cb_ref[...] + cur * cw_ref[CONV_W - 1:CONV_W, :]
    for i in range(CONV_W - 1):
        conv = conv + delayed[L * i:L * (i + 1)] * cw_ref[i:i + 1, :]
    s = cst_ref[...]
    w0, w1, w2 = cw_ref[0:1, :], cw_ref[1:2, :], cw_ref[2:3, :]
    head = jnp.concatenate([w0 * s[0:1] + w1 * s[1:2] + w2 * s[2:3], w0 * s[1:2] + w1 * s[2:3], w0 * s[2:3],
                            jnp.zeros((8 - (CONV_W - 1), CONV_CH), F32)], axis=0)
    conv = jnp.concatenate([conv[0:8] + jnp.where(c == 0, head, 0.0), conv[8:]], axis=0)
    conv = _silu(conv)
    xs = conv[:, :SSM_W]
    ext_scr[0:L, :] = ext_scr[L:2 * L, :]

    x_dt = dt_ref[...] + dtb_ref[...]
    dt = jnp.maximum(x_dt, 0.0) + jnp.log1p(jnp.exp(-jnp.abs(x_dt)))
    ad = dt * (-jnp.exp(alog_ref[...]))
    li = lax.broadcasted_iota(jnp.int32, (L, L), 0)
    si = lax.broadcasted_iota(jnp.int32, (L, L), 1)
    tril = (si <= li).astype(BF16)
    a_cum = _dot(jnp.concatenate([tril] * 3, axis=1), jnp.concatenate(_split3(ad), axis=0))
    sel = sel_ref[...]
    ex = _dot(jnp.concatenate(_split3(a_cum) + _split3(dt), axis=0), sel)
    col = ex[0:L] + ex[L:2 * L] + ex[2 * L:3 * L]
    dt_all = ex[3 * L:4 * L] + ex[4 * L:5 * L] + ex[5 * L:6 * L]
    lane_s = lax.broadcasted_iota(jnp.int32, (L, SSM_W), 1) % SSM_HEAD_DIM
    row_l = lax.broadcasted_iota(jnp.int32, (L, SSM_W), 0)
    row = jnp.sum(jnp.where(lane_s == row_l, col, 0.0), axis=0, keepdims=True)
    lmat = jnp.exp(jnp.where(lane_s <= row_l, col - row, NEG_INF))
    a_last = col[L - 1:L, :]
    exp_a = jnp.exp(col)
    decay = jnp.exp(a_last - col)
    exp_last = jnp.exp(a_last)

    xd = xs * dt_all
    xdd = xd * decay
    bi = lax.broadcasted_iota(jnp.int32, (GROUP_W, GROUP_W), 0) // SSM_HEAD_DIM
    bj = lax.broadcasted_iota(jnp.int32, (GROUP_W, GROUP_W), 1) // SSM_HEAD_DIM
    blockdiag = bi == bj

    for g in range(N_GROUPS):
        gs = slice(GROUP_W * g, GROUP_W * (g + 1))
        b_g = conv[:, SSM_W + D_STATE * g:SSM_W + D_STATE * (g + 1)]
        c_g = conv[:, SSM_W + BC_W + D_STATE * g:SSM_W + BC_W + D_STATE * (g + 1)].astype(BF16)
        b_pad = _pad_rows(b_g, CHUNK)
        cb = _dot_nt(c_g, jnp.concatenate([b_pad.astype(BF16)] * 4, axis=0))
        m_g = (cb * lmat[:, gs]).astype(BF16)
        xd_g = _pad_rows(xd[:, gs], CHUNK).astype(BF16)
        xd_bd = jnp.where(blockdiag, jnp.concatenate([xd_g] * 4, axis=0), jnp.zeros((), BF16))
        y_diag = _dot(m_g, xd_bd)
        st_g = st_scr[:, gs]
        y_off = _dot(c_g, st_g.astype(BF16)) * exp_a[:, gs]
        new_st = _dot(b_pad.T.astype(BF16), _pad_rows(xdd[:, gs], CHUNK).astype(BF16))
        st_scr[:, gs] = st_g * exp_last[:, gs] + new_st

        y = y_diag + y_off + xs[:, gs] * dskip_ref[:, gs]
        z_ref = z0_ref if g < N_GROUPS // 2 else z1_ref
        zs = slice(GROUP_W * (g % (N_GROUPS // 2)), GROUP_W * (g % (N_GROUPS // 2) + 1))
        gated = y * _silu(z_ref[:, zs].astype(F32))
        ms = jnp.mean(gated * gated, axis=-1, keepdims=True)
        y_ref[:, gs] = (gated * lax.rsqrt(ms + RMS_EPS) * nw_ref[:, gs]).astype(y_ref.dtype)


def _ssd_init(c, ins, outs, scr):
    st0_ref = ins[4]
    ext_scr, st_scr = scr
    ns, L = ext_scr.shape[0], ext_scr.shape[1] // 2
    for s in range(ns):
        ext_scr[s, 0:L, :] = jnp.zeros((L, CONV_CH), BF16)
        st_scr[s] = st0_ref[s].T


def _ssd_body(c, ins, outs, scr):
    (lo_ref, hi_ref, dt_ref, cst_ref, _, cw_ref, cb_ref, dtb_ref, alog_ref, dskip_ref, nw_ref, sel_ref,
     shift_ref) = ins
    y_ref = outs[0]
    ext_scr, st_scr = scr
    ns, L = dt_ref.shape[0], dt_ref.shape[1]
    half = SSM_W // 2
    for s in range(ns):
        view = lambda ref, start, s=s: ref.at[s, :, pl.ds(start, half)]
        _ssd_chunk(c, view(lo_ref, 0), view(lo_ref, half), view(lo_ref, 2 * half), view(hi_ref, 0),
                   view(hi_ref, half), view(hi_ref, 2 * half),
                   dt_ref.at[s], cst_ref.at[s], cw_ref, cb_ref, dtb_ref, alog_ref, dskip_ref, nw_ref,
                   sel_ref, shift_ref, y_ref.at[s], ext_scr.at[s], st_scr.at[s], L)


def _ssd_final(c, ins, outs, scr):
    _, ncv_ref, nst_ref = outs
    ext_scr, st_scr = scr
    ns, L = ext_scr.shape[0], ext_scr.shape[1] // 2
    for s in range(ns):
        tail = ext_scr[s, L - BF16_ROWS:L, :].astype(F32)
        ncv_ref[s] = tail[BF16_ROWS - (CONV_W - 1):BF16_ROWS, :]
        nst_ref[s] = st_scr[s].T


def _shift_matrix(L):
    r = jnp.arange(3 * L)
    src = L + r % L - (CONV_W - 1) + r // L
    return (jnp.arange(2 * L)[None, :] == src[:, None]).astype(BF16)


def _ssd_part(proj3, dt3, conv_state, ssm_state, conv_w, conv_b, dtb, alog, dskip_all, ssm_nw, sel, L, ns):
    batch, t, _ = proj3.shape
    const = lambda shape: pl.BlockSpec(shape, lambda g, c: (0,) * len(shape))
    per_g = lambda shape: pl.BlockSpec((ns,) + shape, lambda g, c: (g,) + (0,) * len(shape))
    third = lambda k: pl.BlockSpec((ns, L, QKV_W), lambda g, c: (g, c, k))
    hp = N_SSM_HEADS * SSM_HEAD_DIM
    assert COL_XS == QKV_W and PROJ_W == 3 * QKV_W
    return _Part(
        in_specs=[
            third(1), third(2),
            pl.BlockSpec((ns, L, LANES), lambda g, c: (g, c, 0)),
            per_g((CONV_W - 1, CONV_CH)),
            per_g((hp, D_STATE)),
            const((CONV_W, CONV_CH)), const((1, CONV_CH)), const((1, LANES)), const((1, LANES)),
            const((1, SSM_W)), const((1, SSM_W)), const((LANES, SSM_W)), const((3 * L, 2 * L)),
        ],
        out_specs=[
            pl.BlockSpec((ns, L, SSM_W), lambda g, c: (g, c, 0)),
            per_g((CONV_W - 1, CONV_CH)),
            per_g((hp, D_STATE)),
        ],
        out_shapes=[
            jax.ShapeDtypeStruct((batch, t, SSM_W), BF16),
            jax.ShapeDtypeStruct((batch, CONV_W - 1, CONV_CH), F32),
            jax.ShapeDtypeStruct((batch, hp, D_STATE), F32),
        ],
        scratch=[
            pltpu.VMEM((ns, 2 * L, CONV_CH), BF16),
            pltpu.VMEM((ns, D_STATE, hp), F32),
        ],
        operands=[proj3, proj3, dt3, conv_state, ssm_state, conv_w, conv_b, dtb, alog, dskip_all, ssm_nw, sel,
                  _shift_matrix(L)],
        body=_ssd_body, init=_ssd_init, final=_ssd_final)


def _outproj_kernel(*refs, n_cast):
    a_ref, y_ref, wa_ref, wy_ref, x_ref = refs[:5]
    cast_in, o_ref, cast_out = refs[5:5 + n_cast], refs[5 + n_cast], refs[6 + n_cast:]
    o_ref[...] = x_ref[...] + _dot(a_ref[...], wa_ref[...]) + _dot(y_ref[...], wy_ref[...])
    _cast_blocks(cast_in, cast_out)


def _outproj(attn, y, w_out, x2d, tm, tn, cast=()):
    t = x2d.shape[0]
    half = D_MODEL // 2
    ni = t // tm
    c_in, c_out, c_shape = _cast_riders(cast, (D_MODEL // tn) * ni, lambda j, i: j * ni + i)
    return pl.pallas_call(
        functools.partial(_outproj_kernel, n_cast=len(cast)),
        grid=(D_MODEL // tn, ni),
        in_specs=[
            pl.BlockSpec((tm, half), lambda j, i: (i, 0)),
            pl.BlockSpec((tm, half), lambda j, i: (i, 0)),
            pl.BlockSpec((half, tn), lambda j, i: (0, j)),
            pl.BlockSpec((half, tn), lambda j, i: (1, j)),
            pl.BlockSpec((tm, tn), lambda j, i: (i, j)),
        ] + c_in,
        out_specs=[pl.BlockSpec((tm, tn), lambda j, i: (i, j))] + c_out,
        out_shape=[jax.ShapeDtypeStruct((t, D_MODEL), F32)] + c_shape,
        compiler_params=_params("arbitrary", "arbitrary"),
        name="outproj",
    )(attn, y, w_out, w_out, x2d, *cast)


def _ffn_kernel(x_ref, nw_ref, wu_ref, wd_ref, o_ref, h_scr, *, n_split):
    @pl.when(pl.program_id(1) == 0)
    def _():
        _rmsnorm_rows(x_ref, nw_ref, h_scr, copy_ref=o_ref)

    u = jnp.maximum(_dot(h_scr[...], wu_ref[...]), 0.0)
    u = (u * u).astype(BF16)
    wn = D_MODEL // n_split
    for n in range(n_split):
        o_ref[:, wn * n:wn * (n + 1)] += _dot(u, wd_ref[:, wn * n:wn * (n + 1)])


def _ffn(x2d, norm_w, w_up, w_down, tm, tf):
    t = x2d.shape[0]
    return pl.pallas_call(
        functools.partial(_ffn_kernel, n_split=4),
        grid=(t // tm, FFN_HIDDEN // tf),
        in_specs=[
            pl.BlockSpec((tm, D_MODEL), lambda i, f: (i, 0)),
            pl.BlockSpec((1, D_MODEL), lambda i, f: (0, 0)),
            pl.BlockSpec((D_MODEL, tf), lambda i, f: (0, f)),
            pl.BlockSpec((tf, D_MODEL), lambda i, f: (f, 0)),
        ],
        out_specs=pl.BlockSpec((tm, D_MODEL), lambda i, f: (i, 0)),
        out_shape=jax.ShapeDtypeStruct((t, D_MODEL), F32),
        scratch_shapes=[pltpu.VMEM((tm, D_MODEL), BF16)],
        compiler_params=_params("arbitrary", "arbitrary"),
        name="ffn",
    )(x2d, norm_w, w_up, w_down)


def _rope_tables(pos):
    inv_freq = ROPE_THETA ** (-jnp.arange(ROPE_HALF, dtype=F32) / ROPE_HALF)
    lane = jnp.arange(LANES) % HEAD_DIM
    freq = jnp.where(lane < ROPE_DIM, inv_freq[lane % ROPE_HALF], 0.0)
    ang = pos.astype(F32)[:, None] * freq[None, :]
    cos, sin = jnp.cos(ang), jnp.sin(ang)
    return [cos, jnp.where(lane < ROPE_HALF, -sin, 0.0), jnp.where(lane >= ROPE_HALF, sin, 0.0)]


def _layer_params(norm_mix_w, w_in, q_norm_w, k_norm_w, attn_sinks, conv_w, conv_b, dt_bias, a_log, d_skip,
                  ssm_norm_w, w_out, norm_ffn_w, w_up, w_down):
    pad_h = lambda a: jnp.pad(a.astype(F32), (0, LANES - N_SSM_HEADS))[None, :]
    sel = (jnp.arange(LANES)[:, None] == (jnp.arange(SSM_W) // SSM_HEAD_DIM)[None, :]).astype(BF16)
    return dict(
        norm_mix=norm_mix_w.astype(F32)[None, :],
        qw=jnp.tile(q_norm_w.astype(F32), 2)[None, :], kw=jnp.tile(k_norm_w.astype(F32), 2)[None, :],
        sinks=jnp.repeat(attn_sinks.astype(F32), HEAD_DIM).reshape(Q_TILES, LANES),
        conv_w=conv_w.astype(F32), conv_b=conv_b.astype(F32)[None, :],
        dtb=pad_h(dt_bias), alog=pad_h(a_log),
        dskip=jnp.repeat(d_skip.astype(F32), SSM_HEAD_DIM)[None, :],
        ssm_nw=ssm_norm_w.astype(F32)[None, :], sel=sel,
        norm_ffn=norm_ffn_w.astype(F32)[None, :],
    )


def _stream_back(x, proj, dt_raw, pos, cache, conv_state, ssm_state, p, w_out, w_up, w_down):
    batch, t, _ = x.shape
    ns = max(n for n in STREAMS_PER_STEP if batch % n == 0)
    x2d = x.reshape(batch * t, D_MODEL)
    proj3 = proj.reshape(batch, t, PROJ_W)
    tabs = _rope_tables(pos)
    if cache is None:
        attn_part = _attn_prompt_part(proj3, tabs, p["qw"], p["kw"], p["sinks"], ns)
        L = CHUNK
    else:
        past_k, past_v = cache
        attn_part = _attn_sample_part(proj3, past_k.reshape(batch, WINDOW, KV_W),
                                      past_v.reshape(batch, WINDOW, KV_W), tabs, p["qw"], p["kw"], p["sinks"], ns)
        L = min(CHUNK, t)
    hp = N_SSM_HEADS * SSM_HEAD_DIM
    ssd_part = _ssd_part(proj3, dt_raw.reshape(batch, t, LANES), conv_state,
                         ssm_state.reshape(batch, hp, D_STATE), p["conv_w"], p["conv_b"], p["dtb"], p["alog"],
                         p["dskip"], p["ssm_nw"], p["sel"], L, ns)
    (attn, new_k, new_v), (y, new_conv, new_ssm) = _fused_call([attn_part, ssd_part], (batch // ns, t // L),
                                                              "mixers")
    late = () if w_down.dtype == BF16 else (w_down,)
    x1, *cast = _outproj(attn.reshape(batch * t, ATTN_W), y.reshape(batch * t, SSM_W), w_out, x2d, TILE_M,
                         OUTPROJ_TN, cast=late)
    if late:
        w_down = cast.pop()
    out = _ffn(x1, p["norm_ffn"], w_up, w_down, TILE_M, FFN_TF)
    return (out.reshape(batch, t, D_MODEL),
            new_k.reshape(batch, WINDOW, N_KV_HEADS, HEAD_DIM), new_v.reshape(batch, WINDOW, N_KV_HEADS, HEAD_DIM),
            new_conv, new_ssm.reshape(batch, N_SSM_HEADS, SSM_HEAD_DIM, D_STATE), w_down)


def kernel(x_prompt, x_sample, cache_k, cache_v, state_conv, state_ssm, norm_mix_w, w_in, q_norm_w, k_norm_w,
           attn_sinks, conv_w, conv_b, dt_bias, a_log, d_skip, ssm_norm_w, w_out, norm_ffn_w, w_up, w_down):
    depth = w_in.shape[0]
    b_p, t_p, _ = x_prompt.shape
    b_s, t_s, _ = x_sample.shape
    assert cache_k.shape[2] == WINDOW and t_p % CHUNK == 0 and t_p >= WINDOW
    assert t_s <= CHUNK and t_s % BF16_ROWS == 0
    pos_p = jnp.arange(t_p, dtype=jnp.int32)
    pos_s = PAST_LEN + jnp.arange(t_s, dtype=jnp.int32)
    hp, hs = x_prompt, x_sample
    outs_p, outs_s = [], []
    for layer in range(depth):
        p = _layer_params(norm_mix_w[layer], w_in[layer], q_norm_w[layer], k_norm_w[layer], attn_sinks[layer],
                          conv_w[layer], conv_b[layer], dt_bias[layer], a_log[layer], d_skip[layer],
                          ssm_norm_w[layer], w_out[layer], norm_ffn_w[layer], w_up[layer], w_down[layer])
        conv0 = jnp.zeros((b_p, CONV_W - 1, CONV_CH), F32)
        ssm0 = jnp.zeros((b_p, N_SSM_HEADS, SSM_HEAD_DIM, D_STATE), F32)
        xs2d, xp2d = hs.reshape(b_s * t_s, D_MODEL), hp.reshape(b_p * t_p, D_MODEL)
        w_in_t = w_in[layer].T.astype(BF16)
        proj_p, dt_p, w_out16, w_up16 = _inproj(xp2d, p["norm_mix"], w_in_t, TILE_M, INPROJ_TN,
                                                cast=(w_out[layer], w_up[layer]))
        proj_s, dt_s = _inproj(xs2d, p["norm_mix"], w_in_t, TILE_M, INPROJ_TN)
        hp, *rest_p, w_down16 = _stream_back(hp, proj_p, dt_p, pos_p, None, conv0, ssm0, p,
                                             w_out16, w_up16, w_down[layer])
        hs, *rest_s, _ = _stream_back(hs, proj_s, dt_s, pos_s, (cache_k[layer], cache_v[layer]),
                                      state_conv[layer], state_ssm[layer], p, w_out16, w_up16, w_down16)
        outs_p.append(rest_p)
        outs_s.append(rest_s)
    stack = lambda outs, i: jnp.stack([o[i] for o in outs])
    return (hp, hs, stack(outs_p, 0), stack(outs_p, 1), stack(outs_p, 2), stack(outs_p, 3),
            stack(outs_s, 0), stack(outs_s, 1), stack(outs_s, 2), stack(outs_s, 3))
```

```python
import functools

import jax
import jax.numpy as jnp
from jax import lax
from jax.experimental import pallas as pl
from jax.experimental.pallas import tpu as pltpu

F32 = jnp.float32
BF16 = jnp.bfloat16

D_MODEL = 4096
HEAD_DIM = 64
N_Q_HEADS = 32
N_KV_HEADS = 8
ATTN_W = N_Q_HEADS * HEAD_DIM
KV_W = N_KV_HEADS * HEAD_DIM
ROPE_DIM = 16
ROPE_HALF = ROPE_DIM // 2
ROPE_THETA = 500000.0
ATTN_SCALE = HEAD_DIM ** -0.5
WINDOW = 128
CHUNK = 64
PAST_LEN = 1024
SSM_W = 2048
N_SSM_HEADS = 32
SSM_HEAD_DIM = 64
N_GROUPS = 8
D_STATE = 128
BC_W = N_GROUPS * D_STATE
GROUP_W = SSM_W // N_GROUPS
CONV_W = 4
CONV_CH = SSM_W + 2 * BC_W
FFN_HIDDEN = 4 * D_MODEL
RMS_EPS = 1e-6
NEG_INF = -1e30
LANES = 128
BF16_ROWS = 16
Q_TILES = ATTN_W // LANES
KV_TILES = KV_W // LANES
PROJ_W = ATTN_W + 2 * KV_W + 2 * SSM_W + 2 * BC_W
COL_Q, COL_K, COL_V, COL_XS, COL_Z, COL_B, COL_C = 0, 2048, 2560, 3072, 5120, 7168, 8192
QKV_W = ATTN_W + 2 * KV_W
MIX_W = ATTN_W + SSM_W
VMEM_LIMIT = 62 * 1024 * 1024
STREAMS_PER_STEP = (1, 2, 4)
TILE_M = 512
INPROJ_TN = 1536
OUTPROJ_TN = 1024
FFN_TF = 512


def _params(*sem):
    return pltpu.CompilerParams(dimension_semantics=sem, vmem_limit_bytes=VMEM_LIMIT)


def _dot(a, b):
    return jnp.dot(a, b, preferred_element_type=F32)


def _dot_nt(a, b):
    return lax.dot_general(a, b, (((1,), (1,)), ((), ())), preferred_element_type=F32)


def _split3(x):
    a = x.astype(BF16)
    r = x - a.astype(F32)
    b = r.astype(BF16)
    c = (r - b.astype(F32)).astype(BF16)
    return a, b, c


def _silu(x):
    return x / (1.0 + jnp.exp(-x))


NORM_ROWS = 16
NORM_UNROLL = 8


def _rmsnorm_rows(x_ref, nw_ref, h_scr, copy_ref=None):
    def body(i, carry):
        r = pl.ds(pl.multiple_of(i * NORM_ROWS, NORM_ROWS), NORM_ROWS)
        x = x_ref[r, :]
        ms = jnp.mean(x * x, axis=-1, keepdims=True)
        h_scr[r, :] = (x * lax.rsqrt(ms + RMS_EPS) * nw_ref[...]).astype(BF16)
        if copy_ref is not None:
            copy_ref[r, :] = x
        return carry

    lax.fori_loop(0, x_ref.shape[0] // NORM_ROWS, body, 0, unroll=NORM_UNROLL)


def _cast_riders(arrays, steps, linear_step):
    in_specs, out_specs, out_shapes = [], [], []
    for a in arrays:
        rows, cols = a.shape
        nb = max(n for n in range(1, steps + 1) if rows % n == 0 and (rows // n) % BF16_ROWS == 0)
        spec = pl.BlockSpec((rows // nb, cols), lambda *g, nb=nb: (jnp.minimum(linear_step(*g), nb - 1), 0))
        in_specs.append(spec)
        out_specs.append(spec)
        out_shapes.append(jax.ShapeDtypeStruct(a.shape, BF16))
    return in_specs, out_specs, out_shapes


def _cast_blocks(src_refs, dst_refs):
    for src, dst in zip(src_refs, dst_refs):
        dst[...] = src[...].astype(BF16)


def _inproj_kernel(*refs, n_cast):
    x_ref, nw_ref, w_ref, wdt_ref = refs[:4]
    cast_in = refs[4:4 + n_cast]
    proj_ref, dt_ref = refs[4 + n_cast:6 + n_cast]
    cast_out, h_scr = refs[6 + n_cast:6 + 2 * n_cast], refs[6 + 2 * n_cast]

    @pl.when(pl.program_id(1) == 0)
    def _():
        _rmsnorm_rows(x_ref, nw_ref, h_scr)
        real = lax.broadcasted_iota(jnp.int32, wdt_ref.shape, 0) < N_SSM_HEADS
        dt_ref[...] = _dot_nt(h_scr[...], jnp.where(real, wdt_ref[...], jnp.zeros((), BF16)))

    proj_ref[...] = _dot_nt(h_scr[...], w_ref[...]).astype(BF16)
    _cast_blocks(cast_in, cast_out)


def _inproj(x2d, norm_w, w_in_t, tm, tn, cast=()):
    t = x2d.shape[0]
    nj = PROJ_W // tn
    c_in, c_out, c_shape = _cast_riders(cast, (t // tm) * nj, lambda i, j: i * nj + j)
    return pl.pallas_call(
        functools.partial(_inproj_kernel, n_cast=len(cast)),
        grid=(t // tm, nj),
        in_specs=[
            pl.BlockSpec((tm, D_MODEL), lambda i, j: (i, 0)),
            pl.BlockSpec((1, D_MODEL), lambda i, j: (0, 0)),
            pl.BlockSpec((tn, D_MODEL), lambda i, j: (j, 0)),
            pl.BlockSpec((LANES, D_MODEL), lambda i, j: (PROJ_W // LANES, 0)),
        ] + c_in,
        out_specs=[
            pl.BlockSpec((tm, tn), lambda i, j: (i, j)),
            pl.BlockSpec((tm, LANES), lambda i, j: (i, 0)),
        ] + c_out,
        out_shape=[
            jax.ShapeDtypeStruct((t, PROJ_W), BF16),
            jax.ShapeDtypeStruct((t, LANES), F32),
        ] + c_shape,
        scratch_shapes=[pltpu.VMEM((tm, D_MODEL), BF16)],
        compiler_params=_params("arbitrary", "arbitrary"),
        name="inproj",
    )(x2d, norm_w, w_in_t, w_in_t, *cast)


def _half_ones():
    r = lax.broadcasted_iota(jnp.int32, (LANES, LANES), 0) // HEAD_DIM
    c = lax.broadcasted_iota(jnp.int32, (LANES, LANES), 1) // HEAD_DIM
    return (r == c).astype(BF16)


def _head_norm(x, w_tile):
    ss = _dot((x * x).astype(BF16), _half_ones())
    return x * lax.rsqrt(ss * (1.0 / HEAD_DIM) + RMS_EPS) * w_tile


def _rope(x, cos_t, sin_lo, sin_hi):
    return (x * cos_t + pltpu.roll(x, LANES - ROPE_HALF, 1) * sin_lo
            + pltpu.roll(x, ROPE_HALF, 1) * sin_hi)


def _lane_is_a(shape):
    return lax.broadcasted_iota(jnp.int32, shape, 1) < HEAD_DIM


def _kv_tiles(k_win, v_win):
    is_a = _lane_is_a((CHUNK, LANES))
    ones_a = jnp.where(is_a, 1.0, 0.0).astype(F32)
    ones_b = 1.0 - ones_a
    kks, vvs = [], []
    for h in range(N_KV_HEADS):
        j, odd = h // 2, h % 2
        kt = k_win[:, LANES * j:LANES * (j + 1)]
        vt = v_win[:, LANES * j:LANES * (j + 1)]
        keep = jnp.logical_not(is_a) if odd else is_a
        k_own = jnp.where(keep, kt, 0.0)
        v_own = jnp.where(keep, vt, 0.0)
        k_sw = pltpu.roll(k_own, HEAD_DIM, 1)
        v_sw = pltpu.roll(v_own, HEAD_DIM, 1)
        k_a, k_b = (k_sw, k_own) if odd else (k_own, k_sw)
        v_a, v_b = (v_sw, v_own) if odd else (v_own, v_sw)
        kks.append(jnp.concatenate([k_a, k_b], axis=0).astype(BF16))
        vv = jnp.concatenate([jnp.concatenate([v_a, ones_a], axis=1),
                              jnp.concatenate([v_b, ones_b], axis=1)], axis=0)
        vvs.append(vv.astype(BF16))
    return kks, vvs


def _prep_q(q_ref, qw_ref, cos_ref, slo_ref, shi_ref, tq):
    q2 = jnp.concatenate([q_ref[:, LANES * i:LANES * (i + 1)].astype(F32) for i in range(Q_TILES)], axis=0)
    q2 = _head_norm(q2, qw_ref[...])
    cos_t, slo, shi = cos_ref[...], slo_ref[...], shi_ref[...]
    return [(_rope(q2[tq * i:tq * (i + 1)], cos_t, slo, shi) * ATTN_SCALE).astype(BF16) for i in range(Q_TILES)]


def _attend(q_tiles, kk_of, vv_of, valid_of, sink_ref, o_ref, tq):
    is_a = _lane_is_a((2 * tq, LANES))
    for h in range(N_KV_HEADS):
        qh = jnp.concatenate([q_tiles[2 * h], q_tiles[2 * h + 1]], axis=0)
        s = _dot_nt(qh, kk_of(h))
        sw = [jnp.where(valid_of(w), s[:, LANES * w:LANES * (w + 1)], NEG_INF) for w in range(3)]
        mt = jnp.maximum(jnp.maximum(sw[0], sw[1]), sw[2])
        m_a = jnp.max(jnp.where(is_a, mt, NEG_INF), axis=-1, keepdims=True)
        m_b = jnp.max(jnp.where(is_a, NEG_INF, mt), axis=-1, keepdims=True)
        sink = jnp.concatenate([jnp.broadcast_to(sink_ref[2 * h:2 * h + 1, :], (tq, LANES)),
                                jnp.broadcast_to(sink_ref[2 * h + 1:2 * h + 2, :], (tq, LANES))], axis=0)
        m = jnp.maximum(jnp.where(is_a, m_a, m_b), sink)
        e = jnp.concatenate([jnp.exp(x - m) for x in sw], axis=1).astype(BF16)
        oa = _dot(e, vv_of(h))
        o = oa[:, :LANES] / (oa[:, LANES:] + jnp.exp(sink - m))
        o_ref[:, LANES * 2 * h:LANES * (2 * h + 1)] = o[:tq].astype(o_ref.dtype)
        o_ref[:, LANES * (2 * h + 1):LANES * (2 * h + 2)] = o[tq:].astype(o_ref.dtype)


def _prep_k(k_raw, kw_ref, cos_ref, slo_ref, shi_ref, rows):
    k2 = jnp.concatenate([k_raw[:, LANES * j:LANES * (j + 1)].astype(F32) for j in range(KV_TILES)], axis=0)
    k2 = _head_norm(k2, kw_ref[...])
    cos_t, slo, shi = cos_ref[...], slo_ref[...], shi_ref[...]
    return jnp.concatenate([_rope(k2[rows * j:rows * (j + 1)], cos_t, slo, shi) for j in range(KV_TILES)], axis=1)


class _Part:
    def __init__(self, in_specs, operands, out_specs, out_shapes, scratch, body, init=None, final=None):
        self.in_specs, self.operands, self.out_specs, self.out_shapes = in_specs, operands, out_specs, out_shapes
        self.scratch, self.body, self.init, self.final = scratch, body, init, final


def _fused_kernel(*refs, parts):
    n_in = [len(p.in_specs) for p in parts]
    n_out = [len(p.out_specs) for p in parts]
    n_scr = [len(p.scratch) for p in parts]
    pos = 0
    groups = []
    for counts in (n_in, n_out, n_scr):
        groups.append([])
        for n in counts:
            groups[-1].append(refs[pos:pos + n])
            pos += n
    args = [(p, groups[0][i], groups[1][i], groups[2][i]) for i, p in enumerate(parts)]
    c = pl.program_id(1)
    mix_ref = groups[1][0][0]

    @pl.when(c == 0)
    def _():
        for p, ins, outs, scr in args:
            if p.init is not None:
                p.init(c, ins, outs, scr)

    for p, ins, outs, scr in args:
        p.body(c, ins, outs, scr, mix_ref)

    @pl.when(c == pl.num_programs(1) - 1)
    def _():
        for p, ins, outs, scr in args:
            if p.final is not None:
                p.final(c, ins, outs, scr)


def _fused_call(parts, grid, name):
    outs = pl.pallas_call(
        functools.partial(_fused_kernel, parts=parts),
        grid=grid,
        in_specs=[s for p in parts for s in p.in_specs],
        out_specs=[s for p in parts for s in p.out_specs],
        out_shape=[s for p in parts for s in p.out_shapes],
        scratch_shapes=[s for p in parts for s in p.scratch],
        compiler_params=_params("arbitrary", "arbitrary"),
        name=name,
    )(*[o for p in parts for o in p.operands])
    split, pos = [], 0
    for p in parts:
        split.append(outs[pos:pos + len(p.out_specs)])
        pos += len(p.out_specs)
    return split


def _attn_prompt_init(c, ins, outs, scr):
    for ring in scr:
        ring[...] = jnp.zeros_like(ring)


def _attn_prompt_body(c, ins, outs, scr, mix_ref):
    qkv_ref, cos_ref, slo_ref, shi_ref, qw_ref, kw_ref, sink_ref = ins
    _, nk_ref, nv_ref = outs
    kk_scr, vv_scr = scr
    slot = c % 3
    for s in range(qkv_ref.shape[0]):
        q_ref = qkv_ref.at[s, :, pl.ds(COL_Q, ATTN_W)]
        k_fin = _prep_k(qkv_ref[s, :, COL_K:COL_K + KV_W], kw_ref, cos_ref, slo_ref, shi_ref, CHUNK)
        v_cur = qkv_ref[s, :, COL_V:COL_V + KV_W].astype(F32)
        nk_ref[s] = k_fin
        nv_ref[s] = v_cur
        kks, vvs = _kv_tiles(k_fin, v_cur)
        for h in range(N_KV_HEADS):
            kk_scr[s, slot, h] = kks[h]
            vv_scr[s, slot, h] = vvs[h]

        q_tiles = _prep_q(q_ref, qw_ref, cos_ref, slo_ref, shi_ref, CHUNK)
        _attend(q_tiles,
                lambda h, s=s: jnp.concatenate([kk_scr[s, w, h] for w in range(3)], axis=0),
                lambda h, s=s: jnp.concatenate([vv_scr[s, w, h] for w in range(3)], axis=0),
                lambda w: (c - w + 3) % 3 <= c,
                sink_ref, mix_ref.at[s, :, pl.ds(0, ATTN_W)], CHUNK)


def _attn_prompt_part(proj3, rope_tabs, qw_tile, kw_tile, sink_tab, ns):
    batch, t, _ = proj3.shape
    nc = t // CHUNK
    keep = WINDOW // CHUNK
    const = lambda shape: pl.BlockSpec(shape, lambda g, c: (0,) * len(shape))
    keep_spec = pl.BlockSpec((ns, CHUNK, KV_W), lambda g, c: (g, jnp.maximum(c - (nc - keep), 0), 0))
    return _Part(
        in_specs=[
            pl.BlockSpec((ns, CHUNK, QKV_W), lambda g, c: (g, c, 0)),
            *[pl.BlockSpec((CHUNK, LANES), lambda g, c: (c, 0))] * 3,
            const((1, LANES)), const((1, LANES)), const((Q_TILES, LANES)),
        ],
        operands=[proj3, *rope_tabs, qw_tile, kw_tile, sink_tab],
        out_specs=[pl.BlockSpec((ns, CHUNK, MIX_W), lambda g, c: (g, c, 0)), keep_spec, keep_spec],
        out_shapes=[
            jax.ShapeDtypeStruct((batch, t, MIX_W), BF16),
            jax.ShapeDtypeStruct((batch, WINDOW, KV_W), F32),
            jax.ShapeDtypeStruct((batch, WINDOW, KV_W), F32),
        ],
        scratch=[
            pltpu.VMEM((ns, 3, N_KV_HEADS, 2 * CHUNK, LANES), BF16),
            pltpu.VMEM((ns, 3, N_KV_HEADS, 2 * CHUNK, 2 * LANES), BF16),
        ],
        body=_attn_prompt_body, init=_attn_prompt_init)


def _attn_sample_body(c, ins, outs, scr, mix_ref):
    qkv_ref, ck_ref, cv_ref, cos_ref, slo_ref, shi_ref, qw_ref, kw_ref, sink_ref = ins
    _, nk_ref, nv_ref = outs
    tq = qkv_ref.shape[1]
    new_valid = lax.broadcasted_iota(jnp.int32, (2 * tq, LANES), 1) % HEAD_DIM < tq
    pad = jnp.zeros((CHUNK - tq, KV_W), F32)
    for s in range(qkv_ref.shape[0]):
        q_ref = qkv_ref.at[s, :, pl.ds(COL_Q, ATTN_W)]
        k_fin = _prep_k(qkv_ref[s, :, COL_K:COL_K + KV_W], kw_ref, cos_ref, slo_ref, shi_ref, tq)
        v_cur = qkv_ref[s, :, COL_V:COL_V + KV_W].astype(F32)
        nk_ref[s, 0:WINDOW - tq, :] = ck_ref[s, tq:WINDOW, :]
        nv_ref[s, 0:WINDOW - tq, :] = cv_ref[s, tq:WINDOW, :]
        nk_ref[s, WINDOW - tq:WINDOW, :] = k_fin
        nv_ref[s, WINDOW - tq:WINDOW, :] = v_cur
        wins = [(ck_ref[s, :CHUNK, :], cv_ref[s, :CHUNK, :]),
                (ck_ref[s, CHUNK:, :], cv_ref[s, CHUNK:, :]),
                (jnp.concatenate([k_fin, pad], axis=0), jnp.concatenate([v_cur, pad], axis=0))]
        tiles = [_kv_tiles(kw, vw) for kw, vw in wins]
        q_tiles = _prep_q(q_ref, qw_ref, cos_ref, slo_ref, shi_ref, tq)
        _attend(q_tiles,
                lambda h, tiles=tiles: jnp.concatenate([tiles[w][0][h] for w in range(3)], axis=0),
                lambda h, tiles=tiles: jnp.concatenate([tiles[w][1][h] for w in range(3)], axis=0),
                lambda w: new_valid if w == 2 else True,
                sink_ref, mix_ref.at[s, :, pl.ds(0, ATTN_W)], tq)


def _attn_sample_part(proj3, cache_k, cache_v, rope_tabs, qw_tile, kw_tile, sink_tab, ns):
    batch, tq, _ = proj3.shape
    const = lambda shape: pl.BlockSpec(shape, lambda g, c: (0,) * len(shape))
    cache_spec = pl.BlockSpec((ns, WINDOW, KV_W), lambda g, c: (g, 0, 0))
    return _Part(
        in_specs=[
            pl.BlockSpec((ns, tq, QKV_W), lambda g, c: (g, 0, 0)),
            cache_spec, cache_spec,
            *[pl.BlockSpec((tq, LANES), lambda g, c: (0, 0))] * 3,
            const((1, LANES)), const((1, LANES)), const((Q_TILES, LANES)),
        ],
        operands=[proj3, cache_k, cache_v, *rope_tabs, qw_tile, kw_tile, sink_tab],
        out_specs=[pl.BlockSpec((ns, tq, MIX_W), lambda g, c: (g, 0, 0)), cache_spec, cache_spec],
        out_shapes=[
            jax.ShapeDtypeStruct((batch, tq, MIX_W), BF16),
            jax.ShapeDtypeStruct((batch, WINDOW, KV_W), F32),
            jax.ShapeDtypeStruct((batch, WINDOW, KV_W), F32),
        ],
        scratch=[], body=_attn_sample_body)


def _pad_rows(a, rows):
    if a.shape[0] == rows:
        return a
    return jnp.concatenate([a, jnp.zeros((rows - a.shape[0], a.shape[1]), a.dtype)], axis=0)


def _ssd_chunk(c, xs0_ref, xs1_ref, z0_ref, z1_ref, b_ref, c_ref, dt_ref, cst_ref, cw_ref, cb_ref, dtb_ref,
               alog_ref, dskip_ref, nw_ref, sel_ref, shift_ref, y_ref, ext_scr, st_scr, L):
    half = SSM_W // 2
    ext_scr[L:2 * L, 0:half] = xs0_ref[...]
    ext_scr[L:2 * L, half:SSM_W] = xs1_ref[...]
    ext_scr[L:2 * L, SSM_W:SSM_W + BC_W] = b_ref[...]
    ext_scr[L:2 * L, SSM_W + BC_W:CONV_CH] = c_ref[...]
    delayed = _dot(shift_ref[...], ext_scr[...])
    cur = ext_scr[L:2 * L, :].astype(F32)
    conv = cb_ref[...] + cur * cw_ref[CONV_W - 1:CONV_W, :]
    for i in range(CONV_W - 1):
        conv = conv + delayed[L * i:L * (i + 1)] * cw_ref[i:i + 1, :]
    s = cst_ref[...]
    w0, w1, w2 = cw_ref[0:1, :], cw_ref[1:2, :], cw_ref[2:3, :]
    head = jnp.concatenate([w0 * s[0:1] + w1 * s[1:2] + w2 * s[2:3], w0 * s[1:2] + w1 * s[2:3], w0 * s[2:3],
                            jnp.zeros((8 - (CONV_W - 1), CONV_CH), F32)], axis=0)
    conv = jnp.concatenate([conv[0:8] + jnp.where(c == 0, head, 0.0), conv[8:]], axis=0)
    conv = _silu(conv)
    xs = conv[:, :SSM_W]
    ext_scr[0:L, :] = ext_scr[L:2 * L, :]

    x_dt = dt_ref[...] + dtb_ref[...]
    dt = jnp.maximum(x_dt, 0.0) + jnp.log1p(jnp.exp(-jnp.abs(x_dt)))
    ad = dt * (-jnp.exp(alog_ref[...]))
    li = lax.broadcasted_iota(jnp.int32, (L, L), 0)
    si = lax.broadcasted_iota(jnp.int32, (L, L), 1)
    tril = (si <= li).astype(BF16)
    a_cum = _dot(jnp.concatenate([tril] * 3, axis=1), jnp.concatenate(_split3(ad), axis=0))
    sel = sel_ref[...]
    ex = _dot(jnp.concatenate(_split3(a_cum) + _split3(dt), axis=0), sel)
    col = ex[0:L] + ex[L:2 * L] + ex[2 * L:3 * L]
    dt_all = ex[3 * L:4 * L] + ex[4 * L:5 * L] + ex[5 * L:6 * L]
    lane_s = lax.broadcasted_iota(jnp.int32, (L, SSM_W), 1) % SSM_HEAD_DIM
    row_l = lax.broadcasted_iota(jnp.int32, (L, SSM_W), 0)
    row = jnp.sum(jnp.where(lane_s == row_l, col, 0.0), axis=0, keepdims=True)
    lmat = jnp.exp(jnp.where(lane_s <= row_l, col - row, NEG_INF))
    a_last = col[L - 1:L, :]
    exp_a = jnp.exp(col)
    decay = jnp.exp(a_last - col)
    exp_last = jnp.exp(a_last)

    xd = xs * dt_all
    xdd = xd * decay
    bi = lax.broadcasted_iota(jnp.int32, (GROUP_W, GROUP_W), 0) // SSM_HEAD_DIM
    bj = lax.broadcasted_iota(jnp.int32, (GROUP_W, GROUP_W), 1) // SSM_HEAD_DIM
    blockdiag = bi == bj

    for g in range(N_GROUPS):
        gs = slice(GROUP_W * g, GROUP_W * (g + 1))
        b_g = conv[:, SSM_W + D_STATE * g:SSM_W + D_STATE * (g + 1)]
        c_g = conv[:, SSM_W + BC_W + D_STATE * g:SSM_W + BC_W + D_STATE * (g + 1)].astype(BF16)
        b_pad = _pad_rows(b_g, CHUNK)
        cb = _dot_nt(c_g, jnp.concatenate([b_pad.astype(BF16)] * 4, axis=0))
        m_g = (cb * lmat[:, gs]).astype(BF16)
        xd_g = _pad_rows(xd[:, gs], CHUNK).astype(BF16)
        xd_bd = jnp.where(blockdiag, jnp.concatenate([xd_g] * 4, axis=0), jnp.zeros((), BF16))
        y_diag = _dot(m_g, xd_bd)
        st_g = st_scr[:, gs]
        y_off = _dot(c_g, st_g.astype(BF16)) * exp_a[:, gs]
        new_st = _dot(b_pad.T.astype(BF16), _pad_rows(xdd[:, gs], CHUNK).astype(BF16))
        st_scr[:, gs] = st_g * exp_last[:, gs] + new_st

        y = y_diag + y_off + xs[:, gs] * dskip_ref[:, gs]
        z_ref = z0_ref if g < N_GROUPS // 2 else z1_ref
        zs = slice(GROUP_W * (g % (N_GROUPS // 2)), GROUP_W * (g % (N_GROUPS // 2) + 1))
        gated = y * _silu(z_ref[:, zs].astype(F32))
        ms = jnp.mean(gated * gated, axis=-1, keepdims=True)
        y_ref[:, gs] = (gated * lax.rsqrt(ms + RMS_EPS) * nw_ref[:, gs]).astype(y_ref.dtype)


def _ssd_init(c, ins, outs, scr):
    st0_ref = ins[4]
    ext_scr, st_scr = scr
    ns, L = ext_scr.shape[0], ext_scr.shape[1] // 2
    for s in range(ns):
        ext_scr[s, 0:L, :] = jnp.zeros((L, CONV_CH), BF16)
        st_scr[s] = st0_ref[s].T


def _ssd_body(c, ins, outs, scr, mix_ref):
    (lo_ref, hi_ref, dt_ref, cst_ref, _, cw_ref, cb_ref, dtb_ref, alog_ref, dskip_ref, nw_ref, sel_ref,
     shift_ref) = ins
    ext_scr, st_scr = scr
    ns, L = dt_ref.shape[0], dt_ref.shape[1]
    half = SSM_W // 2
    for s in range(ns):
        view = lambda ref, start, s=s: ref.at[s, :, pl.ds(start, half)]
        _ssd_chunk(c, view(lo_ref, 0), view(lo_ref, half), view(lo_ref, 2 * half), view(hi_ref, 0),
                   view(hi_ref, half), view(hi_ref, 2 * half),
                   dt_ref.at[s], cst_ref.at[s], cw_ref, cb_ref, dtb_ref, alog_ref, dskip_ref, nw_ref,
                   sel_ref, shift_ref, mix_ref.at[s, :, pl.ds(ATTN_W, SSM_W)], ext_scr.at[s], st_scr.at[s], L)


def _ssd_final(c, ins, outs, scr):
    ncv_ref, nst_ref = outs
    ext_scr, st_scr = scr
    ns, L = ext_scr.shape[0], ext_scr.shape[1] // 2
    for s in range(ns):
        tail = ext_scr[s, L - BF16_ROWS:L, :].astype(F32)
        ncv_ref[s] = tail[BF16_ROWS - (CONV_W - 1):BF16_ROWS, :]
        nst_ref[s] = st_scr[s].T


def _shift_matrix(L):
    r = jnp.arange(3 * L)
    src = L + r % L - (CONV_W - 1) + r // L
    return (jnp.arange(2 * L)[None, :] == src[:, None]).astype(BF16)


def _ssd_part(proj3, dt3, conv_state, ssm_state, conv_w, conv_b, dtb, alog, dskip_all, ssm_nw, sel, L, ns):
    batch, t, _ = proj3.shape
    const = lambda shape: pl.BlockSpec(shape, lambda g, c: (0,) * len(shape))
    per_g = lambda shape: pl.BlockSpec((ns,) + shape, lambda g, c: (g,) + (0,) * len(shape))
    third = lambda k: pl.BlockSpec((ns, L, QKV_W), lambda g, c: (g, c, k))
    hp = N_SSM_HEADS * SSM_HEAD_DIM
    assert COL_XS == QKV_W and PROJ_W == 3 * QKV_W
    return _Part(
        in_specs=[
            third(1), third(2),
            pl.BlockSpec((ns, L, LANES), lambda g, c: (g, c, 0)),
            per_g((CONV_W - 1, CONV_CH)),
            per_g((hp, D_STATE)),
            const((CONV_W, CONV_CH)), const((1, CONV_CH)), const((1, LANES)), const((1, LANES)),
            const((1, SSM_W)), const((1, SSM_W)), const((LANES, SSM_W)), const((3 * L, 2 * L)),
        ],
        out_specs=[
            per_g((CONV_W - 1, CONV_CH)),
            per_g((hp, D_STATE)),
        ],
        out_shapes=[
            jax.ShapeDtypeStruct((batch, CONV_W - 1, CONV_CH), F32),
            jax.ShapeDtypeStruct((batch, hp, D_STATE), F32),
        ],
        scratch=[
            pltpu.VMEM((ns, 2 * L, CONV_CH), BF16),
            pltpu.VMEM((ns, D_STATE, hp), F32),
        ],
        operands=[proj3, proj3, dt3, conv_state, ssm_state, conv_w, conv_b, dtb, alog, dskip_all, ssm_nw, sel,
                  _shift_matrix(L)],
        body=_ssd_body, init=_ssd_init, final=_ssd_final)


def _outproj_kernel(*refs, n_cast):
    mix_ref, w_ref, x_ref = refs[:3]
    cast_in, o_ref, cast_out = refs[3:3 + n_cast], refs[3 + n_cast], refs[4 + n_cast:]
    o_ref[...] = x_ref[...] + _dot(mix_ref[...], w_ref[...])
    _cast_blocks(cast_in, cast_out)


def _outproj(mix, w_out, x2d, tm, tn, cast=()):
    t = x2d.shape[0]
    ni = t // tm
    c_in, c_out, c_shape = _cast_riders(cast, (D_MODEL // tn) * ni, lambda j, i: j * ni + i)
    return pl.pallas_call(
        functools.partial(_outproj_kernel, n_cast=len(cast)),
        grid=(D_MODEL // tn, ni),
        in_specs=[
            pl.BlockSpec((tm, MIX_W), lambda j, i: (i, 0)),
            pl.BlockSpec((MIX_W, tn), lambda j, i: (0, j)),
            pl.BlockSpec((tm, tn), lambda j, i: (i, j)),
        ] + c_in,
        out_specs=[pl.BlockSpec((tm, tn), lambda j, i: (i, j))] + c_out,
        out_shape=[jax.ShapeDtypeStruct((t, D_MODEL), F32)] + c_shape,
        compiler_params=_params("arbitrary", "arbitrary"),
        name="outproj",
    )(mix, w_out, x2d, *cast)


def _ffn_kernel(x_ref, nw_ref, wu_ref, wd_ref, o_ref, h_scr, *, n_split):
    @pl.when(pl.program_id(1) == 0)
    def _():
        _rmsnorm_rows(x_ref, nw_ref, h_scr, copy_ref=o_ref)

    u = jnp.maximum(_dot(h_scr[...], wu_ref[...]), 0.0)
    u = (u * u).astype(BF16)
    wn = D_MODEL // n_split
    for n in range(n_split):
        o_ref[:, wn * n:wn * (n + 1)] += _dot(u, wd_ref[:, wn * n:wn * (n + 1)])


def _ffn(x2d, norm_w, w_up, w_down, tm, tf):
    t = x2d.shape[0]
    return pl.pallas_call(
        functools.partial(_ffn_kernel, n_split=4),
        grid=(t // tm, FFN_HIDDEN // tf),
        in_specs=[
            pl.BlockSpec((tm, D_MODEL), lambda i, f: (i, 0)),
            pl.BlockSpec((1, D_MODEL), lambda i, f: (0, 0)),
            pl.BlockSpec((D_MODEL, tf), lambda i, f: (0, f)),
            pl.BlockSpec((tf, D_MODEL), lambda i, f: (f, 0)),
        ],
        out_specs=pl.BlockSpec((tm, D_MODEL), lambda i, f: (i, 0)),
        out_shape=jax.ShapeDtypeStruct((t, D_MODEL), F32),
        scratch_shapes=[pltpu.VMEM((tm, D_MODEL), BF16)],
        compiler_params=_params("arbitrary", "arbitrary"),
        name="ffn",
    )(x2d, norm_w, w_up, w_down)


def _rope_tables(pos):
    inv_freq = ROPE_THETA ** (-jnp.arange(ROPE_HALF, dtype=F32) / ROPE_HALF)
    lane = jnp.arange(LANES) % HEAD_DIM
    freq = jnp.where(lane < ROPE_DIM, inv_freq[lane % ROPE_HALF], 0.0)
    ang = pos.astype(F32)[:, None] * freq[None, :]
    cos, sin = jnp.cos(ang), jnp.sin(ang)
    return [cos, jnp.where(lane < ROPE_HALF, -sin, 0.0), jnp.where(lane >= ROPE_HALF, sin, 0.0)]


def _layer_params(norm_mix_w, w_in, q_norm_w, k_norm_w, attn_sinks, conv_w, conv_b, dt_bias, a_log, d_skip,
                  ssm_norm_w, w_out, norm_ffn_w, w_up, w_down):
    pad_h = lambda a: jnp.pad(a.astype(F32), (0, LANES - N_SSM_HEADS))[None, :]
    sel = (jnp.arange(LANES)[:, None] == (jnp.arange(SSM_W) // SSM_HEAD_DIM)[None, :]).astype(BF16)
    return dict(
        norm_mix=norm_mix_w.astype(F32)[None, :],
        qw=jnp.tile(q_norm_w.astype(F32), 2)[None, :], kw=jnp.tile(k_norm_w.astype(F32), 2)[None, :],
        sinks=jnp.repeat(attn_sinks.astype(F32), HEAD_DIM).reshape(Q_TILES, LANES),
        conv_w=conv_w.astype(F32), conv_b=conv_b.astype(F32)[None, :],
        dtb=pad_h(dt_bias), alog=pad_h(a_log),
        dskip=jnp.repeat(d_skip.astype(F32), SSM_HEAD_DIM)[None, :],
        ssm_nw=ssm_norm_w.astype(F32)[None, :], sel=sel,
        norm_ffn=norm_ffn_w.astype(F32)[None, :],
    )


def _stream_back(x, proj, dt_raw, pos, cache, conv_state, ssm_state, p, w_out, w_up, w_down):
    batch, t, _ = x.shape
    ns = max(n for n in STREAMS_PER_STEP if batch % n == 0)
    x2d = x.reshape(batch * t, D_MODEL)
    proj3 = proj.reshape(batch, t, PROJ_W)
    tabs = _rope_tables(pos)
    if cache is None:
        attn_part = _attn_prompt_part(proj3, tabs, p["qw"], p["kw"], p["sinks"], ns)
        L = CHUNK
    else:
        past_k, past_v = cache
        attn_part = _attn_sample_part(proj3, past_k.reshape(batch, WINDOW, KV_W),
                                      past_v.reshape(batch, WINDOW, KV_W), tabs, p["qw"], p["kw"], p["sinks"], ns)
        L = min(CHUNK, t)
    hp = N_SSM_HEADS * SSM_HEAD_DIM
    ssd_part = _ssd_part(proj3, dt_raw.reshape(batch, t, LANES), conv_state,
                         ssm_state.reshape(batch, hp, D_STATE), p["conv_w"], p["conv_b"], p["dtb"], p["alog"],
                         p["dskip"], p["ssm_nw"], p["sel"], L, ns)
    (mix, new_k, new_v), (new_conv, new_ssm) = _fused_call([attn_part, ssd_part], (batch // ns, t // L), "mixers")
    late = () if w_down.dtype == BF16 else (w_down,)
    x1, *cast = _outproj(mix.reshape(batch * t, MIX_W), w_out, x2d, TILE_M, OUTPROJ_TN, cast=late)
    if late:
        w_down = cast.pop()
    out = _ffn(x1, p["norm_ffn"], w_up, w_down, TILE_M, FFN_TF)
    return (out.reshape(batch, t, D_MODEL),
            new_k.reshape(batch, WINDOW, N_KV_HEADS, HEAD_DIM), new_v.reshape(batch, WINDOW, N_KV_HEADS, HEAD_DIM),
            new_conv, new_ssm.reshape(batch, N_SSM_HEADS, SSM_HEAD_DIM, D_STATE), w_down)


def kernel(x_prompt, x_sample, cache_k, cache_v, state_conv, state_ssm, norm_mix_w, w_in, q_norm_w, k_norm_w,
           attn_sinks, conv_w, conv_b, dt_bias, a_log, d_skip, ssm_norm_w, w_out, norm_ffn_w, w_up, w_down):
    depth = w_in.shape[0]
    b_p, t_p, _ = x_prompt.shape
    b_s, t_s, _ = x_sample.shape
    assert cache_k.shape[2] == WINDOW and t_p % CHUNK == 0 and t_p >= WINDOW
    assert t_s <= CHUNK and t_s % BF16_ROWS == 0
    pos_p = jnp.arange(t_p, dtype=jnp.int32)
    pos_s = PAST_LEN + jnp.arange(t_s, dtype=jnp.int32)
    hp, hs = x_prompt, x_sample
    outs_p, outs_s = [], []
    for layer in range(depth):
        p = _layer_params(norm_mix_w[layer], w_in[layer], q_norm_w[layer], k_norm_w[layer], attn_sinks[layer],
                          conv_w[layer], conv_b[layer], dt_bias[layer], a_log[layer], d_skip[layer],
                          ssm_norm_w[layer], w_out[layer], norm_ffn_w[layer], w_up[layer], w_down[layer])
        conv0 = jnp.zeros((b_p, CONV_W - 1, CONV_CH), F32)
        ssm0 = jnp.zeros((b_p, N_SSM_HEADS, SSM_HEAD_DIM, D_STATE), F32)
        xs2d, xp2d = hs.reshape(b_s * t_s, D_MODEL), hp.reshape(b_p * t_p, D_MODEL)
        w_in_t = w_in[layer].T.astype(BF16)
        proj_p, dt_p, w_out16, w_up16 = _inproj(xp2d, p["norm_mix"], w_in_t, TILE_M, INPROJ_TN,
                                                cast=(w_out[layer], w_up[layer]))
        proj_s, dt_s = _inproj(xs2d, p["norm_mix"], w_in_t, TILE_M, INPROJ_TN)
        hp, *rest_p, w_down16 = _stream_back(hp, proj_p, dt_p, pos_p, None, conv0, ssm0, p,
                                             w_out16, w_up16, w_down[layer])
        hs, *rest_s, _ = _stream_back(hs, proj_s, dt_s, pos_s, (cache_k[layer], cache_v[layer]),
                                      state_conv[layer], state_ssm[layer], p, w_out16, w_up16, w_down16)
        outs_p.append(rest_p)
        outs_s.append(rest_s)
    stack = lambda outs, i: jnp.stack([o[i] for o in outs])
    return (hp, hs, stack(outs_p, 0), stack(outs_p, 1), stack(outs_p, 2), stack(outs_p, 3),
            stack(outs_s, 0), stack(outs_s, 1), stack(outs_s, 2), stack(outs_s, 3))
```

```python
import functools

import jax
import jax.numpy as jnp
from jax import lax
from jax.experimental import pallas as pl
from jax.experimental.pallas import tpu as pltpu

F32 = jnp.float32
BF16 = jnp.bfloat16

D_MODEL = 4096
HEAD_DIM = 64
N_Q_HEADS = 32
N_KV_HEADS = 8
ATTN_W = N_Q_HEADS * HEAD_DIM
KV_W = N_KV_HEADS * HEAD_DIM
ROPE_DIM = 16
ROPE_HALF = ROPE_DIM // 2
ROPE_THETA = 500000.0
ATTN_SCALE = HEAD_DIM ** -0.5
WINDOW = 128
CHUNK = 64
PAST_LEN = 1024
SSM_W = 2048
N_SSM_HEADS = 32
SSM_HEAD_DIM = 64
N_GROUPS = 8
D_STATE = 128
BC_W = N_GROUPS * D_STATE
GROUP_W = SSM_W // N_GROUPS
CONV_W = 4
CONV_CH = SSM_W + 2 * BC_W
FFN_HIDDEN = 4 * D_MODEL
RMS_EPS = 1e-6
NEG_INF = -1e30
LANES = 128
BF16_ROWS = 16
Q_TILES = ATTN_W // LANES
KV_TILES = KV_W // LANES
PROJ_W = ATTN_W + 2 * KV_W + 2 * SSM_W + 2 * BC_W
COL_Q, COL_K, COL_V, COL_XS, COL_Z, COL_B, COL_C = 0, 2048, 2560, 3072, 5120, 7168, 8192
QKV_W = ATTN_W + 2 * KV_W
MIX_W = ATTN_W + SSM_W
VMEM_LIMIT = 62 * 1024 * 1024
STREAMS_PER_STEP = (1, 2, 4)
TILE_M = 512
INPROJ_TN = 1024
INPROJ_GROUP = 2
OUTPROJ_TN = 1024
FFN_TF = 512


def _params(*sem):
    return pltpu.CompilerParams(dimension_semantics=sem, vmem_limit_bytes=VMEM_LIMIT)


def _dot(a, b):
    return jnp.dot(a, b, preferred_element_type=F32)


def _dot_nt(a, b):
    return lax.dot_general(a, b, (((1,), (1,)), ((), ())), preferred_element_type=F32)


def _split3(x):
    a = x.astype(BF16)
    r = x - a.astype(F32)
    b = r.astype(BF16)
    c = (r - b.astype(F32)).astype(BF16)
    return a, b, c


def _silu(x):
    return x / (1.0 + jnp.exp(-x))


NORM_ROWS = 16
NORM_UNROLL = 8


def _rmsnorm_rows(x_ref, nw_ref, h_scr, copy_ref=None):
    def body(i, carry):
        r = pl.ds(pl.multiple_of(i * NORM_ROWS, NORM_ROWS), NORM_ROWS)
        x = x_ref[r, :]
        ms = jnp.mean(x * x, axis=-1, keepdims=True)
        h_scr[r, :] = (x * lax.rsqrt(ms + RMS_EPS) * nw_ref[...]).astype(BF16)
        if copy_ref is not None:
            copy_ref[r, :] = x
        return carry

    lax.fori_loop(0, x_ref.shape[0] // NORM_ROWS, body, 0, unroll=NORM_UNROLL)


def _cast_riders(arrays, steps, linear_step):
    in_specs, out_specs, out_shapes = [], [], []
    for a in arrays:
        rows, cols = a.shape
        nb = max(n for n in range(1, steps + 1) if rows % n == 0 and (rows // n) % BF16_ROWS == 0)
        spec = pl.BlockSpec((rows // nb, cols), lambda *g, nb=nb: (jnp.minimum(linear_step(*g), nb - 1), 0))
        in_specs.append(spec)
        out_specs.append(spec)
        out_shapes.append(jax.ShapeDtypeStruct(a.shape, BF16))
    return in_specs, out_specs, out_shapes


def _cast_blocks(src_refs, dst_refs):
    for src, dst in zip(src_refs, dst_refs):
        dst[...] = src[...].astype(BF16)


def _inproj_kernel(*refs, n_cast):
    x_ref, nw_ref, w_ref, wdt_ref = refs[:4]
    cast_in = refs[4:4 + n_cast]
    proj_ref, dt_ref = refs[4 + n_cast:6 + n_cast]
    cast_out, h_scr = refs[6 + n_cast:6 + 2 * n_cast], refs[6 + 2 * n_cast]
    member = pl.program_id(2)
    h_ref = h_scr.at[member]

    @pl.when(pl.program_id(1) == 0)
    def _():
        _rmsnorm_rows(x_ref, nw_ref, h_ref)
        real = lax.broadcasted_iota(jnp.int32, wdt_ref.shape, 0) < N_SSM_HEADS
        dt_ref[...] = _dot_nt(h_ref[...], jnp.where(real, wdt_ref[...], jnp.zeros((), BF16)))

    proj_ref[...] = _dot_nt(h_ref[...], w_ref[...]).astype(BF16)
    _cast_blocks(cast_in, cast_out)


def _inproj(x2d, norm_w, w_in_t, tm, tn, cast=()):
    t = x2d.shape[0]
    nj = PROJ_W // tn
    ni = t // tm
    grp = INPROJ_GROUP if ni % INPROJ_GROUP == 0 else 1
    tile = lambda g, m: g * grp + m
    rows_at_j0 = lambda g, j, m: (jnp.where(j == 0, tile(g, m), tile(g, grp - 1)), 0)
    c_in, c_out, c_shape = _cast_riders(cast, ni * nj, lambda g, j, m: (g * nj + j) * grp + m)
    return pl.pallas_call(
        functools.partial(_inproj_kernel, n_cast=len(cast)),
        grid=(ni // grp, nj, grp),
        in_specs=[
            pl.BlockSpec((tm, D_MODEL), rows_at_j0),
            pl.BlockSpec((1, D_MODEL), lambda g, j, m: (0, 0)),
            pl.BlockSpec((tn, D_MODEL), lambda g, j, m: (j, 0)),
            pl.BlockSpec((LANES, D_MODEL), lambda g, j, m: (PROJ_W // LANES, 0)),
        ] + c_in,
        out_specs=[
            pl.BlockSpec((tm, tn), lambda g, j, m: (tile(g, m), j)),
            pl.BlockSpec((tm, LANES), rows_at_j0),
        ] + c_out,
        out_shape=[
            jax.ShapeDtypeStruct((t, PROJ_W), BF16),
            jax.ShapeDtypeStruct((t, LANES), F32),
        ] + c_shape,
        scratch_shapes=[pltpu.VMEM((grp, tm, D_MODEL), BF16)],
        compiler_params=_params("arbitrary", "arbitrary", "arbitrary"),
        name="inproj",
    )(x2d, norm_w, w_in_t, w_in_t, *cast)


def _half_ones():
    r = lax.broadcasted_iota(jnp.int32, (LANES, LANES), 0) // HEAD_DIM
    c = lax.broadcasted_iota(jnp.int32, (LANES, LANES), 1) // HEAD_DIM
    return (r == c).astype(BF16)


def _head_norm(x, w_tile):
    ss = _dot((x * x).astype(BF16), _half_ones())
    return x * lax.rsqrt(ss * (1.0 / HEAD_DIM) + RMS_EPS) * w_tile


def _rope(x, cos_t, sin_lo, sin_hi):
    return (x * cos_t + pltpu.roll(x, LANES - ROPE_HALF, 1) * sin_lo
            + pltpu.roll(x, ROPE_HALF, 1) * sin_hi)


def _lane_is_a(shape):
    return lax.broadcasted_iota(jnp.int32, shape, 1) < HEAD_DIM


def _kv_tiles(k_win, v_win):
    is_a = _lane_is_a((CHUNK, LANES))
    ones_a = jnp.where(is_a, 1.0, 0.0).astype(F32)
    ones_b = 1.0 - ones_a
    kks, vvs = [], []
    for h in range(N_KV_HEADS):
        j, odd = h // 2, h % 2
        kt = k_win[:, LANES * j:LANES * (j + 1)]
        vt = v_win[:, LANES * j:LANES * (j + 1)]
        keep = jnp.logical_not(is_a) if odd else is_a
        k_own = jnp.where(keep, kt, 0.0)
        v_own = jnp.where(keep, vt, 0.0)
        k_sw = pltpu.roll(k_own, HEAD_DIM, 1)
        v_sw = pltpu.roll(v_own, HEAD_DIM, 1)
        k_a, k_b = (k_sw, k_own) if odd else (k_own, k_sw)
        v_a, v_b = (v_sw, v_own) if odd else (v_own, v_sw)
        kks.append(jnp.concatenate([k_a, k_b], axis=0).astype(BF16))
        vv = jnp.concatenate([jnp.concatenate([v_a, ones_a], axis=1),
                              jnp.concatenate([v_b, ones_b], axis=1)], axis=0)
        vvs.append(vv.astype(BF16))
    return kks, vvs


def _prep_q(q_ref, qw_ref, cos_ref, slo_ref, shi_ref, tq):
    q2 = jnp.concatenate([q_ref[:, LANES * i:LANES * (i + 1)].astype(F32) for i in range(Q_TILES)], axis=0)
    q2 = _head_norm(q2, qw_ref[...])
    cos_t, slo, shi = cos_ref[...], slo_ref[...], shi_ref[...]
    return [(_rope(q2[tq * i:tq * (i + 1)], cos_t, slo, shi) * ATTN_SCALE).astype(BF16) for i in range(Q_TILES)]


def _attend(q_tiles, kk_of, vv_of, valid_of, sink_ref, o_ref, tq):
    is_a = _lane_is_a((2 * tq, LANES))
    for h in range(N_KV_HEADS):
        qh = jnp.concatenate([q_tiles[2 * h], q_tiles[2 * h + 1]], axis=0)
        s = _dot_nt(qh, kk_of(h))
        sw = [jnp.where(valid_of(w), s[:, LANES * w:LANES * (w + 1)], NEG_INF) for w in range(3)]
        mt = jnp.maximum(jnp.maximum(sw[0], sw[1]), sw[2])
        m_a = jnp.max(jnp.where(is_a, mt, NEG_INF), axis=-1, keepdims=True)
        m_b = jnp.max(jnp.where(is_a, NEG_INF, mt), axis=-1, keepdims=True)
        sink = jnp.concatenate([jnp.broadcast_to(sink_ref[2 * h:2 * h + 1, :], (tq, LANES)),
                                jnp.broadcast_to(sink_ref[2 * h + 1:2 * h + 2, :], (tq, LANES))], axis=0)
        m = jnp.maximum(jnp.where(is_a, m_a, m_b), sink)
        e = jnp.concatenate([jnp.exp(x - m) for x in sw], axis=1).astype(BF16)
        oa = _dot(e, vv_of(h))
        o = oa[:, :LANES] / (oa[:, LANES:] + jnp.exp(sink - m))
        o_ref[:, LANES * 2 * h:LANES * (2 * h + 1)] = o[:tq].astype(o_ref.dtype)
        o_ref[:, LANES * (2 * h + 1):LANES * (2 * h + 2)] = o[tq:].astype(o_ref.dtype)


def _prep_k(k_raw, kw_ref, cos_ref, slo_ref, shi_ref, rows):
    k2 = jnp.concatenate([k_raw[:, LANES * j:LANES * (j + 1)].astype(F32) for j in range(KV_TILES)], axis=0)
    k2 = _head_norm(k2, kw_ref[...])
    cos_t, slo, shi = cos_ref[...], slo_ref[...], shi_ref[...]
    return jnp.concatenate([_rope(k2[rows * j:rows * (j + 1)], cos_t, slo, shi) for j in range(KV_TILES)], axis=1)


class _Part:
    def __init__(self, in_specs, operands, out_specs, out_shapes, scratch, body, init=None, final=None):
        self.in_specs, self.operands, self.out_specs, self.out_shapes = in_specs, operands, out_specs, out_shapes
        self.scratch, self.body, self.init, self.final = scratch, body, init, final


def _fused_kernel(*refs, parts):
    n_in = [len(p.in_specs) for p in parts]
    n_out = [len(p.out_specs) for p in parts]
    n_scr = [len(p.scratch) for p in parts]
    pos = 0
    groups = []
    for counts in (n_in, n_out, n_scr):
        groups.append([])
        for n in counts:
            groups[-1].append(refs[pos:pos + n])
            pos += n
    args = [(p, groups[0][i], groups[1][i], groups[2][i]) for i, p in enumerate(parts)]
    c = pl.program_id(1)
    mix_ref = groups[1][0][0]

    @pl.when(c == 0)
    def _():
        for p, ins, outs, scr in args:
            if p.init is not None:
                p.init(c, ins, outs, scr)

    for p, ins, outs, scr in args:
        p.body(c, ins, outs, scr, mix_ref)

    @pl.when(c == pl.num_programs(1) - 1)
    def _():
        for p, ins, outs, scr in args:
            if p.final is not None:
                p.final(c, ins, outs, scr)


def _fused_call(parts, grid, name):
    outs = pl.pallas_call(
        functools.partial(_fused_kernel, parts=parts),
        grid=grid,
        in_specs=[s for p in parts for s in p.in_specs],
        out_specs=[s for p in parts for s in p.out_specs],
        out_shape=[s for p in parts for s in p.out_shapes],
        scratch_shapes=[s for p in parts for s in p.scratch],
        compiler_params=_params("arbitrary", "arbitrary"),
        name=name,
    )(*[o for p in parts for o in p.operands])
    split, pos = [], 0
    for p in parts:
        split.append(outs[pos:pos + len(p.out_specs)])
        pos += len(p.out_specs)
    return split


def _attn_prompt_init(c, ins, outs, scr):
    for ring in scr:
        ring[...] = jnp.zeros_like(ring)


def _attn_prompt_body(c, ins, outs, scr, mix_ref):
    qkv_ref, cos_ref, slo_ref, shi_ref, qw_ref, kw_ref, sink_ref = ins
    _, nk_ref, nv_ref = outs
    kk_scr, vv_scr = scr
    slot = c % 3
    for s in range(qkv_ref.shape[0]):
        q_ref = qkv_ref.at[s, :, pl.ds(COL_Q, ATTN_W)]
        k_fin = _prep_k(qkv_ref[s, :, COL_K:COL_K + KV_W], kw_ref, cos_ref, slo_ref, shi_ref, CHUNK)
        v_cur = qkv_ref[s, :, COL_V:COL_V + KV_W].astype(F32)
        nk_ref[s] = k_fin
        nv_ref[s] = v_cur
        kks, vvs = _kv_tiles(k_fin, v_cur)
        for h in range(N_KV_HEADS):
            kk_scr[s, slot, h] = kks[h]
            vv_scr[s, slot, h] = vvs[h]

        q_tiles = _prep_q(q_ref, qw_ref, cos_ref, slo_ref, shi_ref, CHUNK)
        _attend(q_tiles,
                lambda h, s=s: jnp.concatenate([kk_scr[s, w, h] for w in range(3)], axis=0),
                lambda h, s=s: jnp.concatenate([vv_scr[s, w, h] for w in range(3)], axis=0),
                lambda w: (c - w + 3) % 3 <= c,
                sink_ref, mix_ref.at[s, :, pl.ds(0, ATTN_W)], CHUNK)


def _attn_prompt_part(proj3, rope_tabs, qw_tile, kw_tile, sink_tab, ns):
    batch, t, _ = proj3.shape
    nc = t // CHUNK
    keep = WINDOW // CHUNK
    const = lambda shape: pl.BlockSpec(shape, lambda g, c: (0,) * len(shape))
    keep_spec = pl.BlockSpec((ns, CHUNK, KV_W), lambda g, c: (g, jnp.maximum(c - (nc - keep), 0), 0))
    return _Part(
        in_specs=[
            pl.BlockSpec((ns, CHUNK, QKV_W), lambda g, c: (g, c, 0)),
            *[pl.BlockSpec((CHUNK, LANES), lambda g, c: (c, 0))] * 3,
            const((1, LANES)), const((1, LANES)), const((Q_TILES, LANES)),
        ],
        operands=[proj3, *rope_tabs, qw_tile, kw_tile, sink_tab],
        out_specs=[pl.BlockSpec((ns, CHUNK, MIX_W), lambda g, c: (g, c, 0)), keep_spec, keep_spec],
        out_shapes=[
            jax.ShapeDtypeStruct((batch, t, MIX_W), BF16),
            jax.ShapeDtypeStruct((batch, WINDOW, KV_W), F32),
            jax.ShapeDtypeStruct((batch, WINDOW, KV_W), F32),
        ],
        scratch=[
            pltpu.VMEM((ns, 3, N_KV_HEADS, 2 * CHUNK, LANES), BF16),
            pltpu.VMEM((ns, 3, N_KV_HEADS, 2 * CHUNK, 2 * LANES), BF16),
        ],
        body=_attn_prompt_body, init=_attn_prompt_init)


def _attn_sample_body(c, ins, outs, scr, mix_ref):
    qkv_ref, ck_ref, cv_ref, cos_ref, slo_ref, shi_ref, qw_ref, kw_ref, sink_ref = ins
    _, nk_ref, nv_ref = outs
    tq = qkv_ref.shape[1]
    new_valid = lax.broadcasted_iota(jnp.int32, (2 * tq, LANES), 1) % HEAD_DIM < tq
    pad = jnp.zeros((CHUNK - tq, KV_W), F32)
    for s in range(qkv_ref.shape[0]):
        q_ref = qkv_ref.at[s, :, pl.ds(COL_Q, ATTN_W)]
        k_fin = _prep_k(qkv_ref[s, :, COL_K:COL_K + KV_W], kw_ref, cos_ref, slo_ref, shi_ref, tq)
        v_cur = qkv_ref[s, :, COL_V:COL_V + KV_W].astype(F32)
        nk_ref[s, 0:WINDOW - tq, :] = ck_ref[s, tq:WINDOW, :]
        nv_ref[s, 0:WINDOW - tq, :] = cv_ref[s, tq:WINDOW, :]
        nk_ref[s, WINDOW - tq:WINDOW, :] = k_fin
        nv_ref[s, WINDOW - tq:WINDOW, :] = v_cur
        wins = [(ck_ref[s, :CHUNK, :], cv_ref[s, :CHUNK, :]),
                (ck_ref[s, CHUNK:, :], cv_ref[s, CHUNK:, :]),
                (jnp.concatenate([k_fin, pad], axis=0), jnp.concatenate([v_cur, pad], axis=0))]
        tiles = [_kv_tiles(kw, vw) for kw, vw in wins]
        q_tiles = _prep_q(q_ref, qw_ref, cos_ref, slo_ref, shi_ref, tq)
        _attend(q_tiles,
                lambda h, tiles=tiles: jnp.concatenate([tiles[w][0][h] for w in range(3)], axis=0),
                lambda h, tiles=tiles: jnp.concatenate([tiles[w][1][h] for w in range(3)], axis=0),
                lambda w: new_valid if w == 2 else True,
                sink_ref, mix_ref.at[s, :, pl.ds(0, ATTN_W)], tq)


def _attn_sample_part(proj3, cache_k, cache_v, rope_tabs, qw_tile, kw_tile, sink_tab, ns):
    batch, tq, _ = proj3.shape
    const = lambda shape: pl.BlockSpec(shape, lambda g, c: (0,) * len(shape))
    cache_spec = pl.BlockSpec((ns, WINDOW, KV_W), lambda g, c: (g, 0, 0))
    return _Part(
        in_specs=[
            pl.BlockSpec((ns, tq, QKV_W), lambda g, c: (g, 0, 0)),
            cache_spec, cache_spec,
            *[pl.BlockSpec((tq, LANES), lambda g, c: (0, 0))] * 3,
            const((1, LANES)), const((1, LANES)), const((Q_TILES, LANES)),
        ],
        operands=[proj3, cache_k, cache_v, *rope_tabs, qw_tile, kw_tile, sink_tab],
        out_specs=[pl.BlockSpec((ns, tq, MIX_W), lambda g, c: (g, 0, 0)), cache_spec, cache_spec],
        out_shapes=[
            jax.ShapeDtypeStruct((batch, tq, MIX_W), BF16),
            jax.ShapeDtypeStruct((batch, WINDOW, KV_W), F32),
            jax.ShapeDtypeStruct((batch, WINDOW, KV_W), F32),
        ],
        scratch=[], body=_attn_sample_body)


def _pad_rows(a, rows):
    if a.shape[0] == rows:
        return a
    return jnp.concatenate([a, jnp.zeros((rows - a.shape[0], a.shape[1]), a.dtype)], axis=0)


def _ssd_chunk(c, xs0_ref, xs1_ref, z0_ref, z1_ref, b_ref, c_ref, dt_ref, cst_ref, cw_ref, cb_ref, dtb_ref,
               alog_ref, dskip_ref, nw_ref, sel_ref, shift_ref, y_ref, ext_scr, st_scr, L):
    half = SSM_W // 2
    ext_scr[L:2 * L, 0:half] = xs0_ref[...]
    ext_scr[L:2 * L, half:SSM_W] = xs1_ref[...]
    ext_scr[L:2 * L, SSM_W:SSM_W + BC_W] = b_ref[...]
    ext_scr[L:2 * L, SSM_W + BC_W:CONV_CH] = c_ref[...]
    delayed = _dot(shift_ref[...], ext_scr[...])
    cur = ext_scr[L:2 * L, :].astype(F32)
    conv = cb_ref[...] + cur * cw_ref[CONV_W - 1:CONV_W, :]
    for i in range(CONV_W - 1):
        conv = conv + delayed[L * i:L * (i + 1)] * cw_ref[i:i + 1, :]
    s = cst_ref[...]
    w0, w1, w2 = cw_ref[0:1, :], cw_ref[1:2, :], cw_ref[2:3, :]
    head = jnp.concatenate([w0 * s[0:1] + w1 * s[1:2] + w2 * s[2:3], w0 * s[1:2] + w1 * s[2:3], w0 * s[2:3],
                            jnp.zeros((8 - (CONV_W - 1), CONV_CH), F32)], axis=0)
    conv = jnp.concatenate([conv[0:8] + jnp.where(c == 0, head, 0.0), conv[8:]], axis=0)
    conv = _silu(conv)
    xs = conv[:, :SSM_W]
    ext_scr[0:L, :] = ext_scr[L:2 * L, :]

    x_dt = dt_ref[...] + dtb_ref[...]
    dt = jnp.maximum(x_dt, 0.0) + jnp.log1p(jnp.exp(-jnp.abs(x_dt)))
    ad = dt * (-jnp.exp(alog_ref[...]))
    li = lax.broadcasted_iota(jnp.int32, (L, L), 0)
    si = lax.broadcasted_iota(jnp.int32, (L, L), 1)
    tril = (si <= li).astype(BF16)
    a_cum = _dot(jnp.concatenate([tril] * 3, axis=1), jnp.concatenate(_split3(ad), axis=0))
    sel = sel_ref[...]
    ex = _dot(jnp.concatenate(_split3(a_cum) + _split3(dt), axis=0), sel)
    col = ex[0:L] + ex[L:2 * L] + ex[2 * L:3 * L]
    dt_all = ex[3 * L:4 * L] + ex[4 * L:5 * L] + ex[5 * L:6 * L]
    lane_s = lax.broadcasted_iota(jnp.int32, (L, SSM_W), 1) % SSM_HEAD_DIM
    row_l = lax.broadcasted_iota(jnp.int32, (L, SSM_W), 0)
    row = jnp.sum(jnp.where(lane_s == row_l, col, 0.0), axis=0, keepdims=True)
    lmat = jnp.exp(jnp.where(lane_s <= row_l, col - row, NEG_INF))
    a_last = col[L - 1:L, :]
    exp_a = jnp.exp(col)
    decay = jnp.exp(a_last - col)
    exp_last = jnp.exp(a_last)

    xd = xs * dt_all
    xdd = xd * decay
    bi = lax.broadcasted_iota(jnp.int32, (GROUP_W, GROUP_W), 0) // SSM_HEAD_DIM
    bj = lax.broadcasted_iota(jnp.int32, (GROUP_W, GROUP_W), 1) // SSM_HEAD_DIM
    blockdiag = bi == bj

    for g in range(N_GROUPS):
        gs = slice(GROUP_W * g, GROUP_W * (g + 1))
        b_g = conv[:, SSM_W + D_STATE * g:SSM_W + D_STATE * (g + 1)]
        c_g = conv[:, SSM_W + BC_W + D_STATE * g:SSM_W + BC_W + D_STATE * (g + 1)].astype(BF16)
        b_pad = _pad_rows(b_g, CHUNK)
        cb = _dot_nt(c_g, jnp.concatenate([b_pad.astype(BF16)] * 4, axis=0))
        m_g = (cb * lmat[:, gs]).astype(BF16)
        xd_g = _pad_rows(xd[:, gs], CHUNK).astype(BF16)
        xd_bd = jnp.where(blockdiag, jnp.concatenate([xd_g] * 4, axis=0), jnp.zeros((), BF16))
        y_diag = _dot(m_g, xd_bd)
        st_g = st_scr[:, gs]
        y_off = _dot(c_g, st_g.astype(BF16)) * exp_a[:, gs]
        new_st = _dot(b_pad.T.astype(BF16), _pad_rows(xdd[:, gs], CHUNK).astype(BF16))
        st_scr[:, gs] = st_g * exp_last[:, gs] + new_st

        y = y_diag + y_off + xs[:, gs] * dskip_ref[:, gs]
        z_ref = z0_ref if g < N_GROUPS // 2 else z1_ref
        zs = slice(GROUP_W * (g % (N_GROUPS // 2)), GROUP_W * (g % (N_GROUPS // 2) + 1))
        gated = y * _silu(z_ref[:, zs].astype(F32))
        ms = jnp.mean(gated * gated, axis=-1, keepdims=True)
        y_ref[:, gs] = (gated * lax.rsqrt(ms + RMS_EPS) * nw_ref[:, gs]).astype(y_ref.dtype)


def _ssd_init(c, ins, outs, scr):
    st0_ref = ins[4]
    ext_scr, st_scr = scr
    ns, L = ext_scr.shape[0], ext_scr.shape[1] // 2
    for s in range(ns):
        ext_scr[s, 0:L, :] = jnp.zeros((L, CONV_CH), BF16)
        st_scr[s] = st0_ref[s].T


def _ssd_body(c, ins, outs, scr, mix_ref):
    (lo_ref, hi_ref, dt_ref, cst_ref, _, cw_ref, cb_ref, dtb_ref, alog_ref, dskip_ref, nw_ref, sel_ref,
     shift_ref) = ins
    ext_scr, st_scr = scr
    ns, L = dt_ref.shape[0], dt_ref.shape[1]
    half = SSM_W // 2
    for s in range(ns):
        view = lambda ref, start, s=s: ref.at[s, :, pl.ds(start, half)]
        _ssd_chunk(c, view(lo_ref, 0), view(lo_ref, half), view(lo_ref, 2 * half), view(hi_ref, 0),
                   view(hi_ref, half), view(hi_ref, 2 * half),
                   dt_ref.at[s], cst_ref.at[s], cw_ref, cb_ref, dtb_ref, alog_ref, dskip_ref, nw_ref,
                   sel_ref, shift_ref, mix_ref.at[s, :, pl.ds(ATTN_W, SSM_W)], ext_scr.at[s], st_scr.at[s], L)


def _ssd_final(c, ins, outs, scr):
    ncv_ref, nst_ref = outs
    ext_scr, st_scr = scr
    ns, L = ext_scr.shape[0], ext_scr.shape[1] // 2
    for s in range(ns):
        tail = ext_scr[s, L - BF16_ROWS:L, :].astype(F32)
        ncv_ref[s] = tail[BF16_ROWS - (CONV_W - 1):BF16_ROWS, :]
        nst_ref[s] = st_scr[s].T


def _shift_matrix(L):
    r = jnp.arange(3 * L)
    src = L + r % L - (CONV_W - 1) + r // L
    return (jnp.arange(2 * L)[None, :] == src[:, None]).astype(BF16)


def _ssd_part(proj3, dt3, conv_state, ssm_state, conv_w, conv_b, dtb, alog, dskip_all, ssm_nw, sel, L, ns):
    batch, t, _ = proj3.shape
    const = lambda shape: pl.BlockSpec(shape, lambda g, c: (0,) * len(shape))
    per_g = lambda shape: pl.BlockSpec((ns,) + shape, lambda g, c: (g,) + (0,) * len(shape))
    third = lambda k: pl.BlockSpec((ns, L, QKV_W), lambda g, c: (g, c, k))
    hp = N_SSM_HEADS * SSM_HEAD_DIM
    assert COL_XS == QKV_W and PROJ_W == 3 * QKV_W
    return _Part(
        in_specs=[
            third(1), third(2),
            pl.BlockSpec((ns, L, LANES), lambda g, c: (g, c, 0)),
            per_g((CONV_W - 1, CONV_CH)),
            per_g((hp, D_STATE)),
            const((CONV_W, CONV_CH)), const((1, CONV_CH)), const((1, LANES)), const((1, LANES)),
            const((1, SSM_W)), const((1, SSM_W)), const((LANES, SSM_W)), const((3 * L, 2 * L)),
        ],
        out_specs=[
            per_g((CONV_W - 1, CONV_CH)),
            per_g((hp, D_STATE)),
        ],
        out_shapes=[
            jax.ShapeDtypeStruct((batch, CONV_W - 1, CONV_CH), F32),
            jax.ShapeDtypeStruct((batch, hp, D_STATE), F32),
        ],
        scratch=[
            pltpu.VMEM((ns, 2 * L, CONV_CH), BF16),
            pltpu.VMEM((ns, D_STATE, hp), F32),
        ],
        operands=[proj3, proj3, dt3, conv_state, ssm_state, conv_w, conv_b, dtb, alog, dskip_all, ssm_nw, sel,
                  _shift_matrix(L)],
        body=_ssd_body, init=_ssd_init, final=_ssd_final)


def _outproj_kernel(*refs, n_cast):
    mix_ref, w_ref, x_ref = refs[:3]
    cast_in, o_ref, cast_out = refs[3:3 + n_cast], refs[3 + n_cast], refs[4 + n_cast:]
    o_ref[...] = x_ref[...] + _dot(mix_ref[...], w_ref[...])
    _cast_blocks(cast_in, cast_out)


def _outproj(mix, w_out, x2d, tm, tn, cast=()):
    t = x2d.shape[0]
    ni = t // tm
    c_in, c_out, c_shape = _cast_riders(cast, (D_MODEL // tn) * ni, lambda j, i: j * ni + i)
    return pl.pallas_call(
        functools.partial(_outproj_kernel, n_cast=len(cast)),
        grid=(D_MODEL // tn, ni),
        in_specs=[
            pl.BlockSpec((tm, MIX_W), lambda j, i: (i, 0)),
            pl.BlockSpec((MIX_W, tn), lambda j, i: (0, j)),
            pl.BlockSpec((tm, tn), lambda j, i: (i, j)),
        ] + c_in,
        out_specs=[pl.BlockSpec((tm, tn), lambda j, i: (i, j))] + c_out,
        out_shape=[jax.ShapeDtypeStruct((t, D_MODEL), F32)] + c_shape,
        compiler_params=_params("arbitrary", "arbitrary"),
        name="outproj",
    )(mix, w_out, x2d, *cast)


def _ffn_kernel(x_ref, nw_ref, wu_ref, wd_ref, o_ref, h_scr, *, n_split):
    @pl.when(pl.program_id(1) == 0)
    def _():
        _rmsnorm_rows(x_ref, nw_ref, h_scr, copy_ref=o_ref)

    u = jnp.maximum(_dot(h_scr[...], wu_ref[...]), 0.0)
    u = (u * u).astype(BF16)
    wn = D_MODEL // n_split
    for n in range(n_split):
        o_ref[:, wn * n:wn * (n + 1)] += _dot(u, wd_ref[:, wn * n:wn * (n + 1)])


def _ffn(x2d, norm_w, w_up, w_down, tm, tf):
    t = x2d.shape[0]
    return pl.pallas_call(
        functools.partial(_ffn_kernel, n_split=4),
        grid=(t // tm, FFN_HIDDEN // tf),
        in_specs=[
            pl.BlockSpec((tm, D_MODEL), lambda i, f: (i, 0)),
            pl.BlockSpec((1, D_MODEL), lambda i, f: (0, 0)),
            pl.BlockSpec((D_MODEL, tf), lambda i, f: (0, f)),
            pl.BlockSpec((tf, D_MODEL), lambda i, f: (f, 0)),
        ],
        out_specs=pl.BlockSpec((tm, D_MODEL), lambda i, f: (i, 0)),
        out_shape=jax.ShapeDtypeStruct((t, D_MODEL), F32),
        scratch_shapes=[pltpu.VMEM((tm, D_MODEL), BF16)],
        compiler_params=_params("arbitrary", "arbitrary"),
        name="ffn",
    )(x2d, norm_w, w_up, w_down)


def _rope_tables(pos):
    inv_freq = ROPE_THETA ** (-jnp.arange(ROPE_HALF, dtype=F32) / ROPE_HALF)
    lane = jnp.arange(LANES) % HEAD_DIM
    freq = jnp.where(lane < ROPE_DIM, inv_freq[lane % ROPE_HALF], 0.0)
    ang = pos.astype(F32)[:, None] * freq[None, :]
    cos, sin = jnp.cos(ang), jnp.sin(ang)
    return [cos, jnp.where(lane < ROPE_HALF, -sin, 0.0), jnp.where(lane >= ROPE_HALF, sin, 0.0)]


def _layer_params(norm_mix_w, w_in, q_norm_w, k_norm_w, attn_sinks, conv_w, conv_b, dt_bias, a_log, d_skip,
                  ssm_norm_w, w_out, norm_ffn_w, w_up, w_down):
    pad_h = lambda a: jnp.pad(a.astype(F32), (0, LANES - N_SSM_HEADS))[None, :]
    sel = (jnp.arange(LANES)[:, None] == (jnp.arange(SSM_W) // SSM_HEAD_DIM)[None, :]).astype(BF16)
    return dict(
        norm_mix=norm_mix_w.astype(F32)[None, :],
        qw=jnp.tile(q_norm_w.astype(F32), 2)[None, :], kw=jnp.tile(k_norm_w.astype(F32), 2)[None, :],
        sinks=jnp.repeat(attn_sinks.astype(F32), HEAD_DIM).reshape(Q_TILES, LANES),
        conv_w=conv_w.astype(F32), conv_b=conv_b.astype(F32)[None, :],
        dtb=pad_h(dt_bias), alog=pad_h(a_log),
        dskip=jnp.repeat(d_skip.astype(F32), SSM_HEAD_DIM)[None, :],
        ssm_nw=ssm_norm_w.astype(F32)[None, :], sel=sel,
        norm_ffn=norm_ffn_w.astype(F32)[None, :],
    )


def _stream_back(x, proj, dt_raw, pos, cache, conv_state, ssm_state, p, w_out, w_up, w_down):
    batch, t, _ = x.shape
    ns = max(n for n in STREAMS_PER_STEP if batch % n == 0)
    x2d = x.reshape(batch * t, D_MODEL)
    proj3 = proj.reshape(batch, t, PROJ_W)
    tabs = _rope_tables(pos)
    if cache is None:
        attn_part = _attn_prompt_part(proj3, tabs, p["qw"], p["kw"], p["sinks"], ns)
        L = CHUNK
    else:
        past_k, past_v = cache
        attn_part = _attn_sample_part(proj3, past_k.reshape(batch, WINDOW, KV_W),
                                      past_v.reshape(batch, WINDOW, KV_W), tabs, p["qw"], p["kw"], p["sinks"], ns)
        L = min(CHUNK, t)
    hp = N_SSM_HEADS * SSM_HEAD_DIM
    ssd_part = _ssd_part(proj3, dt_raw.reshape(batch, t, LANES), conv_state,
                         ssm_state.reshape(batch, hp, D_STATE), p["conv_w"], p["conv_b"], p["dtb"], p["alog"],
                         p["dskip"], p["ssm_nw"], p["sel"], L, ns)
    (mix, new_k, new_v), (new_conv, new_ssm) = _fused_call([attn_part, ssd_part], (batch // ns, t // L), "mixers")
    late = () if w_down.dtype == BF16 else (w_down,)
    x1, *cast = _outproj(mix.reshape(batch * t, MIX_W), w_out, x2d, TILE_M, OUTPROJ_TN, cast=late)
    if late:
        w_down = cast.pop()
    out = _ffn(x1, p["norm_ffn"], w_up, w_down, TILE_M, FFN_TF)
    return (out.reshape(batch, t, D_MODEL),
            new_k.reshape(batch, WINDOW, N_KV_HEADS, HEAD_DIM), new_v.reshape(batch, WINDOW, N_KV_HEADS, HEAD_DIM),
            new_conv, new_ssm.reshape(batch, N_SSM_HEADS, SSM_HEAD_DIM, D_STATE), w_down)


def kernel(x_prompt, x_sample, cache_k, cache_v, state_conv, state_ssm, norm_mix_w, w_in, q_norm_w, k_norm_w,
           attn_sinks, conv_w, conv_b, dt_bias, a_log, d_skip, ssm_norm_w, w_out, norm_ffn_w, w_up, w_down):
    depth = w_in.shape[0]
    b_p, t_p, _ = x_prompt.shape
    b_s, t_s, _ = x_sample.shape
    assert cache_k.shape[2] == WINDOW and t_p % CHUNK == 0 and t_p >= WINDOW
    assert t_s <= CHUNK and t_s % BF16_ROWS == 0
    pos_p = jnp.arange(t_p, dtype=jnp.int32)
    pos_s = PAST_LEN + jnp.arange(t_s, dtype=jnp.int32)
    hp, hs = x_prompt, x_sample
    outs_p, outs_s = [], []
    for layer in range(depth):
        p = _layer_params(norm_mix_w[layer], w_in[layer], q_norm_w[layer], k_norm_w[layer], attn_sinks[layer],
                          conv_w[layer], conv_b[layer], dt_bias[layer], a_log[layer], d_skip[layer],
                          ssm_norm_w[layer], w_out[layer], norm_ffn_w[layer], w_up[layer], w_down[layer])
        conv0 = jnp.zeros((b_p, CONV_W - 1, CONV_CH), F32)
        ssm0 = jnp.zeros((b_p, N_SSM_HEADS, SSM_HEAD_DIM, D_STATE), F32)
        xs2d, xp2d = hs.reshape(b_s * t_s, D_MODEL), hp.reshape(b_p * t_p, D_MODEL)
        w_in_t = w_in[layer].T.astype(BF16)
        proj_p, dt_p, w_out16, w_up16 = _inproj(xp2d, p["norm_mix"], w_in_t, TILE_M, INPROJ_TN,
                                                cast=(w_out[layer], w_up[layer]))
        proj_s, dt_s = _inproj(xs2d, p["norm_mix"], w_in_t, TILE_M, INPROJ_TN)
        hp, *rest_p, w_down16 = _stream_back(hp, proj_p, dt_p, pos_p, None, conv0, ssm0, p,
                                             w_out16, w_up16, w_down[layer])
        hs, *rest_s, _ = _stream_back(hs, proj_s, dt_s, pos_s, (cache_k[layer], cache_v[layer]),
                                      state_conv[layer], state_ssm[layer], p, w_out16, w_up16, w_down16)
        outs_p.append(rest_p)
        outs_s.append(rest_s)
    stack = lambda outs, i: jnp.stack([o[i] for o in outs])
    return (hp, hs, stack(outs_p, 0), stack(outs_p, 1), stack(outs_p, 2), stack(outs_p, 3),
            stack(outs_s, 0), stack(outs_s, 1), stack(outs_s, 2), stack(outs_s, 3))
```

```python
import functools

import jax
import jax.numpy as jnp
from jax import lax
from jax.experimental import pallas as pl
from jax.experimental.pallas import tpu as pltpu

F32 = jnp.float32
BF16 = jnp.bfloat16

D_MODEL = 4096
HEAD_DIM = 64
N_Q_HEADS = 32
N_KV_HEADS = 8
ATTN_W = N_Q_HEADS * HEAD_DIM
KV_W = N_KV_HEADS * HEAD_DIM
ROPE_DIM = 16
ROPE_HALF = ROPE_DIM // 2
ROPE_THETA = 500000.0
ATTN_SCALE = HEAD_DIM ** -0.5
WINDOW = 128
CHUNK = 64
PAST_LEN = 1024
SSM_W = 2048
N_SSM_HEADS = 32
SSM_HEAD_DIM = 64
N_GROUPS = 8
D_STATE = 128
BC_W = N_GROUPS * D_STATE
GROUP_W = SSM_W // N_GROUPS
CONV_W = 4
CONV_CH = SSM_W + 2 * BC_W
FFN_HIDDEN = 4 * D_MODEL
RMS_EPS = 1e-6
NEG_INF = -1e30
LANES = 128
BF16_ROWS = 16
Q_TILES = ATTN_W // LANES
KV_TILES = KV_W // LANES
PROJ_W = ATTN_W + 2 * KV_W + 2 * SSM_W + 2 * BC_W
COL_Q, COL_K, COL_V, COL_XS, COL_Z, COL_B, COL_C = 0, 2048, 2560, 3072, 5120, 7168, 8192
QKV_W = ATTN_W + 2 * KV_W
MIX_W = ATTN_W + SSM_W
VMEM_LIMIT = 63 * 1024 * 1024
STREAMS_PER_STEP = (1, 2, 4)
TILE_M = 512
INPROJ_TN = 1536
INPROJ_GROUP = 2
OUTPROJ_TN = 1024
FFN_TF = 512


def _params(*sem):
    return pltpu.CompilerParams(dimension_semantics=sem, vmem_limit_bytes=VMEM_LIMIT)


def _dot(a, b):
    return jnp.dot(a, b, preferred_element_type=F32)


def _dot_nt(a, b):
    return lax.dot_general(a, b, (((1,), (1,)), ((), ())), preferred_element_type=F32)


def _split3(x):
    a = x.astype(BF16)
    r = x - a.astype(F32)
    b = r.astype(BF16)
    c = (r - b.astype(F32)).astype(BF16)
    return a, b, c


def _silu(x):
    return x / (1.0 + jnp.exp(-x))


NORM_ROWS = 16
NORM_UNROLL = 8


def _rmsnorm_rows(x_ref, nw_ref, h_scr, copy_ref=None):
    def body(i, carry):
        r = pl.ds(pl.multiple_of(i * NORM_ROWS, NORM_ROWS), NORM_ROWS)
        x = x_ref[r, :]
        ms = jnp.mean(x * x, axis=-1, keepdims=True)
        h_scr[r, :] = (x * lax.rsqrt(ms + RMS_EPS) * nw_ref[...]).astype(BF16)
        if copy_ref is not None:
            copy_ref[r, :] = x
        return carry

    lax.fori_loop(0, x_ref.shape[0] // NORM_ROWS, body, 0, unroll=NORM_UNROLL)


def _cast_riders(arrays, steps, linear_step):
    in_specs, out_specs, out_shapes = [], [], []
    for a in arrays:
        rows, cols = a.shape
        nb = max(n for n in range(1, steps + 1) if rows % n == 0 and (rows // n) % BF16_ROWS == 0)
        spec = pl.BlockSpec((rows // nb, cols), lambda *g, nb=nb: (jnp.minimum(linear_step(*g), nb - 1), 0))
        in_specs.append(spec)
        out_specs.append(spec)
        out_shapes.append(jax.ShapeDtypeStruct(a.shape, BF16))
    return in_specs, out_specs, out_shapes


def _cast_blocks(src_refs, dst_refs):
    for src, dst in zip(src_refs, dst_refs):
        dst[...] = src[...].astype(BF16)


def _inproj_kernel(*refs, n_cast):
    x_ref, nw_ref, w_ref, wdt_ref = refs[:4]
    cast_in = refs[4:4 + n_cast]
    proj_ref, dt_ref = refs[4 + n_cast:6 + n_cast]
    cast_out, h_scr = refs[6 + n_cast:6 + 2 * n_cast], refs[6 + 2 * n_cast]
    member = pl.program_id(2)
    h_ref = h_scr.at[member]

    @pl.when(pl.program_id(1) == 0)
    def _():
        _rmsnorm_rows(x_ref, nw_ref, h_ref)
        real = lax.broadcasted_iota(jnp.int32, wdt_ref.shape, 0) < N_SSM_HEADS
        dt_ref[...] = _dot_nt(h_ref[...], jnp.where(real, wdt_ref[...], jnp.zeros((), BF16)))

    proj_ref[...] = _dot_nt(h_ref[...], w_ref[...]).astype(BF16)
    _cast_blocks(cast_in, cast_out)


def _inproj(x2d, norm_w, w_in_t, tm, tn, cast=()):
    t = x2d.shape[0]
    nj = PROJ_W // tn
    ni = t // tm
    grp = INPROJ_GROUP if ni % INPROJ_GROUP == 0 else 1
    tile = lambda g, m: g * grp + m
    rows_at_j0 = lambda g, j, m: (jnp.where(j == 0, tile(g, m), tile(g, grp - 1)), 0)
    c_in, c_out, c_shape = _cast_riders(cast, ni * nj, lambda g, j, m: (g * nj + j) * grp + m)
    return pl.pallas_call(
        functools.partial(_inproj_kernel, n_cast=len(cast)),
        grid=(ni // grp, nj, grp),
        in_specs=[
            pl.BlockSpec((tm, D_MODEL), rows_at_j0),
            pl.BlockSpec((1, D_MODEL), lambda g, j, m: (0, 0)),
            pl.BlockSpec((tn, D_MODEL), lambda g, j, m: (j, 0)),
            pl.BlockSpec((LANES, D_MODEL), lambda g, j, m: (PROJ_W // LANES, 0)),
        ] + c_in,
        out_specs=[
            pl.BlockSpec((tm, tn), lambda g, j, m: (tile(g, m), j)),
            pl.BlockSpec((tm, LANES), rows_at_j0),
        ] + c_out,
        out_shape=[
            jax.ShapeDtypeStruct((t, PROJ_W), BF16),
            jax.ShapeDtypeStruct((t, LANES), F32),
        ] + c_shape,
        scratch_shapes=[pltpu.VMEM((grp, tm, D_MODEL), BF16)],
        compiler_params=_params("arbitrary", "arbitrary", "arbitrary"),
        name="inproj",
    )(x2d, norm_w, w_in_t, w_in_t, *cast)


def _half_ones():
    r = lax.broadcasted_iota(jnp.int32, (LANES, LANES), 0) // HEAD_DIM
    c = lax.broadcasted_iota(jnp.int32, (LANES, LANES), 1) // HEAD_DIM
    return (r == c).astype(BF16)


def _head_norm(x, w_tile):
    ss = _dot((x * x).astype(BF16), _half_ones())
    return x * lax.rsqrt(ss * (1.0 / HEAD_DIM) + RMS_EPS) * w_tile


def _rope(x, cos_t, sin_lo, sin_hi):
    return (x * cos_t + pltpu.roll(x, LANES - ROPE_HALF, 1) * sin_lo
            + pltpu.roll(x, ROPE_HALF, 1) * sin_hi)


def _lane_is_a(shape):
    return lax.broadcasted_iota(jnp.int32, shape, 1) < HEAD_DIM


def _kv_tiles(k_win, v_win):
    is_a = _lane_is_a((CHUNK, LANES))
    ones_a = jnp.where(is_a, 1.0, 0.0).astype(F32)
    ones_b = 1.0 - ones_a
    kks, vvs = [], []
    for h in range(N_KV_HEADS):
        j, odd = h // 2, h % 2
        kt = k_win[:, LANES * j:LANES * (j + 1)]
        vt = v_win[:, LANES * j:LANES * (j + 1)]
        keep = jnp.logical_not(is_a) if odd else is_a
        k_own = jnp.where(keep, kt, 0.0)
        v_own = jnp.where(keep, vt, 0.0)
        k_sw = pltpu.roll(k_own, HEAD_DIM, 1)
        v_sw = pltpu.roll(v_own, HEAD_DIM, 1)
        k_a, k_b = (k_sw, k_own) if odd else (k_own, k_sw)
        v_a, v_b = (v_sw, v_own) if odd else (v_own, v_sw)
        kks.append(jnp.concatenate([k_a, k_b], axis=0).astype(BF16))
        vv = jnp.concatenate([jnp.concatenate([v_a, ones_a], axis=1),
                              jnp.concatenate([v_b, ones_b], axis=1)], axis=0)
        vvs.append(vv.astype(BF16))
    return kks, vvs


def _prep_q(q_ref, qw_ref, cos_ref, slo_ref, shi_ref, tq):
    q2 = jnp.concatenate([q_ref[:, LANES * i:LANES * (i + 1)].astype(F32) for i in range(Q_TILES)], axis=0)
    q2 = _head_norm(q2, qw_ref[...])
    cos_t, slo, shi = cos_ref[...], slo_ref[...], shi_ref[...]
    return [(_rope(q2[tq * i:tq * (i + 1)], cos_t, slo, shi) * ATTN_SCALE).astype(BF16) for i in range(Q_TILES)]


def _attend(q_tiles, kk_of, vv_of, valid_of, sink_ref, o_ref, tq):
    is_a = _lane_is_a((2 * tq, LANES))
    for h in range(N_KV_HEADS):
        qh = jnp.concatenate([q_tiles[2 * h], q_tiles[2 * h + 1]], axis=0)
        s = _dot_nt(qh, kk_of(h))
        sw = [jnp.where(valid_of(w), s[:, LANES * w:LANES * (w + 1)], NEG_INF) for w in range(3)]
        mt = jnp.maximum(jnp.maximum(sw[0], sw[1]), sw[2])
        m_a = jnp.max(jnp.where(is_a, mt, NEG_INF), axis=-1, keepdims=True)
        m_b = jnp.max(jnp.where(is_a, NEG_INF, mt), axis=-1, keepdims=True)
        sink = jnp.concatenate([jnp.broadcast_to(sink_ref[2 * h:2 * h + 1, :], (tq, LANES)),
                                jnp.broadcast_to(sink_ref[2 * h + 1:2 * h + 2, :], (tq, LANES))], axis=0)
        m = jnp.maximum(jnp.where(is_a, m_a, m_b), sink)
        e = jnp.concatenate([jnp.exp(x - m) for x in sw], axis=1).astype(BF16)
        oa = _dot(e, vv_of(h))
        o = oa[:, :LANES] / (oa[:, LANES:] + jnp.exp(sink - m))
        o_ref[:, LANES * 2 * h:LANES * (2 * h + 1)] = o[:tq].astype(o_ref.dtype)
        o_ref[:, LANES * (2 * h + 1):LANES * (2 * h + 2)] = o[tq:].astype(o_ref.dtype)


def _prep_k(k_raw, kw_ref, cos_ref, slo_ref, shi_ref, rows):
    k2 = jnp.concatenate([k_raw[:, LANES * j:LANES * (j + 1)].astype(F32) for j in range(KV_TILES)], axis=0)
    k2 = _head_norm(k2, kw_ref[...])
    cos_t, slo, shi = cos_ref[...], slo_ref[...], shi_ref[...]
    return jnp.concatenate([_rope(k2[rows * j:rows * (j + 1)], cos_t, slo, shi) for j in range(KV_TILES)], axis=1)


class _Part:
    def __init__(self, in_specs, operands, out_specs, out_shapes, scratch, body, init=None, final=None):
        self.in_specs, self.operands, self.out_specs, self.out_shapes = in_specs, operands, out_specs, out_shapes
        self.scratch, self.body, self.init, self.final = scratch, body, init, final


def _fused_kernel(*refs, parts):
    n_in = [len(p.in_specs) for p in parts]
    n_out = [len(p.out_specs) for p in parts]
    n_scr = [len(p.scratch) for p in parts]
    pos = 0
    groups = []
    for counts in (n_in, n_out, n_scr):
        groups.append([])
        for n in counts:
            groups[-1].append(refs[pos:pos + n])
            pos += n
    args = [(p, groups[0][i], groups[1][i], groups[2][i]) for i, p in enumerate(parts)]
    c = pl.program_id(1)
    mix_ref = groups[1][0][0]

    @pl.when(c == 0)
    def _():
        for p, ins, outs, scr in args:
            if p.init is not None:
                p.init(c, ins, outs, scr)

    for p, ins, outs, scr in args:
        p.body(c, ins, outs, scr, mix_ref)

    @pl.when(c == pl.num_programs(1) - 1)
    def _():
        for p, ins, outs, scr in args:
            if p.final is not None:
                p.final(c, ins, outs, scr)


def _fused_call(parts, grid, name):
    outs = pl.pallas_call(
        functools.partial(_fused_kernel, parts=parts),
        grid=grid,
        in_specs=[s for p in parts for s in p.in_specs],
        out_specs=[s for p in parts for s in p.out_specs],
        out_shape=[s for p in parts for s in p.out_shapes],
        scratch_shapes=[s for p in parts for s in p.scratch],
        compiler_params=_params("arbitrary", "arbitrary"),
        name=name,
    )(*[o for p in parts for o in p.operands])
    split, pos = [], 0
    for p in parts:
        split.append(outs[pos:pos + len(p.out_specs)])
        pos += len(p.out_specs)
    return split


def _attn_prompt_init(c, ins, outs, scr):
    for ring in scr:
        ring[...] = jnp.zeros_like(ring)


def _attn_prompt_body(c, ins, outs, scr, mix_ref):
    qkv_ref, cos_ref, slo_ref, shi_ref, qw_ref, kw_ref, sink_ref = ins
    _, nk_ref, nv_ref = outs
    kk_scr, vv_scr = scr
    slot = c % 3
    for s in range(qkv_ref.shape[0]):
        q_ref = qkv_ref.at[s, :, pl.ds(COL_Q, ATTN_W)]
        k_fin = _prep_k(qkv_ref[s, :, COL_K:COL_K + KV_W], kw_ref, cos_ref, slo_ref, shi_ref, CHUNK)
        v_cur = qkv_ref[s, :, COL_V:COL_V + KV_W].astype(F32)
        nk_ref[s] = k_fin
        nv_ref[s] = v_cur
        kks, vvs = _kv_tiles(k_fin, v_cur)
        for h in range(N_KV_HEADS):
            kk_scr[s, slot, h] = kks[h]
            vv_scr[s, slot, h] = vvs[h]

        q_tiles = _prep_q(q_ref, qw_ref, cos_ref, slo_ref, shi_ref, CHUNK)
        _attend(q_tiles,
                lambda h, s=s: jnp.concatenate([kk_scr[s, w, h] for w in range(3)], axis=0),
                lambda h, s=s: jnp.concatenate([vv_scr[s, w, h] for w in range(3)], axis=0),
                lambda w: (c - w + 3) % 3 <= c,
                sink_ref, mix_ref.at[s, :, pl.ds(0, ATTN_W)], CHUNK)


def _attn_prompt_part(proj3, rope_tabs, qw_tile, kw_tile, sink_tab, ns):
    batch, t, _ = proj3.shape
    nc = t // CHUNK
    keep = WINDOW // CHUNK
    const = lambda shape: pl.BlockSpec(shape, lambda g, c: (0,) * len(shape))
    keep_spec = pl.BlockSpec((ns, CHUNK, KV_W), lambda g, c: (g, jnp.maximum(c - (nc - keep), 0), 0))
    return _Part(
        in_specs=[
            pl.BlockSpec((ns, CHUNK, QKV_W), lambda g, c: (g, c, 0)),
            *[pl.BlockSpec((CHUNK, LANES), lambda g, c: (c, 0))] * 3,
            const((1, LANES)), const((1, LANES)), const((Q_TILES, LANES)),
        ],
        operands=[proj3, *rope_tabs, qw_tile, kw_tile, sink_tab],
        out_specs=[pl.BlockSpec((ns, CHUNK, MIX_W), lambda g, c: (g, c, 0)), keep_spec, keep_spec],
        out_shapes=[
            jax.ShapeDtypeStruct((batch, t, MIX_W), BF16),
            jax.ShapeDtypeStruct((batch, WINDOW, KV_W), F32),
            jax.ShapeDtypeStruct((batch, WINDOW, KV_W), F32),
        ],
        scratch=[
            pltpu.VMEM((ns, 3, N_KV_HEADS, 2 * CHUNK, LANES), BF16),
            pltpu.VMEM((ns, 3, N_KV_HEADS, 2 * CHUNK, 2 * LANES), BF16),
        ],
        body=_attn_prompt_body, init=_attn_prompt_init)


def _attn_sample_body(c, ins, outs, scr, mix_ref):
    qkv_ref, ck_ref, cv_ref, cos_ref, slo_ref, shi_ref, qw_ref, kw_ref, sink_ref = ins
    _, nk_ref, nv_ref = outs
    tq = qkv_ref.shape[1]
    new_valid = lax.broadcasted_iota(jnp.int32, (2 * tq, LANES), 1) % HEAD_DIM < tq
    pad = jnp.zeros((CHUNK - tq, KV_W), F32)
    for s in range(qkv_ref.shape[0]):
        q_ref = qkv_ref.at[s, :, pl.ds(COL_Q, ATTN_W)]
        k_fin = _prep_k(qkv_ref[s, :, COL_K:COL_K + KV_W], kw_ref, cos_ref, slo_ref, shi_ref, tq)
        v_cur = qkv_ref[s, :, COL_V:COL_V + KV_W].astype(F32)
        nk_ref[s, 0:WINDOW - tq, :] = ck_ref[s, tq:WINDOW, :]
        nv_ref[s, 0:WINDOW - tq, :] = cv_ref[s, tq:WINDOW, :]
        nk_ref[s, WINDOW - tq:WINDOW, :] = k_fin
        nv_ref[s, WINDOW - tq:WINDOW, :] = v_cur
        wins = [(ck_ref[s, :CHUNK, :], cv_ref[s, :CHUNK, :]),
                (ck_ref[s, CHUNK:, :], cv_ref[s, CHUNK:, :]),
                (jnp.concatenate([k_fin, pad], axis=0), jnp.concatenate([v_cur, pad], axis=0))]
        tiles = [_kv_tiles(kw, vw) for kw, vw in wins]
        q_tiles = _prep_q(q_ref, qw_ref, cos_ref, slo_ref, shi_ref, tq)
        _attend(q_tiles,
                lambda h, tiles=tiles: jnp.concatenate([tiles[w][0][h] for w in range(3)], axis=0),
                lambda h, tiles=tiles: jnp.concatenate([tiles[w][1][h] for w in range(3)], axis=0),
                lambda w: new_valid if w == 2 else True,
                sink_ref, mix_ref.at[s, :, pl.ds(0, ATTN_W)], tq)


def _attn_sample_part(proj3, cache_k, cache_v, rope_tabs, qw_tile, kw_tile, sink_tab, ns):
    batch, tq, _ = proj3.shape
    const = lambda shape: pl.BlockSpec(shape, lambda g, c: (0,) * len(shape))
    cache_spec = pl.BlockSpec((ns, WINDOW, KV_W), lambda g, c: (g, 0, 0))
    return _Part(
        in_specs=[
            pl.BlockSpec((ns, tq, QKV_W), lambda g, c: (g, 0, 0)),
            cache_spec, cache_spec,
            *[pl.BlockSpec((tq, LANES), lambda g, c: (0, 0))] * 3,
            const((1, LANES)), const((1, LANES)), const((Q_TILES, LANES)),
        ],
        operands=[proj3, cache_k, cache_v, *rope_tabs, qw_tile, kw_tile, sink_tab],
        out_specs=[pl.BlockSpec((ns, tq, MIX_W), lambda g, c: (g, 0, 0)), cache_spec, cache_spec],
        out_shapes=[
            jax.ShapeDtypeStruct((batch, tq, MIX_W), BF16),
            jax.ShapeDtypeStruct((batch, WINDOW, KV_W), F32),
            jax.ShapeDtypeStruct((batch, WINDOW, KV_W), F32),
        ],
        scratch=[], body=_attn_sample_body)


def _pad_rows(a, rows):
    if a.shape[0] == rows:
        return a
    return jnp.concatenate([a, jnp.zeros((rows - a.shape[0], a.shape[1]), a.dtype)], axis=0)


def _ssd_chunk(c, xs0_ref, xs1_ref, z0_ref, z1_ref, b_ref, c_ref, dt_ref, cst_ref, cw_ref, cb_ref, dtb_ref,
               alog_ref, dskip_ref, nw_ref, sel_ref, shift_ref, y_ref, ext_scr, st_scr, L):
    half = SSM_W // 2
    ext_scr[L:2 * L, 0:half] = xs0_ref[...]
    ext_scr[L:2 * L, half:SSM_W] = xs1_ref[...]
    ext_scr[L:2 * L, SSM_W:SSM_W + BC_W] = b_ref[...]
    ext_scr[L:2 * L, SSM_W + BC_W:CONV_CH] = c_ref[...]
    delayed = _dot(shift_ref[...], ext_scr[...])
    cur = ext_scr[L:2 * L, :].astype(F32)
    conv = cb_ref[...] + cur * cw_ref[CONV_W - 1:CONV_W, :]
    for i in range(CONV_W - 1):
        conv = conv + delayed[L * i:L * (i + 1)] * cw_ref[i:i + 1, :]
    s = cst_ref[...]
    w0, w1, w2 = cw_ref[0:1, :], cw_ref[1:2, :], cw_ref[2:3, :]
    head = jnp.concatenate([w0 * s[0:1] + w1 * s[1:2] + w2 * s[2:3], w0 * s[1:2] + w1 * s[2:3], w0 * s[2:3],
                            jnp.zeros((8 - (CONV_W - 1), CONV_CH), F32)], axis=0)
    conv = jnp.concatenate([conv[0:8] + jnp.where(c == 0, head, 0.0), conv[8:]], axis=0)
    conv = _silu(conv)
    xs = conv[:, :SSM_W]
    ext_scr[0:L, :] = ext_scr[L:2 * L, :]

    x_dt = dt_ref[...] + dtb_ref[...]
    dt = jnp.maximum(x_dt, 0.0) + jnp.log1p(jnp.exp(-jnp.abs(x_dt)))
    ad = dt * (-jnp.exp(alog_ref[...]))
    li = lax.broadcasted_iota(jnp.int32, (L, L), 0)
    si = lax.broadcasted_iota(jnp.int32, (L, L), 1)
    tril = (si <= li).astype(BF16)
    a_cum = _dot(jnp.concatenate([tril] * 3, axis=1), jnp.concatenate(_split3(ad), axis=0))
    sel = sel_ref[...]
    ex = _dot(jnp.concatenate(_split3(a_cum) + _split3(dt), axis=0), sel)
    col = ex[0:L] + ex[L:2 * L] + ex[2 * L:3 * L]
    dt_all = ex[3 * L:4 * L] + ex[4 * L:5 * L] + ex[5 * L:6 * L]
    lane_s = lax.broadcasted_iota(jnp.int32, (L, SSM_W), 1) % SSM_HEAD_DIM
    row_l = lax.broadcasted_iota(jnp.int32, (L, SSM_W), 0)
    row = jnp.sum(jnp.where(lane_s == row_l, col, 0.0), axis=0, keepdims=True)
    lmat = jnp.exp(jnp.where(lane_s <= row_l, col - row, NEG_INF))
    a_last = col[L - 1:L, :]
    exp_a = jnp.exp(col)
    decay = jnp.exp(a_last - col)
    exp_last = jnp.exp(a_last)

    xd = xs * dt_all
    xdd = xd * decay
    bi = lax.broadcasted_iota(jnp.int32, (GROUP_W, GROUP_W), 0) // SSM_HEAD_DIM
    bj = lax.broadcasted_iota(jnp.int32, (GROUP_W, GROUP_W), 1) // SSM_HEAD_DIM
    blockdiag = bi == bj

    for g in range(N_GROUPS):
        gs = slice(GROUP_W * g, GROUP_W * (g + 1))
        b_g = conv[:, SSM_W + D_STATE * g:SSM_W + D_STATE * (g + 1)]
        c_g = conv[:, SSM_W + BC_W + D_STATE * g:SSM_W + BC_W + D_STATE * (g + 1)].astype(BF16)
        b_pad = _pad_rows(b_g, CHUNK)
        cb = _dot_nt(c_g, jnp.concatenate([b_pad.astype(BF16)] * 4, axis=0))
        m_g = (cb * lmat[:, gs]).astype(BF16)
        xd_g = _pad_rows(xd[:, gs], CHUNK).astype(BF16)
        xd_bd = jnp.where(blockdiag, jnp.concatenate([xd_g] * 4, axis=0), jnp.zeros((), BF16))
        y_diag = _dot(m_g, xd_bd)
        st_g = st_scr[:, gs]
        y_off = _dot(c_g, st_g.astype(BF16)) * exp_a[:, gs]
        new_st = _dot(b_pad.T.astype(BF16), _pad_rows(xdd[:, gs], CHUNK).astype(BF16))
        st_scr[:, gs] = st_g * exp_last[:, gs] + new_st

        y = y_diag + y_off + xs[:, gs] * dskip_ref[:, gs]
        z_ref = z0_ref if g < N_GROUPS // 2 else z1_ref
        zs = slice(GROUP_W * (g % (N_GROUPS // 2)), GROUP_W * (g % (N_GROUPS // 2) + 1))
        gated = y * _silu(z_ref[:, zs].astype(F32))
        ms = jnp.mean(gated * gated, axis=-1, keepdims=True)
        y_ref[:, gs] = (gated * lax.rsqrt(ms + RMS_EPS) * nw_ref[:, gs]).astype(y_ref.dtype)


def _ssd_init(c, ins, outs, scr):
    st0_ref = ins[4]
    ext_scr, st_scr = scr
    ns, L = ext_scr.shape[0], ext_scr.shape[1] // 2
    for s in range(ns):
        ext_scr[s, 0:L, :] = jnp.zeros((L, CONV_CH), BF16)
        st_scr[s] = st0_ref[s].T


def _ssd_body(c, ins, outs, scr, mix_ref):
    (lo_ref, hi_ref, dt_ref, cst_ref, _, cw_ref, cb_ref, dtb_ref, alog_ref, dskip_ref, nw_ref, sel_ref,
     shift_ref) = ins
    ext_scr, st_scr = scr
    ns, L = dt_ref.shape[0], dt_ref.shape[1]
    half = SSM_W // 2
    for s in range(ns):
        view = lambda ref, start, s=s: ref.at[s, :, pl.ds(start, half)]
        _ssd_chunk(c, view(lo_ref, 0), view(lo_ref, half), view(lo_ref, 2 * half), view(hi_ref, 0),
                   view(hi_ref, half), view(hi_ref, 2 * half),
                   dt_ref.at[s], cst_ref.at[s], cw_ref, cb_ref, dtb_ref, alog_ref, dskip_ref, nw_ref,
                   sel_ref, shift_ref, mix_ref.at[s, :, pl.ds(ATTN_W, SSM_W)], ext_scr.at[s], st_scr.at[s], L)


def _ssd_final(c, ins, outs, scr):
    ncv_ref, nst_ref = outs
    ext_scr, st_scr = scr
    ns, L = ext_scr.shape[0], ext_scr.shape[1] // 2
    for s in range(ns):
        tail = ext_scr[s, L - BF16_ROWS:L, :].astype(F32)
        ncv_ref[s] = tail[BF16_ROWS - (CONV_W - 1):BF16_ROWS, :]
        nst_ref[s] = st_scr[s].T


def _shift_matrix(L):
    r = jnp.arange(3 * L)
    src = L + r % L - (CONV_W - 1) + r // L
    return (jnp.arange(2 * L)[None, :] == src[:, None]).astype(BF16)


def _ssd_part(proj3, dt3, conv_state, ssm_state, conv_w, conv_b, dtb, alog, dskip_all, ssm_nw, sel, L, ns):
    batch, t, _ = proj3.shape
    const = lambda shape: pl.BlockSpec(shape, lambda g, c: (0,) * len(shape))
    per_g = lambda shape: pl.BlockSpec((ns,) + shape, lambda g, c: (g,) + (0,) * len(shape))
    third = lambda k: pl.BlockSpec((ns, L, QKV_W), lambda g, c: (g, c, k))
    hp = N_SSM_HEADS * SSM_HEAD_DIM
    assert COL_XS == QKV_W and PROJ_W == 3 * QKV_W
    return _Part(
        in_specs=[
            third(1), third(2),
            pl.BlockSpec((ns, L, LANES), lambda g, c: (g, c, 0)),
            per_g((CONV_W - 1, CONV_CH)),
            per_g((hp, D_STATE)),
            const((CONV_W, CONV_CH)), const((1, CONV_CH)), const((1, LANES)), const((1, LANES)),
            const((1, SSM_W)), const((1, SSM_W)), const((LANES, SSM_W)), const((3 * L, 2 * L)),
        ],
        out_specs=[
            per_g((CONV_W - 1, CONV_CH)),
            per_g((hp, D_STATE)),
        ],
        out_shapes=[
            jax.ShapeDtypeStruct((batch, CONV_W - 1, CONV_CH), F32),
            jax.ShapeDtypeStruct((batch, hp, D_STATE), F32),
        ],
        scratch=[
            pltpu.VMEM((ns, 2 * L, CONV_CH), BF16),
            pltpu.VMEM((ns, D_STATE, hp), F32),
        ],
        operands=[proj3, proj3, dt3, conv_state, ssm_state, conv_w, conv_b, dtb, alog, dskip_all, ssm_nw, sel,
                  _shift_matrix(L)],
        body=_ssd_body, init=_ssd_init, final=_ssd_final)


def _outproj_kernel(*refs, n_cast):
    mix_ref, w_ref, x_ref = refs[:3]
    cast_in, o_ref, cast_out = refs[3:3 + n_cast], refs[3 + n_cast], refs[4 + n_cast:]
    o_ref[...] = x_ref[...] + _dot(mix_ref[...], w_ref[...])
    _cast_blocks(cast_in, cast_out)


def _outproj(mix, w_out, x2d, tm, tn, cast=()):
    t = x2d.shape[0]
    ni = t // tm
    c_in, c_out, c_shape = _cast_riders(cast, (D_MODEL // tn) * ni, lambda j, i: j * ni + i)
    return pl.pallas_call(
        functools.partial(_outproj_kernel, n_cast=len(cast)),
        grid=(D_MODEL // tn, ni),
        in_specs=[
            pl.BlockSpec((tm, MIX_W), lambda j, i: (i, 0)),
            pl.BlockSpec((MIX_W, tn), lambda j, i: (0, j)),
            pl.BlockSpec((tm, tn), lambda j, i: (i, j)),
        ] + c_in,
        out_specs=[pl.BlockSpec((tm, tn), lambda j, i: (i, j))] + c_out,
        out_shape=[jax.ShapeDtypeStruct((t, D_MODEL), F32)] + c_shape,
        compiler_params=_params("arbitrary", "arbitrary"),
        name="outproj",
    )(mix, w_out, x2d, *cast)


def _ffn_kernel(x_ref, nw_ref, wu_ref, wd_ref, o_ref, h_scr, *, n_split):
    @pl.when(pl.program_id(1) == 0)
    def _():
        _rmsnorm_rows(x_ref, nw_ref, h_scr, copy_ref=o_ref)

    u = jnp.maximum(_dot(h_scr[...], wu_ref[...]), 0.0)
    u = (u * u).astype(BF16)
    wn = D_MODEL // n_split
    for n in range(n_split):
        o_ref[:, wn * n:wn * (n + 1)] += _dot(u, wd_ref[:, wn * n:wn * (n + 1)])


def _ffn(x2d, norm_w, w_up, w_down, tm, tf):
    t = x2d.shape[0]
    return pl.pallas_call(
        functools.partial(_ffn_kernel, n_split=4),
        grid=(t // tm, FFN_HIDDEN // tf),
        in_specs=[
            pl.BlockSpec((tm, D_MODEL), lambda i, f: (i, 0)),
            pl.BlockSpec((1, D_MODEL), lambda i, f: (0, 0)),
            pl.BlockSpec((D_MODEL, tf), lambda i, f: (0, f)),
            pl.BlockSpec((tf, D_MODEL), lambda i, f: (f, 0)),
        ],
        out_specs=pl.BlockSpec((tm, D_MODEL), lambda i, f: (i, 0)),
        out_shape=jax.ShapeDtypeStruct((t, D_MODEL), F32),
        scratch_shapes=[pltpu.VMEM((tm, D_MODEL), BF16)],
        compiler_params=_params("arbitrary", "arbitrary"),
        name="ffn",
    )(x2d, norm_w, w_up, w_down)


def _rope_tables(pos):
    inv_freq = ROPE_THETA ** (-jnp.arange(ROPE_HALF, dtype=F32) / ROPE_HALF)
    lane = jnp.arange(LANES) % HEAD_DIM
    freq = jnp.where(lane < ROPE_DIM, inv_freq[lane % ROPE_HALF], 0.0)
    ang = pos.astype(F32)[:, None] * freq[None, :]
    cos, sin = jnp.cos(ang), jnp.sin(ang)
    return [cos, jnp.where(lane < ROPE_HALF, -sin, 0.0), jnp.where(lane >= ROPE_HALF, sin, 0.0)]


def _layer_params(norm_mix_w, w_in, q_norm_w, k_norm_w, attn_sinks, conv_w, conv_b, dt_bias, a_log, d_skip,
                  ssm_norm_w, w_out, norm_ffn_w, w_up, w_down):
    pad_h = lambda a: jnp.pad(a.astype(F32), (0, LANES - N_SSM_HEADS))[None, :]
    sel = (jnp.arange(LANES)[:, None] == (jnp.arange(SSM_W) // SSM_HEAD_DIM)[None, :]).astype(BF16)
    return dict(
        norm_mix=norm_mix_w.astype(F32)[None, :],
        qw=jnp.tile(q_norm_w.astype(F32), 2)[None, :], kw=jnp.tile(k_norm_w.astype(F32), 2)[None, :],
        sinks=jnp.repeat(attn_sinks.astype(F32), HEAD_DIM).reshape(Q_TILES, LANES),
        conv_w=conv_w.astype(F32), conv_b=conv_b.astype(F32)[None, :],
        dtb=pad_h(dt_bias), alog=pad_h(a_log),
        dskip=jnp.repeat(d_skip.astype(F32), SSM_HEAD_DIM)[None, :],
        ssm_nw=ssm_norm_w.astype(F32)[None, :], sel=sel,
        norm_ffn=norm_ffn_w.astype(F32)[None, :],
    )


def _stream_back(x, proj, dt_raw, pos, cache, conv_state, ssm_state, p, w_out, w_up, w_down):
    batch, t, _ = x.shape
    ns = max(n for n in STREAMS_PER_STEP if batch % n == 0)
    x2d = x.reshape(batch * t, D_MODEL)
    proj3 = proj.reshape(batch, t, PROJ_W)
    tabs = _rope_tables(pos)
    if cache is None:
        attn_part = _attn_prompt_part(proj3, tabs, p["qw"], p["kw"], p["sinks"], ns)
        L = CHUNK
    else:
        past_k, past_v = cache
        attn_part = _attn_sample_part(proj3, past_k.reshape(batch, WINDOW, KV_W),
                                      past_v.reshape(batch, WINDOW, KV_W), tabs, p["qw"], p["kw"], p["sinks"], ns)
        L = min(CHUNK, t)
    hp = N_SSM_HEADS * SSM_HEAD_DIM
    ssd_part = _ssd_part(proj3, dt_raw.reshape(batch, t, LANES), conv_state,
                         ssm_state.reshape(batch, hp, D_STATE), p["conv_w"], p["conv_b"], p["dtb"], p["alog"],
                         p["dskip"], p["ssm_nw"], p["sel"], L, ns)
    parts = [attn_part, ssd_part]
    grid = (batch // ns, t // L)
    if w_out.dtype != BF16:
        c_in, c_out, c_shape = _cast_riders((w_out,), grid[0] * grid[1], lambda g, c: g * grid[1] + c)
        parts.append(_Part(c_in, [w_out], c_out, c_shape, [],
                           body=lambda c, ins, outs, scr, mix_ref: _cast_blocks(ins, outs)))
    (mix, new_k, new_v), (new_conv, new_ssm), *rest = _fused_call(parts, grid, "mixers")
    if rest:
        (w_out,) = rest[0]
    late = () if w_down.dtype == BF16 else (w_down,)
    x1, *cast = _outproj(mix.reshape(batch * t, MIX_W), w_out, x2d, TILE_M, OUTPROJ_TN, cast=late)
    if late:
        w_down = cast.pop()
    out = _ffn(x1, p["norm_ffn"], w_up, w_down, TILE_M, FFN_TF)
    return (out.reshape(batch, t, D_MODEL),
            new_k.reshape(batch, WINDOW, N_KV_HEADS, HEAD_DIM), new_v.reshape(batch, WINDOW, N_KV_HEADS, HEAD_DIM),
            new_conv, new_ssm.reshape(batch, N_SSM_HEADS, SSM_HEAD_DIM, D_STATE), (w_out, w_down))


def kernel(x_prompt, x_sample, cache_k, cache_v, state_conv, state_ssm, norm_mix_w, w_in, q_norm_w, k_norm_w,
           attn_sinks, conv_w, conv_b, dt_bias, a_log, d_skip, ssm_norm_w, w_out, norm_ffn_w, w_up, w_down):
    depth = w_in.shape[0]
    b_p, t_p, _ = x_prompt.shape
    b_s, t_s, _ = x_sample.shape
    assert cache_k.shape[2] == WINDOW and t_p % CHUNK == 0 and t_p >= WINDOW
    assert t_s <= CHUNK and t_s % BF16_ROWS == 0
    pos_p = jnp.arange(t_p, dtype=jnp.int32)
    pos_s = PAST_LEN + jnp.arange(t_s, dtype=jnp.int32)
    hp, hs = x_prompt, x_sample
    outs_p, outs_s = [], []
    for layer in range(depth):
        p = _layer_params(norm_mix_w[layer], w_in[layer], q_norm_w[layer], k_norm_w[layer], attn_sinks[layer],
                          conv_w[layer], conv_b[layer], dt_bias[layer], a_log[layer], d_skip[layer],
                          ssm_norm_w[layer], w_out[layer], norm_ffn_w[layer], w_up[layer], w_down[layer])
        conv0 = jnp.zeros((b_p, CONV_W - 1, CONV_CH), F32)
        ssm0 = jnp.zeros((b_p, N_SSM_HEADS, SSM_HEAD_DIM, D_STATE), F32)
        xs2d, xp2d = hs.reshape(b_s * t_s, D_MODEL), hp.reshape(b_p * t_p, D_MODEL)
        w_in_t = w_in[layer].T.astype(BF16)
        proj_p, dt_p, w_up16 = _inproj(xp2d, p["norm_mix"], w_in_t, TILE_M, INPROJ_TN, cast=(w_up[layer],))
        proj_s, dt_s = _inproj(xs2d, p["norm_mix"], w_in_t, TILE_M, INPROJ_TN)
        hp, *rest_p, (w_out16, w_down16) = _stream_back(hp, proj_p, dt_p, pos_p, None, conv0, ssm0, p,
                                                        w_out[layer], w_up16, w_down[layer])
        hs, *rest_s, _ = _stream_back(hs, proj_s, dt_s, pos_s, (cache_k[layer], cache_v[layer]),
                                      state_conv[layer], state_ssm[layer], p, w_out16, w_up16, w_down16)
        outs_p.append(rest_p)
        outs_s.append(rest_s)
    stack = lambda outs, i: jnp.stack([o[i] for o in outs])
    return (hp, hs, stack(outs_p, 0), stack(outs_p, 1), stack(outs_p, 2), stack(outs_p, 3),
            stack(outs_s, 0), stack(outs_s, 1), stack(outs_s, 2), stack(outs_s, 3))
```

```python
import functools

import jax
import jax.numpy as jnp
from jax import lax
from jax.experimental import pallas as pl
from jax.experimental.pallas import tpu as pltpu

F32 = jnp.float32
BF16 = jnp.bfloat16

D_MODEL = 4096
HEAD_DIM = 64
N_Q_HEADS = 32
N_KV_HEADS = 8
ATTN_W = N_Q_HEADS * HEAD_DIM
KV_W = N_KV_HEADS * HEAD_DIM
ROPE_DIM = 16
ROPE_HALF = ROPE_DIM // 2
ROPE_THETA = 500000.0
ATTN_SCALE = HEAD_DIM ** -0.5
WINDOW = 128
CHUNK = 64
PAST_LEN = 1024
SSM_W = 2048
N_SSM_HEADS = 32
SSM_HEAD_DIM = 64
N_GROUPS = 8
D_STATE = 128
BC_W = N_GROUPS * D_STATE
GROUP_W = SSM_W // N_GROUPS
CONV_W = 4
CONV_CH = SSM_W + 2 * BC_W
FFN_HIDDEN = 4 * D_MODEL
RMS_EPS = 1e-6
NEG_INF = -1e30
LANES = 128
BF16_ROWS = 16
Q_TILES = ATTN_W // LANES
KV_TILES = KV_W // LANES
PROJ_W = ATTN_W + 2 * KV_W + 2 * SSM_W + 2 * BC_W
COL_Q, COL_K, COL_V, COL_XS, COL_Z, COL_B, COL_C = 0, 2048, 2560, 3072, 5120, 7168, 8192
QKV_W = ATTN_W + 2 * KV_W
MIX_W = ATTN_W + SSM_W
VMEM_LIMIT = 63 * 1024 * 1024
STREAMS_PER_STEP = (1, 2, 4)
TILE_M = 512
INPROJ_TN = 1536
INPROJ_GROUP = 2
OUTPROJ_TN = 1024
FFN_TF = 512


def _params(*sem):
    return pltpu.CompilerParams(dimension_semantics=sem, vmem_limit_bytes=VMEM_LIMIT)


def _dot(a, b):
    return jnp.dot(a, b, preferred_element_type=F32)


def _dot_nt(a, b):
    return lax.dot_general(a, b, (((1,), (1,)), ((), ())), preferred_element_type=F32)


def _split3(x):
    a = x.astype(BF16)
    r = x - a.astype(F32)
    b = r.astype(BF16)
    c = (r - b.astype(F32)).astype(BF16)
    return a, b, c


def _silu(x):
    return x / (1.0 + jnp.exp(-x))


NORM_ROWS = 16
NORM_UNROLL = 8


def _rmsnorm_rows(x_ref, nw_ref, h_scr, copy_ref=None):
    def body(i, carry):
        r = pl.ds(pl.multiple_of(i * NORM_ROWS, NORM_ROWS), NORM_ROWS)
        x = x_ref[r, :]
        ms = jnp.mean(x * x, axis=-1, keepdims=True)
        h_scr[r, :] = (x * lax.rsqrt(ms + RMS_EPS) * nw_ref[...]).astype(BF16)
        if copy_ref is not None:
            copy_ref[r, :] = x
        return carry

    lax.fori_loop(0, x_ref.shape[0] // NORM_ROWS, body, 0, unroll=NORM_UNROLL)


def _cast_riders(arrays, steps, linear_step):
    in_specs, out_specs, out_shapes = [], [], []
    for a in arrays:
        rows, cols = a.shape
        nb = max(n for n in range(1, steps + 1) if rows % n == 0 and (rows // n) % BF16_ROWS == 0)
        spec = pl.BlockSpec((rows // nb, cols), lambda *g, nb=nb: (jnp.minimum(linear_step(*g), nb - 1), 0))
        in_specs.append(spec)
        out_specs.append(spec)
        out_shapes.append(jax.ShapeDtypeStruct(a.shape, BF16))
    return in_specs, out_specs, out_shapes


def _cast_blocks(src_refs, dst_refs):
    for src, dst in zip(src_refs, dst_refs):
        dst[...] = src[...].astype(BF16)


def _inproj_kernel(*refs, n_cast):
    x_ref, nw_ref, w_ref, wdt_ref = refs[:4]
    cast_in = refs[4:4 + n_cast]
    proj_ref, dt_ref = refs[4 + n_cast:6 + n_cast]
    cast_out, h_scr = refs[6 + n_cast:6 + 2 * n_cast], refs[6 + 2 * n_cast]
    member = pl.program_id(2)
    h_ref = h_scr.at[member]

    @pl.when(pl.program_id(1) == 0)
    def _():
        _rmsnorm_rows(x_ref, nw_ref, h_ref)
        real = lax.broadcasted_iota(jnp.int32, wdt_ref.shape, 0) < N_SSM_HEADS
        dt_ref[...] = _dot_nt(h_ref[...], jnp.where(real, wdt_ref[...], jnp.zeros((), BF16)))

    proj_ref[...] = _dot_nt(h_ref[...], w_ref[...]).astype(BF16)
    _cast_blocks(cast_in, cast_out)


def _inproj(x2d, norm_w, w_in_t, tm, tn, cast=()):
    t = x2d.shape[0]
    nj = PROJ_W // tn
    ni = t // tm
    grp = INPROJ_GROUP if ni % INPROJ_GROUP == 0 else 1
    tile = lambda g, m: g * grp + m
    rows_at_j0 = lambda g, j, m: (jnp.where(j == 0, tile(g, m), tile(g, grp - 1)), 0)
    c_in, c_out, c_shape = _cast_riders(cast, ni * nj, lambda g, j, m: (g * nj + j) * grp + m)
    return pl.pallas_call(
        functools.partial(_inproj_kernel, n_cast=len(cast)),
        grid=(ni // grp, nj, grp),
        in_specs=[
            pl.BlockSpec((tm, D_MODEL), rows_at_j0),
            pl.BlockSpec((1, D_MODEL), lambda g, j, m: (0, 0)),
            pl.BlockSpec((tn, D_MODEL), lambda g, j, m: (j, 0)),
            pl.BlockSpec((LANES, D_MODEL), lambda g, j, m: (PROJ_W // LANES, 0)),
        ] + c_in,
        out_specs=[
            pl.BlockSpec((tm, tn), lambda g, j, m: (tile(g, m), j)),
            pl.BlockSpec((tm, LANES), rows_at_j0),
        ] + c_out,
        out_shape=[
            jax.ShapeDtypeStruct((t, PROJ_W), BF16),
            jax.ShapeDtypeStruct((t, LANES), F32),
        ] + c_shape,
        scratch_shapes=[pltpu.VMEM((grp, tm, D_MODEL), BF16)],
        compiler_params=_params("arbitrary", "arbitrary", "arbitrary"),
        name="inproj",
    )(x2d, norm_w, w_in_t, w_in_t, *cast)


def _half_ones():
    r = lax.broadcasted_iota(jnp.int32, (LANES, LANES), 0) // HEAD_DIM
    c = lax.broadcasted_iota(jnp.int32, (LANES, LANES), 1) // HEAD_DIM
    return (r == c).astype(BF16)


def _head_norm(x, w_tile):
    ss = _dot((x * x).astype(BF16), _half_ones())
    return x * lax.rsqrt(ss * (1.0 / HEAD_DIM) + RMS_EPS) * w_tile


def _rope(x, cos_t, sin_lo, sin_hi):
    return (x * cos_t + pltpu.roll(x, LANES - ROPE_HALF, 1) * sin_lo
            + pltpu.roll(x, ROPE_HALF, 1) * sin_hi)


def _lane_is_a(shape):
    return lax.broadcasted_iota(jnp.int32, shape, 1) < HEAD_DIM


def _kv_tiles(k_win, v_win):
    is_a = _lane_is_a((CHUNK, LANES))
    ones_a = jnp.where(is_a, 1.0, 0.0).astype(F32)
    ones_b = 1.0 - ones_a
    kks, vvs = [], []
    for h in range(N_KV_HEADS):
        j, odd = h // 2, h % 2
        kt = k_win[:, LANES * j:LANES * (j + 1)]
        vt = v_win[:, LANES * j:LANES * (j + 1)]
        keep = jnp.logical_not(is_a) if odd else is_a
        k_own = jnp.where(keep, kt, 0.0)
        v_own = jnp.where(keep, vt, 0.0)
        k_sw = pltpu.roll(k_own, HEAD_DIM, 1)
        v_sw = pltpu.roll(v_own, HEAD_DIM, 1)
        k_a, k_b = (k_sw, k_own) if odd else (k_own, k_sw)
        v_a, v_b = (v_sw, v_own) if odd else (v_own, v_sw)
        kks.append(jnp.concatenate([k_a, k_b], axis=0).astype(BF16))
        vv = jnp.concatenate([jnp.concatenate([v_a, ones_a], axis=1),
                              jnp.concatenate([v_b, ones_b], axis=1)], axis=0)
        vvs.append(vv.astype(BF16))
    return kks, vvs


def _prep_q(q_ref, qw_ref, cos_ref, slo_ref, shi_ref, tq):
    q2 = jnp.concatenate([q_ref[:, LANES * i:LANES * (i + 1)].astype(F32) for i in range(Q_TILES)], axis=0)
    q2 = _head_norm(q2, qw_ref[...])
    cos_t, slo, shi = cos_ref[...], slo_ref[...], shi_ref[...]
    return [(_rope(q2[tq * i:tq * (i + 1)], cos_t, slo, shi) * ATTN_SCALE).astype(BF16) for i in range(Q_TILES)]


def _attend(q_tiles, kk_of, vv_of, valid_of, sink_ref, o_ref, tq):
    is_a = _lane_is_a((2 * tq, LANES))
    for h in range(N_KV_HEADS):
        qh = jnp.concatenate([q_tiles[2 * h], q_tiles[2 * h + 1]], axis=0)
        s = _dot_nt(qh, kk_of(h))
        sw = [jnp.where(valid_of(w), s[:, LANES * w:LANES * (w + 1)], NEG_INF) for w in range(3)]
        mt = jnp.maximum(jnp.maximum(sw[0], sw[1]), sw[2])
        m_a = jnp.max(jnp.where(is_a, mt, NEG_INF), axis=-1, keepdims=True)
        m_b = jnp.max(jnp.where(is_a, NEG_INF, mt), axis=-1, keepdims=True)
        sink = jnp.concatenate([jnp.broadcast_to(sink_ref[2 * h:2 * h + 1, :], (tq, LANES)),
                                jnp.broadcast_to(sink_ref[2 * h + 1:2 * h + 2, :], (tq, LANES))], axis=0)
        m = jnp.maximum(jnp.where(is_a, m_a, m_b), sink)
        e = jnp.concatenate([jnp.exp(x - m) for x in sw], axis=1).astype(BF16)
        oa = _dot(e, vv_of(h))
        o = oa[:, :LANES] / (oa[:, LANES:] + jnp.exp(sink - m))
        o_ref[:, LANES * 2 * h:LANES * (2 * h + 1)] = o[:tq].astype(o_ref.dtype)
        o_ref[:, LANES * (2 * h + 1):LANES * (2 * h + 2)] = o[tq:].astype(o_ref.dtype)


def _prep_k(k_raw, kw_ref, cos_ref, slo_ref, shi_ref, rows):
    k2 = jnp.concatenate([k_raw[:, LANES * j:LANES * (j + 1)].astype(F32) for j in range(KV_TILES)], axis=0)
    k2 = _head_norm(k2, kw_ref[...])
    cos_t, slo, shi = cos_ref[...], slo_ref[...], shi_ref[...]
    return jnp.concatenate([_rope(k2[rows * j:rows * (j + 1)], cos_t, slo, shi) for j in range(KV_TILES)], axis=1)


class _Part:
    def __init__(self, in_specs, operands, out_specs, out_shapes, scratch, body, init=None, final=None):
        self.in_specs, self.operands, self.out_specs, self.out_shapes = in_specs, operands, out_specs, out_shapes
        self.scratch, self.body, self.init, self.final = scratch, body, init, final


def _fused_kernel(*refs, parts):
    n_in = [len(p.in_specs) for p in parts]
    n_out = [len(p.out_specs) for p in parts]
    n_scr = [len(p.scratch) for p in parts]
    pos = 0
    groups = []
    for counts in (n_in, n_out, n_scr):
        groups.append([])
        for n in counts:
            groups[-1].append(refs[pos:pos + n])
            pos += n
    args = [(p, groups[0][i], groups[1][i], groups[2][i]) for i, p in enumerate(parts)]
    c = pl.program_id(1)
    mix_ref = groups[1][0][0]

    @pl.when(c == 0)
    def _():
        for p, ins, outs, scr in args:
            if p.init is not None:
                p.init(c, ins, outs, scr)

    for p, ins, outs, scr in args:
        p.body(c, ins, outs, scr, mix_ref)

    @pl.when(c == pl.num_programs(1) - 1)
    def _():
        for p, ins, outs, scr in args:
            if p.final is not None:
                p.final(c, ins, outs, scr)


def _fused_call(parts, grid, name):
    outs = pl.pallas_call(
        functools.partial(_fused_kernel, parts=parts),
        grid=grid,
        in_specs=[s for p in parts for s in p.in_specs],
        out_specs=[s for p in parts for s in p.out_specs],
        out_shape=[s for p in parts for s in p.out_shapes],
        scratch_shapes=[s for p in parts for s in p.scratch],
        compiler_params=_params("arbitrary", "arbitrary"),
        name=name,
    )(*[o for p in parts for o in p.operands])
    split, pos = [], 0
    for p in parts:
        split.append(outs[pos:pos + len(p.out_specs)])
        pos += len(p.out_specs)
    return split


def _attn_prompt_init(c, ins, outs, scr):
    for ring in scr:
        ring[...] = jnp.zeros_like(ring)


def _attn_prompt_body(c, ins, outs, scr, mix_ref):
    qkv_ref, cos_ref, slo_ref, shi_ref, qw_ref, kw_ref, sink_ref = ins
    _, nk_ref, nv_ref = outs
    kk_scr, vv_scr = scr
    slot = c % 3
    for s in range(qkv_ref.shape[0]):
        q_ref = qkv_ref.at[s, :, pl.ds(COL_Q, ATTN_W)]
        k_fin = _prep_k(qkv_ref[s, :, COL_K:COL_K + KV_W], kw_ref, cos_ref, slo_ref, shi_ref, CHUNK)
        v_cur = qkv_ref[s, :, COL_V:COL_V + KV_W].astype(F32)
        nk_ref[s] = k_fin
        nv_ref[s] = v_cur
        kks, vvs = _kv_tiles(k_fin, v_cur)
        for h in range(N_KV_HEADS):
            kk_scr[s, slot, h] = kks[h]
            vv_scr[s, slot, h] = vvs[h]

        q_tiles = _prep_q(q_ref, qw_ref, cos_ref, slo_ref, shi_ref, CHUNK)
        _attend(q_tiles,
                lambda h, s=s: jnp.concatenate([kk_scr[s, w, h] for w in range(3)], axis=0),
                lambda h, s=s: jnp.concatenate([vv_scr[s, w, h] for w in range(3)], axis=0),
                lambda w: (c - w + 3) % 3 <= c,
                sink_ref, mix_ref.at[s, :, pl.ds(0, ATTN_W)], CHUNK)


def _attn_prompt_part(proj3, rope_tabs, qw_tile, kw_tile, sink_tab, ns):
    batch, t, _ = proj3.shape
    nc = t // CHUNK
    keep = WINDOW // CHUNK
    const = lambda shape: pl.BlockSpec(shape, lambda g, c: (0,) * len(shape))
    keep_spec = pl.BlockSpec((ns, CHUNK, KV_W), lambda g, c: (g, jnp.maximum(c - (nc - keep), 0), 0))
    return _Part(
        in_specs=[
            pl.BlockSpec((ns, CHUNK, QKV_W), lambda g, c: (g, c, 0)),
            *[pl.BlockSpec((CHUNK, LANES), lambda g, c: (c, 0))] * 3,
            const((1, LANES)), const((1, LANES)), const((Q_TILES, LANES)),
        ],
        operands=[proj3, *rope_tabs, qw_tile, kw_tile, sink_tab],
        out_specs=[pl.BlockSpec((ns, CHUNK, MIX_W), lambda g, c: (g, c, 0)), keep_spec, keep_spec],
        out_shapes=[
            jax.ShapeDtypeStruct((batch, t, MIX_W), BF16),
            jax.ShapeDtypeStruct((batch, WINDOW, KV_W), F32),
            jax.ShapeDtypeStruct((batch, WINDOW, KV_W), F32),
        ],
        scratch=[
            pltpu.VMEM((ns, 3, N_KV_HEADS, 2 * CHUNK, LANES), BF16),
            pltpu.VMEM((ns, 3, N_KV_HEADS, 2 * CHUNK, 2 * LANES), BF16),
        ],
        body=_attn_prompt_body, init=_attn_prompt_init)


def _attn_sample_body(c, ins, outs, scr, mix_ref):
    qkv_ref, ck_ref, cv_ref, cos_ref, slo_ref, shi_ref, qw_ref, kw_ref, sink_ref = ins
    _, nk_ref, nv_ref = outs
    tq = qkv_ref.shape[1]
    new_valid = lax.broadcasted_iota(jnp.int32, (2 * tq, LANES), 1) % HEAD_DIM < tq
    pad = jnp.zeros((CHUNK - tq, KV_W), F32)
    for s in range(qkv_ref.shape[0]):
        q_ref = qkv_ref.at[s, :, pl.ds(COL_Q, ATTN_W)]
        k_fin = _prep_k(qkv_ref[s, :, COL_K:COL_K + KV_W], kw_ref, cos_ref, slo_ref, shi_ref, tq)
        v_cur = qkv_ref[s, :, COL_V:COL_V + KV_W].astype(F32)
        nk_ref[s, 0:WINDOW - tq, :] = ck_ref[s, tq:WINDOW, :]
        nv_ref[s, 0:WINDOW - tq, :] = cv_ref[s, tq:WINDOW, :]
        nk_ref[s, WINDOW - tq:WINDOW, :] = k_fin
        nv_ref[s, WINDOW - tq:WINDOW, :] = v_cur
        wins = [(ck_ref[s, :CHUNK, :], cv_ref[s, :CHUNK, :]),
                (ck_ref[s, CHUNK:, :], cv_ref[s, CHUNK:, :]),
                (jnp.concatenate([k_fin, pad], axis=0), jnp.concatenate([v_cur, pad], axis=0))]
        tiles = [_kv_tiles(kw, vw) for kw, vw in wins]
        q_tiles = _prep_q(q_ref, qw_ref, cos_ref, slo_ref, shi_ref, tq)
        _attend(q_tiles,
                lambda h, tiles=tiles: jnp.concatenate([tiles[w][0][h] for w in range(3)], axis=0),
                lambda h, tiles=tiles: jnp.concatenate([tiles[w][1][h] for w in range(3)], axis=0),
                lambda w: new_valid if w == 2 else True,
                sink_ref, mix_ref.at[s, :, pl.ds(0, ATTN_W)], tq)


def _attn_sample_part(proj3, cache_k, cache_v, rope_tabs, qw_tile, kw_tile, sink_tab, ns):
    batch, tq, _ = proj3.shape
    const = lambda shape: pl.BlockSpec(shape, lambda g, c: (0,) * len(shape))
    cache_spec = pl.BlockSpec((ns, WINDOW, KV_W), lambda g, c: (g, 0, 0))
    return _Part(
        in_specs=[
            pl.BlockSpec((ns, tq, QKV_W), lambda g, c: (g, 0, 0)),
            cache_spec, cache_spec,
            *[pl.BlockSpec((tq, LANES), lambda g, c: (0, 0))] * 3,
            const((1, LANES)), const((1, LANES)), const((Q_TILES, LANES)),
        ],
        operands=[proj3, cache_k, cache_v, *rope_tabs, qw_tile, kw_tile, sink_tab],
        out_specs=[pl.BlockSpec((ns, tq, MIX_W), lambda g, c: (g, 0, 0)), cache_spec, cache_spec],
        out_shapes=[
            jax.ShapeDtypeStruct((batch, tq, MIX_W), BF16),
            jax.ShapeDtypeStruct((batch, WINDOW, KV_W), F32),
            jax.ShapeDtypeStruct((batch, WINDOW, KV_W), F32),
        ],
        scratch=[], body=_attn_sample_body)


def _pad_rows(a, rows):
    if a.shape[0] == rows:
        return a
    return jnp.concatenate([a, jnp.zeros((rows - a.shape[0], a.shape[1]), a.dtype)], axis=0)


def _ssd_chunk(c, xs0_ref, xs1_ref, z0_ref, z1_ref, b_ref, c_ref, dt_ref, cst_ref, cw_ref, cb_ref, dtb_ref,
               alog_ref, dskip_ref, nw_ref, sel_ref, shift_ref, y_ref, ext_scr, st_scr, L):
    half = SSM_W // 2
    ext_scr[L:2 * L, 0:half] = xs0_ref[...]
    ext_scr[L:2 * L, half:SSM_W] = xs1_ref[...]
    ext_scr[L:2 * L, SSM_W:SSM_W + BC_W] = b_ref[...]
    ext_scr[L:2 * L, SSM_W + BC_W:CONV_CH] = c_ref[...]
    delayed = _dot(shift_ref[...], ext_scr[...])
    cur = ext_scr[L:2 * L, :].astype(F32)
    conv = cb_ref[...] + cur * cw_ref[CONV_W - 1:CONV_W, :]
    for i in range(CONV_W - 1):
        conv = conv + delayed[L * i:L * (i + 1)] * cw_ref[i:i + 1, :]
    s = cst_ref[...]
    w0, w1, w2 = cw_ref[0:1, :], cw_ref[1:2, :], cw_ref[2:3, :]
    head = jnp.concatenate([w0 * s[0:1] + w1 * s[1:2] + w2 * s[2:3], w0 * s[1:2] + w1 * s[2:3], w0 * s[2:3],
                            jnp.zeros((8 - (CONV_W - 1), CONV_CH), F32)], axis=0)
    conv = jnp.concatenate([conv[0:8] + jnp.where(c == 0, head, 0.0), conv[8:]], axis=0)
    conv = _silu(conv)
    xs = conv[:, :SSM_W]
    ext_scr[0:L, :] = ext_scr[L:2 * L, :]

    x_dt = dt_ref[...] + dtb_ref[...]
    dt = jnp.maximum(x_dt, 0.0) + jnp.log1p(jnp.exp(-jnp.abs(x_dt)))
    ad = dt * (-jnp.exp(alog_ref[...]))
    li = lax.broadcasted_iota(jnp.int32, (L, L), 0)
    si = lax.broadcasted_iota(jnp.int32, (L, L), 1)
    tril = (si <= li).astype(BF16)
    a_cum = _dot(jnp.concatenate([tril] * 3, axis=1), jnp.concatenate(_split3(ad), axis=0))
    sel = sel_ref[...]
    ex = _dot(jnp.concatenate(_split3(a_cum) + _split3(dt), axis=0), sel)
    col = ex[0:L] + ex[L:2 * L] + ex[2 * L:3 * L]
    dt_all = ex[3 * L:4 * L] + ex[4 * L:5 * L] + ex[5 * L:6 * L]
    lane_s = lax.broadcasted_iota(jnp.int32, (L, SSM_W), 1) % SSM_HEAD_DIM
    row_l = lax.broadcasted_iota(jnp.int32, (L, SSM_W), 0)
    row = jnp.sum(jnp.where(lane_s == row_l, col, 0.0), axis=0, keepdims=True)
    lmat = jnp.exp(jnp.where(lane_s <= row_l, col - row, NEG_INF))
    a_last = col[L - 1:L, :]
    exp_a = jnp.exp(col)
    decay = jnp.exp(a_last - col)
    exp_last = jnp.exp(a_last)

    xd = xs * dt_all
    xdd = xd * decay
    bi = lax.broadcasted_iota(jnp.int32, (GROUP_W, GROUP_W), 0) // SSM_HEAD_DIM
    bj = lax.broadcasted_iota(jnp.int32, (GROUP_W, GROUP_W), 1) // SSM_HEAD_DIM
    blockdiag = bi == bj

    for g in range(N_GROUPS):
        gs = slice(GROUP_W * g, GROUP_W * (g + 1))
        b_g = conv[:, SSM_W + D_STATE * g:SSM_W + D_STATE * (g + 1)]
        c_g = conv[:, SSM_W + BC_W + D_STATE * g:SSM_W + BC_W + D_STATE * (g + 1)].astype(BF16)
        b_pad = _pad_rows(b_g, CHUNK)
        cb = _dot_nt(c_g, jnp.concatenate([b_pad.astype(BF16)] * 4, axis=0))
        m_g = (cb * lmat[:, gs]).astype(BF16)
        xd_g = _pad_rows(xd[:, gs], CHUNK).astype(BF16)
        xd_bd = jnp.where(blockdiag, jnp.concatenate([xd_g] * 4, axis=0), jnp.zeros((), BF16))
        y_diag = _dot(m_g, xd_bd)
        st_g = st_scr[:, gs]
        y_off = _dot(c_g, st_g.astype(BF16)) * exp_a[:, gs]
        new_st = _dot(b_pad.T.astype(BF16), _pad_rows(xdd[:, gs], CHUNK).astype(BF16))
        st_scr[:, gs] = st_g * exp_last[:, gs] + new_st

        y = y_diag + y_off + xs[:, gs] * dskip_ref[:, gs]
        z_ref = z0_ref if g < N_GROUPS // 2 else z1_ref
        zs = slice(GROUP_W * (g % (N_GROUPS // 2)), GROUP_W * (g % (N_GROUPS // 2) + 1))
        gated = y * _silu(z_ref[:, zs].astype(F32))
        ms = jnp.mean(gated * gated, axis=-1, keepdims=True)
        y_ref[:, gs] = (gated * lax.rsqrt(ms + RMS_EPS) * nw_ref[:, gs]).astype(y_ref.dtype)


def _ssd_init(c, ins, outs, scr):
    st0_ref = ins[4]
    ext_scr, st_scr = scr
    ns, L = ext_scr.shape[0], ext_scr.shape[1] // 2
    for s in range(ns):
        ext_scr[s, 0:L, :] = jnp.zeros((L, CONV_CH), BF16)
        st_scr[s] = st0_ref[s].T


def _ssd_body(c, ins, outs, scr, mix_ref):
    (lo_ref, hi_ref, dt_ref, cst_ref, _, cw_ref, cb_ref, dtb_ref, alog_ref, dskip_ref, nw_ref, sel_ref,
     shift_ref) = ins
    ext_scr, st_scr = scr
    ns, L = dt_ref.shape[0], dt_ref.shape[1]
    half = SSM_W // 2
    for s in range(ns):
        view = lambda ref, start, s=s: ref.at[s, :, pl.ds(start, half)]
        _ssd_chunk(c, view(lo_ref, 0), view(lo_ref, half), view(lo_ref, 2 * half), view(hi_ref, 0),
                   view(hi_ref, half), view(hi_ref, 2 * half),
                   dt_ref.at[s], cst_ref.at[s], cw_ref, cb_ref, dtb_ref, alog_ref, dskip_ref, nw_ref,
                   sel_ref, shift_ref, mix_ref.at[s, :, pl.ds(ATTN_W, SSM_W)], ext_scr.at[s], st_scr.at[s], L)


def _ssd_final(c, ins, outs, scr):
    ncv_ref, nst_ref = outs
    ext_scr, st_scr = scr
    ns, L = ext_scr.shape[0], ext_scr.shape[1] // 2
    for s in range(ns):
        tail = ext_scr[s, L - BF16_ROWS:L, :].astype(F32)
        ncv_ref[s] = tail[BF16_ROWS - (CONV_W - 1):BF16_ROWS, :]
        nst_ref[s] = st_scr[s].T


def _shift_matrix(L):
    r = jnp.arange(3 * L)
    src = L + r % L - (CONV_W - 1) + r // L
    return (jnp.arange(2 * L)[None, :] == src[:, None]).astype(BF16)


def _ssd_part(proj3, dt3, conv_state, ssm_state, conv_w, conv_b, dtb, alog, dskip_all, ssm_nw, sel, L, ns):
    batch, t, _ = proj3.shape
    const = lambda shape: pl.BlockSpec(shape, lambda g, c: (0,) * len(shape))
    per_g = lambda shape: pl.BlockSpec((ns,) + shape, lambda g, c: (g,) + (0,) * len(shape))
    third = lambda k: pl.BlockSpec((ns, L, QKV_W), lambda g, c: (g, c, k))
    hp = N_SSM_HEADS * SSM_HEAD_DIM
    assert COL_XS == QKV_W and PROJ_W == 3 * QKV_W
    return _Part(
        in_specs=[
            third(1), third(2),
            pl.BlockSpec((ns, L, LANES), lambda g, c: (g, c, 0)),
            per_g((CONV_W - 1, CONV_CH)),
            per_g((hp, D_STATE)),
            const((CONV_W, CONV_CH)), const((1, CONV_CH)), const((1, LANES)), const((1, LANES)),
            const((1, SSM_W)), const((1, SSM_W)), const((LANES, SSM_W)), const((3 * L, 2 * L)),
        ],
        out_specs=[
            per_g((CONV_W - 1, CONV_CH)),
            per_g((hp, D_STATE)),
        ],
        out_shapes=[
            jax.ShapeDtypeStruct((batch, CONV_W - 1, CONV_CH), F32),
            jax.ShapeDtypeStruct((batch, hp, D_STATE), F32),
        ],
        scratch=[
            pltpu.VMEM((ns, 2 * L, CONV_CH), BF16),
            pltpu.VMEM((ns, D_STATE, hp), F32),
        ],
        operands=[proj3, proj3, dt3, conv_state, ssm_state, conv_w, conv_b, dtb, alog, dskip_all, ssm_nw, sel,
                  _shift_matrix(L)],
        body=_ssd_body, init=_ssd_init, final=_ssd_final)


def _outproj_kernel(*refs, n_cast):
    mix_ref, w_ref, x_ref = refs[:3]
    cast_in, o_ref, cast_out = refs[3:3 + n_cast], refs[3 + n_cast], refs[4 + n_cast:]
    o_ref[...] = x_ref[...] + _dot(mix_ref[...], w_ref[...])
    _cast_blocks(cast_in, cast_out)


def _outproj(mix, w_out, x2d, tm, tn, cast=()):
    t = x2d.shape[0]
    ni = t // tm
    c_in, c_out, c_shape = _cast_riders(cast, (D_MODEL // tn) * ni, lambda j, i: j * ni + i)
    return pl.pallas_call(
        functools.partial(_outproj_kernel, n_cast=len(cast)),
        grid=(D_MODEL // tn, ni),
        in_specs=[
            pl.BlockSpec((tm, MIX_W), lambda j, i: (i, 0)),
            pl.BlockSpec((MIX_W, tn), lambda j, i: (0, j)),
            pl.BlockSpec((tm, tn), lambda j, i: (i, j)),
        ] + c_in,
        out_specs=[pl.BlockSpec((tm, tn), lambda j, i: (i, j))] + c_out,
        out_shape=[jax.ShapeDtypeStruct((t, D_MODEL), F32)] + c_shape,
        compiler_params=_params("arbitrary", "arbitrary"),
        name="outproj",
    )(mix, w_out, x2d, *cast)


def _ffn_kernel(x_ref, nw_ref, wu_ref, wd_ref, o_ref, h_scr, *, n_split):
    @pl.when(pl.program_id(1) == 0)
    def _():
        _rmsnorm_rows(x_ref, nw_ref, h_scr, copy_ref=o_ref)

    u = jnp.maximum(_dot(h_scr[...], wu_ref[...]), 0.0)
    u = (u * u).astype(BF16)
    wn = D_MODEL // n_split
    for n in range(n_split):
        o_ref[:, wn * n:wn * (n + 1)] += _dot(u, wd_ref[:, wn * n:wn * (n + 1)])


def _ffn(x2d, norm_w, w_up, w_down, tm, tf):
    t = x2d.shape[0]
    return pl.pallas_call(
        functools.partial(_ffn_kernel, n_split=4),
        grid=(t // tm, FFN_HIDDEN // tf),
        in_specs=[
            pl.BlockSpec((tm, D_MODEL), lambda i, f: (i, 0)),
            pl.BlockSpec((1, D_MODEL), lambda i, f: (0, 0)),
            pl.BlockSpec((D_MODEL, tf), lambda i, f: (0, f)),
            pl.BlockSpec((tf, D_MODEL), lambda i, f: (f, 0)),
        ],
        out_specs=pl.BlockSpec((tm, D_MODEL), lambda i, f: (i, 0)),
        out_shape=jax.ShapeDtypeStruct((t, D_MODEL), F32),
        scratch_shapes=[pltpu.VMEM((tm, D_MODEL), BF16)],
        compiler_params=_params("arbitrary", "arbitrary"),
        name="ffn",
    )(x2d, norm_w, w_up, w_down)


def _rope_tables(pos):
    inv_freq = ROPE_THETA ** (-jnp.arange(ROPE_HALF, dtype=F32) / ROPE_HALF)
    lane = jnp.arange(LANES) % HEAD_DIM
    freq = jnp.where(lane < ROPE_DIM, inv_freq[lane % ROPE_HALF], 0.0)
    ang = pos.astype(F32)[:, None] * freq[None, :]
    cos, sin = jnp.cos(ang), jnp.sin(ang)
    return [cos, jnp.where(lane < ROPE_HALF, -sin, 0.0), jnp.where(lane >= ROPE_HALF, sin, 0.0)]


def _layer_params(norm_mix_w, w_in, q_norm_w, k_norm_w, attn_sinks, conv_w, conv_b, dt_bias, a_log, d_skip,
                  ssm_norm_w, w_out, norm_ffn_w, w_up, w_down):
    pad_h = lambda a: jnp.pad(a.astype(F32), (0, LANES - N_SSM_HEADS))[None, :]
    sel = (jnp.arange(LANES)[:, None] == (jnp.arange(SSM_W) // SSM_HEAD_DIM)[None, :]).astype(BF16)
    return dict(
        norm_mix=norm_mix_w.astype(F32)[None, :],
        qw=jnp.tile(q_norm_w.astype(F32), 2)[None, :], kw=jnp.tile(k_norm_w.astype(F32), 2)[None, :],
        sinks=jnp.repeat(attn_sinks.astype(F32), HEAD_DIM).reshape(Q_TILES, LANES),
        conv_w=conv_w.astype(F32), conv_b=conv_b.astype(F32)[None, :],
        dtb=pad_h(dt_bias), alog=pad_h(a_log),
        dskip=jnp.repeat(d_skip.astype(F32), SSM_HEAD_DIM)[None, :],
        ssm_nw=ssm_norm_w.astype(F32)[None, :], sel=sel,
        norm_ffn=norm_ffn_w.astype(F32)[None, :],
    )


def _stream_back(x, proj, dt_raw, pos, cache, conv_state, ssm_state, p, w_out, w_up, w_down):
    batch, t, _ = x.shape
    ns = max(n for n in STREAMS_PER_STEP if batch % n == 0)
    x2d = x.reshape(batch * t, D_MODEL)
    proj3 = proj.reshape(batch, t, PROJ_W)
    tabs = _rope_tables(pos)
    if cache is None:
        attn_part = _attn_prompt_part(proj3, tabs, p["qw"], p["kw"], p["sinks"], ns)
        L = CHUNK
    else:
        past_k, past_v = cache
        attn_part = _attn_sample_part(proj3, past_k.reshape(batch, WINDOW, KV_W),
                                      past_v.reshape(batch, WINDOW, KV_W), tabs, p["qw"], p["kw"], p["sinks"], ns)
        L = min(CHUNK, t)
    hp = N_SSM_HEADS * SSM_HEAD_DIM
    ssd_part = _ssd_part(proj3, dt_raw.reshape(batch, t, LANES), conv_state,
                         ssm_state.reshape(batch, hp, D_STATE), p["conv_w"], p["conv_b"], p["dtb"], p["alog"],
                         p["dskip"], p["ssm_nw"], p["sel"], L, ns)
    parts = [attn_part, ssd_part]
    grid = (batch // ns, t // L)
    if w_out.dtype != BF16:
        c_in, c_out, c_shape = _cast_riders((w_out, w_up), grid[0] * grid[1], lambda g, c: g * grid[1] + c)
        parts.append(_Part(c_in, [w_out, w_up], c_out, c_shape, [],
                           body=lambda c, ins, outs, scr, mix_ref: _cast_blocks(ins, outs)))
    (mix, new_k, new_v), (new_conv, new_ssm), *rest = _fused_call(parts, grid, "mixers")
    if rest:
        w_out, w_up = rest[0]
    late = () if w_down.dtype == BF16 else (w_down,)
    x1, *cast = _outproj(mix.reshape(batch * t, MIX_W), w_out, x2d, TILE_M, OUTPROJ_TN, cast=late)
    if late:
        w_down = cast.pop()
    out = _ffn(x1, p["norm_ffn"], w_up, w_down, TILE_M, FFN_TF)
    return (out.reshape(batch, t, D_MODEL),
            new_k.reshape(batch, WINDOW, N_KV_HEADS, HEAD_DIM), new_v.reshape(batch, WINDOW, N_KV_HEADS, HEAD_DIM),
            new_conv, new_ssm.reshape(batch, N_SSM_HEADS, SSM_HEAD_DIM, D_STATE), (w_out, w_up, w_down))


def kernel(x_prompt, x_sample, cache_k, cache_v, state_conv, state_ssm, norm_mix_w, w_in, q_norm_w, k_norm_w,
           attn_sinks, conv_w, conv_b, dt_bias, a_log, d_skip, ssm_norm_w, w_out, norm_ffn_w, w_up, w_down):
    depth = w_in.shape[0]
    b_p, t_p, _ = x_prompt.shape
    b_s, t_s, _ = x_sample.shape
    assert cache_k.shape[2] == WINDOW and t_p % CHUNK == 0 and t_p >= WINDOW
    assert t_s <= CHUNK and t_s % BF16_ROWS == 0
    pos_p = jnp.arange(t_p, dtype=jnp.int32)
    pos_s = PAST_LEN + jnp.arange(t_s, dtype=jnp.int32)
    hp, hs = x_prompt, x_sample
    outs_p, outs_s = [], []
    for layer in range(depth):
        p = _layer_params(norm_mix_w[layer], w_in[layer], q_norm_w[layer], k_norm_w[layer], attn_sinks[layer],
                          conv_w[layer], conv_b[layer], dt_bias[layer], a_log[layer], d_skip[layer],
                          ssm_norm_w[layer], w_out[layer], norm_ffn_w[layer], w_up[layer], w_down[layer])
        conv0 = jnp.zeros((b_p, CONV_W - 1, CONV_CH), F32)
        ssm0 = jnp.zeros((b_p, N_SSM_HEADS, SSM_HEAD_DIM, D_STATE), F32)
        xs2d, xp2d = hs.reshape(b_s * t_s, D_MODEL), hp.reshape(b_p * t_p, D_MODEL)
        w_in_t = w_in[layer].T.astype(BF16)
        proj_p, dt_p = _inproj(xp2d, p["norm_mix"], w_in_t, TILE_M, INPROJ_TN)
        proj_s, dt_s = _inproj(xs2d, p["norm_mix"], w_in_t, TILE_M, INPROJ_TN)
        hp, *rest_p, (w_out16, w_up16, w_down16) = _stream_back(hp, proj_p, dt_p, pos_p, None, conv0, ssm0, p,
                                                                w_out[layer], w_up[layer], w_down[layer])
        hs, *rest_s, _ = _stream_back(hs, proj_s, dt_s, pos_s, (cache_k[layer], cache_v[layer]),
                                      state_conv[layer], state_ssm[layer], p, w_out16, w_up16, w_down16)
        outs_p.append(rest_p)
        outs_s.append(rest_s)
    stack = lambda outs, i: jnp.stack([o[i] for o in outs])
    return (hp, hs, stack(outs_p, 0), stack(outs_p, 1), stack(outs_p, 2), stack(outs_p, 3),
            stack(outs_s, 0), stack(outs_s, 1), stack(outs_s, 2), stack(outs_s, 3))
```
